```python
import jax
import jax.numpy as jnp
from jax import lax
import numpy as np

D_MODEL = 2048
BATCH = 2
SEQ = 4096
DEPTH = 1

GRID_W = 64
CTX_LEN = 256
RWKV_HEADS = 16
RWKV_HEAD_DIM = 64
RWKV_WIDTH = RWKV_HEADS * RWKV_HEAD_DIM
DECAY_LORA = 96
ICL_LORA = 96
GATE_LORA = 64
RWKV_GN_EPS = 64e-5
GLA_HEADS = 4
GLA_KEY_DIM = 128
GLA_VAL_DIM = 256
GLA_QK_WIDTH = GLA_HEADS * GLA_KEY_DIM
GLA_V_WIDTH = GLA_HEADS * GLA_VAL_DIM
GLA_GATE_LORA = 16
GLA_TAU = 16.0
GLA_CHUNK = 64
GLA_NORM_EPS = 1e-5
D_FF_RAW = -(-8 * D_MODEL // 3)
D_FF = -(-D_FF_RAW // 256) * 256
NORM_EPS = 1e-6
RWKV_SPLITS = (RWKV_WIDTH, RWKV_WIDTH, RWKV_WIDTH, DECAY_LORA, DECAY_LORA, ICL_LORA, ICL_LORA, GATE_LORA)
GLA_SPLITS = (GLA_QK_WIDTH, GLA_QK_WIDTH, GLA_V_WIDTH, GLA_V_WIDTH, GLA_GATE_LORA, GLA_GATE_LORA)
RWKV_IN = sum(RWKV_SPLITS)
MIX_IN = RWKV_IN + sum(GLA_SPLITS)
N_IN = MIX_IN + 2 * D_MODEL

kernel_name = 'hybrid_rwkv7_gla_gated_dit_block'


def split_cols(p, sizes):
    return jnp.split(p, [int(i) for i in np.cumsum(sizes)[:-1]], axis=-1)


def rms_norm(x, g):
    xf = x.astype(jnp.float32)
    y = xf * lax.rsqrt(jnp.mean(xf * xf, axis=-1, keepdims=True) + NORM_EPS)
    return (y * g).astype(x.dtype)


def modulate(h, shift, scale):
    return h * (1.0 + scale) + shift


def centred_shift_mix(p, mu):
    prev = jnp.pad(p[:, :-1], ((0, 0), (1, 0), (0, 0)))
    nxt = jnp.pad(p[:, 1:], ((0, 0), (0, 1), (0, 0)))
    return p + mu * (0.5 * (prev + nxt) - p)


def to_col_major(t):
    b, n, ch = t.shape
    rows = n // GRID_W
    return t.reshape(b, rows, GRID_W, ch).swapaxes(1, 2).reshape(b, n, ch)


def from_col_major(t):
    b, n, ch = t.shape
    rows = n // GRID_W
    return t.reshape(b, GRID_W, rows, ch).swapaxes(1, 2).reshape(b, n, ch)


def rwkv7_scan(r, w, k, v, kk, a, s0, reverse, want_out):
    xs = tuple(jnp.moveaxis(t, 1, 0) for t in (w, k, v, kk, a))
    if want_out:
        xs = xs + (jnp.moveaxis(r, 1, 0),)

    def step(s, inp):
        w_t, k_t, v_t, kk_t, a_t = inp[:5]
        sa = jnp.einsum('bhvk,bhk->bhv', s, kk_t)
        s = (s * w_t[:, :, None, :] - sa[..., None] * (kk_t * a_t)[:, :, None, :]
             + v_t[..., None] * k_t[:, :, None, :])
        if not want_out:
            return s, None
        return s, jnp.einsum('bhvk,bhk->bhv', s, inp[5])

    s, ys = lax.scan(step, s0, xs, reverse=reverse)
    return s, (jnp.moveaxis(ys, 0, 1) if want_out else None)


def rwkv7_mixer(p, lp, s0_f, s0_b, want_out):
    bsz, n, _ = p.shape
    r, k, v, wd_f, wd_b, ad_f, ad_b, gd = split_cols(p, RWKV_SPLITS)

    def heads(t):
        return t.reshape(bsz, n, RWKV_HEADS, RWKV_HEAD_DIM)

    kk = heads((k * lp['rwkv_k_k']).astype(jnp.float32))
    kk = kk / jnp.maximum(jnp.linalg.norm(kk, axis=-1, keepdims=True), 1e-12)
    rh, vh = heads(r), heads(v)
    y_sum, bonus, states = 0.0, 0.0, []
    for d, (wd, ad, s0) in enumerate(((wd_f, ad_f, s0_f), (wd_b, ad_b, s0_b))):
        w_log = -jax.nn.softplus(-(lp['rwkv_w0'][d] + jnp.tanh(wd) @ lp['rwkv_w2'][d])) - 0.5
        decay = jnp.exp(-jnp.exp(w_log.astype(jnp.float32)))
        a = jax.nn.sigmoid(lp['rwkv_a0'][d] + ad @ lp['rwkv_a2'][d])
        kd = heads(k * (1.0 + (a - 1.0) * lp['rwkv_k_a']))
        s, y = rwkv7_scan(rh, heads(decay), kd, vh, kk, heads(a), s0, d == 1, want_out)
        states.append(s)
        if want_out:
            y_sum = y_sum + y
            bonus = bonus + jnp.sum(rh * kd * lp['rwkv_r_k'], axis=-1, keepdims=True) * vh
    if not want_out:
        return None, states[0], states[1]
    y_sum = y_sum.astype(jnp.float32)
    mean = jnp.mean(y_sum, axis=-1, keepdims=True)
    var = jnp.var(y_sum, axis=-1, keepdims=True)
    gn = ((y_sum - mean) * lax.rsqrt(var + RWKV_GN_EPS)).reshape(bsz, n, RWKV_WIDTH)
    gn = gn * lp['rwkv_ln_g'] + lp['rwkv_ln_b']
    g = jax.nn.sigmoid(gd) @ lp['rwkv_g2']
    return (gn + bonus.reshape(bsz, n, RWKV_WIDTH)) * g, states[0], states[1]


def gla_chunked(q, k, v, log_a, s0, want_out):
    bsz, n, h, _ = k.shape
    dv = v.shape[-1]
    nc = n // GLA_CHUNK

    def to_chunks(t):
        return jnp.moveaxis(t.reshape(bsz, nc, GLA_CHUNK, h, t.shape[-1]), 1, 0).swapaxes(2, 3)

    in_chunk = jnp.tril(jnp.ones((GLA_CHUNK, GLA_CHUNK), dtype=bool))[:, :, None]
    xs = (to_chunks(k), to_chunks(v), to_chunks(log_a))
    if want_out:
        xs = xs + (to_chunks(q),)

    def step(s, inp):
        k_c, v_c, la_c = inp[:3]
        b = jnp.cumsum(la_c.astype(jnp.float32), axis=2)
        b_end = b[:, :, -1:, :]
        s_next = (jnp.exp(b_end)[:, :, 0, :, None] * s
                  + jnp.einsum('bhsd,bhsv->bhdv', k_c * jnp.exp(b_end - b), v_c))
        if not want_out:
            return s_next, None
        q_c = inp[3]
        o_inter = jnp.einsum('bhtd,bhdv->bhtv', q_c * jnp.exp(b), s)
        diff = jnp.where(in_chunk, b[:, :, :, None, :] - b[:, :, None, :, :], -jnp.inf)
        att = jnp.einsum('bhtd,bhsd,bhtsd->bhts', q_c, k_c, jnp.exp(diff))
        return s_next, o_inter + jnp.einsum('bhts,bhsv->bhtv', att, v_c)

    s, o = lax.scan(step, s0, xs)
    if want_out:
        o = jnp.moveaxis(o.swapaxes(2, 3), 0, 1).reshape(bsz, n, h, dv)
    return s, o


def gla_mixer(p, lp, s0_f, s0_b, want_out):
    bsz, n, _ = p.shape
    q, k, v, g, ad_f, ad_b = split_cols(p, GLA_SPLITS)

    def heads(t):
        return t.reshape(bsz, n, GLA_HEADS, -1)

    q = heads(q) * GLA_KEY_DIM ** -0.5
    k, v = heads(k), heads(v)
    la = [heads(jax.nn.log_sigmoid((ad @ lp['gla_alpha_up'][d] + lp['gla_alpha_b'][d]).astype(jnp.float32)) / GLA_TAU)
          for d, ad in enumerate((ad_f, ad_b))]

    def flip(t):
        return jnp.flip(t, axis=1)

    s_f, o_f = gla_chunked(q, k, v, la[0], s0_f, want_out)
    s_b, o_b = gla_chunked(flip(q), flip(k), flip(v), flip(la[1]), s0_b, want_out)
    if not want_out:
        return None, s_f, s_b
    o = (o_f + flip(o_b)).astype(jnp.float32)
    o = o * lax.rsqrt(jnp.mean(o * o, axis=-1, keepdims=True) + GLA_NORM_EPS)
    o = o.reshape(bsz, n, GLA_V_WIDTH) * lp['gla_norm_g']
    return o * jax.nn.silu(g), s_f, s_b


def merge_branches(p_gate, y_a, y_b, lp):
    g_a, g_b = jnp.split(p_gate, 2, axis=-1)
    m = jax.nn.sigmoid(g_a) * (y_a @ lp['w_rwkv_up']) + jax.nn.sigmoid(g_b) * (y_b @ lp['w_gla_up'])
    return m @ lp['w_out']


def swiglu(h, lp):
    return (jax.nn.silu(h @ lp['ffn_w_gate']) * (h @ lp['ffn_w_up'])) @ lp['ffn_w_down']


def trunk_layer(x, ctx, c, c_ctx, lp, last):
    bsz = x.shape[0]
    mod_x = jax.nn.silu(c) @ lp['ada_w'] + lp['ada_b']
    mod_c = jax.nn.silu(c_ctx) @ lp['ada_w'] + lp['ada_b']
    shx1, scx1, gx1, shx2, scx2, gx2 = [t[:, None, :] for t in jnp.split(mod_x, 6, axis=-1)]
    shc1, scc1, gc1, shc2, scc2, gc2 = jnp.split(mod_c, 6, axis=-1)

    hx = modulate(rms_norm(x, lp['norm_pre_mix']), shx1, scx1)
    hc = modulate(rms_norm(ctx, lp['norm_pre_mix']), shc1, scc1)
    px = hx @ lp['w_in']
    pc = hc @ (lp['w_in'][:, :MIX_IN] if last else lp['w_in'])

    zero_r = jnp.zeros((bsz, RWKV_HEADS, RWKV_HEAD_DIM, RWKV_HEAD_DIM), jnp.float32)
    zero_g = jnp.zeros((bsz, GLA_HEADS, GLA_KEY_DIM, GLA_VAL_DIM), jnp.float32)

    yc_a, sr_f, sr_b = rwkv7_mixer(centred_shift_mix(pc[..., :RWKV_IN], lp['shift_mu']), lp, zero_r, zero_r, not last)
    yx_a, _, _ = rwkv7_mixer(centred_shift_mix(px[..., :RWKV_IN], lp['shift_mu']), lp, sr_f, sr_b, True)

    yc_b, sg_f, sg_b = gla_mixer(pc[..., RWKV_IN:MIX_IN], lp, zero_g, zero_g, not last)
    yx_b, _, _ = gla_mixer(to_col_major(px[..., RWKV_IN:MIX_IN]), lp, sg_f, sg_b, True)
    yx_b = from_col_major(yx_b)

    x = x + gx1 * rms_norm(merge_branches(px[..., MIX_IN:], yx_a, yx_b, lp), lp['norm_post_mix'])
    hx = modulate(rms_norm(x, lp['norm_pre_ffn']), shx2, scx2)
    x = x + gx2 * rms_norm(swiglu(hx, lp), lp['norm_post_ffn'])

    if not last:
        ctx = ctx + gc1 * rms_norm(merge_branches(pc[..., MIX_IN:], yc_a, yc_b, lp), lp['norm_post_mix'])
        hc = modulate(rms_norm(ctx, lp['norm_pre_ffn']), shc2, scc2)
        ctx = ctx + gc2 * rms_norm(swiglu(hc, lp), lp['norm_post_ffn'])
    return x, ctx


def setup_inputs(seed: int = 0) -> dict:
    key = jax.random.key(seed)
    ks = jax.random.split(key, 31)
    L, D, W = DEPTH, D_MODEL, RWKV_WIDTH

    def nrm(i, shape, std):
        return std * jax.random.normal(ks[i], shape, jnp.float32)

    def uni(i, shape, lo, hi):
        return jax.random.uniform(ks[i], shape, jnp.float32, lo, hi)

    return {
        'x': nrm(0, (BATCH, SEQ, D), 1.0),
        'c': nrm(1, (BATCH, D), 1.0),
        'ctx': nrm(2, (BATCH, CTX_LEN, D), 1.0),
        'c_ctx': nrm(3, (D,), 1.0),
        'ada_w': nrm(4, (L, D, 6 * D), 0.5 * D ** -0.5),
        'ada_b': nrm(5, (L, 6 * D), 0.02),
        'norm_pre_mix': 1.0 + nrm(6, (L, D), 0.05),
        'norm_post_mix': 1.0 + nrm(7, (L, D), 0.05),
        'norm_pre_ffn': 1.0 + nrm(8, (L, D), 0.05),
        'norm_post_ffn': 1.0 + nrm(9, (L, D), 0.05),
        'w_in': nrm(10, (L, D, N_IN), D ** -0.5),
        'shift_mu': uni(11, (L, RWKV_IN), 0.0, 1.0),
        'rwkv_w0': uni(12, (L, 2, W), -6.0, -1.0),
        'rwkv_w2': nrm(13, (L, 2, DECAY_LORA, W), 0.5 * DECAY_LORA ** -0.5),
        'rwkv_a0': nrm(14, (L, 2, W), 0.1),
        'rwkv_a2': nrm(15, (L, 2, ICL_LORA, W), ICL_LORA ** -0.5),
        'rwkv_g2': nrm(16, (L, GATE_LORA, W), GATE_LORA ** -0.5),
        'rwkv_k_k': 0.85 + nrm(17, (L, W), 0.05),
        'rwkv_k_a': 1.0 + nrm(18, (L, W), 0.05),
        'rwkv_r_k': nrm(19, (L, RWKV_HEADS, RWKV_HEAD_DIM), 0.1),
        'rwkv_ln_g': 1.0 + nrm(20, (L, W), 0.1),
        'rwkv_ln_b': nrm(21, (L, W), 0.02),
        'w_rwkv_up': nrm(22, (L, W, D), W ** -0.5),
        'gla_alpha_up': nrm(23, (L, 2, GLA_GATE_LORA, GLA_QK_WIDTH), 0.5 * GLA_GATE_LORA ** -0.5),
        'gla_alpha_b': uni(24, (L, 2, GLA_QK_WIDTH), 0.0, 4.0),
        'gla_norm_g': 1.0 + nrm(25, (L, GLA_V_WIDTH), 0.1),
        'w_gla_up': nrm(26, (L, GLA_V_WIDTH, D), GLA_V_WIDTH ** -0.5),
        'w_out': nrm(27, (L, D, D), D ** -0.5),
        'ffn_w_gate': nrm(28, (L, D, D_FF), D ** -0.5),
        'ffn_w_up': nrm(29, (L, D, D_FF), D ** -0.5),
        'ffn_w_down': nrm(30, (L, D_FF, D), D_FF ** -0.5),
    }


def reference(x, c, ctx, c_ctx, ada_w, ada_b, norm_pre_mix, norm_post_mix, norm_pre_ffn, norm_post_ffn,
              w_in, shift_mu, rwkv_w0, rwkv_w2, rwkv_a0, rwkv_a2, rwkv_g2, rwkv_k_k, rwkv_k_a, rwkv_r_k,
              rwkv_ln_g, rwkv_ln_b, w_rwkv_up, gla_alpha_up, gla_alpha_b, gla_norm_g, w_gla_up, w_out,
              ffn_w_gate, ffn_w_up, ffn_w_down):
    for i in range(DEPTH):
        lp = {
            'ada_w': ada_w[i], 'ada_b': ada_b[i],
            'norm_pre_mix': norm_pre_mix[i], 'norm_post_mix': norm_post_mix[i],
            'norm_pre_ffn': norm_pre_ffn[i], 'norm_post_ffn': norm_post_ffn[i],
            'w_in': w_in[i], 'shift_mu': shift_mu[i],
            'rwkv_w0': rwkv_w0[i], 'rwkv_w2': rwkv_w2[i], 'rwkv_a0': rwkv_a0[i], 'rwkv_a2': rwkv_a2[i],
            'rwkv_g2': rwkv_g2[i], 'rwkv_k_k': rwkv_k_k[i], 'rwkv_k_a': rwkv_k_a[i], 'rwkv_r_k': rwkv_r_k[i],
            'rwkv_ln_g': rwkv_ln_g[i], 'rwkv_ln_b': rwkv_ln_b[i], 'w_rwkv_up': w_rwkv_up[i],
            'gla_alpha_up': gla_alpha_up[i], 'gla_alpha_b': gla_alpha_b[i], 'gla_norm_g': gla_norm_g[i],
            'w_gla_up': w_gla_up[i], 'w_out': w_out[i],
            'ffn_w_gate': ffn_w_gate[i], 'ffn_w_up': ffn_w_up[i], 'ffn_w_down': ffn_w_down[i],
        }
        x, ctx = trunk_layer(x, ctx, c, c_ctx, lp, i == DEPTH - 1)
    return x
```

```python
import functools

import jax
import jax.numpy as jnp
from jax import lax
from jax.experimental import pallas as pl
from jax.experimental.pallas import tpu as pltpu

F32 = jnp.float32
BF16 = jnp.bfloat16
HIGHEST = lax.Precision.HIGHEST

LANES = 128
VMEM_LIMIT_BYTES = 56 * 1024 * 1024

GRID_W = 64
CHUNK = 64
RWKV_HEADS, RWKV_HEAD_DIM = 16, 64
RWKV_WIDTH = RWKV_HEADS * RWKV_HEAD_DIM
DECAY_LORA = ICL_LORA = 96
GATE_LORA = 64
RWKV_GN_EPS = 64e-5
GLA_HEADS, GLA_KEY_DIM, GLA_VAL_DIM = 4, 128, 256
GLA_QK_WIDTH = GLA_HEADS * GLA_KEY_DIM
GLA_V_WIDTH = GLA_HEADS * GLA_VAL_DIM
GLA_GATE_LORA = 16
GLA_TAU = 16.0
GLA_NORM_EPS = 1e-5
GLA_SUB = 16
NORM_EPS = 1e-6

RWKV_IN = 3 * RWKV_WIDTH + 2 * DECAY_LORA + 2 * ICL_LORA + GATE_LORA
RWKV_PAD = 3584
OFF_WD = 3 * RWKV_WIDTH
OFF_AD = OFF_WD + 2 * DECAY_LORA
OFF_GD = OFF_AD + 2 * ICL_LORA
GLA_IN = 2 * GLA_QK_WIDTH + 2 * GLA_V_WIDTH + 2 * GLA_GATE_LORA
GLA_BLK = 512
GLA_PAD = 7 * GLA_BLK
OFF_GLA_AD = 6 * GLA_BLK

PAIR = 2 * RWKV_HEAD_DIM
N_PAIRS = RWKV_HEADS // 2


def _cparams(semantics):
    return pltpu.CompilerParams(dimension_semantics=semantics, vmem_limit_bytes=VMEM_LIMIT_BYTES)


def _sigmoid(z):
    return 1.0 / (1.0 + jnp.exp(-z))


def _softplus(z):
    return jnp.maximum(z, 0.0) + jnp.log(1.0 + jnp.exp(-jnp.abs(z)))


def _mm(a, b, precision=None):
    return jnp.dot(a, b, precision=precision, preferred_element_type=F32)


def _mm_nt(a, b, precision=None):
    return lax.dot_general(a, b, (((1,), (1,)), ((), ())), precision=precision, preferred_element_type=F32)


def _mm_tn(a, b, precision=None):
    return lax.dot_general(a, b, (((0,), (0,)), ((), ())), precision=precision, preferred_element_type=F32)


def _mod_kernel(c_ref, w_ref, b_ref, o_ref):
    s = c_ref[...]
    s = s * _sigmoid(s)
    o_ref[...] = _mm(s, w_ref[...], HIGHEST) + b_ref[...]


def _modulation(cvecs, ada_w, ada_b, tn=1024):
    m, d = cvecs.shape
    n = ada_w.shape[1]
    return pl.pallas_call(
        _mod_kernel,
        grid=(n // tn,),
        in_specs=[pl.BlockSpec((m, d), lambda j: (0, 0)),
                  pl.BlockSpec((d, tn), lambda j: (0, j)),
                  pl.BlockSpec((1, tn), lambda j: (0, j))],
        out_specs=pl.BlockSpec((m, tn), lambda j: (0, j)),
        out_shape=jax.ShapeDtypeStruct((m, n), F32),
        compiler_params=_cparams(("arbitrary",)),
        name="adaln_mod",
    )(cvecs, ada_w, ada_b.reshape(1, n))


def _proj_kernel(x_ref, sh_ref, sc_ref, g_ref, w_ref, o_ref, h_scr):
    @pl.when(pl.program_id(2) == 0)
    def _():
        x = x_ref[0]
        ms = jnp.mean(x * x, axis=-1, keepdims=True)
        y = x * lax.rsqrt(ms + NORM_EPS) * g_ref[...]
        h_scr[...] = (y * (1.0 + sc_ref[0]) + sh_ref[0]).astype(BF16)

    o_ref[0] = _mm(h_scr[...], w_ref[...])


def _project(x, shift, scale, gain, w, tm, tn, name):
    b, t, d = x.shape
    n = w.shape[1]
    return pl.pallas_call(
        _proj_kernel,
        grid=(b, t // tm, n // tn),
        in_specs=[pl.BlockSpec((1, tm, d), lambda bi, i, j: (bi, i, 0)),
                  pl.BlockSpec((1, 1, d), lambda bi, i, j: (bi, 0, 0)),
                  pl.BlockSpec((1, 1, d), lambda bi, i, j: (bi, 0, 0)),
                  pl.BlockSpec((1, d), lambda bi, i, j: (0, 0)),
                  pl.BlockSpec((d, tn), lambda bi, i, j: (0, j))],
        out_specs=pl.BlockSpec((1, tm, tn), lambda bi, i, j: (bi, i, j)),
        out_shape=jax.ShapeDtypeStruct((b, t, n), F32),
        scratch_shapes=[pltpu.VMEM((tm, d), BF16)],
        compiler_params=_cparams(("arbitrary", "arbitrary", "arbitrary")),
        name=name,
    )(x, shift, scale, gain, w)


def _shiftmix_kernel(n_ctx_blk, n_blk, tb, lat_ref, ctx_ref, lprev_ref, lnext_ref, cprev_ref, cnext_ref, mu_ref, o_ref):
    i = pl.program_id(1)
    is_ctx = i < n_ctx_blk
    p = jnp.where(is_ctx, ctx_ref[0], lat_ref[0])
    first = jnp.logical_or(i == 0, i == n_ctx_blk)
    last = jnp.logical_or(i == n_ctx_blk - 1, i == n_blk - 1)
    hp = jnp.where(is_ctx, cprev_ref[0, 7:8, :], lprev_ref[0, 7:8, :])
    hn = jnp.where(is_ctx, cnext_ref[0, 0:1, :], lnext_ref[0, 0:1, :])
    hp = jnp.where(first, 0.0, hp)
    hn = jnp.where(last, 0.0, hn)
    row = lax.broadcasted_iota(jnp.int32, p.shape, 0)
    prev = jnp.where(row == 0, hp, pltpu.roll(p, 1, 0))
    nxt = jnp.where(row == tb - 1, hn, pltpu.roll(p, tb - 1, 0))
    o_ref[0] = p + mu_ref[...] * (0.5 * (prev + nxt) - p)


def _shiftmix(p_lat, p_ctx, mu, tb=128):
    b, t, w = p_lat.shape
    tc = p_ctx.shape[1]
    n_ctx_blk, n_lat_blk = tc // tb, t // tb
    n_blk = n_ctx_blk + n_lat_blk
    r8 = tb // 8

    def lat_main(bi, i):
        return (bi, jnp.maximum(i - n_ctx_blk, 0), 0)

    def ctx_main(bi, i):
        return (bi, jnp.minimum(i, n_ctx_blk - 1), 0)

    def lat_prev(bi, i):
        return (bi, jnp.maximum((i - n_ctx_blk) * r8 - 1, 0), 0)

    def lat_next(bi, i):
        return (bi, jnp.clip((i - n_ctx_blk + 1) * r8, 0, t // 8 - 1), 0)

    def ctx_prev(bi, i):
        return (bi, jnp.clip(i * r8 - 1, 0, tc // 8 - 1), 0)

    def ctx_next(bi, i):
        return (bi, jnp.minimum((i + 1) * r8, tc // 8 - 1), 0)

    return pl.pallas_call(
        functools.partial(_shiftmix_kernel, n_ctx_blk, n_blk, tb),
        grid=(b, n_blk),
        in_specs=[pl.BlockSpec((1, tb, w), lat_main),
                  pl.BlockSpec((1, tb, w), ctx_main),
                  pl.BlockSpec((1, 8, w), lat_prev),
                  pl.BlockSpec((1, 8, w), lat_next),
                  pl.BlockSpec((1, 8, w), ctx_prev),
                  pl.BlockSpec((1, 8, w), ctx_next),
                  pl.BlockSpec((1, w), lambda bi, i: (0, 0))],
        out_specs=pl.BlockSpec((1, tb, w), lambda bi, i: (bi, i, 0)),
        out_shape=jax.ShapeDtypeStruct((b, tc + t, w), F32),
        compiler_params=_cparams(("arbitrary", "arbitrary")),
        name="rwkv_shiftmix",
    )(p_lat, p_ctx, p_lat, p_lat, p_ctx, p_ctx, mu)


def _head_sum_matrix():
    r = lax.broadcasted_iota(jnp.int32, (PAIR, PAIR), 0)
    c = lax.broadcasted_iota(jnp.int32, (PAIR, PAIR), 1)
    return jnp.where((r >> 6) == (c >> 6), 1.0, 0.0).astype(F32)


def _rwkv_kernel(rev, prec, pm_ref, w2_ref, w0_ref, a2_ref, a0_ref, kk_ref, ka_ref, y_ref, st_scr):
    hd = RWKV_HEAD_DIM
    c2 = 2 * CHUNK

    @pl.when(pl.program_id(1) == 0)
    def _():
        st_scr[...] = jnp.zeros_like(st_scr)

    p = pm_ref[0]
    r = p[:, 0:RWKV_WIDTH]
    k = p[:, RWKV_WIDTH:2 * RWKV_WIDTH]
    v = p[:, 2 * RWKV_WIDTH:3 * RWKV_WIDTH]
    d_off = DECAY_LORA if rev else 0
    wd = p[:, OFF_WD + d_off:OFF_WD + d_off + DECAY_LORA]
    ad = p[:, OFF_AD + d_off:OFF_AD + d_off + ICL_LORA]

    w_log = -_softplus(-(w0_ref[...] + _mm(jnp.tanh(wd), w2_ref[...], HIGHEST))) - 0.5
    lw = -jnp.exp(w_log)
    a = _sigmoid(a0_ref[...] + _mm(ad, a2_ref[...], HIGHEST))
    kk_raw = k * kk_ref[...]
    kd = k * (1.0 + (a - 1.0) * ka_ref[...])

    ri = lax.broadcasted_iota(jnp.int32, (c2, PAIR), 0)
    li = lax.broadcasted_iota(jnp.int32, (c2, PAIR), 1)
    rt, lt = ri & (CHUNK - 1), li & (hd - 1)
    same_head = (ri >> 6) == (li >> 6)
    strict = (lt > rt) if rev else (lt < rt)
    mask_n = jnp.logical_and(same_head, strict)
    mask_k = jnp.logical_and(jnp.logical_not(same_head), strict)
    ident = ri == li
    eye = jnp.where(ident, 1.0, 0.0).astype(F32)
    rc = lax.broadcasted_iota(jnp.int32, (CHUNK, PAIR), 0)
    lc = lax.broadcasted_iota(jnp.int32, (CHUNK, PAIR), 1)
    incl_c = ((lc & (hd - 1)) >= rc) if rev else ((lc & (hd - 1)) <= rc)
    lane_e = lc < hd
    hsum = _head_sum_matrix()

    ci = lax.broadcasted_iota(jnp.int32, (CHUNK, CHUNK), 0)
    cj = lax.broadcasted_iota(jnp.int32, (CHUNK, CHUNK), 1)
    tri = jnp.where((cj >= ci) if rev else (cj <= ci), 1.0, 0.0).astype(F32)
    cum = _mm(tri, lw, HIGHEST)
    total = cum[0:1] if rev else cum[CHUNK - 1:CHUNK]
    e_prev = jnp.exp(cum - lw)
    e_neg = jnp.exp(-cum)
    e_pos = jnp.exp(cum)
    e_rest = jnp.exp(total - cum)
    p_end = jnp.exp(total)

    def split(z):
        ze = jnp.where(lane_e, z, 0.0)
        return ze, z - ze

    for pr in range(N_PAIRS):
        sl = slice(pr * PAIR, (pr + 1) * PAIR)
        kkr = kk_raw[:, sl]
        nrm = jnp.sqrt(_mm(kkr * kkr, hsum, HIGHEST))
        kk = kkr / jnp.maximum(nrm, 1e-12)
        bb = kk * a[:, sl]
        at = kk * e_prev[:, sl]
        bt = bb * e_neg[:, sl]
        kt = kd[:, sl] * e_neg[:, sl]
        rt_ = r[:, sl] * e_pos[:, sl]
        bh = bb * e_rest[:, sl]
        kh = kd[:, sl] * e_rest[:, sl]
        vp = v[:, sl]

        at_e, at_o = split(at)
        r_e, r_o = split(rt_)
        v_e, v_o = split(vp)
        bh_e, bh_o = split(bh)
        kh_e, kh_o = split(kh)

        g_e = _mm_nt(jnp.concatenate([at_e, r_e], 0), jnp.concatenate([bt, kt], 0), prec)
        g_o = _mm_nt(jnp.concatenate([at_o, r_o], 0), jnp.concatenate([kt, bt], 0), prec)
        g_top = jnp.concatenate([g_e[0:CHUNK], g_o[0:CHUNK]], 0)
        g_bot_e, g_bot_o = g_e[CHUNK:c2], g_o[CHUNK:c2]

        nbd = jnp.where(mask_n, g_top, 0.0)
        aak = jnp.where(mask_k, g_top, 0.0)
        v_swap = jnp.concatenate([v_o, v_e], 0)
        x = _mm(aak, v_swap, prec)

        n2 = _mm(nbd, nbd, prec)
        n4 = _mm(n2, n2, prec)
        n8 = _mm(n4, n4, prec)
        n16 = _mm(n8, n8, prec)
        n32 = _mm(n16, n16, prec)
        imn = eye - nbd
        p1 = imn + _mm(imn, n2, prec)
        p2 = eye + n4 + n8 + _mm(n4, n8, prec)
        p3 = eye + n16 + n32 + _mm(n16, n32, prec)
        tinv = _mm(_mm(p1, p2, prec), p3, prec)

        rhs = jnp.concatenate([jnp.concatenate([at_e, at_o], 0), x], 1)
        wu = -_mm(tinv, rhs, prec)

        rab = jnp.where(incl_c, jnp.where(lane_e, g_bot_e, g_bot_o), 0.0)
        rak = jnp.where(incl_c, jnp.where(lane_e, g_bot_o, g_bot_e), 0.0)
        qy = _mm(rab, wu, prec)
        q = rt_ + qy[:, 0:PAIR]
        y0 = qy[:, PAIR:2 * PAIR] + _mm(rak, v_swap, prec)

        mn = _mm_tn(jnp.concatenate([bh_e, bh_o], 0), wu, prec)
        m = jnp.where(ident, p_end[:, sl], 0.0) + mn[:, 0:PAIR]
        n_ = mn[:, PAIR:2 * PAIR] + _mm_tn(jnp.concatenate([kh_e, kh_o], 0), jnp.concatenate([v_e, v_o], 0), prec)

        qm = _mm(jnp.concatenate([q, m], 0), st_scr[pr], prec)
        y_ref[0, :, sl] = qm[0:CHUNK] + y0
        st_scr[pr] = qm[CHUNK:CHUNK + PAIR] + n_


def _rwkv_scan(pmix, w2, w0, a2, a0, k_k, k_a, rev, n_ctx_chunks, prec):
    b, tt, w = pmix.shape
    n_chunks = tt // CHUNK
    n_lat = n_chunks - n_ctx_chunks

    def chunk_of(i):
        if not rev:
            return i
        return jnp.where(i < n_ctx_chunks, n_ctx_chunks - 1 - i, n_chunks + n_ctx_chunks - 1 - i)

    def out_of(i):
        if not rev:
            return jnp.maximum(i - n_ctx_chunks, 0)
        return jnp.where(i < n_ctx_chunks, n_lat - 1, n_chunks - 1 - i)

    vec = lambda bi, i: (0, 0)
    return pl.pallas_call(
        functools.partial(_rwkv_kernel, rev, prec),
        grid=(b, n_chunks),
        in_specs=[pl.BlockSpec((1, CHUNK, w), lambda bi, i: (bi, chunk_of(i), 0)),
                  pl.BlockSpec((DECAY_LORA, RWKV_WIDTH), vec),
                  pl.BlockSpec((1, RWKV_WIDTH), vec),
                  pl.BlockSpec((ICL_LORA, RWKV_WIDTH), vec),
                  pl.BlockSpec((1, RWKV_WIDTH), vec),
                  pl.BlockSpec((1, RWKV_WIDTH), vec),
                  pl.BlockSpec((1, RWKV_WIDTH), vec)],
        out_specs=pl.BlockSpec((1, CHUNK, RWKV_WIDTH), lambda bi, i: (bi, out_of(i), 0)),
        out_shape=jax.ShapeDtypeStruct((b, n_lat * CHUNK, RWKV_WIDTH), F32),
        scratch_shapes=[pltpu.VMEM((N_PAIRS, PAIR, PAIR), F32)],
        compiler_params=_cparams(("arbitrary", "arbitrary")),
        name="rwkv7_bwd" if rev else "rwkv7_fwd",
    )(pmix, w2, w0, a2, a0, k_k, k_a)


def _gla_kernel(rev, prec, n_ctx_chunks, q_ref, k_ref, v0_ref, v1_ref, ad_ref, ctx_ref, aup_ref, ab_ref, o_ref, st_scr):
    i = pl.program_id(1)

    @pl.when(i == 0)
    def _():
        st_scr[...] = jnp.zeros_like(st_scr)

    is_ctx = i < n_ctx_chunks
    pc = ctx_ref[0]
    q = jnp.where(is_ctx, pc[:, 0:GLA_BLK], q_ref[0])
    k = jnp.where(is_ctx, pc[:, GLA_BLK:2 * GLA_BLK], k_ref[0])
    v = jnp.concatenate([jnp.where(is_ctx, pc[:, 2 * GLA_BLK:3 * GLA_BLK], v0_ref[0]),
                         jnp.where(is_ctx, pc[:, 3 * GLA_BLK:4 * GLA_BLK], v1_ref[0])], 1)
    d_off = GLA_GATE_LORA if rev else 0
    ad = jnp.where(is_ctx, pc[:, OFF_GLA_AD:OFF_GLA_AD + LANES], ad_ref[0][:, 0:LANES])
    ad = ad[:, d_off:d_off + GLA_GATE_LORA]

    la = -_softplus(-(_mm(ad, aup_ref[...], HIGHEST) + ab_ref[...])) * (1.0 / GLA_TAU)
    ci = lax.broadcasted_iota(jnp.int32, (CHUNK, CHUNK), 0)
    cj = lax.broadcasted_iota(jnp.int32, (CHUNK, CHUNK), 1)
    tri = jnp.where((cj >= ci) if rev else (cj <= ci), 1.0, 0.0).astype(F32)
    cum = _mm(tri, la, HIGHEST)
    total = cum[0:1] if rev else cum[CHUNK - 1:CHUNK]

    dk, dv, sb = GLA_KEY_DIM, GLA_VAL_DIM, GLA_SUB
    n_sb = CHUNK // sb
    row = lax.broadcasted_iota(jnp.int32, (CHUNK, dk), 0)
    row_in = row & (sb - 1)
    arow = lax.broadcasted_iota(jnp.int32, (CHUNK, CHUNK), 0)
    acol = lax.broadcasted_iota(jnp.int32, (CHUNK, CHUNK), 1)
    scale = GLA_KEY_DIM ** -0.5

    for h in range(GLA_HEADS):
        ks_ = slice(h * dk, (h + 1) * dk)
        qh = q[:, ks_] * scale
        kh = k[:, ks_]
        bh = cum[:, ks_]
        lah = la[:, ks_]
        toth = total[:, ks_]
        vh = v[:, h * dv:(h + 1) * dv]
        st = st_scr[h]

        o = _mm_nt(qh * jnp.exp(bh), st, prec)

        off_rows = []
        for blk in range(n_sb):
            rs = slice(blk * sb, (blk + 1) * sb)
            first = blk * sb + (sb - 1 if rev else 0)
            is_first_blk = (blk == n_sb - 1) if rev else (blk == 0)
            if is_first_blk:
                off_rows.append(jnp.zeros((sb, CHUNK), F32))
                continue
            beta = bh[first:first + 1] - lah[first:first + 1]
            qs = qh[rs] * jnp.exp(bh[rs] - beta)
            before = (row >= (blk + 1) * sb) if rev else (row < blk * sb)
            ksc = jnp.where(before, kh * jnp.exp(jnp.minimum(beta - bh, 0.0)), 0.0)
            off_rows.append(_mm_nt(qs, ksc, prec))
        att = jnp.concatenate(off_rows, 0)

        for s in range(sb):
            pick = lambda z: jnp.concatenate(
                [jnp.broadcast_to(z[blk * sb + s:blk * sb + s + 1], (sb, dk)) for blk in range(n_sb)], 0)
            ok = (row_in <= s) if rev else (row_in >= s)
            e = jnp.exp(jnp.where(ok, bh - pick(bh), 0.0))
            col = jnp.sum(jnp.where(ok, qh * pick(kh) * e, 0.0), axis=-1, keepdims=True)
            tgt = jnp.logical_and(acol == (arow & ~(sb - 1)) + s,
                                  ((arow & (sb - 1)) <= s) if rev else ((arow & (sb - 1)) >= s))
            att = jnp.where(tgt, col, att)

        o_ref[0, :, h * dv:(h + 1) * dv] = o + _mm(att, vh, prec)
        st_scr[h] = st * jnp.exp(toth) + _mm_tn(vh, kh * jnp.exp(toth - bh), prec)


def _gla_scan(p_lat, p_ctx, alpha_up, alpha_b, rev, prec):
    b, t, w = p_lat.shape
    assert w == GLA_PAD and t == GRID_W * CHUNK
    n_cols = GRID_W
    n_ctx_chunks = p_ctx.shape[1] // CHUNK
    n_steps = n_ctx_chunks + n_cols
    nblk = w // GLA_BLK
    lat_view = p_lat.reshape(b, t // GRID_W, GRID_W * w)

    def col_of(i):
        if not rev:
            return jnp.maximum(i - n_ctx_chunks, 0)
        return jnp.where(i < n_ctx_chunks, n_cols - 1, n_steps - 1 - i)

    def ctx_of(i):
        if not rev:
            return jnp.minimum(i, n_ctx_chunks - 1)
        return jnp.maximum(n_ctx_chunks - 1 - i, 0)

    def lat_spec(m):
        return pl.BlockSpec((1, CHUNK, GLA_BLK), lambda bi, i: (bi, 0, col_of(i) * nblk + m))

    vec = lambda bi, i: (0, 0)
    out = pl.pallas_call(
        functools.partial(_gla_kernel, rev, prec, n_ctx_chunks),
        grid=(b, n_steps),
        in_specs=[lat_spec(0), lat_spec(1), lat_spec(2), lat_spec(3), lat_spec(6),
                  pl.BlockSpec((1, CHUNK, w), lambda bi, i: (bi, ctx_of(i), 0)),
                  pl.BlockSpec((GLA_GATE_LORA, GLA_QK_WIDTH), vec),
                  pl.BlockSpec((1, GLA_QK_WIDTH), vec)],
        out_specs=pl.BlockSpec((1, CHUNK, GLA_V_WIDTH), lambda bi, i: (bi, 0, col_of(i))),
        out_shape=jax.ShapeDtypeStruct((b, t // GRID_W, GRID_W * GLA_V_WIDTH), F32),
        scratch_shapes=[pltpu.VMEM((GLA_HEADS, GLA_VAL_DIM, GLA_KEY_DIM), F32)],
        compiler_params=_cparams(("arbitrary", "arbitrary")),
        name="gla_bwd" if rev else "gla_fwd",
    )(lat_view, lat_view, lat_view, lat_view, lat_view, p_ctx, alpha_up, alpha_b)
    return out.reshape(b, t, GLA_V_WIDTH)


def _rwkv_post_kernel(yf_ref, yb_ref, pm_ref, a2_ref, a0_ref, g2_ref, ka_ref, rk_ref, lng_ref, lnb_ref, o_ref):
    p = pm_ref[0]
    r = p[:, 0:RWKV_WIDTH]
    k = p[:, RWKV_WIDTH:2 * RWKV_WIDTH]
    v = p[:, 2 * RWKV_WIDTH:3 * RWKV_WIDTH]
    ad_f = p[:, OFF_AD:OFF_AD + ICL_LORA]
    ad_b = p[:, OFF_AD + ICL_LORA:OFF_AD + 2 * ICL_LORA]
    gd = p[:, OFF_GD:OFF_GD + GATE_LORA]
    ka = ka_ref[...]
    a_f = _sigmoid(a0_ref[0] + _mm(ad_f, a2_ref[0], HIGHEST))
    a_b = _sigmoid(a0_ref[1] + _mm(ad_b, a2_ref[1], HIGHEST))
    kd_sum = k * (1.0 + (a_f - 1.0) * ka) + k * (1.0 + (a_b - 1.0) * ka)
    gate = _mm(_sigmoid(gd), g2_ref[...], HIGHEST)
    rkk = r * kd_sum * rk_ref[...]
    ysum = yf_ref[0] + yb_ref[0]
    hsum = _head_sum_matrix()
    inv_n = 1.0 / RWKV_HEAD_DIM
    for pr in range(N_PAIRS):
        sl = slice(pr * PAIR, (pr + 1) * PAIR)
        ys = ysum[:, sl]
        mean = _mm(ys, hsum, HIGHEST) * inv_n
        dlt = ys - mean
        var = _mm(dlt * dlt, hsum, HIGHEST) * inv_n
        gn = dlt * lax.rsqrt(var + RWKV_GN_EPS) * lng_ref[:, sl] + lnb_ref[:, sl]
        bonus = _mm(rkk[:, sl], hsum, HIGHEST) * v[:, sl]
        o_ref[0, :, sl] = ((gn + bonus) * gate[:, sl]).astype(o_ref.dtype)


def _rwkv_post(y_f, y_b, pmix, n_ctx, a2, a0, g2, k_a, r_k, ln_g, ln_b, tm=256):
    b, t, w = y_f.shape
    off = n_ctx // tm
    row = lambda bi, i: (bi, i, 0)
    vec = lambda bi, i: (0, 0)
    vec3 = lambda bi, i: (0, 0, 0)
    return pl.pallas_call(
        _rwkv_post_kernel,
        grid=(b, t // tm),
        in_specs=[pl.BlockSpec((1, tm, w), row),
                  pl.BlockSpec((1, tm, w), row),
                  pl.BlockSpec((1, tm, pmix.shape[2]), lambda bi, i: (bi, i + off, 0)),
                  pl.BlockSpec((2, ICL_LORA, w), vec3),
                  pl.BlockSpec((2, 1, w), vec3),
                  pl.BlockSpec((GATE_LORA, w), vec),
                  pl.BlockSpec((1, w), vec), pl.BlockSpec((1, w), vec),
                  pl.BlockSpec((1, w), vec), pl.BlockSpec((1, w), vec)],
        out_specs=pl.BlockSpec((1, tm, w), row),
        out_shape=jax.ShapeDtypeStruct((b, t, w), BF16),
        compiler_params=_cparams(("arbitrary", "arbitrary")),
        name="rwkv_post",
    )(y_f, y_b, pmix, a2, a0, g2, k_a, r_k, ln_g, ln_b)


def _gla_post_kernel(of_ref, ob_ref, g_ref, ng_ref, o_ref):
    o = of_ref[0] + ob_ref[0]
    g = g_ref[0]
    dv = GLA_VAL_DIM
    for h in range(GLA_HEADS):
        sl = slice(h * dv, (h + 1) * dv)
        oh = o[:, sl]
        oh = oh * lax.rsqrt(jnp.mean(oh * oh, axis=-1, keepdims=True) + GLA_NORM_EPS) * ng_ref[:, sl]
        gh = g[:, sl]
        o_ref[0, :, sl] = (oh * (gh * _sigmoid(gh))).astype(o_ref.dtype)


def _gla_post(o_f, o_b, p_gla, norm_g, tm=256):
    b, t, w = o_f.shape
    row = lambda bi, i: (bi, i, 0)
    return pl.pallas_call(
        _gla_post_kernel,
        grid=(b, t // tm),
        in_specs=[pl.BlockSpec((1, tm, w), row),
                  pl.BlockSpec((1, tm, w), row),
                  pl.BlockSpec((1, tm, w), lambda bi, i: (bi, i, 2)),
                  pl.BlockSpec((1, w), lambda bi, i: (0, 0))],
        out_specs=pl.BlockSpec((1, tm, w), row),
        out_shape=jax.ShapeDtypeStruct((b, t, w), BF16),
        compiler_params=_cparams(("arbitrary", "arbitrary")),
        name="gla_post",
    )(o_f, o_b, p_gla, norm_g)


def _merge_kernel(ya_ref, yb_ref, wr_ref, wg_ref, ga_ref, gb_ref, o_ref):
    ma = _mm(ya_ref[0], wr_ref[...])
    mb = _mm(yb_ref[0], wg_ref[...])
    o_ref[0] = (_sigmoid(ga_ref[0]) * ma + _sigmoid(gb_ref[0]) * mb).astype(o_ref.dtype)


def _merge(ya, yb, w_r, w_g, p_gate, tm=512, tn=512):
    b, t, w = ya.shape
    d = w_r.shape[1]
    nj = d // tn
    return pl.pallas_call(
        _merge_kernel,
        grid=(b, t // tm, nj),
        in_specs=[pl.BlockSpec((1, tm, w), lambda bi, i, j: (bi, i, 0)),
                  pl.BlockSpec((1, tm, w), lambda bi, i, j: (bi, i, 0)),
                  pl.BlockSpec((w, tn), lambda bi, i, j: (0, j)),
                  pl.BlockSpec((w, tn), lambda bi, i, j: (0, j)),
                  pl.BlockSpec((1, tm, tn), lambda bi, i, j: (bi, i, j)),
                  pl.BlockSpec((1, tm, tn), lambda bi, i, j: (bi, i, j + nj))],
        out_specs=pl.BlockSpec((1, tm, tn), lambda bi, i, j: (bi, i, j)),
        out_shape=jax.ShapeDtypeStruct((b, t, d), BF16),
        compiler_params=_cparams(("arbitrary", "arbitrary", "arbitrary")),
        name="merge_branches",
    )(ya, yb, w_r, w_g, p_gate, p_gate)


def _mix_out_kernel(m_ref, w_ref, x_ref, gate_ref, npost_ref, npre_ref, sh_ref, sc_ref, x1_ref, h_ref):
    z = _mm(m_ref[0], w_ref[...])
    z = z * lax.rsqrt(jnp.mean(z * z, axis=-1, keepdims=True) + NORM_EPS) * npost_ref[...]
    x1 = x_ref[0] + gate_ref[0] * z
    x1_ref[0] = x1
    y = x1 * lax.rsqrt(jnp.mean(x1 * x1, axis=-1, keepdims=True) + NORM_EPS) * npre_ref[...]
    h_ref[0] = (y * (1.0 + sc_ref[0]) + sh_ref[0]).astype(h_ref.dtype)


def _mix_out(m, w_out, x, gate, n_post, n_pre, shift, scale, tm=256):
    b, t, d = x.shape
    row = lambda bi, i: (bi, i, 0)
    per_b = lambda bi, i: (bi, 0, 0)
    vec = lambda bi, i: (0, 0)
    return pl.pallas_call(
        _mix_out_kernel,
        grid=(b, t // tm),
        in_specs=[pl.BlockSpec((1, tm, d), row),
                  pl.BlockSpec((d, d), vec),
                  pl.BlockSpec((1, tm, d), row),
                  pl.BlockSpec((1, 1, d), per_b),
                  pl.BlockSpec((1, d), vec), pl.BlockSpec((1, d), vec),
                  pl.BlockSpec((1, 1, d), per_b), pl.BlockSpec((1, 1, d), per_b)],
        out_specs=[pl.BlockSpec((1, tm, d), row), pl.BlockSpec((1, tm, d), row)],
        out_shape=[jax.ShapeDtypeStruct((b, t, d), F32), jax.ShapeDtypeStruct((b, t, d), BF16)],
        compiler_params=_cparams(("arbitrary", "arbitrary")),
        name="mix_out",
    )(m, w_out, x, gate, n_post, n_pre, shift, scale)


def _ffn_up_kernel(h_ref, wg_ref, wu_ref, o_ref):
    h = h_ref[0]
    a = _mm(h, wg_ref[...])
    u = _mm(h, wu_ref[...])
    o_ref[0] = (a * _sigmoid(a) * u).astype(o_ref.dtype)


def _ffn_up(h, w_gate, w_up, tm=512, tn=512):
    b, t, d = h.shape
    f = w_gate.shape[1]
    return pl.pallas_call(
        _ffn_up_kernel,
        grid=(b, t // tm, f // tn),
        in_specs=[pl.BlockSpec((1, tm, d), lambda bi, i, j: (bi, i, 0)),
                  pl.BlockSpec((d, tn), lambda bi, i, j: (0, j)),
                  pl.BlockSpec((d, tn), lambda bi, i, j: (0, j))],
        out_specs=pl.BlockSpec((1, tm, tn), lambda bi, i, j: (bi, i, j)),
        out_shape=jax.ShapeDtypeStruct((b, t, f), BF16),
        compiler_params=_cparams(("arbitrary", "arbitrary", "arbitrary")),
        name="ffn_up",
    )(h, w_gate, w_up)


def _ffn_down_kernel(h_ref, w_ref, x_ref, gate_ref, npost_ref, o_ref, acc):
    kk = pl.program_id(2)

    @pl.when(kk == 0)
    def _():
        acc[...] = jnp.zeros_like(acc)

    acc[...] += _mm(h_ref[0], w_ref[...])

    @pl.when(kk == pl.num_programs(2) - 1)
    def _():
        z = acc[...]
        z = z * lax.rsqrt(jnp.mean(z * z, axis=-1, keepdims=True) + NORM_EPS) * npost_ref[...]
        o_ref[0] = x_ref[0] + gate_ref[0] * z


def _ffn_down(h, w_down, x1, gate, n_post, tm=512, tk=512):
    b, t, f = h.shape
    d = w_down.shape[1]
    return pl.pallas_call(
        _ffn_down_kernel,
        grid=(b, t // tm, f // tk),
        in_specs=[pl.BlockSpec((1, tm, tk), lambda bi, i, k: (bi, i, k)),
                  pl.BlockSpec((tk, d), lambda bi, i, k: (k, 0)),
                  pl.BlockSpec((1, tm, d), lambda bi, i, k: (bi, i, 0)),
                  pl.BlockSpec((1, 1, d), lambda bi, i, k: (bi, 0, 0)),
                  pl.BlockSpec((1, d), lambda bi, i, k: (0, 0))],
        out_specs=pl.BlockSpec((1, tm, d), lambda bi, i, k: (bi, i, 0)),
        out_shape=jax.ShapeDtypeStruct((b, t, d), F32),
        scratch_shapes=[pltpu.VMEM((tm, d), F32)],
        compiler_params=_cparams(("arbitrary", "arbitrary", "arbitrary")),
        name="ffn_down",
    )(h, w_down, x1, gate, n_post)


def _pad_cols(w, n):
    return jnp.pad(w, ((0, 0), (0, n - w.shape[1])))


def kernel(x, c, ctx, c_ctx, ada_w, ada_b, norm_pre_mix, norm_post_mix, norm_pre_ffn, norm_post_ffn, w_in, shift_mu, rwkv_w0, rwkv_w2, rwkv_a0, rwkv_a2, rwkv_g2, rwkv_k_k, rwkv_k_a, rwkv_r_k, rwkv_ln_g, rwkv_ln_b, w_rwkv_up, gla_alpha_up, gla_alpha_b, gla_norm_g, w_gla_up, w_out, ffn_w_gate, ffn_w_up, ffn_w_down):
    assert ada_w.shape[0] == 1, "single trunk layer"
    bsz, seq, d = x.shape
    n_ctx = ctx.shape[1]
    prec = HIGHEST

    cvecs = jnp.concatenate([c, c_ctx[None, :], jnp.zeros((8 - bsz - 1, d), F32)], 0)
    mod = _modulation(cvecs, ada_w[0], ada_b[0])
    mod_x = mod[:bsz].reshape(bsz, 6, 1, d)
    shx1, scx1, gx1, shx2, scx2, gx2 = (mod_x[:, i] for i in range(6))
    mod_c = jnp.broadcast_to(mod[bsz].reshape(1, 6, 1, d), (bsz, 6, 1, d))
    shc1, scc1 = mod_c[:, 0], mod_c[:, 1]

    w_all = w_in[0]
    mix_in = RWKV_IN + GLA_IN
    w_rwkv = _pad_cols(w_all[:, :RWKV_IN], RWKV_PAD).astype(BF16)
    w_gla = _pad_cols(w_all[:, RWKV_IN:mix_in], GLA_PAD).astype(BF16)
    w_gate = w_all[:, mix_in:].astype(BF16)
    mu = _pad_cols(shift_mu, RWKV_PAD)
    n_pre = norm_pre_mix

    px_rwkv = _project(x, shx1, scx1, n_pre, w_rwkv, 512, 512, "proj_rwkv")
    px_gla = _project(x, shx1, scx1, n_pre, w_gla, 512, 512, "proj_gla")
    px_gate = _project(x, shx1, scx1, n_pre, w_gate, 512, 512, "proj_gate")
    pc_rwkv = _project(ctx, shc1, scc1, n_pre, w_rwkv, n_ctx, 512, "proj_rwkv_ctx")
    pc_gla = _project(ctx, shc1, scc1, n_pre, w_gla, n_ctx, 512, "proj_gla_ctx")

    pmix = _shiftmix(px_rwkv, pc_rwkv, mu)
    n_ctx_chunks = n_ctx // CHUNK
    y_dir = []
    for dr in range(2):
        y_dir.append(_rwkv_scan(pmix, rwkv_w2[0, dr], rwkv_w0[0, dr][None, :], rwkv_a2[0, dr], rwkv_a0[0, dr][None, :],
                                rwkv_k_k, rwkv_k_a, dr == 1, n_ctx_chunks, prec))
    ya = _rwkv_post(y_dir[0], y_dir[1], pmix, n_ctx, rwkv_a2[0], rwkv_a0[0][:, None, :], rwkv_g2[0], rwkv_k_a,
                    rwkv_r_k.reshape(1, RWKV_WIDTH), rwkv_ln_g, rwkv_ln_b)

    o_dir = [_gla_scan(px_gla, pc_gla, gla_alpha_up[0, dr], gla_alpha_b[0, dr][None, :], dr == 1, prec) for dr in range(2)]
    yb = _gla_post(o_dir[0], o_dir[1], px_gla, gla_norm_g)

    m = _merge(ya, yb, w_rwkv_up[0].astype(BF16), w_gla_up[0].astype(BF16), px_gate)
    x1, h2 = _mix_out(m, w_out[0].astype(BF16), x, gx1, norm_post_mix, norm_pre_ffn, shx2, scx2)
    hf = _ffn_up(h2, ffn_w_gate[0].astype(BF16), ffn_w_up[0].astype(BF16))
    return _ffn_down(hf, ffn_w_down[0].astype(BF16), x1, gx2, norm_post_ffn)
```

```python
import functools

import jax
import jax.numpy as jnp
from jax import lax
from jax.experimental import pallas as pl
from jax.experimental.pallas import tpu as pltpu

F32 = jnp.float32
BF16 = jnp.bfloat16
HIGHEST = lax.Precision.HIGHEST

LANES = 128
VMEM_LIMIT_BYTES = 56 * 1024 * 1024

GRID_W = 64
CHUNK = 64
RWKV_HEADS, RWKV_HEAD_DIM = 16, 64
RWKV_WIDTH = RWKV_HEADS * RWKV_HEAD_DIM
DECAY_LORA = ICL_LORA = 96
GATE_LORA = 64
RWKV_GN_EPS = 64e-5
GLA_HEADS, GLA_KEY_DIM, GLA_VAL_DIM = 4, 128, 256
GLA_QK_WIDTH = GLA_HEADS * GLA_KEY_DIM
GLA_V_WIDTH = GLA_HEADS * GLA_VAL_DIM
GLA_GATE_LORA = 16
GLA_TAU = 16.0
GLA_NORM_EPS = 1e-5
GLA_SUB = 16
NORM_EPS = 1e-6

RWKV_IN = 3 * RWKV_WIDTH + 2 * DECAY_LORA + 2 * ICL_LORA + GATE_LORA
RWKV_PAD = 3584
OFF_WD = 3 * RWKV_WIDTH
OFF_AD = OFF_WD + 2 * DECAY_LORA
OFF_GD = OFF_AD + 2 * ICL_LORA
GLA_IN = 2 * GLA_QK_WIDTH + 2 * GLA_V_WIDTH + 2 * GLA_GATE_LORA
GLA_BLK = 512
GLA_PAD = 7 * GLA_BLK
OFF_GLA_AD = 6 * GLA_BLK

PAIR = 2 * RWKV_HEAD_DIM
N_PAIRS = RWKV_HEADS // 2


def _cparams(semantics):
    return pltpu.CompilerParams(dimension_semantics=semantics, vmem_limit_bytes=VMEM_LIMIT_BYTES)


def _sigmoid(z):
    return 1.0 / (1.0 + jnp.exp(-z))


def _softplus(z):
    return jnp.maximum(z, 0.0) + jnp.log(1.0 + jnp.exp(-jnp.abs(z)))


def _dot(a, b, dims, precision):
    if precision is BF16:
        a, b, precision = a.astype(BF16), b.astype(BF16), None
    return lax.dot_general(a, b, (dims, ((), ())), precision=precision, preferred_element_type=F32)


def _mm(a, b, precision=None):
    return _dot(a, b, ((1,), (0,)), precision)


def _mm_nt(a, b, precision=None):
    return _dot(a, b, ((1,), (1,)), precision)


def _mm_tn(a, b, precision=None):
    return _dot(a, b, ((0,), (0,)), precision)


def _mod_kernel(c_ref, w_ref, b_ref, o_ref):
    s = c_ref[...]
    s = s * _sigmoid(s)
    o_ref[...] = _mm(s, w_ref[...], HIGHEST) + b_ref[...]


def _modulation(cvecs, ada_w, ada_b, tn=1024):
    m, d = cvecs.shape
    n = ada_w.shape[1]
    return pl.pallas_call(
        _mod_kernel,
        grid=(n // tn,),
        in_specs=[pl.BlockSpec((m, d), lambda j: (0, 0)),
                  pl.BlockSpec((d, tn), lambda j: (0, j)),
                  pl.BlockSpec((1, tn), lambda j: (0, j))],
        out_specs=pl.BlockSpec((m, tn), lambda j: (0, j)),
        out_shape=jax.ShapeDtypeStruct((m, n), F32),
        compiler_params=_cparams(("arbitrary",)),
        name="adaln_mod",
    )(cvecs, ada_w, ada_b.reshape(1, n))


def _proj_kernel(x_ref, sh_ref, sc_ref, g_ref, w_ref, o_ref, h_scr):
    @pl.when(pl.program_id(2) == 0)
    def _():
        x = x_ref[0]
        ms = jnp.mean(x * x, axis=-1, keepdims=True)
        y = x * lax.rsqrt(ms + NORM_EPS) * g_ref[...]
        h_scr[...] = (y * (1.0 + sc_ref[0]) + sh_ref[0]).astype(BF16)

    o_ref[0] = _mm(h_scr[...], w_ref[...])


def _project(x, shift, scale, gain, w, tm, tn, name):
    b, t, d = x.shape
    n = w.shape[1]
    return pl.pallas_call(
        _proj_kernel,
        grid=(b, t // tm, n // tn),
        in_specs=[pl.BlockSpec((1, tm, d), lambda bi, i, j: (bi, i, 0)),
                  pl.BlockSpec((1, 1, d), lambda bi, i, j: (bi, 0, 0)),
                  pl.BlockSpec((1, 1, d), lambda bi, i, j: (bi, 0, 0)),
                  pl.BlockSpec((1, d), lambda bi, i, j: (0, 0)),
                  pl.BlockSpec((d, tn), lambda bi, i, j: (0, j))],
        out_specs=pl.BlockSpec((1, tm, tn), lambda bi, i, j: (bi, i, j)),
        out_shape=jax.ShapeDtypeStruct((b, t, n), F32),
        scratch_shapes=[pltpu.VMEM((tm, d), BF16)],
        compiler_params=_cparams(("arbitrary", "arbitrary", "arbitrary")),
        name=name,
    )(x, shift, scale, gain, w)


def _shiftmix_kernel(n_ctx_blk, n_blk, tb, lat_ref, ctx_ref, lprev_ref, lnext_ref, cprev_ref, cnext_ref, mu_ref, o_ref):
    i = pl.program_id(1)
    is_ctx = i < n_ctx_blk
    p = jnp.where(is_ctx, ctx_ref[0], lat_ref[0])
    first = jnp.logical_or(i == 0, i == n_ctx_blk)
    last = jnp.logical_or(i == n_ctx_blk - 1, i == n_blk - 1)
    hp = jnp.where(is_ctx, cprev_ref[0, 7:8, :], lprev_ref[0, 7:8, :])
    hn = jnp.where(is_ctx, cnext_ref[0, 0:1, :], lnext_ref[0, 0:1, :])
    hp = jnp.where(first, 0.0, hp)
    hn = jnp.where(last, 0.0, hn)
    row = lax.broadcasted_iota(jnp.int32, p.shape, 0)
    prev = jnp.where(row == 0, hp, pltpu.roll(p, 1, 0))
    nxt = jnp.where(row == tb - 1, hn, pltpu.roll(p, tb - 1, 0))
    o_ref[0] = p + mu_ref[...] * (0.5 * (prev + nxt) - p)


def _shiftmix(p_lat, p_ctx, mu, tb=128):
    b, t, w = p_lat.shape
    tc = p_ctx.shape[1]
    n_ctx_blk, n_lat_blk = tc // tb, t // tb
    n_blk = n_ctx_blk + n_lat_blk
    r8 = tb // 8

    def lat_main(bi, i):
        return (bi, jnp.maximum(i - n_ctx_blk, 0), 0)

    def ctx_main(bi, i):
        return (bi, jnp.minimum(i, n_ctx_blk - 1), 0)

    def lat_prev(bi, i):
        return (bi, jnp.maximum((i - n_ctx_blk) * r8 - 1, 0), 0)

    def lat_next(bi, i):
        return (bi, jnp.clip((i - n_ctx_blk + 1) * r8, 0, t // 8 - 1), 0)

    def ctx_prev(bi, i):
        return (bi, jnp.clip(i * r8 - 1, 0, tc // 8 - 1), 0)

    def ctx_next(bi, i):
        return (bi, jnp.minimum((i + 1) * r8, tc // 8 - 1), 0)

    return pl.pallas_call(
        functools.partial(_shiftmix_kernel, n_ctx_blk, n_blk, tb),
        grid=(b, n_blk),
        in_specs=[pl.BlockSpec((1, tb, w), lat_main),
                  pl.BlockSpec((1, tb, w), ctx_main),
                  pl.BlockSpec((1, 8, w), lat_prev),
                  pl.BlockSpec((1, 8, w), lat_next),
                  pl.BlockSpec((1, 8, w), ctx_prev),
                  pl.BlockSpec((1, 8, w), ctx_next),
                  pl.BlockSpec((1, w), lambda bi, i: (0, 0))],
        out_specs=pl.BlockSpec((1, tb, w), lambda bi, i: (bi, i, 0)),
        out_shape=jax.ShapeDtypeStruct((b, tc + t, w), F32),
        compiler_params=_cparams(("arbitrary", "arbitrary")),
        name="rwkv_shiftmix",
    )(p_lat, p_ctx, p_lat, p_lat, p_ctx, p_ctx, mu)


def _head_sum_matrix():
    r = lax.broadcasted_iota(jnp.int32, (PAIR, PAIR), 0)
    c = lax.broadcasted_iota(jnp.int32, (PAIR, PAIR), 1)
    return jnp.where((r >> 6) == (c >> 6), 1.0, 0.0).astype(F32)


def _rwkv_kernel(rev, prec, pm_ref, w2_ref, w0_ref, a2_ref, a0_ref, kk_ref, ka_ref, y_ref, st_scr):
    hd = RWKV_HEAD_DIM
    c2 = 2 * CHUNK

    @pl.when(pl.program_id(1) == 0)
    def _():
        st_scr[...] = jnp.zeros_like(st_scr)

    p = pm_ref[0]
    r = p[:, 0:RWKV_WIDTH]
    k = p[:, RWKV_WIDTH:2 * RWKV_WIDTH]
    v = p[:, 2 * RWKV_WIDTH:3 * RWKV_WIDTH]
    d_off = DECAY_LORA if rev else 0
    wd = p[:, OFF_WD + d_off:OFF_WD + d_off + DECAY_LORA]
    ad = p[:, OFF_AD + d_off:OFF_AD + d_off + ICL_LORA]

    w_log = -_softplus(-(w0_ref[...] + _mm(jnp.tanh(wd), w2_ref[...], HIGHEST))) - 0.5
    lw = -jnp.exp(w_log)
    a = _sigmoid(a0_ref[...] + _mm(ad, a2_ref[...], HIGHEST))
    kk_raw = k * kk_ref[...]
    kd = k * (1.0 + (a - 1.0) * ka_ref[...])

    ri = lax.broadcasted_iota(jnp.int32, (c2, PAIR), 0)
    li = lax.broadcasted_iota(jnp.int32, (c2, PAIR), 1)
    rt, lt = ri & (CHUNK - 1), li & (hd - 1)
    same_head = (ri >> 6) == (li >> 6)
    strict = (lt > rt) if rev else (lt < rt)
    mask_n = jnp.logical_and(same_head, strict)
    mask_k = jnp.logical_and(jnp.logical_not(same_head), strict)
    ident = ri == li
    eye = jnp.where(ident, 1.0, 0.0).astype(F32)
    rc = lax.broadcasted_iota(jnp.int32, (CHUNK, PAIR), 0)
    lc = lax.broadcasted_iota(jnp.int32, (CHUNK, PAIR), 1)
    incl_c = ((lc & (hd - 1)) >= rc) if rev else ((lc & (hd - 1)) <= rc)
    lane_e = lc < hd
    hsum = _head_sum_matrix()

    ci = lax.broadcasted_iota(jnp.int32, (CHUNK, CHUNK), 0)
    cj = lax.broadcasted_iota(jnp.int32, (CHUNK, CHUNK), 1)
    tri = jnp.where((cj >= ci) if rev else (cj <= ci), 1.0, 0.0).astype(F32)
    cum = _mm(tri, lw, HIGHEST)
    total = cum[0:1] if rev else cum[CHUNK - 1:CHUNK]
    e_prev = jnp.exp(cum - lw)
    e_neg = jnp.exp(-cum)
    e_pos = jnp.exp(cum)
    e_rest = jnp.exp(total - cum)
    p_end = jnp.exp(total)

    def split(z):
        ze = jnp.where(lane_e, z, 0.0)
        return ze, z - ze

    for pr in range(N_PAIRS):
        sl = slice(pr * PAIR, (pr + 1) * PAIR)
        kkr = kk_raw[:, sl]
        nrm = jnp.sqrt(_mm(kkr * kkr, hsum, HIGHEST))
        kk = kkr / jnp.maximum(nrm, 1e-12)
        bb = kk * a[:, sl]
        at = kk * e_prev[:, sl]
        bt = bb * e_neg[:, sl]
        kt = kd[:, sl] * e_neg[:, sl]
        rt_ = r[:, sl] * e_pos[:, sl]
        bh = bb * e_rest[:, sl]
        kh = kd[:, sl] * e_rest[:, sl]
        vp = v[:, sl]

        at_e, at_o = split(at)
        r_e, r_o = split(rt_)
        v_e, v_o = split(vp)
        bh_e, bh_o = split(bh)
        kh_e, kh_o = split(kh)

        g_e = _mm_nt(jnp.concatenate([at_e, r_e], 0), jnp.concatenate([bt, kt], 0), prec)
        g_o = _mm_nt(jnp.concatenate([at_o, r_o], 0), jnp.concatenate([kt, bt], 0), prec)
        g_top = jnp.concatenate([g_e[0:CHUNK], g_o[0:CHUNK]], 0)
        g_bot_e, g_bot_o = g_e[CHUNK:c2], g_o[CHUNK:c2]

        nbd = jnp.where(mask_n, g_top, 0.0)
        aak = jnp.where(mask_k, g_top, 0.0)
        v_swap = jnp.concatenate([v_o, v_e], 0)
        x = _mm(aak, v_swap, prec)

        n2 = _mm(nbd, nbd, prec)
        n4 = _mm(n2, n2, prec)
        n8 = _mm(n4, n4, prec)
        n16 = _mm(n8, n8, prec)
        n32 = _mm(n16, n16, prec)
        imn = eye - nbd
        p1 = imn + _mm(imn, n2, prec)
        p2 = eye + n4 + n8 + _mm(n4, n8, prec)
        p3 = eye + n16 + n32 + _mm(n16, n32, prec)
        tinv = _mm(_mm(p1, p2, prec), p3, prec)

        rhs = jnp.concatenate([jnp.concatenate([at_e, at_o], 0), x], 1)
        wu = -_mm(tinv, rhs, prec)

        rab = jnp.where(incl_c, jnp.where(lane_e, g_bot_e, g_bot_o), 0.0)
        rak = jnp.where(incl_c, jnp.where(lane_e, g_bot_o, g_bot_e), 0.0)
        qy = _mm(rab, wu, prec)
        q = rt_ + qy[:, 0:PAIR]
        y0 = qy[:, PAIR:2 * PAIR] + _mm(rak, v_swap, prec)

        mn = _mm_tn(jnp.concatenate([bh_e, bh_o], 0), wu, prec)
        m = jnp.where(ident, p_end[:, sl], 0.0) + mn[:, 0:PAIR]
        n_ = mn[:, PAIR:2 * PAIR] + _mm_tn(jnp.concatenate([kh_e, kh_o], 0), jnp.concatenate([v_e, v_o], 0), prec)

        qm = _mm(jnp.concatenate([q, m], 0), st_scr[pr], prec)
        y_ref[0, :, sl] = qm[0:CHUNK] + y0
        st_scr[pr] = qm[CHUNK:CHUNK + PAIR] + n_


def _rwkv_scan(pmix, w2, w0, a2, a0, k_k, k_a, rev, n_ctx_chunks, prec):
    b, tt, w = pmix.shape
    n_chunks = tt // CHUNK
    n_lat = n_chunks - n_ctx_chunks

    def chunk_of(i):
        if not rev:
            return i
        return jnp.where(i < n_ctx_chunks, n_ctx_chunks - 1 - i, n_chunks + n_ctx_chunks - 1 - i)

    def out_of(i):
        if not rev:
            return jnp.maximum(i - n_ctx_chunks, 0)
        return jnp.where(i < n_ctx_chunks, n_lat - 1, n_chunks - 1 - i)

    vec = lambda bi, i: (0, 0)
    return pl.pallas_call(
        functools.partial(_rwkv_kernel, rev, prec),
        grid=(b, n_chunks),
        in_specs=[pl.BlockSpec((1, CHUNK, w), lambda bi, i: (bi, chunk_of(i), 0)),
                  pl.BlockSpec((DECAY_LORA, RWKV_WIDTH), vec),
                  pl.BlockSpec((1, RWKV_WIDTH), vec),
                  pl.BlockSpec((ICL_LORA, RWKV_WIDTH), vec),
                  pl.BlockSpec((1, RWKV_WIDTH), vec),
                  pl.BlockSpec((1, RWKV_WIDTH), vec),
                  pl.BlockSpec((1, RWKV_WIDTH), vec)],
        out_specs=pl.BlockSpec((1, CHUNK, RWKV_WIDTH), lambda bi, i: (bi, out_of(i), 0)),
        out_shape=jax.ShapeDtypeStruct((b, n_lat * CHUNK, RWKV_WIDTH), F32),
        scratch_shapes=[pltpu.VMEM((N_PAIRS, PAIR, PAIR), F32)],
        compiler_params=_cparams(("arbitrary", "arbitrary")),
        name="rwkv7_bwd" if rev else "rwkv7_fwd",
    )(pmix, w2, w0, a2, a0, k_k, k_a)


def _gla_kernel(rev, prec, n_ctx_chunks, q_ref, k_ref, v0_ref, v1_ref, ad_ref, ctx_ref, aup_ref, ab_ref, o_ref, st_scr):
    i = pl.program_id(1)

    @pl.when(i == 0)
    def _():
        st_scr[...] = jnp.zeros_like(st_scr)

    is_ctx = i < n_ctx_chunks
    pc = ctx_ref[0]
    q = jnp.where(is_ctx, pc[:, 0:GLA_BLK], q_ref[0])
    k = jnp.where(is_ctx, pc[:, GLA_BLK:2 * GLA_BLK], k_ref[0])
    v = jnp.concatenate([jnp.where(is_ctx, pc[:, 2 * GLA_BLK:3 * GLA_BLK], v0_ref[0]),
                         jnp.where(is_ctx, pc[:, 3 * GLA_BLK:4 * GLA_BLK], v1_ref[0])], 1)
    d_off = GLA_GATE_LORA if rev else 0
    ad = jnp.where(is_ctx, pc[:, OFF_GLA_AD:OFF_GLA_AD + LANES], ad_ref[0][:, 0:LANES])
    ad = ad[:, d_off:d_off + GLA_GATE_LORA]

    la = -_softplus(-(_mm(ad, aup_ref[...], HIGHEST) + ab_ref[...])) * (1.0 / GLA_TAU)
    ci = lax.broadcasted_iota(jnp.int32, (CHUNK, CHUNK), 0)
    cj = lax.broadcasted_iota(jnp.int32, (CHUNK, CHUNK), 1)
    tri = jnp.where((cj >= ci) if rev else (cj <= ci), 1.0, 0.0).astype(F32)
    cum = _mm(tri, la, HIGHEST)
    total = cum[0:1] if rev else cum[CHUNK - 1:CHUNK]

    dk, dv, sb = GLA_KEY_DIM, GLA_VAL_DIM, GLA_SUB
    n_sb = CHUNK // sb
    row = lax.broadcasted_iota(jnp.int32, (CHUNK, dk), 0)
    row_in = row & (sb - 1)
    arow = lax.broadcasted_iota(jnp.int32, (CHUNK, CHUNK), 0)
    acol = lax.broadcasted_iota(jnp.int32, (CHUNK, CHUNK), 1)
    scale = GLA_KEY_DIM ** -0.5

    for h in range(GLA_HEADS):
        ks_ = slice(h * dk, (h + 1) * dk)
        qh = q[:, ks_] * scale
        kh = k[:, ks_]
        bh = cum[:, ks_]
        lah = la[:, ks_]
        toth = total[:, ks_]
        vh = v[:, h * dv:(h + 1) * dv]
        st = st_scr[h]

        o = _mm_nt(qh * jnp.exp(bh), st, prec)

        off_rows = []
        for blk in range(n_sb):
            rs = slice(blk * sb, (blk + 1) * sb)
            first = blk * sb + (sb - 1 if rev else 0)
            is_first_blk = (blk == n_sb - 1) if rev else (blk == 0)
            if is_first_blk:
                off_rows.append(jnp.zeros((sb, CHUNK), F32))
                continue
            beta = bh[first:first + 1] - lah[first:first + 1]
            qs = qh[rs] * jnp.exp(bh[rs] - beta)
            before = (row >= (blk + 1) * sb) if rev else (row < blk * sb)
            ksc = jnp.where(before, kh * jnp.exp(jnp.minimum(beta - bh, 0.0)), 0.0)
            off_rows.append(_mm_nt(qs, ksc, prec))
        att = jnp.concatenate(off_rows, 0)

        for s in range(sb):
            pick = lambda z: jnp.concatenate(
                [jnp.broadcast_to(z[blk * sb + s:blk * sb + s + 1], (sb, dk)) for blk in range(n_sb)], 0)
            ok = (row_in <= s) if rev else (row_in >= s)
            e = jnp.exp(jnp.where(ok, bh - pick(bh), 0.0))
            col = jnp.sum(jnp.where(ok, qh * pick(kh) * e, 0.0), axis=-1, keepdims=True)
            tgt = jnp.logical_and(acol == (arow & ~(sb - 1)) + s,
                                  ((arow & (sb - 1)) <= s) if rev else ((arow & (sb - 1)) >= s))
            att = jnp.where(tgt, col, att)

        o_ref[0, :, h * dv:(h + 1) * dv] = o + _mm(att, vh, prec)
        st_scr[h] = st * jnp.exp(toth) + _mm_tn(vh, kh * jnp.exp(toth - bh), prec)


def _gla_scan(p_lat, p_ctx, alpha_up, alpha_b, rev, prec):
    b, t, w = p_lat.shape
    assert w == GLA_PAD and t == GRID_W * CHUNK
    n_cols = GRID_W
    n_ctx_chunks = p_ctx.shape[1] // CHUNK
    n_steps = n_ctx_chunks + n_cols
    nblk = w // GLA_BLK
    lat_view = p_lat.reshape(b, t // GRID_W, GRID_W * w)

    def col_of(i):
        if not rev:
            return jnp.maximum(i - n_ctx_chunks, 0)
        return jnp.where(i < n_ctx_chunks, n_cols - 1, n_steps - 1 - i)

    def ctx_of(i):
        if not rev:
            return jnp.minimum(i, n_ctx_chunks - 1)
        return jnp.maximum(n_ctx_chunks - 1 - i, 0)

    def lat_spec(m):
        return pl.BlockSpec((1, CHUNK, GLA_BLK), lambda bi, i: (bi, 0, col_of(i) * nblk + m))

    vec = lambda bi, i: (0, 0)
    out = pl.pallas_call(
        functools.partial(_gla_kernel, rev, prec, n_ctx_chunks),
        grid=(b, n_steps),
        in_specs=[lat_spec(0), lat_spec(1), lat_spec(2), lat_spec(3), lat_spec(6),
                  pl.BlockSpec((1, CHUNK, w), lambda bi, i: (bi, ctx_of(i), 0)),
                  pl.BlockSpec((GLA_GATE_LORA, GLA_QK_WIDTH), vec),
                  pl.BlockSpec((1, GLA_QK_WIDTH), vec)],
        out_specs=pl.BlockSpec((1, CHUNK, GLA_V_WIDTH), lambda bi, i: (bi, 0, col_of(i))),
        out_shape=jax.ShapeDtypeStruct((b, t // GRID_W, GRID_W * GLA_V_WIDTH), F32),
        scratch_shapes=[pltpu.VMEM((GLA_HEADS, GLA_VAL_DIM, GLA_KEY_DIM), F32)],
        compiler_params=_cparams(("arbitrary", "arbitrary")),
        name="gla_bwd" if rev else "gla_fwd",
    )(lat_view, lat_view, lat_view, lat_view, lat_view, p_ctx, alpha_up, alpha_b)
    return out.reshape(b, t, GLA_V_WIDTH)


def _rwkv_post_kernel(yf_ref, yb_ref, pm_ref, a2_ref, a0_ref, g2_ref, ka_ref, rk_ref, lng_ref, lnb_ref, o_ref):
    p = pm_ref[0]
    r = p[:, 0:RWKV_WIDTH]
    k = p[:, RWKV_WIDTH:2 * RWKV_WIDTH]
    v = p[:, 2 * RWKV_WIDTH:3 * RWKV_WIDTH]
    ad_f = p[:, OFF_AD:OFF_AD + ICL_LORA]
    ad_b = p[:, OFF_AD + ICL_LORA:OFF_AD + 2 * ICL_LORA]
    gd = p[:, OFF_GD:OFF_GD + GATE_LORA]
    ka = ka_ref[...]
    a_f = _sigmoid(a0_ref[0] + _mm(ad_f, a2_ref[0], HIGHEST))
    a_b = _sigmoid(a0_ref[1] + _mm(ad_b, a2_ref[1], HIGHEST))
    kd_sum = k * (1.0 + (a_f - 1.0) * ka) + k * (1.0 + (a_b - 1.0) * ka)
    gate = _mm(_sigmoid(gd), g2_ref[...], HIGHEST)
    rkk = r * kd_sum * rk_ref[...]
    ysum = yf_ref[0] + yb_ref[0]
    hsum = _head_sum_matrix()
    inv_n = 1.0 / RWKV_HEAD_DIM
    for pr in range(N_PAIRS):
        sl = slice(pr * PAIR, (pr + 1) * PAIR)
        ys = ysum[:, sl]
        mean = _mm(ys, hsum, HIGHEST) * inv_n
        dlt = ys - mean
        var = _mm(dlt * dlt, hsum, HIGHEST) * inv_n
        gn = dlt * lax.rsqrt(var + RWKV_GN_EPS) * lng_ref[:, sl] + lnb_ref[:, sl]
        bonus = _mm(rkk[:, sl], hsum, HIGHEST) * v[:, sl]
        o_ref[0, :, sl] = ((gn + bonus) * gate[:, sl]).astype(o_ref.dtype)


def _rwkv_post(y_f, y_b, pmix, n_ctx, a2, a0, g2, k_a, r_k, ln_g, ln_b, tm=256):
    b, t, w = y_f.shape
    off = n_ctx // tm
    row = lambda bi, i: (bi, i, 0)
    vec = lambda bi, i: (0, 0)
    vec3 = lambda bi, i: (0, 0, 0)
    return pl.pallas_call(
        _rwkv_post_kernel,
        grid=(b, t // tm),
        in_specs=[pl.BlockSpec((1, tm, w), row),
                  pl.BlockSpec((1, tm, w), row),
                  pl.BlockSpec((1, tm, pmix.shape[2]), lambda bi, i: (bi, i + off, 0)),
                  pl.BlockSpec((2, ICL_LORA, w), vec3),
                  pl.BlockSpec((2, 1, w), vec3),
                  pl.BlockSpec((GATE_LORA, w), vec),
                  pl.BlockSpec((1, w), vec), pl.BlockSpec((1, w), vec),
                  pl.BlockSpec((1, w), vec), pl.BlockSpec((1, w), vec)],
        out_specs=pl.BlockSpec((1, tm, w), row),
        out_shape=jax.ShapeDtypeStruct((b, t, w), BF16),
        compiler_params=_cparams(("arbitrary", "arbitrary")),
        name="rwkv_post",
    )(y_f, y_b, pmix, a2, a0, g2, k_a, r_k, ln_g, ln_b)


def _gla_post_kernel(of_ref, ob_ref, g_ref, ng_ref, o_ref):
    o = of_ref[0] + ob_ref[0]
    g = g_ref[0]
    dv = GLA_VAL_DIM
    for h in range(GLA_HEADS):
        sl = slice(h * dv, (h + 1) * dv)
        oh = o[:, sl]
        oh = oh * lax.rsqrt(jnp.mean(oh * oh, axis=-1, keepdims=True) + GLA_NORM_EPS) * ng_ref[:, sl]
        gh = g[:, sl]
        o_ref[0, :, sl] = (oh * (gh * _sigmoid(gh))).astype(o_ref.dtype)


def _gla_post(o_f, o_b, p_gla, norm_g, tm=256):
    b, t, w = o_f.shape
    row = lambda bi, i: (bi, i, 0)
    return pl.pallas_call(
        _gla_post_kernel,
        grid=(b, t // tm),
        in_specs=[pl.BlockSpec((1, tm, w), row),
                  pl.BlockSpec((1, tm, w), row),
                  pl.BlockSpec((1, tm, w), lambda bi, i: (bi, i, 2)),
                  pl.BlockSpec((1, w), lambda bi, i: (0, 0))],
        out_specs=pl.BlockSpec((1, tm, w), row),
        out_shape=jax.ShapeDtypeStruct((b, t, w), BF16),
        compiler_params=_cparams(("arbitrary", "arbitrary")),
        name="gla_post",
    )(o_f, o_b, p_gla, norm_g)


def _merge_kernel(ya_ref, yb_ref, wr_ref, wg_ref, ga_ref, gb_ref, o_ref):
    ma = _mm(ya_ref[0], wr_ref[...])
    mb = _mm(yb_ref[0], wg_ref[...])
    o_ref[0] = (_sigmoid(ga_ref[0]) * ma + _sigmoid(gb_ref[0]) * mb).astype(o_ref.dtype)


def _merge(ya, yb, w_r, w_g, p_gate, tm=512, tn=512):
    b, t, w = ya.shape
    d = w_r.shape[1]
    nj = d // tn
    return pl.pallas_call(
        _merge_kernel,
        grid=(b, t // tm, nj),
        in_specs=[pl.BlockSpec((1, tm, w), lambda bi, i, j: (bi, i, 0)),
                  pl.BlockSpec((1, tm, w), lambda bi, i, j: (bi, i, 0)),
                  pl.BlockSpec((w, tn), lambda bi, i, j: (0, j)),
                  pl.BlockSpec((w, tn), lambda bi, i, j: (0, j)),
                  pl.BlockSpec((1, tm, tn), lambda bi, i, j: (bi, i, j)),
                  pl.BlockSpec((1, tm, tn), lambda bi, i, j: (bi, i, j + nj))],
        out_specs=pl.BlockSpec((1, tm, tn), lambda bi, i, j: (bi, i, j)),
        out_shape=jax.ShapeDtypeStruct((b, t, d), BF16),
        compiler_params=_cparams(("arbitrary", "arbitrary", "arbitrary")),
        name="merge_branches",
    )(ya, yb, w_r, w_g, p_gate, p_gate)


def _mix_out_kernel(m_ref, w_ref, x_ref, gate_ref, npost_ref, npre_ref, sh_ref, sc_ref, x1_ref, h_ref):
    z = _mm(m_ref[0], w_ref[...])
    z = z * lax.rsqrt(jnp.mean(z * z, axis=-1, keepdims=True) + NORM_EPS) * npost_ref[...]
    x1 = x_ref[0] + gate_ref[0] * z
    x1_ref[0] = x1
    y = x1 * lax.rsqrt(jnp.mean(x1 * x1, axis=-1, keepdims=True) + NORM_EPS) * npre_ref[...]
    h_ref[0] = (y * (1.0 + sc_ref[0]) + sh_ref[0]).astype(h_ref.dtype)


def _mix_out(m, w_out, x, gate, n_post, n_pre, shift, scale, tm=256):
    b, t, d = x.shape
    row = lambda bi, i: (bi, i, 0)
    per_b = lambda bi, i: (bi, 0, 0)
    vec = lambda bi, i: (0, 0)
    return pl.pallas_call(
        _mix_out_kernel,
        grid=(b, t // tm),
        in_specs=[pl.BlockSpec((1, tm, d), row),
                  pl.BlockSpec((d, d), vec),
                  pl.BlockSpec((1, tm, d), row),
                  pl.BlockSpec((1, 1, d), per_b),
                  pl.BlockSpec((1, d), vec), pl.BlockSpec((1, d), vec),
                  pl.BlockSpec((1, 1, d), per_b), pl.BlockSpec((1, 1, d), per_b)],
        out_specs=[pl.BlockSpec((1, tm, d), row), pl.BlockSpec((1, tm, d), row)],
        out_shape=[jax.ShapeDtypeStruct((b, t, d), F32), jax.ShapeDtypeStruct((b, t, d), BF16)],
        compiler_params=_cparams(("arbitrary", "arbitrary")),
        name="mix_out",
    )(m, w_out, x, gate, n_post, n_pre, shift, scale)


def _ffn_up_kernel(h_ref, wg_ref, wu_ref, o_ref):
    h = h_ref[0]
    a = _mm(h, wg_ref[...])
    u = _mm(h, wu_ref[...])
    o_ref[0] = (a * _sigmoid(a) * u).astype(o_ref.dtype)


def _ffn_up(h, w_gate, w_up, tm=512, tn=512):
    b, t, d = h.shape
    f = w_gate.shape[1]
    return pl.pallas_call(
        _ffn_up_kernel,
        grid=(b, t // tm, f // tn),
        in_specs=[pl.BlockSpec((1, tm, d), lambda bi, i, j: (bi, i, 0)),
                  pl.BlockSpec((d, tn), lambda bi, i, j: (0, j)),
                  pl.BlockSpec((d, tn), lambda bi, i, j: (0, j))],
        out_specs=pl.BlockSpec((1, tm, tn), lambda bi, i, j: (bi, i, j)),
        out_shape=jax.ShapeDtypeStruct((b, t, f), BF16),
        compiler_params=_cparams(("arbitrary", "arbitrary", "arbitrary")),
        name="ffn_up",
    )(h, w_gate, w_up)


def _ffn_down_kernel(h_ref, w_ref, x_ref, gate_ref, npost_ref, o_ref, acc):
    kk = pl.program_id(2)

    @pl.when(kk == 0)
    def _():
        acc[...] = jnp.zeros_like(acc)

    acc[...] += _mm(h_ref[0], w_ref[...])

    @pl.when(kk == pl.num_programs(2) - 1)
    def _():
        z = acc[...]
        z = z * lax.rsqrt(jnp.mean(z * z, axis=-1, keepdims=True) + NORM_EPS) * npost_ref[...]
        o_ref[0] = x_ref[0] + gate_ref[0] * z


def _ffn_down(h, w_down, x1, gate, n_post, tm=512, tk=512):
    b, t, f = h.shape
    d = w_down.shape[1]
    return pl.pallas_call(
        _ffn_down_kernel,
        grid=(b, t // tm, f // tk),
        in_specs=[pl.BlockSpec((1, tm, tk), lambda bi, i, k: (bi, i, k)),
                  pl.BlockSpec((tk, d), lambda bi, i, k: (k, 0)),
                  pl.BlockSpec((1, tm, d), lambda bi, i, k: (bi, i, 0)),
                  pl.BlockSpec((1, 1, d), lambda bi, i, k: (bi, 0, 0)),
                  pl.BlockSpec((1, d), lambda bi, i, k: (0, 0))],
        out_specs=pl.BlockSpec((1, tm, d), lambda bi, i, k: (bi, i, 0)),
        out_shape=jax.ShapeDtypeStruct((b, t, d), F32),
        scratch_shapes=[pltpu.VMEM((tm, d), F32)],
        compiler_params=_cparams(("arbitrary", "arbitrary", "arbitrary")),
        name="ffn_down",
    )(h, w_down, x1, gate, n_post)


def _pad_cols(w, n):
    return jnp.pad(w, ((0, 0), (0, n - w.shape[1])))


def kernel(x, c, ctx, c_ctx, ada_w, ada_b, norm_pre_mix, norm_post_mix, norm_pre_ffn, norm_post_ffn, w_in, shift_mu, rwkv_w0, rwkv_w2, rwkv_a0, rwkv_a2, rwkv_g2, rwkv_k_k, rwkv_k_a, rwkv_r_k, rwkv_ln_g, rwkv_ln_b, w_rwkv_up, gla_alpha_up, gla_alpha_b, gla_norm_g, w_gla_up, w_out, ffn_w_gate, ffn_w_up, ffn_w_down):
    assert ada_w.shape[0] == 1, "single trunk layer"
    bsz, seq, d = x.shape
    n_ctx = ctx.shape[1]
    prec = BF16

    cvecs = jnp.concatenate([c, c_ctx[None, :], jnp.zeros((8 - bsz - 1, d), F32)], 0)
    mod = _modulation(cvecs, ada_w[0], ada_b[0])
    mod_x = mod[:bsz].reshape(bsz, 6, 1, d)
    shx1, scx1, gx1, shx2, scx2, gx2 = (mod_x[:, i] for i in range(6))
    mod_c = jnp.broadcast_to(mod[bsz].reshape(1, 6, 1, d), (bsz, 6, 1, d))
    shc1, scc1 = mod_c[:, 0], mod_c[:, 1]

    w_all = w_in[0]
    mix_in = RWKV_IN + GLA_IN
    w_rwkv = _pad_cols(w_all[:, :RWKV_IN], RWKV_PAD).astype(BF16)
    w_gla = _pad_cols(w_all[:, RWKV_IN:mix_in], GLA_PAD).astype(BF16)
    w_gate = w_all[:, mix_in:].astype(BF16)
    mu = _pad_cols(shift_mu, RWKV_PAD)
    n_pre = norm_pre_mix

    px_rwkv = _project(x, shx1, scx1, n_pre, w_rwkv, 512, 512, "proj_rwkv")
    px_gla = _project(x, shx1, scx1, n_pre, w_gla, 512, 512, "proj_gla")
    px_gate = _project(x, shx1, scx1, n_pre, w_gate, 512, 512, "proj_gate")
    pc_rwkv = _project(ctx, shc1, scc1, n_pre, w_rwkv, n_ctx, 512, "proj_rwkv_ctx")
    pc_gla = _project(ctx, shc1, scc1, n_pre, w_gla, n_ctx, 512, "proj_gla_ctx")

    pmix = _shiftmix(px_rwkv, pc_rwkv, mu)
    n_ctx_chunks = n_ctx // CHUNK
    y_dir = []
    for dr in range(2):
        y_dir.append(_rwkv_scan(pmix, rwkv_w2[0, dr], rwkv_w0[0, dr][None, :], rwkv_a2[0, dr], rwkv_a0[0, dr][None, :],
                                rwkv_k_k, rwkv_k_a, dr == 1, n_ctx_chunks, prec))
    ya = _rwkv_post(y_dir[0], y_dir[1], pmix, n_ctx, rwkv_a2[0], rwkv_a0[0][:, None, :], rwkv_g2[0], rwkv_k_a,
                    rwkv_r_k.reshape(1, RWKV_WIDTH), rwkv_ln_g, rwkv_ln_b)

    o_dir = [_gla_scan(px_gla, pc_gla, gla_alpha_up[0, dr], gla_alpha_b[0, dr][None, :], dr == 1, prec) for dr in range(2)]
    yb = _gla_post(o_dir[0], o_dir[1], px_gla, gla_norm_g)

    m = _merge(ya, yb, w_rwkv_up[0].astype(BF16), w_gla_up[0].astype(BF16), px_gate)
    x1, h2 = _mix_out(m, w_out[0].astype(BF16), x, gx1, norm_post_mix, norm_pre_ffn, shx2, scx2)
    hf = _ffn_up(h2, ffn_w_gate[0].astype(BF16), ffn_w_up[0].astype(BF16))
    return _ffn_down(hf, ffn_w_down[0].astype(BF16), x1, gx2, norm_post_ffn)
```

```python
import functools

import jax
import jax.numpy as jnp
from jax import lax
from jax.experimental import pallas as pl
from jax.experimental.pallas import tpu as pltpu

F32 = jnp.float32
BF16 = jnp.bfloat16
HIGHEST = lax.Precision.HIGHEST

LANES = 128
VMEM_LIMIT_BYTES = 56 * 1024 * 1024

GRID_W = 64
CHUNK = 64
RWKV_HEADS, RWKV_HEAD_DIM = 16, 64
RWKV_WIDTH = RWKV_HEADS * RWKV_HEAD_DIM
DECAY_LORA = ICL_LORA = 96
GATE_LORA = 64
RWKV_GN_EPS = 64e-5
GLA_HEADS, GLA_KEY_DIM, GLA_VAL_DIM = 4, 128, 256
GLA_QK_WIDTH = GLA_HEADS * GLA_KEY_DIM
GLA_V_WIDTH = GLA_HEADS * GLA_VAL_DIM
GLA_GATE_LORA = 16
GLA_TAU = 16.0
GLA_NORM_EPS = 1e-5
GLA_SUB = 16
NORM_EPS = 1e-6

RWKV_IN = 3 * RWKV_WIDTH + 2 * DECAY_LORA + 2 * ICL_LORA + GATE_LORA
RWKV_PAD = 3584
OFF_WD = 3 * RWKV_WIDTH
OFF_AD = OFF_WD + 2 * DECAY_LORA
OFF_GD = OFF_AD + 2 * ICL_LORA
GLA_IN = 2 * GLA_QK_WIDTH + 2 * GLA_V_WIDTH + 2 * GLA_GATE_LORA
GLA_BLK = 512
GLA_PAD = 7 * GLA_BLK
OFF_GLA_AD = 6 * GLA_BLK

PAIR = 2 * RWKV_HEAD_DIM
N_PAIRS = RWKV_HEADS // 2


def _cparams(semantics):
    return pltpu.CompilerParams(dimension_semantics=semantics, vmem_limit_bytes=VMEM_LIMIT_BYTES)


def _sigmoid(z):
    return 1.0 / (1.0 + jnp.exp(-z))


def _softplus(z):
    return jnp.maximum(z, 0.0) + jnp.log(1.0 + jnp.exp(-jnp.abs(z)))


def _dot(a, b, dims, precision):
    if precision is BF16:
        a, b, precision = a.astype(BF16), b.astype(BF16), None
    return lax.dot_general(a, b, (dims, ((), ())), precision=precision, preferred_element_type=F32)


def _mm(a, b, precision=None):
    return _dot(a, b, ((1,), (0,)), precision)


def _mm_nt(a, b, precision=None):
    return _dot(a, b, ((1,), (1,)), precision)


def _mm_tn(a, b, precision=None):
    return _dot(a, b, ((0,), (0,)), precision)


def _mod_kernel(c_ref, w_ref, b_ref, o_ref):
    s = c_ref[...]
    s = s * _sigmoid(s)
    o_ref[...] = _mm(s, w_ref[...], HIGHEST) + b_ref[...]


def _modulation(cvecs, ada_w, ada_b, tn=1024):
    m, d = cvecs.shape
    n = ada_w.shape[1]
    return pl.pallas_call(
        _mod_kernel,
        grid=(n // tn,),
        in_specs=[pl.BlockSpec((m, d), lambda j: (0, 0)),
                  pl.BlockSpec((d, tn), lambda j: (0, j)),
                  pl.BlockSpec((1, tn), lambda j: (0, j))],
        out_specs=pl.BlockSpec((m, tn), lambda j: (0, j)),
        out_shape=jax.ShapeDtypeStruct((m, n), F32),
        compiler_params=_cparams(("arbitrary",)),
        name="adaln_mod",
    )(cvecs, ada_w, ada_b.reshape(1, n))


def _proj_kernel(x_ref, sh_ref, sc_ref, g_ref, w_ref, o_ref, h_scr):
    @pl.when(pl.program_id(2) == 0)
    def _():
        x = x_ref[0]
        ms = jnp.mean(x * x, axis=-1, keepdims=True)
        y = x * lax.rsqrt(ms + NORM_EPS) * g_ref[...]
        h_scr[...] = (y * (1.0 + sc_ref[0]) + sh_ref[0]).astype(BF16)

    o_ref[0] = _mm(h_scr[...], w_ref[...])


def _project(x, shift, scale, gain, w, tm, tn, name):
    b, t, d = x.shape
    n = w.shape[1]
    return pl.pallas_call(
        _proj_kernel,
        grid=(b, t // tm, n // tn),
        in_specs=[pl.BlockSpec((1, tm, d), lambda bi, i, j: (bi, i, 0)),
                  pl.BlockSpec((1, 1, d), lambda bi, i, j: (bi, 0, 0)),
                  pl.BlockSpec((1, 1, d), lambda bi, i, j: (bi, 0, 0)),
                  pl.BlockSpec((1, d), lambda bi, i, j: (0, 0)),
                  pl.BlockSpec((d, tn), lambda bi, i, j: (0, j))],
        out_specs=pl.BlockSpec((1, tm, tn), lambda bi, i, j: (bi, i, j)),
        out_shape=jax.ShapeDtypeStruct((b, t, n), F32),
        scratch_shapes=[pltpu.VMEM((tm, d), BF16)],
        compiler_params=_cparams(("arbitrary", "arbitrary", "arbitrary")),
        name=name,
    )(x, shift, scale, gain, w)


def _shiftmix_kernel(n_ctx_blk, n_blk, tb, lat_ref, ctx_ref, lprev_ref, lnext_ref, cprev_ref, cnext_ref, mu_ref, o_ref):
    i = pl.program_id(1)
    is_ctx = i < n_ctx_blk
    p = jnp.where(is_ctx, ctx_ref[0], lat_ref[0])
    first = jnp.logical_or(i == 0, i == n_ctx_blk)
    last = jnp.logical_or(i == n_ctx_blk - 1, i == n_blk - 1)
    hp = jnp.where(is_ctx, cprev_ref[0, 7:8, :], lprev_ref[0, 7:8, :])
    hn = jnp.where(is_ctx, cnext_ref[0, 0:1, :], lnext_ref[0, 0:1, :])
    hp = jnp.where(first, 0.0, hp)
    hn = jnp.where(last, 0.0, hn)
    row = lax.broadcasted_iota(jnp.int32, p.shape, 0)
    prev = jnp.where(row == 0, hp, pltpu.roll(p, 1, 0))
    nxt = jnp.where(row == tb - 1, hn, pltpu.roll(p, tb - 1, 0))
    o_ref[0] = p + mu_ref[...] * (0.5 * (prev + nxt) - p)


def _shiftmix(p_lat, p_ctx, mu, tb=128):
    b, t, w = p_lat.shape
    tc = p_ctx.shape[1]
    n_ctx_blk, n_lat_blk = tc // tb, t // tb
    n_blk = n_ctx_blk + n_lat_blk
    r8 = tb // 8

    def lat_main(bi, i):
        return (bi, jnp.maximum(i - n_ctx_blk, 0), 0)

    def ctx_main(bi, i):
        return (bi, jnp.minimum(i, n_ctx_blk - 1), 0)

    def lat_prev(bi, i):
        return (bi, jnp.maximum((i - n_ctx_blk) * r8 - 1, 0), 0)

    def lat_next(bi, i):
        return (bi, jnp.clip((i - n_ctx_blk + 1) * r8, 0, t // 8 - 1), 0)

    def ctx_prev(bi, i):
        return (bi, jnp.clip(i * r8 - 1, 0, tc // 8 - 1), 0)

    def ctx_next(bi, i):
        return (bi, jnp.minimum((i + 1) * r8, tc // 8 - 1), 0)

    return pl.pallas_call(
        functools.partial(_shiftmix_kernel, n_ctx_blk, n_blk, tb),
        grid=(b, n_blk),
        in_specs=[pl.BlockSpec((1, tb, w), lat_main),
                  pl.BlockSpec((1, tb, w), ctx_main),
                  pl.BlockSpec((1, 8, w), lat_prev),
                  pl.BlockSpec((1, 8, w), lat_next),
                  pl.BlockSpec((1, 8, w), ctx_prev),
                  pl.BlockSpec((1, 8, w), ctx_next),
                  pl.BlockSpec((1, w), lambda bi, i: (0, 0))],
        out_specs=pl.BlockSpec((1, tb, w), lambda bi, i: (bi, i, 0)),
        out_shape=jax.ShapeDtypeStruct((b, tc + t, w), F32),
        compiler_params=_cparams(("arbitrary", "arbitrary")),
        name="rwkv_shiftmix",
    )(p_lat, p_ctx, p_lat, p_lat, p_ctx, p_ctx, mu)


def _head_sum_matrix():
    r = lax.broadcasted_iota(jnp.int32, (PAIR, PAIR), 0)
    c = lax.broadcasted_iota(jnp.int32, (PAIR, PAIR), 1)
    return jnp.where((r >> 6) == (c >> 6), 1.0, 0.0).astype(F32)


def _rwkv_kernel(rev, prec, pm_ref, w2_ref, w0_ref, a2_ref, a0_ref, kk_ref, ka_ref, y_ref, st_scr):
    hd = RWKV_HEAD_DIM
    c2 = 2 * CHUNK

    @pl.when(pl.program_id(1) == 0)
    def _():
        st_scr[...] = jnp.zeros_like(st_scr)

    p = pm_ref[0]
    r = p[:, 0:RWKV_WIDTH]
    k = p[:, RWKV_WIDTH:2 * RWKV_WIDTH]
    v = p[:, 2 * RWKV_WIDTH:3 * RWKV_WIDTH]
    d_off = DECAY_LORA if rev else 0
    wd = p[:, OFF_WD + d_off:OFF_WD + d_off + DECAY_LORA]
    ad = p[:, OFF_AD + d_off:OFF_AD + d_off + ICL_LORA]

    w_log = -_softplus(-(w0_ref[...] + _mm(jnp.tanh(wd), w2_ref[...], HIGHEST))) - 0.5
    lw = -jnp.exp(w_log)
    a = _sigmoid(a0_ref[...] + _mm(ad, a2_ref[...], HIGHEST))
    kk_raw = k * kk_ref[...]
    kd = k * (1.0 + (a - 1.0) * ka_ref[...])

    ri = lax.broadcasted_iota(jnp.int32, (c2, PAIR), 0)
    li = lax.broadcasted_iota(jnp.int32, (c2, PAIR), 1)
    rt, lt = ri & (CHUNK - 1), li & (hd - 1)
    same_head = (ri >> 6) == (li >> 6)
    strict = (lt > rt) if rev else (lt < rt)
    mask_n = jnp.logical_and(same_head, strict)
    mask_k = jnp.logical_and(jnp.logical_not(same_head), strict)
    ident = ri == li
    eye = jnp.where(ident, 1.0, 0.0).astype(F32)
    rc = lax.broadcasted_iota(jnp.int32, (CHUNK, PAIR), 0)
    lc = lax.broadcasted_iota(jnp.int32, (CHUNK, PAIR), 1)
    incl_c = ((lc & (hd - 1)) >= rc) if rev else ((lc & (hd - 1)) <= rc)
    lane_e = lc < hd
    hsum = _head_sum_matrix()

    ci = lax.broadcasted_iota(jnp.int32, (CHUNK, CHUNK), 0)
    cj = lax.broadcasted_iota(jnp.int32, (CHUNK, CHUNK), 1)
    tri = jnp.where((cj >= ci) if rev else (cj <= ci), 1.0, 0.0).astype(F32)
    cum = _mm(tri, lw, HIGHEST)
    total = cum[0:1] if rev else cum[CHUNK - 1:CHUNK]
    e_prev = jnp.exp(cum - lw)
    e_neg = jnp.exp(-cum)
    e_pos = jnp.exp(cum)
    e_rest = jnp.exp(total - cum)
    p_end = jnp.exp(total)

    def split(z):
        ze = jnp.where(lane_e, z, 0.0)
        return ze, z - ze

    pairs = range(N_PAIRS)
    sls = [slice(pr * PAIR, (pr + 1) * PAIR) for pr in pairs]
    cat0 = lambda *z: jnp.concatenate(z, 0)

    kkr = [kk_raw[:, sl] for sl in sls]
    nrm2 = [_mm(z * z, hsum, HIGHEST) for z in kkr]
    kk = [z / jnp.maximum(jnp.sqrt(n), 1e-12) for z, n in zip(kkr, nrm2)]
    bb = [z * a[:, sl] for z, sl in zip(kk, sls)]
    at = [split(z * e_prev[:, sl]) for z, sl in zip(kk, sls)]
    bt = [z * e_neg[:, sl] for z, sl in zip(bb, sls)]
    kt = [kd[:, sl] * e_neg[:, sl] for sl in sls]
    rt_ = [r[:, sl] * e_pos[:, sl] for sl in sls]
    rs = [split(z) for z in rt_]
    vs = [split(v[:, sl]) for sl in sls]
    bh = [cat0(*split(z * e_rest[:, sl])) for z, sl in zip(bb, sls)]
    kh = [cat0(*split(kd[:, sl] * e_rest[:, sl])) for sl in sls]
    v_swap = [cat0(vo, ve) for ve, vo in vs]
    v_stack = [cat0(ve, vo) for ve, vo in vs]

    g_e = [_mm_nt(cat0(at[i][0], rs[i][0]), cat0(bt[i], kt[i]), prec) for i in pairs]
    g_o = [_mm_nt(cat0(at[i][1], rs[i][1]), cat0(kt[i], bt[i]), prec) for i in pairs]
    g_top = [cat0(g_e[i][0:CHUNK], g_o[i][0:CHUNK]) for i in pairs]
    nbd = [jnp.where(mask_n, z, 0.0) for z in g_top]
    aak = [jnp.where(mask_k, z, 0.0) for z in g_top]
    rab = [jnp.where(incl_c, jnp.where(lane_e, g_e[i][CHUNK:c2], g_o[i][CHUNK:c2]), 0.0) for i in pairs]
    rak = [jnp.where(incl_c, jnp.where(lane_e, g_o[i][CHUNK:c2], g_e[i][CHUNK:c2]), 0.0) for i in pairs]

    x = [_mm(aak[i], v_swap[i], prec) for i in pairs]
    n2 = [_mm(z, z, prec) for z in nbd]
    y0b = [_mm(rak[i], v_swap[i], prec) for i in pairs]
    n4 = [_mm(z, z, prec) for z in n2]
    imn = [eye - z for z in nbd]
    p1 = [imn[i] + _mm(imn[i], n2[i], prec) for i in pairs]
    n8 = [_mm(z, z, prec) for z in n4]
    nb = [_mm_tn(kh[i], v_stack[i], prec) for i in pairs]
    n16 = [_mm(z, z, prec) for z in n8]
    p2 = [eye + n4[i] + n8[i] + _mm(n4[i], n8[i], prec) for i in pairs]
    n32 = [_mm(z, z, prec) for z in n16]
    p12 = [_mm(p1[i], p2[i], prec) for i in pairs]
    p3 = [eye + n16[i] + n32[i] + _mm(n16[i], n32[i], prec) for i in pairs]
    tinv = [_mm(p12[i], p3[i], prec) for i in pairs]
    wu = [-_mm(tinv[i], jnp.concatenate([cat0(*at[i]), x[i]], 1), prec) for i in pairs]
    qy = [_mm(rab[i], wu[i], prec) for i in pairs]
    mn = [_mm_tn(bh[i], wu[i], prec) for i in pairs]
    q = [rt_[i] + qy[i][:, 0:PAIR] for i in pairs]
    m = [jnp.where(ident, p_end[:, sls[i]], 0.0) + mn[i][:, 0:PAIR] for i in pairs]
    qm = [_mm(cat0(q[i], m[i]), st_scr[i], prec) for i in pairs]
    for i in pairs:
        y_ref[0, :, sls[i]] = qm[i][0:CHUNK] + qy[i][:, PAIR:2 * PAIR] + y0b[i]
        st_scr[i] = qm[i][CHUNK:CHUNK + PAIR] + mn[i][:, PAIR:2 * PAIR] + nb[i]


def _rwkv_scan(pmix, w2, w0, a2, a0, k_k, k_a, rev, n_ctx_chunks, prec):
    b, tt, w = pmix.shape
    n_chunks = tt // CHUNK
    n_lat = n_chunks - n_ctx_chunks

    def chunk_of(i):
        if not rev:
            return i
        return jnp.where(i < n_ctx_chunks, n_ctx_chunks - 1 - i, n_chunks + n_ctx_chunks - 1 - i)

    def out_of(i):
        if not rev:
            return jnp.maximum(i - n_ctx_chunks, 0)
        return jnp.where(i < n_ctx_chunks, n_lat - 1, n_chunks - 1 - i)

    vec = lambda bi, i: (0, 0)
    return pl.pallas_call(
        functools.partial(_rwkv_kernel, rev, prec),
        grid=(b, n_chunks),
        in_specs=[pl.BlockSpec((1, CHUNK, w), lambda bi, i: (bi, chunk_of(i), 0)),
                  pl.BlockSpec((DECAY_LORA, RWKV_WIDTH), vec),
                  pl.BlockSpec((1, RWKV_WIDTH), vec),
                  pl.BlockSpec((ICL_LORA, RWKV_WIDTH), vec),
                  pl.BlockSpec((1, RWKV_WIDTH), vec),
                  pl.BlockSpec((1, RWKV_WIDTH), vec),
                  pl.BlockSpec((1, RWKV_WIDTH), vec)],
        out_specs=pl.BlockSpec((1, CHUNK, RWKV_WIDTH), lambda bi, i: (bi, out_of(i), 0)),
        out_shape=jax.ShapeDtypeStruct((b, n_lat * CHUNK, RWKV_WIDTH), F32),
        scratch_shapes=[pltpu.VMEM((N_PAIRS, PAIR, PAIR), F32)],
        compiler_params=_cparams(("arbitrary", "arbitrary")),
        name="rwkv7_bwd" if rev else "rwkv7_fwd",
    )(pmix, w2, w0, a2, a0, k_k, k_a)


def _gla_kernel(rev, prec, n_ctx_chunks, q_ref, k_ref, v0_ref, v1_ref, ad_ref, ctx_ref, aup_ref, ab_ref, o_ref, st_scr):
    i = pl.program_id(1)

    @pl.when(i == 0)
    def _():
        st_scr[...] = jnp.zeros_like(st_scr)

    is_ctx = i < n_ctx_chunks
    pc = ctx_ref[0]
    q = jnp.where(is_ctx, pc[:, 0:GLA_BLK], q_ref[0])
    k = jnp.where(is_ctx, pc[:, GLA_BLK:2 * GLA_BLK], k_ref[0])
    v = jnp.concatenate([jnp.where(is_ctx, pc[:, 2 * GLA_BLK:3 * GLA_BLK], v0_ref[0]),
                         jnp.where(is_ctx, pc[:, 3 * GLA_BLK:4 * GLA_BLK], v1_ref[0])], 1)
    d_off = GLA_GATE_LORA if rev else 0
    ad = jnp.where(is_ctx, pc[:, OFF_GLA_AD:OFF_GLA_AD + LANES], ad_ref[0][:, 0:LANES])
    ad = ad[:, d_off:d_off + GLA_GATE_LORA]

    la = -_softplus(-(_mm(ad, aup_ref[...], HIGHEST) + ab_ref[...])) * (1.0 / GLA_TAU)
    ci = lax.broadcasted_iota(jnp.int32, (CHUNK, CHUNK), 0)
    cj = lax.broadcasted_iota(jnp.int32, (CHUNK, CHUNK), 1)
    tri = jnp.where((cj >= ci) if rev else (cj <= ci), 1.0, 0.0).astype(F32)
    cum = _mm(tri, la, HIGHEST)
    total = cum[0:1] if rev else cum[CHUNK - 1:CHUNK]

    dk, dv, sb = GLA_KEY_DIM, GLA_VAL_DIM, GLA_SUB
    n_sb = CHUNK // sb
    row = lax.broadcasted_iota(jnp.int32, (CHUNK, dk), 0)
    row_in = row & (sb - 1)
    arow = lax.broadcasted_iota(jnp.int32, (CHUNK, CHUNK), 0)
    acol = lax.broadcasted_iota(jnp.int32, (CHUNK, CHUNK), 1)
    scale = GLA_KEY_DIM ** -0.5

    for h in range(GLA_HEADS):
        ks_ = slice(h * dk, (h + 1) * dk)
        qh = q[:, ks_] * scale
        kh = k[:, ks_]
        bh = cum[:, ks_]
        lah = la[:, ks_]
        toth = total[:, ks_]
        vh = v[:, h * dv:(h + 1) * dv]
        st = st_scr[h]

        o = _mm_nt(qh * jnp.exp(bh), st, prec)

        off_rows = []
        for blk in range(n_sb):
            rs = slice(blk * sb, (blk + 1) * sb)
            first = blk * sb + (sb - 1 if rev else 0)
            is_first_blk = (blk == n_sb - 1) if rev else (blk == 0)
            if is_first_blk:
                off_rows.append(jnp.zeros((sb, CHUNK), F32))
                continue
            beta = bh[first:first + 1] - lah[first:first + 1]
            qs = qh[rs] * jnp.exp(bh[rs] - beta)
            before = (row >= (blk + 1) * sb) if rev else (row < blk * sb)
            ksc = jnp.where(before, kh * jnp.exp(jnp.minimum(beta - bh, 0.0)), 0.0)
            off_rows.append(_mm_nt(qs, ksc, prec))
        att = jnp.concatenate(off_rows, 0)

        for s in range(sb):
            pick = lambda z: jnp.concatenate(
                [jnp.broadcast_to(z[blk * sb + s:blk * sb + s + 1], (sb, dk)) for blk in range(n_sb)], 0)
            ok = (row_in <= s) if rev else (row_in >= s)
            e = jnp.exp(jnp.where(ok, bh - pick(bh), 0.0))
            col = jnp.sum(jnp.where(ok, qh * pick(kh) * e, 0.0), axis=-1, keepdims=True)
            tgt = jnp.logical_and(acol == (arow & ~(sb - 1)) + s,
                                  ((arow & (sb - 1)) <= s) if rev else ((arow & (sb - 1)) >= s))
            att = jnp.where(tgt, col, att)

        o_ref[0, :, h * dv:(h + 1) * dv] = o + _mm(att, vh, prec)
        st_scr[h] = st * jnp.exp(toth) + _mm_tn(vh, kh * jnp.exp(toth - bh), prec)


def _gla_scan(p_lat, p_ctx, alpha_up, alpha_b, rev, prec):
    b, t, w = p_lat.shape
    assert w == GLA_PAD and t == GRID_W * CHUNK
    n_cols = GRID_W
    n_ctx_chunks = p_ctx.shape[1] // CHUNK
    n_steps = n_ctx_chunks + n_cols
    nblk = w // GLA_BLK
    lat_view = p_lat.reshape(b, t // GRID_W, GRID_W * w)

    def col_of(i):
        if not rev:
            return jnp.maximum(i - n_ctx_chunks, 0)
        return jnp.where(i < n_ctx_chunks, n_cols - 1, n_steps - 1 - i)

    def ctx_of(i):
        if not rev:
            return jnp.minimum(i, n_ctx_chunks - 1)
        return jnp.maximum(n_ctx_chunks - 1 - i, 0)

    def lat_spec(m):
        return pl.BlockSpec((1, CHUNK, GLA_BLK), lambda bi, i: (bi, 0, col_of(i) * nblk + m))

    vec = lambda bi, i: (0, 0)
    out = pl.pallas_call(
        functools.partial(_gla_kernel, rev, prec, n_ctx_chunks),
        grid=(b, n_steps),
        in_specs=[lat_spec(0), lat_spec(1), lat_spec(2), lat_spec(3), lat_spec(6),
                  pl.BlockSpec((1, CHUNK, w), lambda bi, i: (bi, ctx_of(i), 0)),
                  pl.BlockSpec((GLA_GATE_LORA, GLA_QK_WIDTH), vec),
                  pl.BlockSpec((1, GLA_QK_WIDTH), vec)],
        out_specs=pl.BlockSpec((1, CHUNK, GLA_V_WIDTH), lambda bi, i: (bi, 0, col_of(i))),
        out_shape=jax.ShapeDtypeStruct((b, t // GRID_W, GRID_W * GLA_V_WIDTH), F32),
        scratch_shapes=[pltpu.VMEM((GLA_HEADS, GLA_VAL_DIM, GLA_KEY_DIM), F32)],
        compiler_params=_cparams(("arbitrary", "arbitrary")),
        name="gla_bwd" if rev else "gla_fwd",
    )(lat_view, lat_view, lat_view, lat_view, lat_view, p_ctx, alpha_up, alpha_b)
    return out.reshape(b, t, GLA_V_WIDTH)


def _rwkv_post_kernel(yf_ref, yb_ref, pm_ref, a2_ref, a0_ref, g2_ref, ka_ref, rk_ref, lng_ref, lnb_ref, o_ref):
    p = pm_ref[0]
    r = p[:, 0:RWKV_WIDTH]
    k = p[:, RWKV_WIDTH:2 * RWKV_WIDTH]
    v = p[:, 2 * RWKV_WIDTH:3 * RWKV_WIDTH]
    ad_f = p[:, OFF_AD:OFF_AD + ICL_LORA]
    ad_b = p[:, OFF_AD + ICL_LORA:OFF_AD + 2 * ICL_LORA]
    gd = p[:, OFF_GD:OFF_GD + GATE_LORA]
    ka = ka_ref[...]
    a_f = _sigmoid(a0_ref[0] + _mm(ad_f, a2_ref[0], HIGHEST))
    a_b = _sigmoid(a0_ref[1] + _mm(ad_b, a2_ref[1], HIGHEST))
    kd_sum = k * (1.0 + (a_f - 1.0) * ka) + k * (1.0 + (a_b - 1.0) * ka)
    gate = _mm(_sigmoid(gd), g2_ref[...], HIGHEST)
    rkk = r * kd_sum * rk_ref[...]
    ysum = yf_ref[0] + yb_ref[0]
    hsum = _head_sum_matrix()
    inv_n = 1.0 / RWKV_HEAD_DIM
    for pr in range(N_PAIRS):
        sl = slice(pr * PAIR, (pr + 1) * PAIR)
        ys = ysum[:, sl]
        mean = _mm(ys, hsum, HIGHEST) * inv_n
        dlt = ys - mean
        var = _mm(dlt * dlt, hsum, HIGHEST) * inv_n
        gn = dlt * lax.rsqrt(var + RWKV_GN_EPS) * lng_ref[:, sl] + lnb_ref[:, sl]
        bonus = _mm(rkk[:, sl], hsum, HIGHEST) * v[:, sl]
        o_ref[0, :, sl] = ((gn + bonus) * gate[:, sl]).astype(o_ref.dtype)


def _rwkv_post(y_f, y_b, pmix, n_ctx, a2, a0, g2, k_a, r_k, ln_g, ln_b, tm=256):
    b, t, w = y_f.shape
    off = n_ctx // tm
    row = lambda bi, i: (bi, i, 0)
    vec = lambda bi, i: (0, 0)
    vec3 = lambda bi, i: (0, 0, 0)
    return pl.pallas_call(
        _rwkv_post_kernel,
        grid=(b, t // tm),
        in_specs=[pl.BlockSpec((1, tm, w), row),
                  pl.BlockSpec((1, tm, w), row),
                  pl.BlockSpec((1, tm, pmix.shape[2]), lambda bi, i: (bi, i + off, 0)),
                  pl.BlockSpec((2, ICL_LORA, w), vec3),
                  pl.BlockSpec((2, 1, w), vec3),
                  pl.BlockSpec((GATE_LORA, w), vec),
                  pl.BlockSpec((1, w), vec), pl.BlockSpec((1, w), vec),
                  pl.BlockSpec((1, w), vec), pl.BlockSpec((1, w), vec)],
        out_specs=pl.BlockSpec((1, tm, w), row),
        out_shape=jax.ShapeDtypeStruct((b, t, w), BF16),
        compiler_params=_cparams(("arbitrary", "arbitrary")),
        name="rwkv_post",
    )(y_f, y_b, pmix, a2, a0, g2, k_a, r_k, ln_g, ln_b)


def _gla_post_kernel(of_ref, ob_ref, g_ref, ng_ref, o_ref):
    o = of_ref[0] + ob_ref[0]
    g = g_ref[0]
    dv = GLA_VAL_DIM
    for h in range(GLA_HEADS):
        sl = slice(h * dv, (h + 1) * dv)
        oh = o[:, sl]
        oh = oh * lax.rsqrt(jnp.mean(oh * oh, axis=-1, keepdims=True) + GLA_NORM_EPS) * ng_ref[:, sl]
        gh = g[:, sl]
        o_ref[0, :, sl] = (oh * (gh * _sigmoid(gh))).astype(o_ref.dtype)


def _gla_post(o_f, o_b, p_gla, norm_g, tm=256):
    b, t, w = o_f.shape
    row = lambda bi, i: (bi, i, 0)
    return pl.pallas_call(
        _gla_post_kernel,
        grid=(b, t // tm),
        in_specs=[pl.BlockSpec((1, tm, w), row),
                  pl.BlockSpec((1, tm, w), row),
                  pl.BlockSpec((1, tm, w), lambda bi, i: (bi, i, 2)),
                  pl.BlockSpec((1, w), lambda bi, i: (0, 0))],
        out_specs=pl.BlockSpec((1, tm, w), row),
        out_shape=jax.ShapeDtypeStruct((b, t, w), BF16),
        compiler_params=_cparams(("arbitrary", "arbitrary")),
        name="gla_post",
    )(o_f, o_b, p_gla, norm_g)


def _merge_kernel(ya_ref, yb_ref, wr_ref, wg_ref, ga_ref, gb_ref, o_ref):
    ma = _mm(ya_ref[0], wr_ref[...])
    mb = _mm(yb_ref[0], wg_ref[...])
    o_ref[0] = (_sigmoid(ga_ref[0]) * ma + _sigmoid(gb_ref[0]) * mb).astype(o_ref.dtype)


def _merge(ya, yb, w_r, w_g, p_gate, tm=512, tn=512):
    b, t, w = ya.shape
    d = w_r.shape[1]
    nj = d // tn
    return pl.pallas_call(
        _merge_kernel,
        grid=(b, t // tm, nj),
        in_specs=[pl.BlockSpec((1, tm, w), lambda bi, i, j: (bi, i, 0)),
                  pl.BlockSpec((1, tm, w), lambda bi, i, j: (bi, i, 0)),
                  pl.BlockSpec((w, tn), lambda bi, i, j: (0, j)),
                  pl.BlockSpec((w, tn), lambda bi, i, j: (0, j)),
                  pl.BlockSpec((1, tm, tn), lambda bi, i, j: (bi, i, j)),
                  pl.BlockSpec((1, tm, tn), lambda bi, i, j: (bi, i, j + nj))],
        out_specs=pl.BlockSpec((1, tm, tn), lambda bi, i, j: (bi, i, j)),
        out_shape=jax.ShapeDtypeStruct((b, t, d), BF16),
        compiler_params=_cparams(("arbitrary", "arbitrary", "arbitrary")),
        name="merge_branches",
    )(ya, yb, w_r, w_g, p_gate, p_gate)


def _mix_out_kernel(m_ref, w_ref, x_ref, gate_ref, npost_ref, npre_ref, sh_ref, sc_ref, x1_ref, h_ref):
    z = _mm(m_ref[0], w_ref[...])
    z = z * lax.rsqrt(jnp.mean(z * z, axis=-1, keepdims=True) + NORM_EPS) * npost_ref[...]
    x1 = x_ref[0] + gate_ref[0] * z
    x1_ref[0] = x1
    y = x1 * lax.rsqrt(jnp.mean(x1 * x1, axis=-1, keepdims=True) + NORM_EPS) * npre_ref[...]
    h_ref[0] = (y * (1.0 + sc_ref[0]) + sh_ref[0]).astype(h_ref.dtype)


def _mix_out(m, w_out, x, gate, n_post, n_pre, shift, scale, tm=256):
    b, t, d = x.shape
    row = lambda bi, i: (bi, i, 0)
    per_b = lambda bi, i: (bi, 0, 0)
    vec = lambda bi, i: (0, 0)
    return pl.pallas_call(
        _mix_out_kernel,
        grid=(b, t // tm),
        in_specs=[pl.BlockSpec((1, tm, d), row),
                  pl.BlockSpec((d, d), vec),
                  pl.BlockSpec((1, tm, d), row),
                  pl.BlockSpec((1, 1, d), per_b),
                  pl.BlockSpec((1, d), vec), pl.BlockSpec((1, d), vec),
                  pl.BlockSpec((1, 1, d), per_b), pl.BlockSpec((1, 1, d), per_b)],
        out_specs=[pl.BlockSpec((1, tm, d), row), pl.BlockSpec((1, tm, d), row)],
        out_shape=[jax.ShapeDtypeStruct((b, t, d), F32), jax.ShapeDtypeStruct((b, t, d), BF16)],
        compiler_params=_cparams(("arbitrary", "arbitrary")),
        name="mix_out",
    )(m, w_out, x, gate, n_post, n_pre, shift, scale)


def _ffn_up_kernel(h_ref, wg_ref, wu_ref, o_ref):
    h = h_ref[0]
    a = _mm(h, wg_ref[...])
    u = _mm(h, wu_ref[...])
    o_ref[0] = (a * _sigmoid(a) * u).astype(o_ref.dtype)


def _ffn_up(h, w_gate, w_up, tm=512, tn=512):
    b, t, d = h.shape
    f = w_gate.shape[1]
    return pl.pallas_call(
        _ffn_up_kernel,
        grid=(b, t // tm, f // tn),
        in_specs=[pl.BlockSpec((1, tm, d), lambda bi, i, j: (bi, i, 0)),
                  pl.BlockSpec((d, tn), lambda bi, i, j: (0, j)),
                  pl.BlockSpec((d, tn), lambda bi, i, j: (0, j))],
        out_specs=pl.BlockSpec((1, tm, tn), lambda bi, i, j: (bi, i, j)),
        out_shape=jax.ShapeDtypeStruct((b, t, f), BF16),
        compiler_params=_cparams(("arbitrary", "arbitrary", "arbitrary")),
        name="ffn_up",
    )(h, w_gate, w_up)


def _ffn_down_kernel(h_ref, w_ref, x_ref, gate_ref, npost_ref, o_ref, acc):
    kk = pl.program_id(2)

    @pl.when(kk == 0)
    def _():
        acc[...] = jnp.zeros_like(acc)

    acc[...] += _mm(h_ref[0], w_ref[...])

    @pl.when(kk == pl.num_programs(2) - 1)
    def _():
        z = acc[...]
        z = z * lax.rsqrt(jnp.mean(z * z, axis=-1, keepdims=True) + NORM_EPS) * npost_ref[...]
        o_ref[0] = x_ref[0] + gate_ref[0] * z


def _ffn_down(h, w_down, x1, gate, n_post, tm=512, tk=512):
    b, t, f = h.shape
    d = w_down.shape[1]
    return pl.pallas_call(
        _ffn_down_kernel,
        grid=(b, t // tm, f // tk),
        in_specs=[pl.BlockSpec((1, tm, tk), lambda bi, i, k: (bi, i, k)),
                  pl.BlockSpec((tk, d), lambda bi, i, k: (k, 0)),
                  pl.BlockSpec((1, tm, d), lambda bi, i, k: (bi, i, 0)),
                  pl.BlockSpec((1, 1, d), lambda bi, i, k: (bi, 0, 0)),
                  pl.BlockSpec((1, d), lambda bi, i, k: (0, 0))],
        out_specs=pl.BlockSpec((1, tm, d), lambda bi, i, k: (bi, i, 0)),
        out_shape=jax.ShapeDtypeStruct((b, t, d), F32),
        scratch_shapes=[pltpu.VMEM((tm, d), F32)],
        compiler_params=_cparams(("arbitrary", "arbitrary", "arbitrary")),
        name="ffn_down",
    )(h, w_down, x1, gate, n_post)


def _pad_cols(w, n):
    return jnp.pad(w, ((0, 0), (0, n - w.shape[1])))


def kernel(x, c, ctx, c_ctx, ada_w, ada_b, norm_pre_mix, norm_post_mix, norm_pre_ffn, norm_post_ffn, w_in, shift_mu, rwkv_w0, rwkv_w2, rwkv_a0, rwkv_a2, rwkv_g2, rwkv_k_k, rwkv_k_a, rwkv_r_k, rwkv_ln_g, rwkv_ln_b, w_rwkv_up, gla_alpha_up, gla_alpha_b, gla_norm_g, w_gla_up, w_out, ffn_w_gate, ffn_w_up, ffn_w_down):
    assert ada_w.shape[0] == 1, "single trunk layer"
    bsz, seq, d = x.shape
    n_ctx = ctx.shape[1]
    prec = BF16

    cvecs = jnp.concatenate([c, c_ctx[None, :], jnp.zeros((8 - bsz - 1, d), F32)], 0)
    mod = _modulation(cvecs, ada_w[0], ada_b[0])
    mod_x = mod[:bsz].reshape(bsz, 6, 1, d)
    shx1, scx1, gx1, shx2, scx2, gx2 = (mod_x[:, i] for i in range(6))
    mod_c = jnp.broadcast_to(mod[bsz].reshape(1, 6, 1, d), (bsz, 6, 1, d))
    shc1, scc1 = mod_c[:, 0], mod_c[:, 1]

    w_all = w_in[0]
    mix_in = RWKV_IN + GLA_IN
    w_rwkv = _pad_cols(w_all[:, :RWKV_IN], RWKV_PAD).astype(BF16)
    w_gla = _pad_cols(w_all[:, RWKV_IN:mix_in], GLA_PAD).astype(BF16)
    w_gate = w_all[:, mix_in:].astype(BF16)
    mu = _pad_cols(shift_mu, RWKV_PAD)
    n_pre = norm_pre_mix

    px_rwkv = _project(x, shx1, scx1, n_pre, w_rwkv, 512, 512, "proj_rwkv")
    px_gla = _project(x, shx1, scx1, n_pre, w_gla, 512, 512, "proj_gla")
    px_gate = _project(x, shx1, scx1, n_pre, w_gate, 512, 512, "proj_gate")
    pc_rwkv = _project(ctx, shc1, scc1, n_pre, w_rwkv, n_ctx, 512, "proj_rwkv_ctx")
    pc_gla = _project(ctx, shc1, scc1, n_pre, w_gla, n_ctx, 512, "proj_gla_ctx")

    pmix = _shiftmix(px_rwkv, pc_rwkv, mu)
    n_ctx_chunks = n_ctx // CHUNK
    y_dir = []
    for dr in range(2):
        y_dir.append(_rwkv_scan(pmix, rwkv_w2[0, dr], rwkv_w0[0, dr][None, :], rwkv_a2[0, dr], rwkv_a0[0, dr][None, :],
                                rwkv_k_k, rwkv_k_a, dr == 1, n_ctx_chunks, prec))
    ya = _rwkv_post(y_dir[0], y_dir[1], pmix, n_ctx, rwkv_a2[0], rwkv_a0[0][:, None, :], rwkv_g2[0], rwkv_k_a,
                    rwkv_r_k.reshape(1, RWKV_WIDTH), rwkv_ln_g, rwkv_ln_b)

    o_dir = [_gla_scan(px_gla, pc_gla, gla_alpha_up[0, dr], gla_alpha_b[0, dr][None, :], dr == 1, prec) for dr in range(2)]
    yb = _gla_post(o_dir[0], o_dir[1], px_gla, gla_norm_g)

    m = _merge(ya, yb, w_rwkv_up[0].astype(BF16), w_gla_up[0].astype(BF16), px_gate)
    x1, h2 = _mix_out(m, w_out[0].astype(BF16), x, gx1, norm_post_mix, norm_pre_ffn, shx2, scx2)
    hf = _ffn_up(h2, ffn_w_gate[0].astype(BF16), ffn_w_up[0].astype(BF16))
    return _ffn_down(hf, ffn_w_down[0].astype(BF16), x1, gx2, norm_post_ffn)
```

```python
import functools

import jax
import jax.numpy as jnp
from jax import lax
from jax.experimental import pallas as pl
from jax.experimental.pallas import tpu as pltpu

F32 = jnp.float32
BF16 = jnp.bfloat16
HIGHEST = lax.Precision.HIGHEST

LANES = 128
VMEM_LIMIT_BYTES = 56 * 1024 * 1024

GRID_W = 64
CHUNK = 64
RWKV_HEADS, RWKV_HEAD_DIM = 16, 64
RWKV_WIDTH = RWKV_HEADS * RWKV_HEAD_DIM
DECAY_LORA = ICL_LORA = 96
GATE_LORA = 64
RWKV_GN_EPS = 64e-5
GLA_HEADS, GLA_KEY_DIM, GLA_VAL_DIM = 4, 128, 256
GLA_QK_WIDTH = GLA_HEADS * GLA_KEY_DIM
GLA_V_WIDTH = GLA_HEADS * GLA_VAL_DIM
GLA_GATE_LORA = 16
GLA_TAU = 16.0
GLA_NORM_EPS = 1e-5
GLA_SUB = 16
NORM_EPS = 1e-6
PROLOGUE_ROWS = 256

RWKV_IN = 3 * RWKV_WIDTH + 2 * DECAY_LORA + 2 * ICL_LORA + GATE_LORA
RWKV_PAD = 3584
OFF_WD = 3 * RWKV_WIDTH
OFF_AD = OFF_WD + 2 * DECAY_LORA
OFF_GD = OFF_AD + 2 * ICL_LORA
GLA_IN = 2 * GLA_QK_WIDTH + 2 * GLA_V_WIDTH + 2 * GLA_GATE_LORA
GLA_BLK = 512
GLA_PAD = 7 * GLA_BLK
OFF_GLA_AD = 6 * GLA_BLK

PAIR = 2 * RWKV_HEAD_DIM
N_PAIRS = RWKV_HEADS // 2


def _cparams(semantics):
    return pltpu.CompilerParams(dimension_semantics=semantics, vmem_limit_bytes=VMEM_LIMIT_BYTES)


def _sigmoid(z):
    return 1.0 / (1.0 + jnp.exp(-z))


def _softplus(z):
    return jnp.maximum(z, 0.0) + jnp.log(1.0 + jnp.exp(-jnp.abs(z)))


def _dot(a, b, dims, precision):
    if precision is BF16:
        a, b, precision = a.astype(BF16), b.astype(BF16), None
    return lax.dot_general(a, b, (dims, ((), ())), precision=precision, preferred_element_type=F32)


def _mm(a, b, precision=None):
    return _dot(a, b, ((1,), (0,)), precision)


def _mm_nt(a, b, precision=None):
    return _dot(a, b, ((1,), (1,)), precision)


def _mm_tn(a, b, precision=None):
    return _dot(a, b, ((0,), (0,)), precision)


def _mod_kernel(c_ref, w_ref, b_ref, o_ref):
    s = c_ref[...]
    s = s * _sigmoid(s)
    o_ref[...] = _mm(s, w_ref[...], HIGHEST) + b_ref[...]


def _modulation(cvecs, ada_w, ada_b, tn=1024):
    m, d = cvecs.shape
    n = ada_w.shape[1]
    return pl.pallas_call(
        _mod_kernel,
        grid=(n // tn,),
        in_specs=[pl.BlockSpec((m, d), lambda j: (0, 0)),
                  pl.BlockSpec((d, tn), lambda j: (0, j)),
                  pl.BlockSpec((1, tn), lambda j: (0, j))],
        out_specs=pl.BlockSpec((m, tn), lambda j: (0, j)),
        out_shape=jax.ShapeDtypeStruct((m, n), F32),
        compiler_params=_cparams(("arbitrary",)),
        name="adaln_mod",
    )(cvecs, ada_w, ada_b.reshape(1, n))


def _proj_kernel(x_ref, sh_ref, sc_ref, g_ref, w_ref, o_ref, h_scr):
    @pl.when(pl.program_id(2) == 0)
    def _():
        rows = min(PROLOGUE_ROWS, h_scr.shape[0])

        def body(rb, carry):
            sl = pl.ds(pl.multiple_of(rb * rows, rows), rows)
            x = x_ref[0, sl, :]
            ms = jnp.mean(x * x, axis=-1, keepdims=True)
            y = x * lax.rsqrt(ms + NORM_EPS) * g_ref[...]
            h_scr[sl, :] = (y * (1.0 + sc_ref[0]) + sh_ref[0]).astype(BF16)
            return carry

        lax.fori_loop(0, h_scr.shape[0] // rows, body, 0)

    o_ref[0] = _mm(h_scr[...], w_ref[...])


def _project(x, shift, scale, gain, w, tm, tn, name):
    b, t, d = x.shape
    n = w.shape[1]
    return pl.pallas_call(
        _proj_kernel,
        grid=(b, t // tm, n // tn),
        in_specs=[pl.BlockSpec((1, tm, d), lambda bi, i, j: (bi, i, 0)),
                  pl.BlockSpec((1, 1, d), lambda bi, i, j: (bi, 0, 0)),
                  pl.BlockSpec((1, 1, d), lambda bi, i, j: (bi, 0, 0)),
                  pl.BlockSpec((1, d), lambda bi, i, j: (0, 0)),
                  pl.BlockSpec((d, tn), lambda bi, i, j: (0, j))],
        out_specs=pl.BlockSpec((1, tm, tn), lambda bi, i, j: (bi, i, j)),
        out_shape=jax.ShapeDtypeStruct((b, t, n), F32),
        scratch_shapes=[pltpu.VMEM((tm, d), BF16)],
        compiler_params=_cparams(("arbitrary", "arbitrary", "arbitrary")),
        name=name,
    )(x, shift, scale, gain, w)


def _shiftmix_kernel(n_ctx_blk, n_blk, tb, lat_ref, ctx_ref, lprev_ref, lnext_ref, cprev_ref, cnext_ref, mu_ref, o_ref):
    i = pl.program_id(1)
    is_ctx = i < n_ctx_blk
    p = jnp.where(is_ctx, ctx_ref[0], lat_ref[0])
    first = jnp.logical_or(i == 0, i == n_ctx_blk)
    last = jnp.logical_or(i == n_ctx_blk - 1, i == n_blk - 1)
    hp = jnp.where(is_ctx, cprev_ref[0, 7:8, :], lprev_ref[0, 7:8, :])
    hn = jnp.where(is_ctx, cnext_ref[0, 0:1, :], lnext_ref[0, 0:1, :])
    hp = jnp.where(first, 0.0, hp)
    hn = jnp.where(last, 0.0, hn)
    row = lax.broadcasted_iota(jnp.int32, p.shape, 0)
    prev = jnp.where(row == 0, hp, pltpu.roll(p, 1, 0))
    nxt = jnp.where(row == tb - 1, hn, pltpu.roll(p, tb - 1, 0))
    o_ref[0] = p + mu_ref[...] * (0.5 * (prev + nxt) - p)


def _shiftmix(p_lat, p_ctx, mu, tb=128):
    b, t, w = p_lat.shape
    tc = p_ctx.shape[1]
    n_ctx_blk, n_lat_blk = tc // tb, t // tb
    n_blk = n_ctx_blk + n_lat_blk
    r8 = tb // 8

    def lat_main(bi, i):
        return (bi, jnp.maximum(i - n_ctx_blk, 0), 0)

    def ctx_main(bi, i):
        return (bi, jnp.minimum(i, n_ctx_blk - 1), 0)

    def lat_prev(bi, i):
        return (bi, jnp.maximum((i - n_ctx_blk) * r8 - 1, 0), 0)

    def lat_next(bi, i):
        return (bi, jnp.clip((i - n_ctx_blk + 1) * r8, 0, t // 8 - 1), 0)

    def ctx_prev(bi, i):
        return (bi, jnp.clip(i * r8 - 1, 0, tc // 8 - 1), 0)

    def ctx_next(bi, i):
        return (bi, jnp.minimum((i + 1) * r8, tc // 8 - 1), 0)

    return pl.pallas_call(
        functools.partial(_shiftmix_kernel, n_ctx_blk, n_blk, tb),
        grid=(b, n_blk),
        in_specs=[pl.BlockSpec((1, tb, w), lat_main),
                  pl.BlockSpec((1, tb, w), ctx_main),
                  pl.BlockSpec((1, 8, w), lat_prev),
                  pl.BlockSpec((1, 8, w), lat_next),
                  pl.BlockSpec((1, 8, w), ctx_prev),
                  pl.BlockSpec((1, 8, w), ctx_next),
                  pl.BlockSpec((1, w), lambda bi, i: (0, 0))],
        out_specs=pl.BlockSpec((1, tb, w), lambda bi, i: (bi, i, 0)),
        out_shape=jax.ShapeDtypeStruct((b, tc + t, w), F32),
        compiler_params=_cparams(("arbitrary", "arbitrary")),
        name="rwkv_shiftmix",
    )(p_lat, p_ctx, p_lat, p_lat, p_ctx, p_ctx, mu)


def _head_sum_matrix():
    r = lax.broadcasted_iota(jnp.int32, (PAIR, PAIR), 0)
    c = lax.broadcasted_iota(jnp.int32, (PAIR, PAIR), 1)
    return jnp.where((r >> 6) == (c >> 6), 1.0, 0.0).astype(BF16)


def _head_sum(z, hsum):
    hi = z.astype(BF16)
    lo = (z - hi.astype(F32)).astype(BF16)
    return _mm(hi, hsum) + _mm(lo, hsum)


def _rwkv_kernel(rev, prec, pm_ref, w2_ref, w0_ref, a2_ref, a0_ref, kk_ref, ka_ref, y_ref, st_scr):
    hd = RWKV_HEAD_DIM
    c2 = 2 * CHUNK

    @pl.when(pl.program_id(1) == 0)
    def _():
        st_scr[...] = jnp.zeros_like(st_scr)

    p = pm_ref[0]
    r = p[:, 0:RWKV_WIDTH]
    k = p[:, RWKV_WIDTH:2 * RWKV_WIDTH]
    v = p[:, 2 * RWKV_WIDTH:3 * RWKV_WIDTH]
    d_off = DECAY_LORA if rev else 0
    wd = p[:, OFF_WD + d_off:OFF_WD + d_off + DECAY_LORA]
    ad = p[:, OFF_AD + d_off:OFF_AD + d_off + ICL_LORA]

    w_log = -_softplus(-(w0_ref[...] + _mm(jnp.tanh(wd), w2_ref[...], HIGHEST))) - 0.5
    lw = -jnp.exp(w_log)
    a = _sigmoid(a0_ref[...] + _mm(ad, a2_ref[...], HIGHEST))
    kk_raw = k * kk_ref[...]
    kd = k * (1.0 + (a - 1.0) * ka_ref[...])

    ri = lax.broadcasted_iota(jnp.int32, (c2, PAIR), 0)
    li = lax.broadcasted_iota(jnp.int32, (c2, PAIR), 1)
    rt, lt = ri & (CHUNK - 1), li & (hd - 1)
    same_head = (ri >> 6) == (li >> 6)
    strict = (lt > rt) if rev else (lt < rt)
    mask_n = jnp.logical_and(same_head, strict)
    mask_k = jnp.logical_and(jnp.logical_not(same_head), strict)
    ident = ri == li
    eye = jnp.where(ident, 1.0, 0.0).astype(F32)
    rc = lax.broadcasted_iota(jnp.int32, (CHUNK, PAIR), 0)
    lc = lax.broadcasted_iota(jnp.int32, (CHUNK, PAIR), 1)
    incl_c = ((lc & (hd - 1)) >= rc) if rev else ((lc & (hd - 1)) <= rc)
    lane_e = lc < hd
    hsum = _head_sum_matrix()

    ci = lax.broadcasted_iota(jnp.int32, (CHUNK, CHUNK), 0)
    cj = lax.broadcasted_iota(jnp.int32, (CHUNK, CHUNK), 1)
    tri = jnp.where((cj >= ci) if rev else (cj <= ci), 1.0, 0.0).astype(F32)
    cum = _mm(tri, lw, HIGHEST)
    total = cum[0:1] if rev else cum[CHUNK - 1:CHUNK]
    e_prev = jnp.exp(cum - lw)
    e_neg = jnp.exp(-cum)
    e_pos = jnp.exp(cum)
    e_rest = jnp.exp(total - cum)
    p_end = jnp.exp(total)

    def split(z):
        ze = jnp.where(lane_e, z, 0.0)
        return ze, z - ze

    pairs = range(N_PAIRS)
    sls = [slice(pr * PAIR, (pr + 1) * PAIR) for pr in pairs]
    cat0 = lambda *z: jnp.concatenate(z, 0)

    kkr = [kk_raw[:, sl] for sl in sls]
    nrm2 = [_head_sum(z * z, hsum) for z in kkr]
    kk = [z / jnp.maximum(jnp.sqrt(n), 1e-12) for z, n in zip(kkr, nrm2)]
    bb = [z * a[:, sl] for z, sl in zip(kk, sls)]
    at = [split(z * e_prev[:, sl]) for z, sl in zip(kk, sls)]
    bt = [z * e_neg[:, sl] for z, sl in zip(bb, sls)]
    kt = [kd[:, sl] * e_neg[:, sl] for sl in sls]
    rt_ = [r[:, sl] * e_pos[:, sl] for sl in sls]
    rs = [split(z) for z in rt_]
    vs = [split(v[:, sl]) for sl in sls]
    bh = [cat0(*split(z * e_rest[:, sl])) for z, sl in zip(bb, sls)]
    kh = [cat0(*split(kd[:, sl] * e_rest[:, sl])) for sl in sls]
    v_swap = [cat0(vo, ve) for ve, vo in vs]
    v_stack = [cat0(ve, vo) for ve, vo in vs]

    g_e = [_mm_nt(cat0(at[i][0], rs[i][0]), cat0(bt[i], kt[i]), prec) for i in pairs]
    g_o = [_mm_nt(cat0(at[i][1], rs[i][1]), cat0(kt[i], bt[i]), prec) for i in pairs]
    g_top = [cat0(g_e[i][0:CHUNK], g_o[i][0:CHUNK]) for i in pairs]
    nbd = [jnp.where(mask_n, z, 0.0) for z in g_top]
    aak = [jnp.where(mask_k, z, 0.0) for z in g_top]
    rab = [jnp.where(incl_c, jnp.where(lane_e, g_e[i][CHUNK:c2], g_o[i][CHUNK:c2]), 0.0) for i in pairs]
    rak = [jnp.where(incl_c, jnp.where(lane_e, g_o[i][CHUNK:c2], g_e[i][CHUNK:c2]), 0.0) for i in pairs]

    x = [_mm(aak[i], v_swap[i], prec) for i in pairs]
    n2 = [_mm(z, z, prec) for z in nbd]
    y0b = [_mm(rak[i], v_swap[i], prec) for i in pairs]
    n4 = [_mm(z, z, prec) for z in n2]
    imn = [eye - z for z in nbd]
    p1 = [imn[i] + _mm(imn[i], n2[i], prec) for i in pairs]
    n8 = [_mm(z, z, prec) for z in n4]
    nb = [_mm_tn(kh[i], v_stack[i], prec) for i in pairs]
    n16 = [_mm(z, z, prec) for z in n8]
    p2 = [eye + n4[i] + n8[i] + _mm(n4[i], n8[i], prec) for i in pairs]
    n32 = [_mm(z, z, prec) for z in n16]
    p12 = [_mm(p1[i], p2[i], prec) for i in pairs]
    p3 = [eye + n16[i] + n32[i] + _mm(n16[i], n32[i], prec) for i in pairs]
    tinv = [_mm(p12[i], p3[i], prec) for i in pairs]
    wu = [-_mm(tinv[i], jnp.concatenate([cat0(*at[i]), x[i]], 1), prec) for i in pairs]
    qy = [_mm(rab[i], wu[i], prec) for i in pairs]
    mn = [_mm_tn(bh[i], wu[i], prec) for i in pairs]
    q = [rt_[i] + qy[i][:, 0:PAIR] for i in pairs]
    m = [jnp.where(ident, p_end[:, sls[i]], 0.0) + mn[i][:, 0:PAIR] for i in pairs]
    qm = [_mm(cat0(q[i], m[i]), st_scr[i], prec) for i in pairs]
    for i in pairs:
        y_ref[0, :, sls[i]] = qm[i][0:CHUNK] + qy[i][:, PAIR:2 * PAIR] + y0b[i]
        st_scr[i] = qm[i][CHUNK:CHUNK + PAIR] + mn[i][:, PAIR:2 * PAIR] + nb[i]


def _rwkv_scan(pmix, w2, w0, a2, a0, k_k, k_a, rev, n_ctx_chunks, prec):
    b, tt, w = pmix.shape
    n_chunks = tt // CHUNK
    n_lat = n_chunks - n_ctx_chunks

    def chunk_of(i):
        if not rev:
            return i
        return jnp.where(i < n_ctx_chunks, n_ctx_chunks - 1 - i, n_chunks + n_ctx_chunks - 1 - i)

    def out_of(i):
        if not rev:
            return jnp.maximum(i - n_ctx_chunks, 0)
        return jnp.where(i < n_ctx_chunks, n_lat - 1, n_chunks - 1 - i)

    vec = lambda bi, i: (0, 0)
    return pl.pallas_call(
        functools.partial(_rwkv_kernel, rev, prec),
        grid=(b, n_chunks),
        in_specs=[pl.BlockSpec((1, CHUNK, w), lambda bi, i: (bi, chunk_of(i), 0)),
                  pl.BlockSpec((DECAY_LORA, RWKV_WIDTH), vec),
                  pl.BlockSpec((1, RWKV_WIDTH), vec),
                  pl.BlockSpec((ICL_LORA, RWKV_WIDTH), vec),
                  pl.BlockSpec((1, RWKV_WIDTH), vec),
                  pl.BlockSpec((1, RWKV_WIDTH), vec),
                  pl.BlockSpec((1, RWKV_WIDTH), vec)],
        out_specs=pl.BlockSpec((1, CHUNK, RWKV_WIDTH), lambda bi, i: (bi, out_of(i), 0)),
        out_shape=jax.ShapeDtypeStruct((b, n_lat * CHUNK, RWKV_WIDTH), F32),
        scratch_shapes=[pltpu.VMEM((N_PAIRS, PAIR, PAIR), F32)],
        compiler_params=_cparams(("arbitrary", "arbitrary")),
        name="rwkv7_bwd" if rev else "rwkv7_fwd",
    )(pmix, w2, w0, a2, a0, k_k, k_a)


def _gla_kernel(rev, prec, n_ctx_chunks, q_ref, k_ref, v0_ref, v1_ref, ad_ref, ctx_ref, aup_ref, ab_ref, o_ref, st_scr):
    i = pl.program_id(1)

    @pl.when(i == 0)
    def _():
        st_scr[...] = jnp.zeros_like(st_scr)

    is_ctx = i < n_ctx_chunks
    pc = ctx_ref[0]
    q = jnp.where(is_ctx, pc[:, 0:GLA_BLK], q_ref[0])
    k = jnp.where(is_ctx, pc[:, GLA_BLK:2 * GLA_BLK], k_ref[0])
    v = jnp.concatenate([jnp.where(is_ctx, pc[:, 2 * GLA_BLK:3 * GLA_BLK], v0_ref[0]),
                         jnp.where(is_ctx, pc[:, 3 * GLA_BLK:4 * GLA_BLK], v1_ref[0])], 1)
    d_off = GLA_GATE_LORA if rev else 0
    ad = jnp.where(is_ctx, pc[:, OFF_GLA_AD:OFF_GLA_AD + LANES], ad_ref[0][:, 0:LANES])
    ad = ad[:, d_off:d_off + GLA_GATE_LORA]

    la = -_softplus(-(_mm(ad, aup_ref[...], HIGHEST) + ab_ref[...])) * (1.0 / GLA_TAU)
    ci = lax.broadcasted_iota(jnp.int32, (CHUNK, CHUNK), 0)
    cj = lax.broadcasted_iota(jnp.int32, (CHUNK, CHUNK), 1)
    tri = jnp.where((cj >= ci) if rev else (cj <= ci), 1.0, 0.0).astype(F32)
    cum = _mm(tri, la, HIGHEST)
    total = cum[0:1] if rev else cum[CHUNK - 1:CHUNK]

    dk, dv, sb = GLA_KEY_DIM, GLA_VAL_DIM, GLA_SUB
    n_sb = CHUNK // sb
    row = lax.broadcasted_iota(jnp.int32, (CHUNK, dk), 0)
    row_in = row & (sb - 1)
    arow = lax.broadcasted_iota(jnp.int32, (CHUNK, CHUNK), 0)
    acol = lax.broadcasted_iota(jnp.int32, (CHUNK, CHUNK), 1)
    scale = GLA_KEY_DIM ** -0.5

    heads = range(GLA_HEADS)
    ksl = [slice(h * dk, (h + 1) * dk) for h in heads]
    qh = [q[:, s_] * scale for s_ in ksl]
    kh = [k[:, s_] for s_ in ksl]
    bh = [cum[:, s_] for s_ in ksl]
    lah = [la[:, s_] for s_ in ksl]
    toth = [total[:, s_] for s_ in ksl]
    vh = [v[:, h * dv:(h + 1) * dv] for h in heads]
    st = [st_scr[h] for h in heads]

    o_inter = [_mm_nt(qh[h] * jnp.exp(bh[h]), st[h], prec) for h in heads]
    st_new = [st[h] * jnp.exp(toth[h]) + _mm_tn(vh[h], kh[h] * jnp.exp(toth[h] - bh[h]), prec) for h in heads]

    off_rows = [[] for _ in heads]
    for blk in range(n_sb):
        rs = slice(blk * sb, (blk + 1) * sb)
        first = blk * sb + (sb - 1 if rev else 0)
        is_first_blk = (blk == n_sb - 1) if rev else (blk == 0)
        before = (row >= (blk + 1) * sb) if rev else (row < blk * sb)
        for h in heads:
            if is_first_blk:
                off_rows[h].append(jnp.zeros((sb, CHUNK), F32))
                continue
            beta = bh[h][first:first + 1] - lah[h][first:first + 1]
            qs = qh[h][rs] * jnp.exp(bh[h][rs] - beta)
            ksc = jnp.where(before, kh[h] * jnp.exp(jnp.minimum(beta - bh[h], 0.0)), 0.0)
            off_rows[h].append(_mm_nt(qs, ksc, prec))
    att = [jnp.concatenate(off_rows[h], 0) for h in heads]

    for s in range(sb):
        pick = lambda z: jnp.concatenate(
            [jnp.broadcast_to(z[blk * sb + s:blk * sb + s + 1], (sb, dk)) for blk in range(n_sb)], 0)
        ok = (row_in <= s) if rev else (row_in >= s)
        tgt = jnp.logical_and(acol == (arow & ~(sb - 1)) + s,
                              ((arow & (sb - 1)) <= s) if rev else ((arow & (sb - 1)) >= s))
        for h in heads:
            e = jnp.exp(jnp.where(ok, bh[h] - pick(bh[h]), 0.0))
            col = jnp.sum(jnp.where(ok, qh[h] * pick(kh[h]) * e, 0.0), axis=-1, keepdims=True)
            att[h] = jnp.where(tgt, col, att[h])

    for h in heads:
        o_ref[0, :, h * dv:(h + 1) * dv] = o_inter[h] + _mm(att[h], vh[h], prec)
        st_scr[h] = st_new[h]


def _gla_scan(p_lat, p_ctx, alpha_up, alpha_b, rev, prec):
    b, t, w = p_lat.shape
    assert w == GLA_PAD and t == GRID_W * CHUNK
    n_cols = GRID_W
    n_ctx_chunks = p_ctx.shape[1] // CHUNK
    n_steps = n_ctx_chunks + n_cols
    nblk = w // GLA_BLK
    lat_view = p_lat.reshape(b, t // GRID_W, GRID_W * w)

    def col_of(i):
        if not rev:
            return jnp.maximum(i - n_ctx_chunks, 0)
        return jnp.where(i < n_ctx_chunks, n_cols - 1, n_steps - 1 - i)

    def ctx_of(i):
        if not rev:
            return jnp.minimum(i, n_ctx_chunks - 1)
        return jnp.maximum(n_ctx_chunks - 1 - i, 0)

    def lat_spec(m):
        return pl.BlockSpec((1, CHUNK, GLA_BLK), lambda bi, i: (bi, 0, col_of(i) * nblk + m))

    vec = lambda bi, i: (0, 0)
    out = pl.pallas_call(
        functools.partial(_gla_kernel, rev, prec, n_ctx_chunks),
        grid=(b, n_steps),
        in_specs=[lat_spec(0), lat_spec(1), lat_spec(2), lat_spec(3), lat_spec(6),
                  pl.BlockSpec((1, CHUNK, w), lambda bi, i: (bi, ctx_of(i), 0)),
                  pl.BlockSpec((GLA_GATE_LORA, GLA_QK_WIDTH), vec),
                  pl.BlockSpec((1, GLA_QK_WIDTH), vec)],
        out_specs=pl.BlockSpec((1, CHUNK, GLA_V_WIDTH), lambda bi, i: (bi, 0, col_of(i))),
        out_shape=jax.ShapeDtypeStruct((b, t // GRID_W, GRID_W * GLA_V_WIDTH), F32),
        scratch_shapes=[pltpu.VMEM((GLA_HEADS, GLA_VAL_DIM, GLA_KEY_DIM), F32)],
        compiler_params=_cparams(("arbitrary", "arbitrary")),
        name="gla_bwd" if rev else "gla_fwd",
    )(lat_view, lat_view, lat_view, lat_view, lat_view, p_ctx, alpha_up, alpha_b)
    return out.reshape(b, t, GLA_V_WIDTH)


def _rwkv_post_kernel(yf_ref, yb_ref, pm_ref, a2_ref, a0_ref, g2_ref, ka_ref, rk_ref, lng_ref, lnb_ref, o_ref):
    p = pm_ref[0]
    r = p[:, 0:RWKV_WIDTH]
    k = p[:, RWKV_WIDTH:2 * RWKV_WIDTH]
    v = p[:, 2 * RWKV_WIDTH:3 * RWKV_WIDTH]
    ad_f = p[:, OFF_AD:OFF_AD + ICL_LORA]
    ad_b = p[:, OFF_AD + ICL_LORA:OFF_AD + 2 * ICL_LORA]
    gd = p[:, OFF_GD:OFF_GD + GATE_LORA]
    ka = ka_ref[...]
    a_f = _sigmoid(a0_ref[0] + _mm(ad_f, a2_ref[0], HIGHEST))
    a_b = _sigmoid(a0_ref[1] + _mm(ad_b, a2_ref[1], HIGHEST))
    kd_sum = k * (1.0 + (a_f - 1.0) * ka) + k * (1.0 + (a_b - 1.0) * ka)
    gate = _mm(_sigmoid(gd), g2_ref[...], HIGHEST)
    rkk = r * kd_sum * rk_ref[...]
    ysum = yf_ref[0] + yb_ref[0]
    hsum = _head_sum_matrix()
    inv_n = 1.0 / RWKV_HEAD_DIM
    for pr in range(N_PAIRS):
        sl = slice(pr * PAIR, (pr + 1) * PAIR)
        ys = ysum[:, sl]
        mean = _head_sum(ys, hsum) * inv_n
        dlt = ys - mean
        var = _head_sum(dlt * dlt, hsum) * inv_n
        gn = dlt * lax.rsqrt(var + RWKV_GN_EPS) * lng_ref[:, sl] + lnb_ref[:, sl]
        bonus = _head_sum(rkk[:, sl], hsum) * v[:, sl]
        o_ref[0, :, sl] = ((gn + bonus) * gate[:, sl]).astype(o_ref.dtype)


def _rwkv_post(y_f, y_b, pmix, n_ctx, a2, a0, g2, k_a, r_k, ln_g, ln_b, tm=256):
    b, t, w = y_f.shape
    off = n_ctx // tm
    row = lambda bi, i: (bi, i, 0)
    vec = lambda bi, i: (0, 0)
    vec3 = lambda bi, i: (0, 0, 0)
    return pl.pallas_call(
        _rwkv_post_kernel,
        grid=(b, t // tm),
        in_specs=[pl.BlockSpec((1, tm, w), row),
                  pl.BlockSpec((1, tm, w), row),
                  pl.BlockSpec((1, tm, pmix.shape[2]), lambda bi, i: (bi, i + off, 0)),
                  pl.BlockSpec((2, ICL_LORA, w), vec3),
                  pl.BlockSpec((2, 1, w), vec3),
                  pl.BlockSpec((GATE_LORA, w), vec),
                  pl.BlockSpec((1, w), vec), pl.BlockSpec((1, w), vec),
                  pl.BlockSpec((1, w), vec), pl.BlockSpec((1, w), vec)],
        out_specs=pl.BlockSpec((1, tm, w), row),
        out_shape=jax.ShapeDtypeStruct((b, t, w), BF16),
        compiler_params=_cparams(("arbitrary", "arbitrary")),
        name="rwkv_post",
    )(y_f, y_b, pmix, a2, a0, g2, k_a, r_k, ln_g, ln_b)


def _gla_post_kernel(of_ref, ob_ref, g_ref, ng_ref, o_ref):
    o = of_ref[0] + ob_ref[0]
    g = g_ref[0]
    dv = GLA_VAL_DIM
    for h in range(GLA_HEADS):
        sl = slice(h * dv, (h + 1) * dv)
        oh = o[:, sl]
        oh = oh * lax.rsqrt(jnp.mean(oh * oh, axis=-1, keepdims=True) + GLA_NORM_EPS) * ng_ref[:, sl]
        gh = g[:, sl]
        o_ref[0, :, sl] = (oh * (gh * _sigmoid(gh))).astype(o_ref.dtype)


def _gla_post(o_f, o_b, p_gla, norm_g, tm=256):
    b, t, w = o_f.shape
    row = lambda bi, i: (bi, i, 0)
    return pl.pallas_call(
        _gla_post_kernel,
        grid=(b, t // tm),
        in_specs=[pl.BlockSpec((1, tm, w), row),
                  pl.BlockSpec((1, tm, w), row),
                  pl.BlockSpec((1, tm, w), lambda bi, i: (bi, i, 2)),
                  pl.BlockSpec((1, w), lambda bi, i: (0, 0))],
        out_specs=pl.BlockSpec((1, tm, w), row),
        out_shape=jax.ShapeDtypeStruct((b, t, w), BF16),
        compiler_params=_cparams(("arbitrary", "arbitrary")),
        name="gla_post",
    )(o_f, o_b, p_gla, norm_g)


def _merge_kernel(ya_ref, yb_ref, wr_ref, wg_ref, ga_ref, gb_ref, o_ref):
    ma = _mm(ya_ref[0], wr_ref[...])
    mb = _mm(yb_ref[0], wg_ref[...])
    o_ref[0] = (_sigmoid(ga_ref[0]) * ma + _sigmoid(gb_ref[0]) * mb).astype(o_ref.dtype)


def _merge(ya, yb, w_r, w_g, p_gate, tm=512, tn=1024):
    b, t, w = ya.shape
    d = w_r.shape[1]
    nj = d // tn
    return pl.pallas_call(
        _merge_kernel,
        grid=(b, t // tm, nj),
        in_specs=[pl.BlockSpec((1, tm, w), lambda bi, i, j: (bi, i, 0)),
                  pl.BlockSpec((1, tm, w), lambda bi, i, j: (bi, i, 0)),
                  pl.BlockSpec((w, tn), lambda bi, i, j: (0, j)),
                  pl.BlockSpec((w, tn), lambda bi, i, j: (0, j)),
                  pl.BlockSpec((1, tm, tn), lambda bi, i, j: (bi, i, j)),
                  pl.BlockSpec((1, tm, tn), lambda bi, i, j: (bi, i, j + nj))],
        out_specs=pl.BlockSpec((1, tm, tn), lambda bi, i, j: (bi, i, j)),
        out_shape=jax.ShapeDtypeStruct((b, t, d), BF16),
        compiler_params=_cparams(("arbitrary", "arbitrary", "arbitrary")),
        name="merge_branches",
    )(ya, yb, w_r, w_g, p_gate, p_gate)


def _mix_out_kernel(m_ref, w_ref, x_ref, gate_ref, npost_ref, npre_ref, sh_ref, sc_ref, x1_ref, h_ref):
    z = _mm(m_ref[0], w_ref[...])
    z = z * lax.rsqrt(jnp.mean(z * z, axis=-1, keepdims=True) + NORM_EPS) * npost_ref[...]
    x1 = x_ref[0] + gate_ref[0] * z
    x1_ref[0] = x1
    y = x1 * lax.rsqrt(jnp.mean(x1 * x1, axis=-1, keepdims=True) + NORM_EPS) * npre_ref[...]
    h_ref[0] = (y * (1.0 + sc_ref[0]) + sh_ref[0]).astype(h_ref.dtype)


def _mix_out(m, w_out, x, gate, n_post, n_pre, shift, scale, tm=256):
    b, t, d = x.shape
    row = lambda bi, i: (bi, i, 0)
    per_b = lambda bi, i: (bi, 0, 0)
    vec = lambda bi, i: (0, 0)
    return pl.pallas_call(
        _mix_out_kernel,
        grid=(b, t // tm),
        in_specs=[pl.BlockSpec((1, tm, d), row),
                  pl.BlockSpec((d, d), vec),
                  pl.BlockSpec((1, tm, d), row),
                  pl.BlockSpec((1, 1, d), per_b),
                  pl.BlockSpec((1, d), vec), pl.BlockSpec((1, d), vec),
                  pl.BlockSpec((1, 1, d), per_b), pl.BlockSpec((1, 1, d), per_b)],
        out_specs=[pl.BlockSpec((1, tm, d), row), pl.BlockSpec((1, tm, d), row)],
        out_shape=[jax.ShapeDtypeStruct((b, t, d), F32), jax.ShapeDtypeStruct((b, t, d), BF16)],
        compiler_params=_cparams(("arbitrary", "arbitrary")),
        name="mix_out",
    )(m, w_out, x, gate, n_post, n_pre, shift, scale)


def _ffn_up_kernel(h_ref, wg_ref, wu_ref, o_ref):
    h = h_ref[0]
    a = _mm(h, wg_ref[...])
    u = _mm(h, wu_ref[...])
    o_ref[0] = (a * _sigmoid(a) * u).astype(o_ref.dtype)


def _ffn_up(h, w_gate, w_up, tm=1024, tn=512):
    b, t, d = h.shape
    f = w_gate.shape[1]
    return pl.pallas_call(
        _ffn_up_kernel,
        grid=(b, t // tm, f // tn),
        in_specs=[pl.BlockSpec((1, tm, d), lambda bi, i, j: (bi, i, 0)),
                  pl.BlockSpec((d, tn), lambda bi, i, j: (0, j)),
                  pl.BlockSpec((d, tn), lambda bi, i, j: (0, j))],
        out_specs=pl.BlockSpec((1, tm, tn), lambda bi, i, j: (bi, i, j)),
        out_shape=jax.ShapeDtypeStruct((b, t, f), BF16),
        compiler_params=_cparams(("arbitrary", "arbitrary", "arbitrary")),
        name="ffn_up",
    )(h, w_gate, w_up)


def _ffn_down_kernel(h_ref, w_ref, x_ref, gate_ref, npost_ref, o_ref, acc):
    kk = pl.program_id(2)

    @pl.when(kk == 0)
    def _():
        acc[...] = jnp.zeros_like(acc)

    acc[...] += _mm(h_ref[0], w_ref[...])

    @pl.when(kk == pl.num_programs(2) - 1)
    def _():
        z = acc[...]
        z = z * lax.rsqrt(jnp.mean(z * z, axis=-1, keepdims=True) + NORM_EPS) * npost_ref[...]
        o_ref[0] = x_ref[0] + gate_ref[0] * z


def _ffn_down(h, w_down, x1, gate, n_post, tm=512, tk=1408):
    b, t, f = h.shape
    d = w_down.shape[1]
    return pl.pallas_call(
        _ffn_down_kernel,
        grid=(b, t // tm, f // tk),
        in_specs=[pl.BlockSpec((1, tm, tk), lambda bi, i, k: (bi, i, k)),
                  pl.BlockSpec((tk, d), lambda bi, i, k: (k, 0)),
                  pl.BlockSpec((1, tm, d), lambda bi, i, k: (bi, i, 0)),
                  pl.BlockSpec((1, 1, d), lambda bi, i, k: (bi, 0, 0)),
                  pl.BlockSpec((1, d), lambda bi, i, k: (0, 0))],
        out_specs=pl.BlockSpec((1, tm, d), lambda bi, i, k: (bi, i, 0)),
        out_shape=jax.ShapeDtypeStruct((b, t, d), F32),
        scratch_shapes=[pltpu.VMEM((tm, d), F32)],
        compiler_params=_cparams(("arbitrary", "arbitrary", "arbitrary")),
        name="ffn_down",
    )(h, w_down, x1, gate, n_post)


def _pad_cols(w, n):
    return jnp.pad(w, ((0, 0), (0, n - w.shape[1])))


def kernel(x, c, ctx, c_ctx, ada_w, ada_b, norm_pre_mix, norm_post_mix, norm_pre_ffn, norm_post_ffn, w_in, shift_mu, rwkv_w0, rwkv_w2, rwkv_a0, rwkv_a2, rwkv_g2, rwkv_k_k, rwkv_k_a, rwkv_r_k, rwkv_ln_g, rwkv_ln_b, w_rwkv_up, gla_alpha_up, gla_alpha_b, gla_norm_g, w_gla_up, w_out, ffn_w_gate, ffn_w_up, ffn_w_down):
    assert ada_w.shape[0] == 1, "single trunk layer"
    bsz, seq, d = x.shape
    n_ctx = ctx.shape[1]
    prec = BF16

    cvecs = jnp.concatenate([c, c_ctx[None, :], jnp.zeros((8 - bsz - 1, d), F32)], 0)
    mod = _modulation(cvecs, ada_w[0], ada_b[0])
    mod_x = mod[:bsz].reshape(bsz, 6, 1, d)
    shx1, scx1, gx1, shx2, scx2, gx2 = (mod_x[:, i] for i in range(6))
    mod_c = jnp.broadcast_to(mod[bsz].reshape(1, 6, 1, d), (bsz, 6, 1, d))
    shc1, scc1 = mod_c[:, 0], mod_c[:, 1]

    w_all = w_in[0]
    mix_in = RWKV_IN + GLA_IN
    w_rwkv = _pad_cols(w_all[:, :RWKV_IN], RWKV_PAD).astype(BF16)
    w_gla = _pad_cols(w_all[:, RWKV_IN:mix_in], GLA_PAD).astype(BF16)
    w_gate = w_all[:, mix_in:].astype(BF16)
    mu = _pad_cols(shift_mu, RWKV_PAD)
    n_pre = norm_pre_mix

    px_rwkv = _project(x, shx1, scx1, n_pre, w_rwkv, 1024, 512, "proj_rwkv")
    px_gla = _project(x, shx1, scx1, n_pre, w_gla, 1024, 512, "proj_gla")
    px_gate = _project(x, shx1, scx1, n_pre, w_gate, 1024, 1024, "proj_gate")
    pc_rwkv = _project(ctx, shc1, scc1, n_pre, w_rwkv, n_ctx, 512, "proj_rwkv_ctx")
    pc_gla = _project(ctx, shc1, scc1, n_pre, w_gla, n_ctx, 512, "proj_gla_ctx")

    pmix = _shiftmix(px_rwkv, pc_rwkv, mu)
    n_ctx_chunks = n_ctx // CHUNK
    y_dir = []
    for dr in range(2):
        y_dir.append(_rwkv_scan(pmix, rwkv_w2[0, dr], rwkv_w0[0, dr][None, :], rwkv_a2[0, dr], rwkv_a0[0, dr][None, :],
                                rwkv_k_k, rwkv_k_a, dr == 1, n_ctx_chunks, prec))
    ya = _rwkv_post(y_dir[0], y_dir[1], pmix, n_ctx, rwkv_a2[0], rwkv_a0[0][:, None, :], rwkv_g2[0], rwkv_k_a,
                    rwkv_r_k.reshape(1, RWKV_WIDTH), rwkv_ln_g, rwkv_ln_b)

    o_dir = [_gla_scan(px_gla, pc_gla, gla_alpha_up[0, dr], gla_alpha_b[0, dr][None, :], dr == 1, prec) for dr in range(2)]
    yb = _gla_post(o_dir[0], o_dir[1], px_gla, gla_norm_g)

    m = _merge(ya, yb, w_rwkv_up[0].astype(BF16), w_gla_up[0].astype(BF16), px_gate)
    x1, h2 = _mix_out(m, w_out[0].astype(BF16), x, gx1, norm_post_mix, norm_pre_ffn, shx2, scx2)
    hf = _ffn_up(h2, ffn_w_gate[0].astype(BF16), ffn_w_up[0].astype(BF16))
    return _ffn_down(hf, ffn_w_down[0].astype(BF16), x1, gx2, norm_post_ffn)
```

```python
import functools

import jax
import jax.numpy as jnp
from jax import lax
from jax.experimental import pallas as pl
from jax.experimental.pallas import tpu as pltpu

F32 = jnp.float32
BF16 = jnp.bfloat16
HIGHEST = lax.Precision.HIGHEST

LANES = 128
VMEM_LIMIT_BYTES = 56 * 1024 * 1024

GRID_W = 64
CHUNK = 64
RWKV_HEADS, RWKV_HEAD_DIM = 16, 64
RWKV_WIDTH = RWKV_HEADS * RWKV_HEAD_DIM
DECAY_LORA = ICL_LORA = 96
GATE_LORA = 64
RWKV_GN_EPS = 64e-5
GLA_HEADS, GLA_KEY_DIM, GLA_VAL_DIM = 4, 128, 256
GLA_QK_WIDTH = GLA_HEADS * GLA_KEY_DIM
GLA_V_WIDTH = GLA_HEADS * GLA_VAL_DIM
GLA_GATE_LORA = 16
GLA_TAU = 16.0
GLA_NORM_EPS = 1e-5
GLA_SUB = 16
NORM_EPS = 1e-6
PROLOGUE_ROWS = 256

RWKV_IN = 3 * RWKV_WIDTH + 2 * DECAY_LORA + 2 * ICL_LORA + GATE_LORA
RWKV_PAD = 3584
OFF_WD = 3 * RWKV_WIDTH
OFF_AD = OFF_WD + 2 * DECAY_LORA
OFF_GD = OFF_AD + 2 * ICL_LORA
GLA_IN = 2 * GLA_QK_WIDTH + 2 * GLA_V_WIDTH + 2 * GLA_GATE_LORA
GLA_BLK = 512
GLA_PAD = 7 * GLA_BLK
OFF_GLA_AD = 6 * GLA_BLK

PAIR = 2 * RWKV_HEAD_DIM
N_PAIRS = RWKV_HEADS // 2


def _cparams(semantics):
    return pltpu.CompilerParams(dimension_semantics=semantics, vmem_limit_bytes=VMEM_LIMIT_BYTES)


def _sigmoid(z):
    return 1.0 / (1.0 + jnp.exp(-z))


def _softplus(z):
    return jnp.maximum(z, 0.0) + jnp.log(1.0 + jnp.exp(-jnp.abs(z)))


def _dot(a, b, dims, precision):
    if precision is BF16:
        a, b, precision = a.astype(BF16), b.astype(BF16), None
    return lax.dot_general(a, b, (dims, ((), ())), precision=precision, preferred_element_type=F32)


def _mm(a, b, precision=None):
    return _dot(a, b, ((1,), (0,)), precision)


def _mm_nt(a, b, precision=None):
    return _dot(a, b, ((1,), (1,)), precision)


def _mm_tn(a, b, precision=None):
    return _dot(a, b, ((0,), (0,)), precision)


def _split2(z):
    hi = z.astype(BF16)
    return hi, (z - hi.astype(F32)).astype(BF16)


def _mm_split(a, b):
    ah, al = _split2(a)
    bh, bl = _split2(b)
    return (_mm(al, bh) + _mm(ah, bl)) + _mm(ah, bh)


def _mm_exact_lhs(e, b):
    hi = b.astype(BF16)
    r = b - hi.astype(F32)
    mid = r.astype(BF16)
    lo = (r - mid.astype(F32)).astype(BF16)
    e = e.astype(BF16)
    return (_mm(e, lo) + _mm(e, mid)) + _mm(e, hi)


def _mod_kernel(c_ref, w_ref, b_ref, o_ref):
    s = c_ref[...]
    s = s * _sigmoid(s)
    o_ref[...] = _mm_split(s, w_ref[...]) + b_ref[...]


def _modulation(cvecs, ada_w, ada_b, tn=1024):
    m, d = cvecs.shape
    n = ada_w.shape[1]
    return pl.pallas_call(
        _mod_kernel,
        grid=(n // tn,),
        in_specs=[pl.BlockSpec((m, d), lambda j: (0, 0)),
                  pl.BlockSpec((d, tn), lambda j: (0, j)),
                  pl.BlockSpec((1, tn), lambda j: (0, j))],
        out_specs=pl.BlockSpec((m, tn), lambda j: (0, j)),
        out_shape=jax.ShapeDtypeStruct((m, n), F32),
        compiler_params=_cparams(("arbitrary",)),
        name="adaln_mod",
    )(cvecs, ada_w, ada_b.reshape(1, n))


def _proj_kernel(x_ref, sh_ref, sc_ref, g_ref, w_ref, o_ref, h_scr):
    @pl.when(pl.program_id(2) == 0)
    def _():
        rows = min(PROLOGUE_ROWS, h_scr.shape[0])

        def body(rb, carry):
            sl = pl.ds(pl.multiple_of(rb * rows, rows), rows)
            x = x_ref[0, sl, :]
            ms = jnp.mean(x * x, axis=-1, keepdims=True)
            y = x * lax.rsqrt(ms + NORM_EPS) * g_ref[...]
            h_scr[sl, :] = (y * (1.0 + sc_ref[0]) + sh_ref[0]).astype(BF16)
            return carry

        lax.fori_loop(0, h_scr.shape[0] // rows, body, 0)

    o_ref[0] = _mm(h_scr[...], w_ref[...])


def _project(x, shift, scale, gain, w, tm, tn, name):
    b, t, d = x.shape
    n = w.shape[1]
    return pl.pallas_call(
        _proj_kernel,
        grid=(b, t // tm, n // tn),
        in_specs=[pl.BlockSpec((1, tm, d), lambda bi, i, j: (bi, i, 0)),
                  pl.BlockSpec((1, 1, d), lambda bi, i, j: (bi, 0, 0)),
                  pl.BlockSpec((1, 1, d), lambda bi, i, j: (bi, 0, 0)),
                  pl.BlockSpec((1, d), lambda bi, i, j: (0, 0)),
                  pl.BlockSpec((d, tn), lambda bi, i, j: (0, j))],
        out_specs=pl.BlockSpec((1, tm, tn), lambda bi, i, j: (bi, i, j)),
        out_shape=jax.ShapeDtypeStruct((b, t, n), F32),
        scratch_shapes=[pltpu.VMEM((tm, d), BF16)],
        compiler_params=_cparams(("arbitrary", "arbitrary", "arbitrary")),
        name=name,
    )(x, shift, scale, gain, w)


def _shiftmix_kernel(n_ctx_blk, n_blk, tb, lat_ref, ctx_ref, lprev_ref, lnext_ref, cprev_ref, cnext_ref, mu_ref, o_ref):
    i = pl.program_id(1)
    is_ctx = i < n_ctx_blk
    p = jnp.where(is_ctx, ctx_ref[0], lat_ref[0])
    first = jnp.logical_or(i == 0, i == n_ctx_blk)
    last = jnp.logical_or(i == n_ctx_blk - 1, i == n_blk - 1)
    hp = jnp.where(is_ctx, cprev_ref[0, 7:8, :], lprev_ref[0, 7:8, :])
    hn = jnp.where(is_ctx, cnext_ref[0, 0:1, :], lnext_ref[0, 0:1, :])
    hp = jnp.where(first, 0.0, hp)
    hn = jnp.where(last, 0.0, hn)
    row = lax.broadcasted_iota(jnp.int32, p.shape, 0)
    prev = jnp.where(row == 0, hp, pltpu.roll(p, 1, 0))
    nxt = jnp.where(row == tb - 1, hn, pltpu.roll(p, tb - 1, 0))
    o_ref[0] = p + mu_ref[...] * (0.5 * (prev + nxt) - p)


def _shiftmix(p_lat, p_ctx, mu, tb=128):
    b, t, w = p_lat.shape
    tc = p_ctx.shape[1]
    n_ctx_blk, n_lat_blk = tc // tb, t // tb
    n_blk = n_ctx_blk + n_lat_blk
    r8 = tb // 8

    def lat_main(bi, i):
        return (bi, jnp.maximum(i - n_ctx_blk, 0), 0)

    def ctx_main(bi, i):
        return (bi, jnp.minimum(i, n_ctx_blk - 1), 0)

    def lat_prev(bi, i):
        return (bi, jnp.maximum((i - n_ctx_blk) * r8 - 1, 0), 0)

    def lat_next(bi, i):
        return (bi, jnp.clip((i - n_ctx_blk + 1) * r8, 0, t // 8 - 1), 0)

    def ctx_prev(bi, i):
        return (bi, jnp.clip(i * r8 - 1, 0, tc // 8 - 1), 0)

    def ctx_next(bi, i):
        return (bi, jnp.minimum((i + 1) * r8, tc // 8 - 1), 0)

    return pl.pallas_call(
        functools.partial(_shiftmix_kernel, n_ctx_blk, n_blk, tb),
        grid=(b, n_blk),
        in_specs=[pl.BlockSpec((1, tb, w), lat_main),
                  pl.BlockSpec((1, tb, w), ctx_main),
                  pl.BlockSpec((1, 8, w), lat_prev),
                  pl.BlockSpec((1, 8, w), lat_next),
                  pl.BlockSpec((1, 8, w), ctx_prev),
                  pl.BlockSpec((1, 8, w), ctx_next),
                  pl.BlockSpec((1, w), lambda bi, i: (0, 0))],
        out_specs=pl.BlockSpec((1, tb, w), lambda bi, i: (bi, i, 0)),
        out_shape=jax.ShapeDtypeStruct((b, tc + t, w), F32),
        compiler_params=_cparams(("arbitrary", "arbitrary")),
        name="rwkv_shiftmix",
    )(p_lat, p_ctx, p_lat, p_lat, p_ctx, p_ctx, mu)


def _head_sum_matrix():
    r = lax.broadcasted_iota(jnp.int32, (PAIR, PAIR), 0)
    c = lax.broadcasted_iota(jnp.int32, (PAIR, PAIR), 1)
    return jnp.where((r >> 6) == (c >> 6), 1.0, 0.0).astype(BF16)


def _head_sum(z, hsum):
    hi = z.astype(BF16)
    lo = (z - hi.astype(F32)).astype(BF16)
    return _mm(hi, hsum) + _mm(lo, hsum)


def _rwkv_kernel(rev, prec, pm_ref, w2_ref, w0_ref, a2_ref, a0_ref, kk_ref, ka_ref, y_ref, st_scr):
    hd = RWKV_HEAD_DIM
    c2 = 2 * CHUNK

    @pl.when(pl.program_id(1) == 0)
    def _():
        st_scr[...] = jnp.zeros_like(st_scr)

    p = pm_ref[0]
    r = p[:, 0:RWKV_WIDTH]
    k = p[:, RWKV_WIDTH:2 * RWKV_WIDTH]
    v = p[:, 2 * RWKV_WIDTH:3 * RWKV_WIDTH]
    d_off = DECAY_LORA if rev else 0
    wd = p[:, OFF_WD + d_off:OFF_WD + d_off + DECAY_LORA]
    ad = p[:, OFF_AD + d_off:OFF_AD + d_off + ICL_LORA]

    w_log = -_softplus(-(w0_ref[...] + _mm_split(jnp.tanh(wd), w2_ref[...]))) - 0.5
    lw = -jnp.exp(w_log)
    a = _sigmoid(a0_ref[...] + _mm(ad, a2_ref[...], BF16))
    kk_raw = k * kk_ref[...]
    kd = k * (1.0 + (a - 1.0) * ka_ref[...])

    ri = lax.broadcasted_iota(jnp.int32, (c2, PAIR), 0)
    li = lax.broadcasted_iota(jnp.int32, (c2, PAIR), 1)
    rt, lt = ri & (CHUNK - 1), li & (hd - 1)
    same_head = (ri >> 6) == (li >> 6)
    strict = (lt > rt) if rev else (lt < rt)
    mask_n = jnp.logical_and(same_head, strict)
    mask_k = jnp.logical_and(jnp.logical_not(same_head), strict)
    ident = ri == li
    eye = jnp.where(ident, 1.0, 0.0).astype(F32)
    rc = lax.broadcasted_iota(jnp.int32, (CHUNK, PAIR), 0)
    lc = lax.broadcasted_iota(jnp.int32, (CHUNK, PAIR), 1)
    incl_c = ((lc & (hd - 1)) >= rc) if rev else ((lc & (hd - 1)) <= rc)
    lane_e = lc < hd
    hsum = _head_sum_matrix()

    ci = lax.broadcasted_iota(jnp.int32, (CHUNK, CHUNK), 0)
    cj = lax.broadcasted_iota(jnp.int32, (CHUNK, CHUNK), 1)
    tri = jnp.where((cj >= ci) if rev else (cj <= ci), 1.0, 0.0).astype(F32)
    cum = _mm_exact_lhs(tri, lw)
    total = cum[0:1] if rev else cum[CHUNK - 1:CHUNK]
    e_prev = jnp.exp(cum - lw)
    e_neg = jnp.exp(-cum)
    e_pos = jnp.exp(cum)
    e_rest = jnp.exp(total - cum)
    p_end = jnp.exp(total)

    def split(z):
        ze = jnp.where(lane_e, z, 0.0)
        return ze, z - ze

    pairs = range(N_PAIRS)
    sls = [slice(pr * PAIR, (pr + 1) * PAIR) for pr in pairs]
    cat0 = lambda *z: jnp.concatenate(z, 0)

    kkr = [kk_raw[:, sl] for sl in sls]
    nrm2 = [_head_sum(z * z, hsum) for z in kkr]
    kk = [z / jnp.maximum(jnp.sqrt(n), 1e-12) for z, n in zip(kkr, nrm2)]
    bb = [z * a[:, sl] for z, sl in zip(kk, sls)]
    at = [split(z * e_prev[:, sl]) for z, sl in zip(kk, sls)]
    bt = [z * e_neg[:, sl] for z, sl in zip(bb, sls)]
    kt = [kd[:, sl] * e_neg[:, sl] for sl in sls]
    rt_ = [r[:, sl] * e_pos[:, sl] for sl in sls]
    rs = [split(z) for z in rt_]
    vs = [split(v[:, sl]) for sl in sls]
    bh = [cat0(*split(z * e_rest[:, sl])) for z, sl in zip(bb, sls)]
    kh = [cat0(*split(kd[:, sl] * e_rest[:, sl])) for sl in sls]
    v_swap = [cat0(vo, ve) for ve, vo in vs]
    v_stack = [cat0(ve, vo) for ve, vo in vs]

    g_e = [_mm_nt(cat0(at[i][0], rs[i][0]), cat0(bt[i], kt[i]), prec) for i in pairs]
    g_o = [_mm_nt(cat0(at[i][1], rs[i][1]), cat0(kt[i], bt[i]), prec) for i in pairs]
    g_top = [cat0(g_e[i][0:CHUNK], g_o[i][0:CHUNK]) for i in pairs]
    nbd = [jnp.where(mask_n, z, 0.0) for z in g_top]
    aak = [jnp.where(mask_k, z, 0.0) for z in g_top]
    rab = [jnp.where(incl_c, jnp.where(lane_e, g_e[i][CHUNK:c2], g_o[i][CHUNK:c2]), 0.0) for i in pairs]
    rak = [jnp.where(incl_c, jnp.where(lane_e, g_o[i][CHUNK:c2], g_e[i][CHUNK:c2]), 0.0) for i in pairs]

    x = [_mm(aak[i], v_swap[i], prec) for i in pairs]
    n2 = [_mm(z, z, prec) for z in nbd]
    y0b = [_mm(rak[i], v_swap[i], prec) for i in pairs]
    n4 = [_mm(z, z, prec) for z in n2]
    imn = [eye - z for z in nbd]
    p1 = [imn[i] + _mm(imn[i], n2[i], prec) for i in pairs]
    n8 = [_mm(z, z, prec) for z in n4]
    nb = [_mm_tn(kh[i], v_stack[i], prec) for i in pairs]
    n16 = [_mm(z, z, prec) for z in n8]
    p2 = [eye + n4[i] + n8[i] + _mm(n4[i], n8[i], prec) for i in pairs]
    n32 = [_mm(z, z, prec) for z in n16]
    p12 = [_mm(p1[i], p2[i], prec) for i in pairs]
    p3 = [eye + n16[i] + n32[i] + _mm(n16[i], n32[i], prec) for i in pairs]
    tinv = [_mm(p12[i], p3[i], prec) for i in pairs]
    wu = [-_mm(tinv[i], jnp.concatenate([cat0(*at[i]), x[i]], 1), prec) for i in pairs]
    qy = [_mm(rab[i], wu[i], prec) for i in pairs]
    mn = [_mm_tn(bh[i], wu[i], prec) for i in pairs]
    q = [rt_[i] + qy[i][:, 0:PAIR] for i in pairs]
    m = [jnp.where(ident, p_end[:, sls[i]], 0.0) + mn[i][:, 0:PAIR] for i in pairs]
    qm = [_mm(cat0(q[i], m[i]), st_scr[i], prec) for i in pairs]
    for i in pairs:
        y_ref[0, :, sls[i]] = qm[i][0:CHUNK] + qy[i][:, PAIR:2 * PAIR] + y0b[i]
        st_scr[i] = qm[i][CHUNK:CHUNK + PAIR] + mn[i][:, PAIR:2 * PAIR] + nb[i]


def _rwkv_scan(pmix, w2, w0, a2, a0, k_k, k_a, rev, n_ctx_chunks, prec):
    b, tt, w = pmix.shape
    n_chunks = tt // CHUNK
    n_lat = n_chunks - n_ctx_chunks

    def chunk_of(i):
        if not rev:
            return i
        return jnp.where(i < n_ctx_chunks, n_ctx_chunks - 1 - i, n_chunks + n_ctx_chunks - 1 - i)

    def out_of(i):
        if not rev:
            return jnp.maximum(i - n_ctx_chunks, 0)
        return jnp.where(i < n_ctx_chunks, n_lat - 1, n_chunks - 1 - i)

    vec = lambda bi, i: (0, 0)
    return pl.pallas_call(
        functools.partial(_rwkv_kernel, rev, prec),
        grid=(b, n_chunks),
        in_specs=[pl.BlockSpec((1, CHUNK, w), lambda bi, i: (bi, chunk_of(i), 0)),
                  pl.BlockSpec((DECAY_LORA, RWKV_WIDTH), vec),
                  pl.BlockSpec((1, RWKV_WIDTH), vec),
                  pl.BlockSpec((ICL_LORA, RWKV_WIDTH), vec),
                  pl.BlockSpec((1, RWKV_WIDTH), vec),
                  pl.BlockSpec((1, RWKV_WIDTH), vec),
                  pl.BlockSpec((1, RWKV_WIDTH), vec)],
        out_specs=pl.BlockSpec((1, CHUNK, RWKV_WIDTH), lambda bi, i: (bi, out_of(i), 0)),
        out_shape=jax.ShapeDtypeStruct((b, n_lat * CHUNK, RWKV_WIDTH), F32),
        scratch_shapes=[pltpu.VMEM((N_PAIRS, PAIR, PAIR), F32)],
        compiler_params=_cparams(("arbitrary", "arbitrary")),
        name="rwkv7_bwd" if rev else "rwkv7_fwd",
    )(pmix, w2, w0, a2, a0, k_k, k_a)


def _gla_kernel(rev, prec, n_ctx_chunks, q_ref, k_ref, v0_ref, v1_ref, ad_ref, ctx_ref, aup_ref, ab_ref, o_ref, st_scr):
    i = pl.program_id(1)

    @pl.when(i == 0)
    def _():
        st_scr[...] = jnp.zeros_like(st_scr)

    is_ctx = i < n_ctx_chunks
    pc = ctx_ref[0]
    q = jnp.where(is_ctx, pc[:, 0:GLA_BLK], q_ref[0])
    k = jnp.where(is_ctx, pc[:, GLA_BLK:2 * GLA_BLK], k_ref[0])
    v = jnp.concatenate([jnp.where(is_ctx, pc[:, 2 * GLA_BLK:3 * GLA_BLK], v0_ref[0]),
                         jnp.where(is_ctx, pc[:, 3 * GLA_BLK:4 * GLA_BLK], v1_ref[0])], 1)
    d_off = GLA_GATE_LORA if rev else 0
    ad = jnp.where(is_ctx, pc[:, OFF_GLA_AD:OFF_GLA_AD + LANES], ad_ref[0][:, 0:LANES])
    ad = ad[:, d_off:d_off + GLA_GATE_LORA]

    la = -_softplus(-(_mm_split(ad, aup_ref[...]) + ab_ref[...])) * (1.0 / GLA_TAU)
    ci = lax.broadcasted_iota(jnp.int32, (CHUNK, CHUNK), 0)
    cj = lax.broadcasted_iota(jnp.int32, (CHUNK, CHUNK), 1)
    tri = jnp.where((cj >= ci) if rev else (cj <= ci), 1.0, 0.0).astype(F32)
    cum = _mm_exact_lhs(tri, la)
    total = cum[0:1] if rev else cum[CHUNK - 1:CHUNK]

    dk, dv, sb = GLA_KEY_DIM, GLA_VAL_DIM, GLA_SUB
    n_sb = CHUNK // sb
    row = lax.broadcasted_iota(jnp.int32, (CHUNK, dk), 0)
    row_in = row & (sb - 1)
    arow = lax.broadcasted_iota(jnp.int32, (CHUNK, CHUNK), 0)
    acol = lax.broadcasted_iota(jnp.int32, (CHUNK, CHUNK), 1)
    scale = GLA_KEY_DIM ** -0.5

    heads = range(GLA_HEADS)
    ksl = [slice(h * dk, (h + 1) * dk) for h in heads]
    qh = [q[:, s_] * scale for s_ in ksl]
    kh = [k[:, s_] for s_ in ksl]
    bh = [cum[:, s_] for s_ in ksl]
    lah = [la[:, s_] for s_ in ksl]
    toth = [total[:, s_] for s_ in ksl]
    vh = [v[:, h * dv:(h + 1) * dv] for h in heads]
    st = [st_scr[h] for h in heads]

    o_inter = [_mm_nt(qh[h] * jnp.exp(bh[h]), st[h], prec) for h in heads]
    st_new = [st[h] * jnp.exp(toth[h]) + _mm_tn(vh[h], kh[h] * jnp.exp(toth[h] - bh[h]), prec) for h in heads]

    off_rows = [[] for _ in heads]
    for blk in range(n_sb):
        rs = slice(blk * sb, (blk + 1) * sb)
        first = blk * sb + (sb - 1 if rev else 0)
        is_first_blk = (blk == n_sb - 1) if rev else (blk == 0)
        before = (row >= (blk + 1) * sb) if rev else (row < blk * sb)
        for h in heads:
            if is_first_blk:
                off_rows[h].append(jnp.zeros((sb, CHUNK), F32))
                continue
            beta = bh[h][first:first + 1] - lah[h][first:first + 1]
            qs = qh[h][rs] * jnp.exp(bh[h][rs] - beta)
            ksc = jnp.where(before, kh[h] * jnp.exp(jnp.minimum(beta - bh[h], 0.0)), 0.0)
            off_rows[h].append(_mm_nt(qs, ksc, prec))
    att = [jnp.concatenate(off_rows[h], 0) for h in heads]

    for s in range(sb):
        pick = lambda z: jnp.concatenate(
            [jnp.broadcast_to(z[blk * sb + s:blk * sb + s + 1], (sb, dk)) for blk in range(n_sb)], 0)
        ok = (row_in <= s) if rev else (row_in >= s)
        tgt = jnp.logical_and(acol == (arow & ~(sb - 1)) + s,
                              ((arow & (sb - 1)) <= s) if rev else ((arow & (sb - 1)) >= s))
        for h in heads:
            e = jnp.exp(jnp.where(ok, bh[h] - pick(bh[h]), 0.0))
            col = jnp.sum(jnp.where(ok, qh[h] * pick(kh[h]) * e, 0.0), axis=-1, keepdims=True)
            att[h] = jnp.where(tgt, col, att[h])

    for h in heads:
        o_ref[0, :, h * dv:(h + 1) * dv] = o_inter[h] + _mm(att[h], vh[h], prec)
        st_scr[h] = st_new[h]


def _gla_scan(p_lat, p_ctx, alpha_up, alpha_b, rev, prec):
    b, t, w = p_lat.shape
    assert w == GLA_PAD and t == GRID_W * CHUNK
    n_cols = GRID_W
    n_ctx_chunks = p_ctx.shape[1] // CHUNK
    n_steps = n_ctx_chunks + n_cols
    nblk = w // GLA_BLK
    lat_view = p_lat.reshape(b, t // GRID_W, GRID_W * w)

    def col_of(i):
        if not rev:
            return jnp.maximum(i - n_ctx_chunks, 0)
        return jnp.where(i < n_ctx_chunks, n_cols - 1, n_steps - 1 - i)

    def ctx_of(i):
        if not rev:
            return jnp.minimum(i, n_ctx_chunks - 1)
        return jnp.maximum(n_ctx_chunks - 1 - i, 0)

    def lat_spec(m):
        return pl.BlockSpec((1, CHUNK, GLA_BLK), lambda bi, i: (bi, 0, col_of(i) * nblk + m))

    vec = lambda bi, i: (0, 0)
    out = pl.pallas_call(
        functools.partial(_gla_kernel, rev, prec, n_ctx_chunks),
        grid=(b, n_steps),
        in_specs=[lat_spec(0), lat_spec(1), lat_spec(2), lat_spec(3), lat_spec(6),
                  pl.BlockSpec((1, CHUNK, w), lambda bi, i: (bi, ctx_of(i), 0)),
                  pl.BlockSpec((GLA_GATE_LORA, GLA_QK_WIDTH), vec),
                  pl.BlockSpec((1, GLA_QK_WIDTH), vec)],
        out_specs=pl.BlockSpec((1, CHUNK, GLA_V_WIDTH), lambda bi, i: (bi, 0, col_of(i))),
        out_shape=jax.ShapeDtypeStruct((b, t // GRID_W, GRID_W * GLA_V_WIDTH), F32),
        scratch_shapes=[pltpu.VMEM((GLA_HEADS, GLA_VAL_DIM, GLA_KEY_DIM), F32)],
        compiler_params=_cparams(("arbitrary", "arbitrary")),
        name="gla_bwd" if rev else "gla_fwd",
    )(lat_view, lat_view, lat_view, lat_view, lat_view, p_ctx, alpha_up, alpha_b)
    return out.reshape(b, t, GLA_V_WIDTH)


def _rwkv_post_kernel(yf_ref, yb_ref, pm_ref, a2_ref, a0_ref, g2_ref, ka_ref, rk_ref, lng_ref, lnb_ref, o_ref):
    p = pm_ref[0]
    r = p[:, 0:RWKV_WIDTH]
    k = p[:, RWKV_WIDTH:2 * RWKV_WIDTH]
    v = p[:, 2 * RWKV_WIDTH:3 * RWKV_WIDTH]
    ad_f = p[:, OFF_AD:OFF_AD + ICL_LORA]
    ad_b = p[:, OFF_AD + ICL_LORA:OFF_AD + 2 * ICL_LORA]
    gd = p[:, OFF_GD:OFF_GD + GATE_LORA]
    ka = ka_ref[...]
    a_f = _sigmoid(a0_ref[0] + _mm(ad_f, a2_ref[0], BF16))
    a_b = _sigmoid(a0_ref[1] + _mm(ad_b, a2_ref[1], BF16))
    kd_sum = k * (1.0 + (a_f - 1.0) * ka) + k * (1.0 + (a_b - 1.0) * ka)
    gate = _mm(_sigmoid(gd), g2_ref[...], BF16)
    rkk = r * kd_sum * rk_ref[...]
    ysum = yf_ref[0] + yb_ref[0]
    hsum = _head_sum_matrix()
    inv_n = 1.0 / RWKV_HEAD_DIM
    for pr in range(N_PAIRS):
        sl = slice(pr * PAIR, (pr + 1) * PAIR)
        ys = ysum[:, sl]
        mean = _head_sum(ys, hsum) * inv_n
        dlt = ys - mean
        var = _head_sum(dlt * dlt, hsum) * inv_n
        gn = dlt * lax.rsqrt(var + RWKV_GN_EPS) * lng_ref[:, sl] + lnb_ref[:, sl]
        bonus = _head_sum(rkk[:, sl], hsum) * v[:, sl]
        o_ref[0, :, sl] = ((gn + bonus) * gate[:, sl]).astype(o_ref.dtype)


def _rwkv_post(y_f, y_b, pmix, n_ctx, a2, a0, g2, k_a, r_k, ln_g, ln_b, tm=256):
    b, t, w = y_f.shape
    off = n_ctx // tm
    row = lambda bi, i: (bi, i, 0)
    vec = lambda bi, i: (0, 0)
    vec3 = lambda bi, i: (0, 0, 0)
    return pl.pallas_call(
        _rwkv_post_kernel,
        grid=(b, t // tm),
        in_specs=[pl.BlockSpec((1, tm, w), row),
                  pl.BlockSpec((1, tm, w), row),
                  pl.BlockSpec((1, tm, pmix.shape[2]), lambda bi, i: (bi, i + off, 0)),
                  pl.BlockSpec((2, ICL_LORA, w), vec3),
                  pl.BlockSpec((2, 1, w), vec3),
                  pl.BlockSpec((GATE_LORA, w), vec),
                  pl.BlockSpec((1, w), vec), pl.BlockSpec((1, w), vec),
                  pl.BlockSpec((1, w), vec), pl.BlockSpec((1, w), vec)],
        out_specs=pl.BlockSpec((1, tm, w), row),
        out_shape=jax.ShapeDtypeStruct((b, t, w), BF16),
        compiler_params=_cparams(("arbitrary", "arbitrary")),
        name="rwkv_post",
    )(y_f, y_b, pmix, a2, a0, g2, k_a, r_k, ln_g, ln_b)


def _gla_post_kernel(of_ref, ob_ref, g_ref, ng_ref, o_ref):
    o = of_ref[0] + ob_ref[0]
    g = g_ref[0]
    dv = GLA_VAL_DIM
    for h in range(GLA_HEADS):
        sl = slice(h * dv, (h + 1) * dv)
        oh = o[:, sl]
        oh = oh * lax.rsqrt(jnp.mean(oh * oh, axis=-1, keepdims=True) + GLA_NORM_EPS) * ng_ref[:, sl]
        gh = g[:, sl]
        o_ref[0, :, sl] = (oh * (gh * _sigmoid(gh))).astype(o_ref.dtype)


def _gla_post(o_f, o_b, p_gla, norm_g, tm=256):
    b, t, w = o_f.shape
    row = lambda bi, i: (bi, i, 0)
    return pl.pallas_call(
        _gla_post_kernel,
        grid=(b, t // tm),
        in_specs=[pl.BlockSpec((1, tm, w), row),
                  pl.BlockSpec((1, tm, w), row),
                  pl.BlockSpec((1, tm, w), lambda bi, i: (bi, i, 2)),
                  pl.BlockSpec((1, w), lambda bi, i: (0, 0))],
        out_specs=pl.BlockSpec((1, tm, w), row),
        out_shape=jax.ShapeDtypeStruct((b, t, w), BF16),
        compiler_params=_cparams(("arbitrary", "arbitrary")),
        name="gla_post",
    )(o_f, o_b, p_gla, norm_g)


def _merge_kernel(ya_ref, yb_ref, wr_ref, wg_ref, ga_ref, gb_ref, o_ref):
    ma = _mm(ya_ref[0], wr_ref[...])
    mb = _mm(yb_ref[0], wg_ref[...])
    o_ref[0] = (_sigmoid(ga_ref[0]) * ma + _sigmoid(gb_ref[0]) * mb).astype(o_ref.dtype)


def _merge(ya, yb, w_r, w_g, p_gate, tm=512, tn=1024):
    b, t, w = ya.shape
    d = w_r.shape[1]
    nj = d // tn
    return pl.pallas_call(
        _merge_kernel,
        grid=(b, t // tm, nj),
        in_specs=[pl.BlockSpec((1, tm, w), lambda bi, i, j: (bi, i, 0)),
                  pl.BlockSpec((1, tm, w), lambda bi, i, j: (bi, i, 0)),
                  pl.BlockSpec((w, tn), lambda bi, i, j: (0, j)),
                  pl.BlockSpec((w, tn), lambda bi, i, j: (0, j)),
                  pl.BlockSpec((1, tm, tn), lambda bi, i, j: (bi, i, j)),
                  pl.BlockSpec((1, tm, tn), lambda bi, i, j: (bi, i, j + nj))],
        out_specs=pl.BlockSpec((1, tm, tn), lambda bi, i, j: (bi, i, j)),
        out_shape=jax.ShapeDtypeStruct((b, t, d), BF16),
        compiler_params=_cparams(("arbitrary", "arbitrary", "arbitrary")),
        name="merge_branches",
    )(ya, yb, w_r, w_g, p_gate, p_gate)


def _mix_out_kernel(m_ref, w_ref, x_ref, gate_ref, npost_ref, npre_ref, sh_ref, sc_ref, x1_ref, h_ref):
    z = _mm(m_ref[0], w_ref[...])
    z = z * lax.rsqrt(jnp.mean(z * z, axis=-1, keepdims=True) + NORM_EPS) * npost_ref[...]
    x1 = x_ref[0] + gate_ref[0] * z
    x1_ref[0] = x1
    y = x1 * lax.rsqrt(jnp.mean(x1 * x1, axis=-1, keepdims=True) + NORM_EPS) * npre_ref[...]
    h_ref[0] = (y * (1.0 + sc_ref[0]) + sh_ref[0]).astype(h_ref.dtype)


def _mix_out(m, w_out, x, gate, n_post, n_pre, shift, scale, tm=256):
    b, t, d = x.shape
    row = lambda bi, i: (bi, i, 0)
    per_b = lambda bi, i: (bi, 0, 0)
    vec = lambda bi, i: (0, 0)
    return pl.pallas_call(
        _mix_out_kernel,
        grid=(b, t // tm),
        in_specs=[pl.BlockSpec((1, tm, d), row),
                  pl.BlockSpec((d, d), vec),
                  pl.BlockSpec((1, tm, d), row),
                  pl.BlockSpec((1, 1, d), per_b),
                  pl.BlockSpec((1, d), vec), pl.BlockSpec((1, d), vec),
                  pl.BlockSpec((1, 1, d), per_b), pl.BlockSpec((1, 1, d), per_b)],
        out_specs=[pl.BlockSpec((1, tm, d), row), pl.BlockSpec((1, tm, d), row)],
        out_shape=[jax.ShapeDtypeStruct((b, t, d), F32), jax.ShapeDtypeStruct((b, t, d), BF16)],
        compiler_params=_cparams(("arbitrary", "arbitrary")),
        name="mix_out",
    )(m, w_out, x, gate, n_post, n_pre, shift, scale)


def _ffn_up_kernel(h_ref, wg_ref, wu_ref, o_ref):
    h = h_ref[0]
    a = _mm(h, wg_ref[...])
    u = _mm(h, wu_ref[...])
    o_ref[0] = (a * _sigmoid(a) * u).astype(o_ref.dtype)


def _ffn_up(h, w_gate, w_up, tm=1024, tn=512):
    b, t, d = h.shape
    f = w_gate.shape[1]
    return pl.pallas_call(
        _ffn_up_kernel,
        grid=(b, t // tm, f // tn),
        in_specs=[pl.BlockSpec((1, tm, d), lambda bi, i, j: (bi, i, 0)),
                  pl.BlockSpec((d, tn), lambda bi, i, j: (0, j)),
                  pl.BlockSpec((d, tn), lambda bi, i, j: (0, j))],
        out_specs=pl.BlockSpec((1, tm, tn), lambda bi, i, j: (bi, i, j)),
        out_shape=jax.ShapeDtypeStruct((b, t, f), BF16),
        compiler_params=_cparams(("arbitrary", "arbitrary", "arbitrary")),
        name="ffn_up",
    )(h, w_gate, w_up)


def _ffn_down_kernel(h_ref, w_ref, x_ref, gate_ref, npost_ref, o_ref, acc):
    kk = pl.program_id(2)

    @pl.when(kk == 0)
    def _():
        acc[...] = jnp.zeros_like(acc)

    acc[...] += _mm(h_ref[0], w_ref[...])

    @pl.when(kk == pl.num_programs(2) - 1)
    def _():
        z = acc[...]
        z = z * lax.rsqrt(jnp.mean(z * z, axis=-1, keepdims=True) + NORM_EPS) * npost_ref[...]
        o_ref[0] = x_ref[0] + gate_ref[0] * z


def _ffn_down(h, w_down, x1, gate, n_post, tm=512, tk=1408):
    b, t, f = h.shape
    d = w_down.shape[1]
    return pl.pallas_call(
        _ffn_down_kernel,
        grid=(b, t // tm, f // tk),
        in_specs=[pl.BlockSpec((1, tm, tk), lambda bi, i, k: (bi, i, k)),
                  pl.BlockSpec((tk, d), lambda bi, i, k: (k, 0)),
                  pl.BlockSpec((1, tm, d), lambda bi, i, k: (bi, i, 0)),
                  pl.BlockSpec((1, 1, d), lambda bi, i, k: (bi, 0, 0)),
                  pl.BlockSpec((1, d), lambda bi, i, k: (0, 0))],
        out_specs=pl.BlockSpec((1, tm, d), lambda bi, i, k: (bi, i, 0)),
        out_shape=jax.ShapeDtypeStruct((b, t, d), F32),
        scratch_shapes=[pltpu.VMEM((tm, d), F32)],
        compiler_params=_cparams(("arbitrary", "arbitrary", "arbitrary")),
        name="ffn_down",
    )(h, w_down, x1, gate, n_post)


def _pad_cols(w, n):
    return jnp.pad(w, ((0, 0), (0, n - w.shape[1])))


def kernel(x, c, ctx, c_ctx, ada_w, ada_b, norm_pre_mix, norm_post_mix, norm_pre_ffn, norm_post_ffn, w_in, shift_mu, rwkv_w0, rwkv_w2, rwkv_a0, rwkv_a2, rwkv_g2, rwkv_k_k, rwkv_k_a, rwkv_r_k, rwkv_ln_g, rwkv_ln_b, w_rwkv_up, gla_alpha_up, gla_alpha_b, gla_norm_g, w_gla_up, w_out, ffn_w_gate, ffn_w_up, ffn_w_down):
    assert ada_w.shape[0] == 1, "single trunk layer"
    bsz, seq, d = x.shape
    n_ctx = ctx.shape[1]
    prec = BF16

    cvecs = jnp.concatenate([c, c_ctx[None, :], jnp.zeros((8 - bsz - 1, d), F32)], 0)
    mod = _modulation(cvecs, ada_w[0], ada_b[0])
    mod_x = mod[:bsz].reshape(bsz, 6, 1, d)
    shx1, scx1, gx1, shx2, scx2, gx2 = (mod_x[:, i] for i in range(6))
    mod_c = jnp.broadcast_to(mod[bsz].reshape(1, 6, 1, d), (bsz, 6, 1, d))
    shc1, scc1 = mod_c[:, 0], mod_c[:, 1]

    w_all = w_in[0]
    mix_in = RWKV_IN + GLA_IN
    w_rwkv = _pad_cols(w_all[:, :RWKV_IN], RWKV_PAD).astype(BF16)
    w_gla = _pad_cols(w_all[:, RWKV_IN:mix_in], GLA_PAD).astype(BF16)
    w_gate = w_all[:, mix_in:].astype(BF16)
    mu = _pad_cols(shift_mu, RWKV_PAD)
    n_pre = norm_pre_mix

    px_rwkv = _project(x, shx1, scx1, n_pre, w_rwkv, 1024, 512, "proj_rwkv")
    px_gla = _project(x, shx1, scx1, n_pre, w_gla, 1024, 512, "proj_gla")
    px_gate = _project(x, shx1, scx1, n_pre, w_gate, 1024, 1024, "proj_gate")
    pc_rwkv = _project(ctx, shc1, scc1, n_pre, w_rwkv, n_ctx, 512, "proj_rwkv_ctx")
    pc_gla = _project(ctx, shc1, scc1, n_pre, w_gla, n_ctx, 512, "proj_gla_ctx")

    pmix = _shiftmix(px_rwkv, pc_rwkv, mu)
    n_ctx_chunks = n_ctx // CHUNK
    y_dir = []
    for dr in range(2):
        y_dir.append(_rwkv_scan(pmix, rwkv_w2[0, dr], rwkv_w0[0, dr][None, :], rwkv_a2[0, dr], rwkv_a0[0, dr][None, :],
                                rwkv_k_k, rwkv_k_a, dr == 1, n_ctx_chunks, prec))
    ya = _rwkv_post(y_dir[0], y_dir[1], pmix, n_ctx, rwkv_a2[0], rwkv_a0[0][:, None, :], rwkv_g2[0], rwkv_k_a,
                    rwkv_r_k.reshape(1, RWKV_WIDTH), rwkv_ln_g, rwkv_ln_b)

    o_dir = [_gla_scan(px_gla, pc_gla, gla_alpha_up[0, dr], gla_alpha_b[0, dr][None, :], dr == 1, prec) for dr in range(2)]
    yb = _gla_post(o_dir[0], o_dir[1], px_gla, gla_norm_g)

    m = _merge(ya, yb, w_rwkv_up[0].astype(BF16), w_gla_up[0].astype(BF16), px_gate)
    x1, h2 = _mix_out(m, w_out[0].astype(BF16), x, gx1, norm_post_mix, norm_pre_ffn, shx2, scx2)
    hf = _ffn_up(h2, ffn_w_gate[0].astype(BF16), ffn_w_up[0].astype(BF16))
    return _ffn_down(hf, ffn_w_down[0].astype(BF16), x1, gx2, norm_post_ffn)
```

```python
import functools

import jax
import jax.numpy as jnp
from jax import lax
from jax.experimental import pallas as pl
from jax.experimental.pallas import tpu as pltpu

F32 = jnp.float32
BF16 = jnp.bfloat16
HIGHEST = lax.Precision.HIGHEST

LANES = 128
VMEM_LIMIT_BYTES = 56 * 1024 * 1024

GRID_W = 64
CHUNK = 64
RWKV_HEADS, RWKV_HEAD_DIM = 16, 64
RWKV_WIDTH = RWKV_HEADS * RWKV_HEAD_DIM
DECAY_LORA = ICL_LORA = 96
GATE_LORA = 64
RWKV_GN_EPS = 64e-5
GLA_HEADS, GLA_KEY_DIM, GLA_VAL_DIM = 4, 128, 256
GLA_QK_WIDTH = GLA_HEADS * GLA_KEY_DIM
GLA_V_WIDTH = GLA_HEADS * GLA_VAL_DIM
GLA_GATE_LORA = 16
GLA_TAU = 16.0
GLA_NORM_EPS = 1e-5
GLA_SUB = 16
NORM_EPS = 1e-6
PROLOGUE_ROWS = 256

RWKV_IN = 3 * RWKV_WIDTH + 2 * DECAY_LORA + 2 * ICL_LORA + GATE_LORA
RWKV_PAD = 3584
OFF_WD = 3 * RWKV_WIDTH
OFF_AD = OFF_WD + 2 * DECAY_LORA
OFF_GD = OFF_AD + 2 * ICL_LORA
GLA_IN = 2 * GLA_QK_WIDTH + 2 * GLA_V_WIDTH + 2 * GLA_GATE_LORA
GLA_BLK = 512
GLA_PAD = 7 * GLA_BLK
OFF_GLA_AD = 6 * GLA_BLK

PAIR = 2 * RWKV_HEAD_DIM
N_PAIRS = RWKV_HEADS // 2


def _cparams(semantics):
    return pltpu.CompilerParams(dimension_semantics=semantics, vmem_limit_bytes=VMEM_LIMIT_BYTES)


def _sigmoid(z):
    return 1.0 / (1.0 + jnp.exp(-z))


def _softplus(z):
    return jnp.maximum(z, 0.0) + jnp.log(1.0 + jnp.exp(-jnp.abs(z)))


def _dot(a, b, dims, precision):
    if precision is BF16:
        a, b, precision = a.astype(BF16), b.astype(BF16), None
    return lax.dot_general(a, b, (dims, ((), ())), precision=precision, preferred_element_type=F32)


def _mm(a, b, precision=None):
    return _dot(a, b, ((1,), (0,)), precision)


def _mm_nt(a, b, precision=None):
    return _dot(a, b, ((1,), (1,)), precision)


def _mm_tn(a, b, precision=None):
    return _dot(a, b, ((0,), (0,)), precision)


def _split2(z):
    hi = z.astype(BF16)
    return hi, (z - hi.astype(F32)).astype(BF16)


def _mm_split(a, b):
    ah, al = _split2(a)
    bh, bl = _split2(b)
    return (_mm(al, bh) + _mm(ah, bl)) + _mm(ah, bh)


def _mm_exact_lhs(e, b):
    hi = b.astype(BF16)
    r = b - hi.astype(F32)
    mid = r.astype(BF16)
    lo = (r - mid.astype(F32)).astype(BF16)
    e = e.astype(BF16)
    return (_mm(e, lo) + _mm(e, mid)) + _mm(e, hi)


def _mod_kernel(c_ref, w_ref, b_ref, o_ref):
    s = c_ref[...]
    s = s * _sigmoid(s)
    o_ref[...] = _mm_split(s, w_ref[...]) + b_ref[...]


def _modulation(cvecs, ada_w, ada_b, tn=1024):
    m, d = cvecs.shape
    n = ada_w.shape[1]
    return pl.pallas_call(
        _mod_kernel,
        grid=(n // tn,),
        in_specs=[pl.BlockSpec((m, d), lambda j: (0, 0)),
                  pl.BlockSpec((d, tn), lambda j: (0, j)),
                  pl.BlockSpec((1, tn), lambda j: (0, j))],
        out_specs=pl.BlockSpec((m, tn), lambda j: (0, j)),
        out_shape=jax.ShapeDtypeStruct((m, n), F32),
        compiler_params=_cparams(("arbitrary",)),
        name="adaln_mod",
    )(cvecs, ada_w, ada_b.reshape(1, n))


def _proj_kernel(x_ref, sh_ref, sc_ref, g_ref, w_ref, o_ref, h_scr):
    @pl.when(pl.program_id(2) == 0)
    def _():
        rows = min(PROLOGUE_ROWS, h_scr.shape[0])

        def body(rb, carry):
            sl = pl.ds(pl.multiple_of(rb * rows, rows), rows)
            x = x_ref[0, sl, :]
            ms = jnp.mean(x * x, axis=-1, keepdims=True)
            y = x * lax.rsqrt(ms + NORM_EPS) * g_ref[...]
            h_scr[sl, :] = (y * (1.0 + sc_ref[0]) + sh_ref[0]).astype(BF16)
            return carry

        lax.fori_loop(0, h_scr.shape[0] // rows, body, 0)

    o_ref[0] = _mm(h_scr[...], w_ref[...])


def _project(x, shift, scale, gain, w, tm, tn, name):
    b, t, d = x.shape
    n = w.shape[1]
    return pl.pallas_call(
        _proj_kernel,
        grid=(b, t // tm, n // tn),
        in_specs=[pl.BlockSpec((1, tm, d), lambda bi, i, j: (bi, i, 0)),
                  pl.BlockSpec((1, 1, d), lambda bi, i, j: (bi, 0, 0)),
                  pl.BlockSpec((1, 1, d), lambda bi, i, j: (bi, 0, 0)),
                  pl.BlockSpec((1, d), lambda bi, i, j: (0, 0)),
                  pl.BlockSpec((d, tn), lambda bi, i, j: (0, j))],
        out_specs=pl.BlockSpec((1, tm, tn), lambda bi, i, j: (bi, i, j)),
        out_shape=jax.ShapeDtypeStruct((b, t, n), F32),
        scratch_shapes=[pltpu.VMEM((tm, d), BF16)],
        compiler_params=_cparams(("arbitrary", "arbitrary", "arbitrary")),
        name=name,
    )(x, shift, scale, gain, w)


def _colmajor_perm(n_rows, n_cols):
    dst = jnp.arange(n_rows * n_cols)
    src = (dst % n_rows) * n_cols + dst // n_rows
    return (src[:, None] == dst[None, :]).astype(BF16)


def _proj_cm_kernel(x_ref, sh_ref, sc_ref, g_ref, perm_ref, w_ref, o_ref, hr_scr, h_scr):
    @pl.when(pl.program_id(2) == 0)
    def _():
        n_r, n_c, d = x_ref.shape[1:]
        rb = PROLOGUE_ROWS // n_c
        rows = rb * n_c

        def body(kb, carry):
            x = x_ref[0, pl.ds(pl.multiple_of(kb * rb, rb), rb), :, :].reshape(rows, d)
            ms = jnp.mean(x * x, axis=-1, keepdims=True)
            y = x * lax.rsqrt(ms + NORM_EPS) * g_ref[...]
            hr_scr[pl.ds(pl.multiple_of(kb * rows, rows), rows), :] = (y * (1.0 + sc_ref[0]) + sh_ref[0]).astype(BF16)
            return carry

        lax.fori_loop(0, n_r // rb, body, 0)
        for cb in range(d // GLA_BLK):
            cs = slice(cb * GLA_BLK, (cb + 1) * GLA_BLK)
            h_scr[:, cs] = _mm(perm_ref[...], hr_scr[:, cs]).astype(BF16)

    o_ref[0] = _mm(h_scr[...], w_ref[...])


def _project_colmajor(x, shift, scale, gain, w, n_cols, tn, name):
    b, t, d = x.shape
    n = w.shape[1]
    n_r = t // GRID_W
    tm = n_r * n_cols
    x4 = x.reshape(b, n_r, GRID_W, d)
    perm = _colmajor_perm(n_r, n_cols)
    return pl.pallas_call(
        _proj_cm_kernel,
        grid=(b, GRID_W // n_cols, n // tn),
        in_specs=[pl.BlockSpec((1, n_r, n_cols, d), lambda bi, i, j: (bi, 0, i, 0)),
                  pl.BlockSpec((1, 1, d), lambda bi, i, j: (bi, 0, 0)),
                  pl.BlockSpec((1, 1, d), lambda bi, i, j: (bi, 0, 0)),
                  pl.BlockSpec((1, d), lambda bi, i, j: (0, 0)),
                  pl.BlockSpec((tm, tm), lambda bi, i, j: (0, 0)),
                  pl.BlockSpec((d, tn), lambda bi, i, j: (0, j))],
        out_specs=pl.BlockSpec((1, tm, tn), lambda bi, i, j: (bi, i, j)),
        out_shape=jax.ShapeDtypeStruct((b, t, n), F32),
        scratch_shapes=[pltpu.VMEM((tm, d), BF16), pltpu.VMEM((tm, d), BF16)],
        compiler_params=_cparams(("arbitrary", "arbitrary", "arbitrary")),
        name=name,
    )(x4, shift, scale, gain, perm, w)


def _shiftmix_kernel(n_ctx_blk, n_blk, tb, lat_ref, ctx_ref, lprev_ref, lnext_ref, cprev_ref, cnext_ref, mu_ref, o_ref):
    i = pl.program_id(1)
    is_ctx = i < n_ctx_blk
    p = jnp.where(is_ctx, ctx_ref[0], lat_ref[0])
    first = jnp.logical_or(i == 0, i == n_ctx_blk)
    last = jnp.logical_or(i == n_ctx_blk - 1, i == n_blk - 1)
    hp = jnp.where(is_ctx, cprev_ref[0, 7:8, :], lprev_ref[0, 7:8, :])
    hn = jnp.where(is_ctx, cnext_ref[0, 0:1, :], lnext_ref[0, 0:1, :])
    hp = jnp.where(first, 0.0, hp)
    hn = jnp.where(last, 0.0, hn)
    row = lax.broadcasted_iota(jnp.int32, p.shape, 0)
    prev = jnp.where(row == 0, hp, pltpu.roll(p, 1, 0))
    nxt = jnp.where(row == tb - 1, hn, pltpu.roll(p, tb - 1, 0))
    o_ref[0] = p + mu_ref[...] * (0.5 * (prev + nxt) - p)


def _shiftmix(p_lat, p_ctx, mu, tb=128):
    b, t, w = p_lat.shape
    tc = p_ctx.shape[1]
    n_ctx_blk, n_lat_blk = tc // tb, t // tb
    n_blk = n_ctx_blk + n_lat_blk
    r8 = tb // 8

    def lat_main(bi, i):
        return (bi, jnp.maximum(i - n_ctx_blk, 0), 0)

    def ctx_main(bi, i):
        return (bi, jnp.minimum(i, n_ctx_blk - 1), 0)

    def lat_prev(bi, i):
        return (bi, jnp.maximum((i - n_ctx_blk) * r8 - 1, 0), 0)

    def lat_next(bi, i):
        return (bi, jnp.clip((i - n_ctx_blk + 1) * r8, 0, t // 8 - 1), 0)

    def ctx_prev(bi, i):
        return (bi, jnp.clip(i * r8 - 1, 0, tc // 8 - 1), 0)

    def ctx_next(bi, i):
        return (bi, jnp.minimum((i + 1) * r8, tc // 8 - 1), 0)

    return pl.pallas_call(
        functools.partial(_shiftmix_kernel, n_ctx_blk, n_blk, tb),
        grid=(b, n_blk),
        in_specs=[pl.BlockSpec((1, tb, w), lat_main),
                  pl.BlockSpec((1, tb, w), ctx_main),
                  pl.BlockSpec((1, 8, w), lat_prev),
                  pl.BlockSpec((1, 8, w), lat_next),
                  pl.BlockSpec((1, 8, w), ctx_prev),
                  pl.BlockSpec((1, 8, w), ctx_next),
                  pl.BlockSpec((1, w), lambda bi, i: (0, 0))],
        out_specs=pl.BlockSpec((1, tb, w), lambda bi, i: (bi, i, 0)),
        out_shape=jax.ShapeDtypeStruct((b, tc + t, w), F32),
        compiler_params=_cparams(("arbitrary", "arbitrary")),
        name="rwkv_shiftmix",
    )(p_lat, p_ctx, p_lat, p_lat, p_ctx, p_ctx, mu)


def _head_sum_matrix():
    r = lax.broadcasted_iota(jnp.int32, (PAIR, PAIR), 0)
    c = lax.broadcasted_iota(jnp.int32, (PAIR, PAIR), 1)
    return jnp.where((r >> 6) == (c >> 6), 1.0, 0.0).astype(BF16)


def _head_sum(z, hsum):
    hi = z.astype(BF16)
    lo = (z - hi.astype(F32)).astype(BF16)
    return _mm(hi, hsum) + _mm(lo, hsum)


def _rwkv_kernel(rev, prec, pm_ref, w2_ref, w0_ref, a2_ref, a0_ref, kk_ref, ka_ref, y_ref, st_scr):
    hd = RWKV_HEAD_DIM
    c2 = 2 * CHUNK

    @pl.when(pl.program_id(1) == 0)
    def _():
        st_scr[...] = jnp.zeros_like(st_scr)

    p = pm_ref[0]
    r = p[:, 0:RWKV_WIDTH]
    k = p[:, RWKV_WIDTH:2 * RWKV_WIDTH]
    v = p[:, 2 * RWKV_WIDTH:3 * RWKV_WIDTH]
    d_off = DECAY_LORA if rev else 0
    wd = p[:, OFF_WD + d_off:OFF_WD + d_off + DECAY_LORA]
    ad = p[:, OFF_AD + d_off:OFF_AD + d_off + ICL_LORA]

    w_log = -_softplus(-(w0_ref[...] + _mm_split(jnp.tanh(wd), w2_ref[...]))) - 0.5
    lw = -jnp.exp(w_log)
    a = _sigmoid(a0_ref[...] + _mm(ad, a2_ref[...], BF16))
    kk_raw = k * kk_ref[...]
    kd = k * (1.0 + (a - 1.0) * ka_ref[...])

    ri = lax.broadcasted_iota(jnp.int32, (c2, PAIR), 0)
    li = lax.broadcasted_iota(jnp.int32, (c2, PAIR), 1)
    rt, lt = ri & (CHUNK - 1), li & (hd - 1)
    same_head = (ri >> 6) == (li >> 6)
    strict = (lt > rt) if rev else (lt < rt)
    mask_n = jnp.logical_and(same_head, strict)
    mask_k = jnp.logical_and(jnp.logical_not(same_head), strict)
    ident = ri == li
    eye = jnp.where(ident, 1.0, 0.0).astype(F32)
    rc = lax.broadcasted_iota(jnp.int32, (CHUNK, PAIR), 0)
    lc = lax.broadcasted_iota(jnp.int32, (CHUNK, PAIR), 1)
    incl_c = ((lc & (hd - 1)) >= rc) if rev else ((lc & (hd - 1)) <= rc)
    lane_e = lc < hd
    hsum = _head_sum_matrix()

    ci = lax.broadcasted_iota(jnp.int32, (CHUNK, CHUNK), 0)
    cj = lax.broadcasted_iota(jnp.int32, (CHUNK, CHUNK), 1)
    tri = jnp.where((cj >= ci) if rev else (cj <= ci), 1.0, 0.0).astype(F32)
    cum = _mm_exact_lhs(tri, lw)
    total = cum[0:1] if rev else cum[CHUNK - 1:CHUNK]
    e_prev = jnp.exp(cum - lw)
    e_neg = jnp.exp(-cum)
    e_pos = jnp.exp(cum)
    e_rest = jnp.exp(total - cum)
    p_end = jnp.exp(total)

    def split(z):
        ze = jnp.where(lane_e, z, 0.0)
        return ze, z - ze

    pairs = range(N_PAIRS)
    sls = [slice(pr * PAIR, (pr + 1) * PAIR) for pr in pairs]
    cat0 = lambda *z: jnp.concatenate(z, 0)

    kkr = [kk_raw[:, sl] for sl in sls]
    nrm2 = [_head_sum(z * z, hsum) for z in kkr]
    kk = [z / jnp.maximum(jnp.sqrt(n), 1e-12) for z, n in zip(kkr, nrm2)]
    bb = [z * a[:, sl] for z, sl in zip(kk, sls)]
    at = [split(z * e_prev[:, sl]) for z, sl in zip(kk, sls)]
    bt = [z * e_neg[:, sl] for z, sl in zip(bb, sls)]
    kt = [kd[:, sl] * e_neg[:, sl] for sl in sls]
    rt_ = [r[:, sl] * e_pos[:, sl] for sl in sls]
    rs = [split(z) for z in rt_]
    vs = [split(v[:, sl]) for sl in sls]
    bh = [cat0(*split(z * e_rest[:, sl])) for z, sl in zip(bb, sls)]
    kh = [cat0(*split(kd[:, sl] * e_rest[:, sl])) for sl in sls]
    v_swap = [cat0(vo, ve) for ve, vo in vs]
    v_stack = [cat0(ve, vo) for ve, vo in vs]

    g_e = [_mm_nt(cat0(at[i][0], rs[i][0]), cat0(bt[i], kt[i]), prec) for i in pairs]
    g_o = [_mm_nt(cat0(at[i][1], rs[i][1]), cat0(kt[i], bt[i]), prec) for i in pairs]
    g_top = [cat0(g_e[i][0:CHUNK], g_o[i][0:CHUNK]) for i in pairs]
    nbd = [jnp.where(mask_n, z, 0.0) for z in g_top]
    aak = [jnp.where(mask_k, z, 0.0) for z in g_top]
    rab = [jnp.where(incl_c, jnp.where(lane_e, g_e[i][CHUNK:c2], g_o[i][CHUNK:c2]), 0.0) for i in pairs]
    rak = [jnp.where(incl_c, jnp.where(lane_e, g_o[i][CHUNK:c2], g_e[i][CHUNK:c2]), 0.0) for i in pairs]

    x = [_mm(aak[i], v_swap[i], prec) for i in pairs]
    n2 = [_mm(z, z, prec) for z in nbd]
    y0b = [_mm(rak[i], v_swap[i], prec) for i in pairs]
    n4 = [_mm(z, z, prec) for z in n2]
    imn = [eye - z for z in nbd]
    p1 = [imn[i] + _mm(imn[i], n2[i], prec) for i in pairs]
    n8 = [_mm(z, z, prec) for z in n4]
    nb = [_mm_tn(kh[i], v_stack[i], prec) for i in pairs]
    n16 = [_mm(z, z, prec) for z in n8]
    p2 = [eye + n4[i] + n8[i] + _mm(n4[i], n8[i], prec) for i in pairs]
    n32 = [_mm(z, z, prec) for z in n16]
    p12 = [_mm(p1[i], p2[i], prec) for i in pairs]
    p3 = [eye + n16[i] + n32[i] + _mm(n16[i], n32[i], prec) for i in pairs]
    tinv = [_mm(p12[i], p3[i], prec) for i in pairs]
    wu = [-_mm(tinv[i], jnp.concatenate([cat0(*at[i]), x[i]], 1), prec) for i in pairs]
    qy = [_mm(rab[i], wu[i], prec) for i in pairs]
    mn = [_mm_tn(bh[i], wu[i], prec) for i in pairs]
    q = [rt_[i] + qy[i][:, 0:PAIR] for i in pairs]
    m = [jnp.where(ident, p_end[:, sls[i]], 0.0) + mn[i][:, 0:PAIR] for i in pairs]
    qm = [_mm(cat0(q[i], m[i]), st_scr[i], prec) for i in pairs]
    for i in pairs:
        y_ref[0, :, sls[i]] = qm[i][0:CHUNK] + qy[i][:, PAIR:2 * PAIR] + y0b[i]
        st_scr[i] = qm[i][CHUNK:CHUNK + PAIR] + mn[i][:, PAIR:2 * PAIR] + nb[i]


def _rwkv_scan(pmix, w2, w0, a2, a0, k_k, k_a, rev, n_ctx_chunks, prec):
    b, tt, w = pmix.shape
    n_chunks = tt // CHUNK
    n_lat = n_chunks - n_ctx_chunks

    def chunk_of(i):
        if not rev:
            return i
        return jnp.where(i < n_ctx_chunks, n_ctx_chunks - 1 - i, n_chunks + n_ctx_chunks - 1 - i)

    def out_of(i):
        if not rev:
            return jnp.maximum(i - n_ctx_chunks, 0)
        return jnp.where(i < n_ctx_chunks, n_lat - 1, n_chunks - 1 - i)

    vec = lambda bi, i: (0, 0)
    return pl.pallas_call(
        functools.partial(_rwkv_kernel, rev, prec),
        grid=(b, n_chunks),
        in_specs=[pl.BlockSpec((1, CHUNK, w), lambda bi, i: (bi, chunk_of(i), 0)),
                  pl.BlockSpec((DECAY_LORA, RWKV_WIDTH), vec),
                  pl.BlockSpec((1, RWKV_WIDTH), vec),
                  pl.BlockSpec((ICL_LORA, RWKV_WIDTH), vec),
                  pl.BlockSpec((1, RWKV_WIDTH), vec),
                  pl.BlockSpec((1, RWKV_WIDTH), vec),
                  pl.BlockSpec((1, RWKV_WIDTH), vec)],
        out_specs=pl.BlockSpec((1, CHUNK, RWKV_WIDTH), lambda bi, i: (bi, out_of(i), 0)),
        out_shape=jax.ShapeDtypeStruct((b, n_lat * CHUNK, RWKV_WIDTH), F32),
        scratch_shapes=[pltpu.VMEM((N_PAIRS, PAIR, PAIR), F32)],
        compiler_params=_cparams(("arbitrary", "arbitrary")),
        name="rwkv7_bwd" if rev else "rwkv7_fwd",
    )(pmix, w2, w0, a2, a0, k_k, k_a)


def _gla_kernel(rev, prec, n_ctx_chunks, q_ref, k_ref, v0_ref, v1_ref, ad_ref, ctx_ref, aup_ref, ab_ref, o_ref, st_scr):
    i = pl.program_id(1)

    @pl.when(i == 0)
    def _():
        st_scr[...] = jnp.zeros_like(st_scr)

    is_ctx = i < n_ctx_chunks
    pc = ctx_ref[0]
    q = jnp.where(is_ctx, pc[:, 0:GLA_BLK], q_ref[0])
    k = jnp.where(is_ctx, pc[:, GLA_BLK:2 * GLA_BLK], k_ref[0])
    v = jnp.concatenate([jnp.where(is_ctx, pc[:, 2 * GLA_BLK:3 * GLA_BLK], v0_ref[0]),
                         jnp.where(is_ctx, pc[:, 3 * GLA_BLK:4 * GLA_BLK], v1_ref[0])], 1)
    d_off = GLA_GATE_LORA if rev else 0
    ad = jnp.where(is_ctx, pc[:, OFF_GLA_AD:OFF_GLA_AD + LANES], ad_ref[0][:, 0:LANES])
    ad = ad[:, d_off:d_off + GLA_GATE_LORA]

    la = -_softplus(-(_mm_split(ad, aup_ref[...]) + ab_ref[...])) * (1.0 / GLA_TAU)
    ci = lax.broadcasted_iota(jnp.int32, (CHUNK, CHUNK), 0)
    cj = lax.broadcasted_iota(jnp.int32, (CHUNK, CHUNK), 1)
    tri = jnp.where((cj >= ci) if rev else (cj <= ci), 1.0, 0.0).astype(F32)
    cum = _mm_exact_lhs(tri, la)
    total = cum[0:1] if rev else cum[CHUNK - 1:CHUNK]

    dk, dv, sb = GLA_KEY_DIM, GLA_VAL_DIM, GLA_SUB
    n_sb = CHUNK // sb
    row = lax.broadcasted_iota(jnp.int32, (CHUNK, dk), 0)
    row_in = row & (sb - 1)
    arow = lax.broadcasted_iota(jnp.int32, (CHUNK, CHUNK), 0)
    acol = lax.broadcasted_iota(jnp.int32, (CHUNK, CHUNK), 1)
    scale = GLA_KEY_DIM ** -0.5

    heads = range(GLA_HEADS)
    ksl = [slice(h * dk, (h + 1) * dk) for h in heads]
    qh = [q[:, s_] * scale for s_ in ksl]
    kh = [k[:, s_] for s_ in ksl]
    bh = [cum[:, s_] for s_ in ksl]
    lah = [la[:, s_] for s_ in ksl]
    toth = [total[:, s_] for s_ in ksl]
    vh = [v[:, h * dv:(h + 1) * dv] for h in heads]
    st = [st_scr[h] for h in heads]

    o_inter = [_mm_nt(qh[h] * jnp.exp(bh[h]), st[h], prec) for h in heads]
    st_new = [st[h] * jnp.exp(toth[h]) + _mm_tn(vh[h], kh[h] * jnp.exp(toth[h] - bh[h]), prec) for h in heads]

    off_rows = [[] for _ in heads]
    for blk in range(n_sb):
        rs = slice(blk * sb, (blk + 1) * sb)
        first = blk * sb + (sb - 1 if rev else 0)
        is_first_blk = (blk == n_sb - 1) if rev else (blk == 0)
        before = (row >= (blk + 1) * sb) if rev else (row < blk * sb)
        for h in heads:
            if is_first_blk:
                off_rows[h].append(jnp.zeros((sb, CHUNK), F32))
                continue
            beta = bh[h][first:first + 1] - lah[h][first:first + 1]
            qs = qh[h][rs] * jnp.exp(bh[h][rs] - beta)
            ksc = jnp.where(before, kh[h] * jnp.exp(jnp.minimum(beta - bh[h], 0.0)), 0.0)
            off_rows[h].append(_mm_nt(qs, ksc, prec))
    att = [jnp.concatenate(off_rows[h], 0) for h in heads]

    for s in range(sb):
        pick = lambda z: jnp.concatenate(
            [jnp.broadcast_to(z[blk * sb + s:blk * sb + s + 1], (sb, dk)) for blk in range(n_sb)], 0)
        ok = (row_in <= s) if rev else (row_in >= s)
        tgt = jnp.logical_and(acol == (arow & ~(sb - 1)) + s,
                              ((arow & (sb - 1)) <= s) if rev else ((arow & (sb - 1)) >= s))
        for h in heads:
            e = jnp.exp(jnp.where(ok, bh[h] - pick(bh[h]), 0.0))
            col = jnp.sum(jnp.where(ok, qh[h] * pick(kh[h]) * e, 0.0), axis=-1, keepdims=True)
            att[h] = jnp.where(tgt, col, att[h])

    for h in heads:
        o_ref[0, :, h * dv:(h + 1) * dv] = o_inter[h] + _mm(att[h], vh[h], prec)
        st_scr[h] = st_new[h]


def _gla_scan(p_lat, p_ctx, alpha_up, alpha_b, rev, prec):
    b, t, w = p_lat.shape
    assert w == GLA_PAD and t == GRID_W * CHUNK
    n_cols = GRID_W
    n_ctx_chunks = p_ctx.shape[1] // CHUNK
    n_steps = n_ctx_chunks + n_cols

    def col_of(i):
        if not rev:
            return jnp.maximum(i - n_ctx_chunks, 0)
        return jnp.where(i < n_ctx_chunks, n_cols - 1, n_steps - 1 - i)

    def ctx_of(i):
        if not rev:
            return jnp.minimum(i, n_ctx_chunks - 1)
        return jnp.maximum(n_ctx_chunks - 1 - i, 0)

    def lat_spec(m):
        return pl.BlockSpec((1, CHUNK, GLA_BLK), lambda bi, i: (bi, col_of(i), m))

    vec = lambda bi, i: (0, 0)
    return pl.pallas_call(
        functools.partial(_gla_kernel, rev, prec, n_ctx_chunks),
        grid=(b, n_steps),
        in_specs=[lat_spec(0), lat_spec(1), lat_spec(2), lat_spec(3), lat_spec(6),
                  pl.BlockSpec((1, CHUNK, w), lambda bi, i: (bi, ctx_of(i), 0)),
                  pl.BlockSpec((GLA_GATE_LORA, GLA_QK_WIDTH), vec),
                  pl.BlockSpec((1, GLA_QK_WIDTH), vec)],
        out_specs=pl.BlockSpec((1, CHUNK, GLA_V_WIDTH), lambda bi, i: (bi, col_of(i), 0)),
        out_shape=jax.ShapeDtypeStruct((b, t, GLA_V_WIDTH), F32),
        scratch_shapes=[pltpu.VMEM((GLA_HEADS, GLA_VAL_DIM, GLA_KEY_DIM), F32)],
        compiler_params=_cparams(("arbitrary", "arbitrary")),
        name="gla_bwd" if rev else "gla_fwd",
    )(p_lat, p_lat, p_lat, p_lat, p_lat, p_ctx, alpha_up, alpha_b)


def _rwkv_post_kernel(yf_ref, yb_ref, pm_ref, a2_ref, a0_ref, g2_ref, ka_ref, rk_ref, lng_ref, lnb_ref, o_ref):
    p = pm_ref[0]
    r = p[:, 0:RWKV_WIDTH]
    k = p[:, RWKV_WIDTH:2 * RWKV_WIDTH]
    v = p[:, 2 * RWKV_WIDTH:3 * RWKV_WIDTH]
    ad_f = p[:, OFF_AD:OFF_AD + ICL_LORA]
    ad_b = p[:, OFF_AD + ICL_LORA:OFF_AD + 2 * ICL_LORA]
    gd = p[:, OFF_GD:OFF_GD + GATE_LORA]
    ka = ka_ref[...]
    a_f = _sigmoid(a0_ref[0] + _mm(ad_f, a2_ref[0], BF16))
    a_b = _sigmoid(a0_ref[1] + _mm(ad_b, a2_ref[1], BF16))
    kd_sum = k * (1.0 + (a_f - 1.0) * ka) + k * (1.0 + (a_b - 1.0) * ka)
    gate = _mm(_sigmoid(gd), g2_ref[...], BF16)
    rkk = r * kd_sum * rk_ref[...]
    ysum = yf_ref[0] + yb_ref[0]
    hsum = _head_sum_matrix()
    inv_n = 1.0 / RWKV_HEAD_DIM
    for pr in range(N_PAIRS):
        sl = slice(pr * PAIR, (pr + 1) * PAIR)
        ys = ysum[:, sl]
        mean = _head_sum(ys, hsum) * inv_n
        dlt = ys - mean
        var = _head_sum(dlt * dlt, hsum) * inv_n
        gn = dlt * lax.rsqrt(var + RWKV_GN_EPS) * lng_ref[:, sl] + lnb_ref[:, sl]
        bonus = _head_sum(rkk[:, sl], hsum) * v[:, sl]
        o_ref[0, :, sl] = ((gn + bonus) * gate[:, sl]).astype(o_ref.dtype)


def _rwkv_post(y_f, y_b, pmix, n_ctx, a2, a0, g2, k_a, r_k, ln_g, ln_b, tm=256):
    b, t, w = y_f.shape
    off = n_ctx // tm
    row = lambda bi, i: (bi, i, 0)
    vec = lambda bi, i: (0, 0)
    vec3 = lambda bi, i: (0, 0, 0)
    return pl.pallas_call(
        _rwkv_post_kernel,
        grid=(b, t // tm),
        in_specs=[pl.BlockSpec((1, tm, w), row),
                  pl.BlockSpec((1, tm, w), row),
                  pl.BlockSpec((1, tm, pmix.shape[2]), lambda bi, i: (bi, i + off, 0)),
                  pl.BlockSpec((2, ICL_LORA, w), vec3),
                  pl.BlockSpec((2, 1, w), vec3),
                  pl.BlockSpec((GATE_LORA, w), vec),
                  pl.BlockSpec((1, w), vec), pl.BlockSpec((1, w), vec),
                  pl.BlockSpec((1, w), vec), pl.BlockSpec((1, w), vec)],
        out_specs=pl.BlockSpec((1, tm, w), row),
        out_shape=jax.ShapeDtypeStruct((b, t, w), BF16),
        compiler_params=_cparams(("arbitrary", "arbitrary")),
        name="rwkv_post",
    )(y_f, y_b, pmix, a2, a0, g2, k_a, r_k, ln_g, ln_b)


def _gla_post_kernel(of_ref, ob_ref, g_ref, ng_ref, perm_ref, o_ref):
    n_r, n_c = o_ref.shape[1:3]
    dv = GLA_VAL_DIM
    for h in range(GLA_HEADS):
        sl = slice(h * dv, (h + 1) * dv)
        oh = of_ref[0, :, sl] + ob_ref[0, :, sl]
        oh = oh * lax.rsqrt(jnp.mean(oh * oh, axis=-1, keepdims=True) + GLA_NORM_EPS) * ng_ref[:, sl]
        gh = g_ref[0, :, sl]
        y = (oh * (gh * _sigmoid(gh))).astype(BF16)
        o_ref[0, :, :, sl] = _mm(perm_ref[...], y).reshape(n_r, n_c, dv).astype(o_ref.dtype)


def _gla_post(o_f, o_b, p_gla, norm_g, n_cols=16):
    b, t, w = o_f.shape
    n_r = t // GRID_W
    tm = n_r * n_cols
    perm_t = _colmajor_perm(n_r, n_cols).T
    row = lambda bi, i: (bi, i, 0)
    out = pl.pallas_call(
        _gla_post_kernel,
        grid=(b, GRID_W // n_cols),
        in_specs=[pl.BlockSpec((1, tm, w), row),
                  pl.BlockSpec((1, tm, w), row),
                  pl.BlockSpec((1, tm, w), lambda bi, i: (bi, i, 2)),
                  pl.BlockSpec((1, w), lambda bi, i: (0, 0)),
                  pl.BlockSpec((tm, tm), lambda bi, i: (0, 0))],
        out_specs=pl.BlockSpec((1, n_r, n_cols, w), lambda bi, i: (bi, 0, i, 0)),
        out_shape=jax.ShapeDtypeStruct((b, n_r, GRID_W, w), BF16),
        compiler_params=_cparams(("arbitrary", "arbitrary")),
        name="gla_post",
    )(o_f, o_b, p_gla, norm_g, perm_t)
    return out.reshape(b, t, w)


def _merge_kernel(ya_ref, yb_ref, wr_ref, wg_ref, ga_ref, gb_ref, o_ref):
    ma = _mm(ya_ref[0], wr_ref[...])
    mb = _mm(yb_ref[0], wg_ref[...])
    o_ref[0] = (_sigmoid(ga_ref[0]) * ma + _sigmoid(gb_ref[0]) * mb).astype(o_ref.dtype)


def _merge(ya, yb, w_r, w_g, p_gate, tm=512, tn=1024):
    b, t, w = ya.shape
    d = w_r.shape[1]
    nj = d // tn
    return pl.pallas_call(
        _merge_kernel,
        grid=(b, t // tm, nj),
        in_specs=[pl.BlockSpec((1, tm, w), lambda bi, i, j: (bi, i, 0)),
                  pl.BlockSpec((1, tm, w), lambda bi, i, j: (bi, i, 0)),
                  pl.BlockSpec((w, tn), lambda bi, i, j: (0, j)),
                  pl.BlockSpec((w, tn), lambda bi, i, j: (0, j)),
                  pl.BlockSpec((1, tm, tn), lambda bi, i, j: (bi, i, j)),
                  pl.BlockSpec((1, tm, tn), lambda bi, i, j: (bi, i, j + nj))],
        out_specs=pl.BlockSpec((1, tm, tn), lambda bi, i, j: (bi, i, j)),
        out_shape=jax.ShapeDtypeStruct((b, t, d), BF16),
        compiler_params=_cparams(("arbitrary", "arbitrary", "arbitrary")),
        name="merge_branches",
    )(ya, yb, w_r, w_g, p_gate, p_gate)


def _mix_out_kernel(m_ref, w_ref, x_ref, gate_ref, npost_ref, npre_ref, sh_ref, sc_ref, x1_ref, h_ref):
    z = _mm(m_ref[0], w_ref[...])
    z = z * lax.rsqrt(jnp.mean(z * z, axis=-1, keepdims=True) + NORM_EPS) * npost_ref[...]
    x1 = x_ref[0] + gate_ref[0] * z
    x1_ref[0] = x1
    y = x1 * lax.rsqrt(jnp.mean(x1 * x1, axis=-1, keepdims=True) + NORM_EPS) * npre_ref[...]
    h_ref[0] = (y * (1.0 + sc_ref[0]) + sh_ref[0]).astype(h_ref.dtype)


def _mix_out(m, w_out, x, gate, n_post, n_pre, shift, scale, tm=256):
    b, t, d = x.shape
    row = lambda bi, i: (bi, i, 0)
    per_b = lambda bi, i: (bi, 0, 0)
    vec = lambda bi, i: (0, 0)
    return pl.pallas_call(
        _mix_out_kernel,
        grid=(b, t // tm),
        in_specs=[pl.BlockSpec((1, tm, d), row),
                  pl.BlockSpec((d, d), vec),
                  pl.BlockSpec((1, tm, d), row),
                  pl.BlockSpec((1, 1, d), per_b),
                  pl.BlockSpec((1, d), vec), pl.BlockSpec((1, d), vec),
                  pl.BlockSpec((1, 1, d), per_b), pl.BlockSpec((1, 1, d), per_b)],
        out_specs=[pl.BlockSpec((1, tm, d), row), pl.BlockSpec((1, tm, d), row)],
        out_shape=[jax.ShapeDtypeStruct((b, t, d), F32), jax.ShapeDtypeStruct((b, t, d), BF16)],
        compiler_params=_cparams(("arbitrary", "arbitrary")),
        name="mix_out",
    )(m, w_out, x, gate, n_post, n_pre, shift, scale)


def _ffn_up_kernel(h_ref, wg_ref, wu_ref, o_ref):
    h = h_ref[0]
    a = _mm(h, wg_ref[...])
    u = _mm(h, wu_ref[...])
    o_ref[0] = (a * _sigmoid(a) * u).astype(o_ref.dtype)


def _ffn_up(h, w_gate, w_up, tm=1024, tn=512):
    b, t, d = h.shape
    f = w_gate.shape[1]
    return pl.pallas_call(
        _ffn_up_kernel,
        grid=(b, t // tm, f // tn),
        in_specs=[pl.BlockSpec((1, tm, d), lambda bi, i, j: (bi, i, 0)),
                  pl.BlockSpec((d, tn), lambda bi, i, j: (0, j)),
                  pl.BlockSpec((d, tn), lambda bi, i, j: (0, j))],
        out_specs=pl.BlockSpec((1, tm, tn), lambda bi, i, j: (bi, i, j)),
        out_shape=jax.ShapeDtypeStruct((b, t, f), BF16),
        compiler_params=_cparams(("arbitrary", "arbitrary", "arbitrary")),
        name="ffn_up",
    )(h, w_gate, w_up)


def _ffn_down_kernel(h_ref, w_ref, x_ref, gate_ref, npost_ref, o_ref, acc):
    kk = pl.program_id(2)

    @pl.when(kk == 0)
    def _():
        acc[...] = jnp.zeros_like(acc)

    acc[...] += _mm(h_ref[0], w_ref[...])

    @pl.when(kk == pl.num_programs(2) - 1)
    def _():
        z = acc[...]
        z = z * lax.rsqrt(jnp.mean(z * z, axis=-1, keepdims=True) + NORM_EPS) * npost_ref[...]
        o_ref[0] = x_ref[0] + gate_ref[0] * z


def _ffn_down(h, w_down, x1, gate, n_post, tm=512, tk=1408):
    b, t, f = h.shape
    d = w_down.shape[1]
    return pl.pallas_call(
        _ffn_down_kernel,
        grid=(b, t // tm, f // tk),
        in_specs=[pl.BlockSpec((1, tm, tk), lambda bi, i, k: (bi, i, k)),
                  pl.BlockSpec((tk, d), lambda bi, i, k: (k, 0)),
                  pl.BlockSpec((1, tm, d), lambda bi, i, k: (bi, i, 0)),
                  pl.BlockSpec((1, 1, d), lambda bi, i, k: (bi, 0, 0)),
                  pl.BlockSpec((1, d), lambda bi, i, k: (0, 0))],
        out_specs=pl.BlockSpec((1, tm, d), lambda bi, i, k: (bi, i, 0)),
        out_shape=jax.ShapeDtypeStruct((b, t, d), F32),
        scratch_shapes=[pltpu.VMEM((tm, d), F32)],
        compiler_params=_cparams(("arbitrary", "arbitrary", "arbitrary")),
        name="ffn_down",
    )(h, w_down, x1, gate, n_post)


def _pad_cols(w, n):
    return jnp.pad(w, ((0, 0), (0, n - w.shape[1])))


def kernel(x, c, ctx, c_ctx, ada_w, ada_b, norm_pre_mix, norm_post_mix, norm_pre_ffn, norm_post_ffn, w_in, shift_mu, rwkv_w0, rwkv_w2, rwkv_a0, rwkv_a2, rwkv_g2, rwkv_k_k, rwkv_k_a, rwkv_r_k, rwkv_ln_g, rwkv_ln_b, w_rwkv_up, gla_alpha_up, gla_alpha_b, gla_norm_g, w_gla_up, w_out, ffn_w_gate, ffn_w_up, ffn_w_down):
    assert ada_w.shape[0] == 1, "single trunk layer"
    bsz, seq, d = x.shape
    n_ctx = ctx.shape[1]
    prec = BF16

    cvecs = jnp.concatenate([c, c_ctx[None, :], jnp.zeros((8 - bsz - 1, d), F32)], 0)
    mod = _modulation(cvecs, ada_w[0], ada_b[0])
    mod_x = mod[:bsz].reshape(bsz, 6, 1, d)
    shx1, scx1, gx1, shx2, scx2, gx2 = (mod_x[:, i] for i in range(6))
    mod_c = jnp.broadcast_to(mod[bsz].reshape(1, 6, 1, d), (bsz, 6, 1, d))
    shc1, scc1 = mod_c[:, 0], mod_c[:, 1]

    w_all = w_in[0]
    mix_in = RWKV_IN + GLA_IN
    w_rwkv = _pad_cols(w_all[:, :RWKV_IN], RWKV_PAD).astype(BF16)
    w_gla = _pad_cols(w_all[:, RWKV_IN:mix_in], GLA_PAD).astype(BF16)
    w_gate = w_all[:, mix_in:].astype(BF16)
    mu = _pad_cols(shift_mu, RWKV_PAD)
    n_pre = norm_pre_mix

    px_rwkv = _project(x, shx1, scx1, n_pre, w_rwkv, 1024, 512, "proj_rwkv")
    px_gla = _project_colmajor(x, shx1, scx1, n_pre, w_gla, 16, 512, "proj_gla")
    px_gate = _project(x, shx1, scx1, n_pre, w_gate, 1024, 1024, "proj_gate")
    pc_rwkv = _project(ctx, shc1, scc1, n_pre, w_rwkv, n_ctx, 512, "proj_rwkv_ctx")
    pc_gla = _project(ctx, shc1, scc1, n_pre, w_gla, n_ctx, 512, "proj_gla_ctx")

    pmix = _shiftmix(px_rwkv, pc_rwkv, mu)
    n_ctx_chunks = n_ctx // CHUNK
    y_dir = []
    for dr in range(2):
        y_dir.append(_rwkv_scan(pmix, rwkv_w2[0, dr], rwkv_w0[0, dr][None, :], rwkv_a2[0, dr], rwkv_a0[0, dr][None, :],
                                rwkv_k_k, rwkv_k_a, dr == 1, n_ctx_chunks, prec))
    ya = _rwkv_post(y_dir[0], y_dir[1], pmix, n_ctx, rwkv_a2[0], rwkv_a0[0][:, None, :], rwkv_g2[0], rwkv_k_a,
                    rwkv_r_k.reshape(1, RWKV_WIDTH), rwkv_ln_g, rwkv_ln_b)

    o_dir = [_gla_scan(px_gla, pc_gla, gla_alpha_up[0, dr], gla_alpha_b[0, dr][None, :], dr == 1, prec) for dr in range(2)]
    yb = _gla_post(o_dir[0], o_dir[1], px_gla, gla_norm_g)

    m = _merge(ya, yb, w_rwkv_up[0].astype(BF16), w_gla_up[0].astype(BF16), px_gate)
    x1, h2 = _mix_out(m, w_out[0].astype(BF16), x, gx1, norm_post_mix, norm_pre_ffn, shx2, scx2)
    hf = _ffn_up(h2, ffn_w_gate[0].astype(BF16), ffn_w_up[0].astype(BF16))
    return _ffn_down(hf, ffn_w_down[0].astype(BF16), x1, gx2, norm_post_ffn)
```

```python
import functools

import jax
import jax.numpy as jnp
from jax import lax
from jax.experimental import pallas as pl
from jax.experimental.pallas import tpu as pltpu

F32 = jnp.float32
BF16 = jnp.bfloat16
HIGHEST = lax.Precision.HIGHEST

LANES = 128
VMEM_LIMIT_BYTES = 56 * 1024 * 1024

GRID_W = 64
CHUNK = 64
RWKV_HEADS, RWKV_HEAD_DIM = 16, 64
RWKV_WIDTH = RWKV_HEADS * RWKV_HEAD_DIM
DECAY_LORA = ICL_LORA = 96
GATE_LORA = 64
RWKV_GN_EPS = 64e-5
GLA_HEADS, GLA_KEY_DIM, GLA_VAL_DIM = 4, 128, 256
GLA_QK_WIDTH = GLA_HEADS * GLA_KEY_DIM
GLA_V_WIDTH = GLA_HEADS * GLA_VAL_DIM
GLA_GATE_LORA = 16
GLA_TAU = 16.0
GLA_NORM_EPS = 1e-5
GLA_SUB = 16
NORM_EPS = 1e-6
PROLOGUE_ROWS = 256

RWKV_IN = 3 * RWKV_WIDTH + 2 * DECAY_LORA + 2 * ICL_LORA + GATE_LORA
RWKV_PAD = 3584
OFF_WD = 3 * RWKV_WIDTH
OFF_AD = OFF_WD + 2 * DECAY_LORA
OFF_GD = OFF_AD + 2 * ICL_LORA
GLA_IN = 2 * GLA_QK_WIDTH + 2 * GLA_V_WIDTH + 2 * GLA_GATE_LORA
GLA_BLK = 512
GLA_PAD = 7 * GLA_BLK
OFF_GLA_AD = 6 * GLA_BLK

PAIR = 2 * RWKV_HEAD_DIM
N_PAIRS = RWKV_HEADS // 2


def _cparams(semantics):
    return pltpu.CompilerParams(dimension_semantics=semantics, vmem_limit_bytes=VMEM_LIMIT_BYTES)


def _sigmoid(z):
    return 1.0 / (1.0 + jnp.exp(-z))


def _softplus(z):
    return jnp.maximum(z, 0.0) + jnp.log(1.0 + jnp.exp(-jnp.abs(z)))


def _dot(a, b, dims, precision):
    if precision is BF16:
        a, b, precision = a.astype(BF16), b.astype(BF16), None
    return lax.dot_general(a, b, (dims, ((), ())), precision=precision, preferred_element_type=F32)


def _mm(a, b, precision=None):
    return _dot(a, b, ((1,), (0,)), precision)


def _mm_nt(a, b, precision=None):
    return _dot(a, b, ((1,), (1,)), precision)


def _mm_tn(a, b, precision=None):
    return _dot(a, b, ((0,), (0,)), precision)


def _split2(z):
    hi = z.astype(BF16)
    return hi, (z - hi.astype(F32)).astype(BF16)


def _mm_split(a, b):
    ah, al = _split2(a)
    bh, bl = _split2(b)
    return (_mm(al, bh) + _mm(ah, bl)) + _mm(ah, bh)


def _mm_exact_lhs(e, b):
    hi = b.astype(BF16)
    r = b - hi.astype(F32)
    mid = r.astype(BF16)
    lo = (r - mid.astype(F32)).astype(BF16)
    e = e.astype(BF16)
    return (_mm(e, lo) + _mm(e, mid)) + _mm(e, hi)


def _mod_kernel(c_ref, w_ref, b_ref, o_ref):
    s = c_ref[...]
    s = s * _sigmoid(s)
    o_ref[...] = _mm_split(s, w_ref[...]) + b_ref[...]


def _modulation(cvecs, ada_w, ada_b, tn=1024):
    m, d = cvecs.shape
    n = ada_w.shape[1]
    return pl.pallas_call(
        _mod_kernel,
        grid=(n // tn,),
        in_specs=[pl.BlockSpec((m, d), lambda j: (0, 0)),
                  pl.BlockSpec((d, tn), lambda j: (0, j)),
                  pl.BlockSpec((1, tn), lambda j: (0, j))],
        out_specs=pl.BlockSpec((m, tn), lambda j: (0, j)),
        out_shape=jax.ShapeDtypeStruct((m, n), F32),
        compiler_params=_cparams(("arbitrary",)),
        name="adaln_mod",
    )(cvecs, ada_w, ada_b.reshape(1, n))


def _proj_kernel(x_ref, sh_ref, sc_ref, g_ref, w_ref, o_ref, h_scr):
    @pl.when(pl.program_id(2) == 0)
    def _():
        rows = min(PROLOGUE_ROWS, h_scr.shape[0])

        def body(rb, carry):
            sl = pl.ds(pl.multiple_of(rb * rows, rows), rows)
            x = x_ref[0, sl, :]
            ms = jnp.mean(x * x, axis=-1, keepdims=True)
            y = x * lax.rsqrt(ms + NORM_EPS) * g_ref[...]
            h_scr[sl, :] = (y * (1.0 + sc_ref[0]) + sh_ref[0]).astype(BF16)
            return carry

        lax.fori_loop(0, h_scr.shape[0] // rows, body, 0)

    o_ref[0] = _mm(h_scr[...], w_ref[...])


def _project(x, shift, scale, gain, w, tm, tn, name):
    b, t, d = x.shape
    n = w.shape[1]
    return pl.pallas_call(
        _proj_kernel,
        grid=(b, t // tm, n // tn),
        in_specs=[pl.BlockSpec((1, tm, d), lambda bi, i, j: (bi, i, 0)),
                  pl.BlockSpec((1, 1, d), lambda bi, i, j: (bi, 0, 0)),
                  pl.BlockSpec((1, 1, d), lambda bi, i, j: (bi, 0, 0)),
                  pl.BlockSpec((1, d), lambda bi, i, j: (0, 0)),
                  pl.BlockSpec((d, tn), lambda bi, i, j: (0, j))],
        out_specs=pl.BlockSpec((1, tm, tn), lambda bi, i, j: (bi, i, j)),
        out_shape=jax.ShapeDtypeStruct((b, t, n), F32),
        scratch_shapes=[pltpu.VMEM((tm, d), BF16)],
        compiler_params=_cparams(("arbitrary", "arbitrary", "arbitrary")),
        name=name,
    )(x, shift, scale, gain, w)


def _colmajor_perm(n_rows, n_cols):
    dst = jnp.arange(n_rows * n_cols)
    src = (dst % n_rows) * n_cols + dst // n_rows
    return (src[:, None] == dst[None, :]).astype(BF16)


def _proj_cm_kernel(x_ref, sh_ref, sc_ref, g_ref, perm_ref, w_ref, o_ref, hr_scr, h_scr):
    @pl.when(pl.program_id(2) == 0)
    def _():
        n_r, n_c, d = x_ref.shape[1:]
        rb = PROLOGUE_ROWS // n_c
        rows = rb * n_c

        def body(kb, carry):
            x = x_ref[0, pl.ds(pl.multiple_of(kb * rb, rb), rb), :, :].reshape(rows, d)
            ms = jnp.mean(x * x, axis=-1, keepdims=True)
            y = x * lax.rsqrt(ms + NORM_EPS) * g_ref[...]
            hr_scr[pl.ds(pl.multiple_of(kb * rows, rows), rows), :] = (y * (1.0 + sc_ref[0]) + sh_ref[0]).astype(BF16)
            return carry

        lax.fori_loop(0, n_r // rb, body, 0)
        for cb in range(d // GLA_BLK):
            cs = slice(cb * GLA_BLK, (cb + 1) * GLA_BLK)
            h_scr[:, cs] = _mm(perm_ref[...], hr_scr[:, cs]).astype(BF16)

    o_ref[0] = _mm(h_scr[...], w_ref[...])


def _project_colmajor(x, shift, scale, gain, w, n_cols, tn, name):
    b, t, d = x.shape
    n = w.shape[1]
    n_r = t // GRID_W
    tm = n_r * n_cols
    x4 = x.reshape(b, n_r, GRID_W, d)
    perm = _colmajor_perm(n_r, n_cols)
    return pl.pallas_call(
        _proj_cm_kernel,
        grid=(b, GRID_W // n_cols, n // tn),
        in_specs=[pl.BlockSpec((1, n_r, n_cols, d), lambda bi, i, j: (bi, 0, i, 0)),
                  pl.BlockSpec((1, 1, d), lambda bi, i, j: (bi, 0, 0)),
                  pl.BlockSpec((1, 1, d), lambda bi, i, j: (bi, 0, 0)),
                  pl.BlockSpec((1, d), lambda bi, i, j: (0, 0)),
                  pl.BlockSpec((tm, tm), lambda bi, i, j: (0, 0)),
                  pl.BlockSpec((d, tn), lambda bi, i, j: (0, j))],
        out_specs=pl.BlockSpec((1, tm, tn), lambda bi, i, j: (bi, i, j)),
        out_shape=jax.ShapeDtypeStruct((b, t, n), F32),
        scratch_shapes=[pltpu.VMEM((tm, d), BF16), pltpu.VMEM((tm, d), BF16)],
        compiler_params=_cparams(("arbitrary", "arbitrary", "arbitrary")),
        name=name,
    )(x4, shift, scale, gain, perm, w)


def _norm_modulate(x, g_ref, sh_ref, sc_ref):
    ms = jnp.mean(x * x, axis=-1, keepdims=True)
    y = x * lax.rsqrt(ms + NORM_EPS) * g_ref[...]
    return (y * (1.0 + sc_ref[0]) + sh_ref[0]).astype(BF16)


def _proj_shift_kernel(x_ref, xp_ref, xn_ref, sh_ref, sc_ref, g_ref, w_ref, mu_ref, o_ref, h_scr, halo_scr):
    i = pl.program_id(1)
    tm = h_scr.shape[0]

    @pl.when(pl.program_id(2) == 0)
    def _():
        rows = min(PROLOGUE_ROWS, tm)

        def body(rb, carry):
            sl = pl.ds(pl.multiple_of(rb * rows, rows), rows)
            h_scr[sl, :] = _norm_modulate(x_ref[0, sl, :], g_ref, sh_ref, sc_ref)
            return carry

        lax.fori_loop(0, tm // rows, body, 0)
        halo_scr[...] = _norm_modulate(jnp.concatenate([xp_ref[0], xn_ref[0]], 0), g_ref, sh_ref, sc_ref)

    p = _mm(h_scr[...], w_ref[...])
    ph = _mm(halo_scr[...], w_ref[...])
    before = jnp.where(i == 0, 0.0, ph[7:8])
    after = jnp.where(i == pl.num_programs(1) - 1, 0.0, ph[8:9])
    row = lax.broadcasted_iota(jnp.int32, p.shape, 0)
    prev = jnp.where(row == 0, before, pltpu.roll(p, 1, 0))
    nxt = jnp.where(row == tm - 1, after, pltpu.roll(p, tm - 1, 0))
    o_ref[0] = p + mu_ref[...] * (0.5 * (prev + nxt) - p)


def _project_shift(x, shift, scale, gain, w, mu, tm, tn, name):
    b, t, d = x.shape
    n = w.shape[1]
    r8, n8 = tm // 8, t // 8
    return pl.pallas_call(
        _proj_shift_kernel,
        grid=(b, t // tm, n // tn),
        in_specs=[pl.BlockSpec((1, tm, d), lambda bi, i, j: (bi, i, 0)),
                  pl.BlockSpec((1, 8, d), lambda bi, i, j: (bi, jnp.maximum(i * r8 - 1, 0), 0)),
                  pl.BlockSpec((1, 8, d), lambda bi, i, j: (bi, jnp.minimum((i + 1) * r8, n8 - 1), 0)),
                  pl.BlockSpec((1, 1, d), lambda bi, i, j: (bi, 0, 0)),
                  pl.BlockSpec((1, 1, d), lambda bi, i, j: (bi, 0, 0)),
                  pl.BlockSpec((1, d), lambda bi, i, j: (0, 0)),
                  pl.BlockSpec((d, tn), lambda bi, i, j: (0, j)),
                  pl.BlockSpec((1, tn), lambda bi, i, j: (0, j))],
        out_specs=pl.BlockSpec((1, tm, tn), lambda bi, i, j: (bi, i, j)),
        out_shape=jax.ShapeDtypeStruct((b, t, n), F32),
        scratch_shapes=[pltpu.VMEM((tm, d), BF16), pltpu.VMEM((16, d), BF16)],
        compiler_params=_cparams(("arbitrary", "arbitrary", "arbitrary")),
        name=name,
    )(x, x, x, shift, scale, gain, w, mu)


def _head_sum_matrix():
    r = lax.broadcasted_iota(jnp.int32, (PAIR, PAIR), 0)
    c = lax.broadcasted_iota(jnp.int32, (PAIR, PAIR), 1)
    return jnp.where((r >> 6) == (c >> 6), 1.0, 0.0).astype(BF16)


def _head_sum(z, hsum):
    hi = z.astype(BF16)
    lo = (z - hi.astype(F32)).astype(BF16)
    return _mm(hi, hsum) + _mm(lo, hsum)


def _rwkv_kernel(rev, prec, n_ctx_chunks, pl_ref, pc_ref, w2_ref, w0_ref, a2_ref, a0_ref, kk_ref, ka_ref, y_ref, st_scr):
    hd = RWKV_HEAD_DIM
    c2 = 2 * CHUNK

    @pl.when(pl.program_id(1) == 0)
    def _():
        st_scr[...] = jnp.zeros_like(st_scr)

    p = jnp.where(pl.program_id(1) < n_ctx_chunks, pc_ref[0], pl_ref[0])
    r = p[:, 0:RWKV_WIDTH]
    k = p[:, RWKV_WIDTH:2 * RWKV_WIDTH]
    v = p[:, 2 * RWKV_WIDTH:3 * RWKV_WIDTH]
    d_off = DECAY_LORA if rev else 0
    wd = p[:, OFF_WD + d_off:OFF_WD + d_off + DECAY_LORA]
    ad = p[:, OFF_AD + d_off:OFF_AD + d_off + ICL_LORA]

    w_log = -_softplus(-(w0_ref[...] + _mm_split(jnp.tanh(wd), w2_ref[...]))) - 0.5
    lw = -jnp.exp(w_log)
    a = _sigmoid(a0_ref[...] + _mm(ad, a2_ref[...], BF16))
    kk_raw = k * kk_ref[...]
    kd = k * (1.0 + (a - 1.0) * ka_ref[...])

    ri = lax.broadcasted_iota(jnp.int32, (c2, PAIR), 0)
    li = lax.broadcasted_iota(jnp.int32, (c2, PAIR), 1)
    rt, lt = ri & (CHUNK - 1), li & (hd - 1)
    same_head = (ri >> 6) == (li >> 6)
    strict = (lt > rt) if rev else (lt < rt)
    mask_n = jnp.logical_and(same_head, strict)
    mask_k = jnp.logical_and(jnp.logical_not(same_head), strict)
    ident = ri == li
    eye = jnp.where(ident, 1.0, 0.0).astype(F32)
    rc = lax.broadcasted_iota(jnp.int32, (CHUNK, PAIR), 0)
    lc = lax.broadcasted_iota(jnp.int32, (CHUNK, PAIR), 1)
    incl_c = ((lc & (hd - 1)) >= rc) if rev else ((lc & (hd - 1)) <= rc)
    lane_e = lc < hd
    hsum = _head_sum_matrix()

    ci = lax.broadcasted_iota(jnp.int32, (CHUNK, CHUNK), 0)
    cj = lax.broadcasted_iota(jnp.int32, (CHUNK, CHUNK), 1)
    tri = jnp.where((cj >= ci) if rev else (cj <= ci), 1.0, 0.0).astype(F32)
    cum = _mm_exact_lhs(tri, lw)
    total = cum[0:1] if rev else cum[CHUNK - 1:CHUNK]
    e_prev = jnp.exp(cum - lw)
    e_neg = jnp.exp(-cum)
    e_pos = jnp.exp(cum)
    e_rest = jnp.exp(total - cum)
    p_end = jnp.exp(total)

    def split(z):
        ze = jnp.where(lane_e, z, 0.0)
        return ze, z - ze

    pairs = range(N_PAIRS)
    sls = [slice(pr * PAIR, (pr + 1) * PAIR) for pr in pairs]
    cat0 = lambda *z: jnp.concatenate(z, 0)

    kkr = [kk_raw[:, sl] for sl in sls]
    nrm2 = [_head_sum(z * z, hsum) for z in kkr]
    kk = [z / jnp.maximum(jnp.sqrt(n), 1e-12) for z, n in zip(kkr, nrm2)]
    bb = [z * a[:, sl] for z, sl in zip(kk, sls)]
    at = [split(z * e_prev[:, sl]) for z, sl in zip(kk, sls)]
    bt = [z * e_neg[:, sl] for z, sl in zip(bb, sls)]
    kt = [kd[:, sl] * e_neg[:, sl] for sl in sls]
    rt_ = [r[:, sl] * e_pos[:, sl] for sl in sls]
    rs = [split(z) for z in rt_]
    vs = [split(v[:, sl]) for sl in sls]
    bh = [cat0(*split(z * e_rest[:, sl])) for z, sl in zip(bb, sls)]
    kh = [cat0(*split(kd[:, sl] * e_rest[:, sl])) for sl in sls]
    v_swap = [cat0(vo, ve) for ve, vo in vs]
    v_stack = [cat0(ve, vo) for ve, vo in vs]

    g_e = [_mm_nt(cat0(at[i][0], rs[i][0]), cat0(bt[i], kt[i]), prec) for i in pairs]
    g_o = [_mm_nt(cat0(at[i][1], rs[i][1]), cat0(kt[i], bt[i]), prec) for i in pairs]
    g_top = [cat0(g_e[i][0:CHUNK], g_o[i][0:CHUNK]) for i in pairs]
    nbd = [jnp.where(mask_n, z, 0.0) for z in g_top]
    aak = [jnp.where(mask_k, z, 0.0) for z in g_top]
    rab = [jnp.where(incl_c, jnp.where(lane_e, g_e[i][CHUNK:c2], g_o[i][CHUNK:c2]), 0.0) for i in pairs]
    rak = [jnp.where(incl_c, jnp.where(lane_e, g_o[i][CHUNK:c2], g_e[i][CHUNK:c2]), 0.0) for i in pairs]

    x = [_mm(aak[i], v_swap[i], prec) for i in pairs]
    n2 = [_mm(z, z, prec) for z in nbd]
    y0b = [_mm(rak[i], v_swap[i], prec) for i in pairs]
    n4 = [_mm(z, z, prec) for z in n2]
    imn = [eye - z for z in nbd]
    p1 = [imn[i] + _mm(imn[i], n2[i], prec) for i in pairs]
    n8 = [_mm(z, z, prec) for z in n4]
    nb = [_mm_tn(kh[i], v_stack[i], prec) for i in pairs]
    n16 = [_mm(z, z, prec) for z in n8]
    p2 = [eye + n4[i] + n8[i] + _mm(n4[i], n8[i], prec) for i in pairs]
    n32 = [_mm(z, z, prec) for z in n16]
    p12 = [_mm(p1[i], p2[i], prec) for i in pairs]
    p3 = [eye + n16[i] + n32[i] + _mm(n16[i], n32[i], prec) for i in pairs]
    tinv = [_mm(p12[i], p3[i], prec) for i in pairs]
    wu = [-_mm(tinv[i], jnp.concatenate([cat0(*at[i]), x[i]], 1), prec) for i in pairs]
    qy = [_mm(rab[i], wu[i], prec) for i in pairs]
    mn = [_mm_tn(bh[i], wu[i], prec) for i in pairs]
    q = [rt_[i] + qy[i][:, 0:PAIR] for i in pairs]
    m = [jnp.where(ident, p_end[:, sls[i]], 0.0) + mn[i][:, 0:PAIR] for i in pairs]
    qm = [_mm(cat0(q[i], m[i]), st_scr[i], prec) for i in pairs]
    for i in pairs:
        y_ref[0, :, sls[i]] = qm[i][0:CHUNK] + qy[i][:, PAIR:2 * PAIR] + y0b[i]
        st_scr[i] = qm[i][CHUNK:CHUNK + PAIR] + mn[i][:, PAIR:2 * PAIR] + nb[i]


def _rwkv_scan(p_lat, p_ctx, w2, w0, a2, a0, k_k, k_a, rev, prec):
    b, t, w = p_lat.shape
    n_lat = t // CHUNK
    n_ctx_chunks = p_ctx.shape[1] // CHUNK
    n_chunks = n_lat + n_ctx_chunks

    def lat_of(i):
        if not rev:
            return jnp.maximum(i - n_ctx_chunks, 0)
        return jnp.where(i < n_ctx_chunks, n_lat - 1, n_chunks - 1 - i)

    def ctx_of(i):
        if not rev:
            return jnp.minimum(i, n_ctx_chunks - 1)
        return jnp.maximum(n_ctx_chunks - 1 - i, 0)

    out_of = lat_of
    vec = lambda bi, i: (0, 0)
    return pl.pallas_call(
        functools.partial(_rwkv_kernel, rev, prec, n_ctx_chunks),
        grid=(b, n_chunks),
        in_specs=[pl.BlockSpec((1, CHUNK, w), lambda bi, i: (bi, lat_of(i), 0)),
                  pl.BlockSpec((1, CHUNK, w), lambda bi, i: (bi, ctx_of(i), 0)),
                  pl.BlockSpec((DECAY_LORA, RWKV_WIDTH), vec),
                  pl.BlockSpec((1, RWKV_WIDTH), vec),
                  pl.BlockSpec((ICL_LORA, RWKV_WIDTH), vec),
                  pl.BlockSpec((1, RWKV_WIDTH), vec),
                  pl.BlockSpec((1, RWKV_WIDTH), vec),
                  pl.BlockSpec((1, RWKV_WIDTH), vec)],
        out_specs=pl.BlockSpec((1, CHUNK, RWKV_WIDTH), lambda bi, i: (bi, out_of(i), 0)),
        out_shape=jax.ShapeDtypeStruct((b, n_lat * CHUNK, RWKV_WIDTH), F32),
        scratch_shapes=[pltpu.VMEM((N_PAIRS, PAIR, PAIR), F32)],
        compiler_params=_cparams(("arbitrary", "arbitrary")),
        name="rwkv7_bwd" if rev else "rwkv7_fwd",
    )(p_lat, p_ctx, w2, w0, a2, a0, k_k, k_a)


def _gla_kernel(rev, prec, n_ctx_chunks, q_ref, k_ref, v0_ref, v1_ref, ad_ref, ctx_ref, aup_ref, ab_ref, o_ref, st_scr):
    i = pl.program_id(1)

    @pl.when(i == 0)
    def _():
        st_scr[...] = jnp.zeros_like(st_scr)

    is_ctx = i < n_ctx_chunks
    pc = ctx_ref[0]
    q = jnp.where(is_ctx, pc[:, 0:GLA_BLK], q_ref[0])
    k = jnp.where(is_ctx, pc[:, GLA_BLK:2 * GLA_BLK], k_ref[0])
    v = jnp.concatenate([jnp.where(is_ctx, pc[:, 2 * GLA_BLK:3 * GLA_BLK], v0_ref[0]),
                         jnp.where(is_ctx, pc[:, 3 * GLA_BLK:4 * GLA_BLK], v1_ref[0])], 1)
    d_off = GLA_GATE_LORA if rev else 0
    ad = jnp.where(is_ctx, pc[:, OFF_GLA_AD:OFF_GLA_AD + LANES], ad_ref[0][:, 0:LANES])
    ad = ad[:, d_off:d_off + GLA_GATE_LORA]

    la = -_softplus(-(_mm_split(ad, aup_ref[...]) + ab_ref[...])) * (1.0 / GLA_TAU)
    ci = lax.broadcasted_iota(jnp.int32, (CHUNK, CHUNK), 0)
    cj = lax.broadcasted_iota(jnp.int32, (CHUNK, CHUNK), 1)
    tri = jnp.where((cj >= ci) if rev else (cj <= ci), 1.0, 0.0).astype(F32)
    cum = _mm_exact_lhs(tri, la)
    total = cum[0:1] if rev else cum[CHUNK - 1:CHUNK]

    dk, dv, sb = GLA_KEY_DIM, GLA_VAL_DIM, GLA_SUB
    n_sb = CHUNK // sb
    row = lax.broadcasted_iota(jnp.int32, (CHUNK, dk), 0)
    row_in = row & (sb - 1)
    arow = lax.broadcasted_iota(jnp.int32, (CHUNK, CHUNK), 0)
    acol = lax.broadcasted_iota(jnp.int32, (CHUNK, CHUNK), 1)
    scale = GLA_KEY_DIM ** -0.5

    heads = range(GLA_HEADS)
    ksl = [slice(h * dk, (h + 1) * dk) for h in heads]
    qh = [q[:, s_] * scale for s_ in ksl]
    kh = [k[:, s_] for s_ in ksl]
    bh = [cum[:, s_] for s_ in ksl]
    lah = [la[:, s_] for s_ in ksl]
    toth = [total[:, s_] for s_ in ksl]
    vh = [v[:, h * dv:(h + 1) * dv] for h in heads]
    st = [st_scr[h] for h in heads]

    o_inter = [_mm_nt(qh[h] * jnp.exp(bh[h]), st[h], prec) for h in heads]
    st_new = [st[h] * jnp.exp(toth[h]) + _mm_tn(vh[h], kh[h] * jnp.exp(toth[h] - bh[h]), prec) for h in heads]

    off_rows = [[] for _ in heads]
    for blk in range(n_sb):
        rs = slice(blk * sb, (blk + 1) * sb)
        first = blk * sb + (sb - 1 if rev else 0)
        is_first_blk = (blk == n_sb - 1) if rev else (blk == 0)
        before = (row >= (blk + 1) * sb) if rev else (row < blk * sb)
        for h in heads:
            if is_first_blk:
                off_rows[h].append(jnp.zeros((sb, CHUNK), F32))
                continue
            beta = bh[h][first:first + 1] - lah[h][first:first + 1]
            qs = qh[h][rs] * jnp.exp(bh[h][rs] - beta)
            ksc = jnp.where(before, kh[h] * jnp.exp(jnp.minimum(beta - bh[h], 0.0)), 0.0)
            off_rows[h].append(_mm_nt(qs, ksc, prec))
    att = [jnp.concatenate(off_rows[h], 0) for h in heads]

    for s in range(sb):
        pick = lambda z: jnp.concatenate(
            [jnp.broadcast_to(z[blk * sb + s:blk * sb + s + 1], (sb, dk)) for blk in range(n_sb)], 0)
        ok = (row_in <= s) if rev else (row_in >= s)
        tgt = jnp.logical_and(acol == (arow & ~(sb - 1)) + s,
                              ((arow & (sb - 1)) <= s) if rev else ((arow & (sb - 1)) >= s))
        for h in heads:
            e = jnp.exp(jnp.where(ok, bh[h] - pick(bh[h]), 0.0))
            col = jnp.sum(jnp.where(ok, qh[h] * pick(kh[h]) * e, 0.0), axis=-1, keepdims=True)
            att[h] = jnp.where(tgt, col, att[h])

    for h in heads:
        o_ref[0, :, h * dv:(h + 1) * dv] = o_inter[h] + _mm(att[h], vh[h], prec)
        st_scr[h] = st_new[h]


def _gla_scan(p_lat, p_ctx, alpha_up, alpha_b, rev, prec):
    b, t, w = p_lat.shape
    assert w == GLA_PAD and t == GRID_W * CHUNK
    n_cols = GRID_W
    n_ctx_chunks = p_ctx.shape[1] // CHUNK
    n_steps = n_ctx_chunks + n_cols

    def col_of(i):
        if not rev:
            return jnp.maximum(i - n_ctx_chunks, 0)
        return jnp.where(i < n_ctx_chunks, n_cols - 1, n_steps - 1 - i)

    def ctx_of(i):
        if not rev:
            return jnp.minimum(i, n_ctx_chunks - 1)
        return jnp.maximum(n_ctx_chunks - 1 - i, 0)

    def lat_spec(m):
        return pl.BlockSpec((1, CHUNK, GLA_BLK), lambda bi, i: (bi, col_of(i), m))

    vec = lambda bi, i: (0, 0)
    return pl.pallas_call(
        functools.partial(_gla_kernel, rev, prec, n_ctx_chunks),
        grid=(b, n_steps),
        in_specs=[lat_spec(0), lat_spec(1), lat_spec(2), lat_spec(3), lat_spec(6),
                  pl.BlockSpec((1, CHUNK, w), lambda bi, i: (bi, ctx_of(i), 0)),
                  pl.BlockSpec((GLA_GATE_LORA, GLA_QK_WIDTH), vec),
                  pl.BlockSpec((1, GLA_QK_WIDTH), vec)],
        out_specs=pl.BlockSpec((1, CHUNK, GLA_V_WIDTH), lambda bi, i: (bi, col_of(i), 0)),
        out_shape=jax.ShapeDtypeStruct((b, t, GLA_V_WIDTH), F32),
        scratch_shapes=[pltpu.VMEM((GLA_HEADS, GLA_VAL_DIM, GLA_KEY_DIM), F32)],
        compiler_params=_cparams(("arbitrary", "arbitrary")),
        name="gla_bwd" if rev else "gla_fwd",
    )(p_lat, p_lat, p_lat, p_lat, p_lat, p_ctx, alpha_up, alpha_b)


def _rwkv_post_kernel(yf_ref, yb_ref, pm_ref, a2_ref, a0_ref, g2_ref, ka_ref, rk_ref, lng_ref, lnb_ref, o_ref):
    p = pm_ref[0]
    r = p[:, 0:RWKV_WIDTH]
    k = p[:, RWKV_WIDTH:2 * RWKV_WIDTH]
    v = p[:, 2 * RWKV_WIDTH:3 * RWKV_WIDTH]
    ad_f = p[:, OFF_AD:OFF_AD + ICL_LORA]
    ad_b = p[:, OFF_AD + ICL_LORA:OFF_AD + 2 * ICL_LORA]
    gd = p[:, OFF_GD:OFF_GD + GATE_LORA]
    ka = ka_ref[...]
    a_f = _sigmoid(a0_ref[0] + _mm(ad_f, a2_ref[0], BF16))
    a_b = _sigmoid(a0_ref[1] + _mm(ad_b, a2_ref[1], BF16))
    kd_sum = k * (1.0 + (a_f - 1.0) * ka) + k * (1.0 + (a_b - 1.0) * ka)
    gate = _mm(_sigmoid(gd), g2_ref[...], BF16)
    rkk = r * kd_sum * rk_ref[...]
    ysum = yf_ref[0] + yb_ref[0]
    hsum = _head_sum_matrix()
    inv_n = 1.0 / RWKV_HEAD_DIM
    for pr in range(N_PAIRS):
        sl = slice(pr * PAIR, (pr + 1) * PAIR)
        ys = ysum[:, sl]
        mean = _head_sum(ys, hsum) * inv_n
        dlt = ys - mean
        var = _head_sum(dlt * dlt, hsum) * inv_n
        gn = dlt * lax.rsqrt(var + RWKV_GN_EPS) * lng_ref[:, sl] + lnb_ref[:, sl]
        bonus = _head_sum(rkk[:, sl], hsum) * v[:, sl]
        o_ref[0, :, sl] = ((gn + bonus) * gate[:, sl]).astype(o_ref.dtype)


def _rwkv_post(y_f, y_b, pmix, a2, a0, g2, k_a, r_k, ln_g, ln_b, tm=256):
    b, t, w = y_f.shape
    row = lambda bi, i: (bi, i, 0)
    vec = lambda bi, i: (0, 0)
    vec3 = lambda bi, i: (0, 0, 0)
    return pl.pallas_call(
        _rwkv_post_kernel,
        grid=(b, t // tm),
        in_specs=[pl.BlockSpec((1, tm, w), row),
                  pl.BlockSpec((1, tm, w), row),
                  pl.BlockSpec((1, tm, pmix.shape[2]), row),
                  pl.BlockSpec((2, ICL_LORA, w), vec3),
                  pl.BlockSpec((2, 1, w), vec3),
                  pl.BlockSpec((GATE_LORA, w), vec),
                  pl.BlockSpec((1, w), vec), pl.BlockSpec((1, w), vec),
                  pl.BlockSpec((1, w), vec), pl.BlockSpec((1, w), vec)],
        out_specs=pl.BlockSpec((1, tm, w), row),
        out_shape=jax.ShapeDtypeStruct((b, t, w), BF16),
        compiler_params=_cparams(("arbitrary", "arbitrary")),
        name="rwkv_post",
    )(y_f, y_b, pmix, a2, a0, g2, k_a, r_k, ln_g, ln_b)


def _gla_post_kernel(of_ref, ob_ref, g_ref, ng_ref, perm_ref, o_ref):
    n_r, n_c = o_ref.shape[1:3]
    dv = GLA_VAL_DIM
    for h in range(GLA_HEADS):
        sl = slice(h * dv, (h + 1) * dv)
        oh = of_ref[0, :, sl] + ob_ref[0, :, sl]
        oh = oh * lax.rsqrt(jnp.mean(oh * oh, axis=-1, keepdims=True) + GLA_NORM_EPS) * ng_ref[:, sl]
        gh = g_ref[0, :, sl]
        y = (oh * (gh * _sigmoid(gh))).astype(BF16)
        o_ref[0, :, :, sl] = _mm(perm_ref[...], y).reshape(n_r, n_c, dv).astype(o_ref.dtype)


def _gla_post(o_f, o_b, p_gla, norm_g, n_cols=16):
    b, t, w = o_f.shape
    n_r = t // GRID_W
    tm = n_r * n_cols
    perm_t = _colmajor_perm(n_r, n_cols).T
    row = lambda bi, i: (bi, i, 0)
    out = pl.pallas_call(
        _gla_post_kernel,
        grid=(b, GRID_W // n_cols),
        in_specs=[pl.BlockSpec((1, tm, w), row),
                  pl.BlockSpec((1, tm, w), row),
                  pl.BlockSpec((1, tm, w), lambda bi, i: (bi, i, 2)),
                  pl.BlockSpec((1, w), lambda bi, i: (0, 0)),
                  pl.BlockSpec((tm, tm), lambda bi, i: (0, 0))],
        out_specs=pl.BlockSpec((1, n_r, n_cols, w), lambda bi, i: (bi, 0, i, 0)),
        out_shape=jax.ShapeDtypeStruct((b, n_r, GRID_W, w), BF16),
        compiler_params=_cparams(("arbitrary", "arbitrary")),
        name="gla_post",
    )(o_f, o_b, p_gla, norm_g, perm_t)
    return out.reshape(b, t, w)


def _merge_kernel(ya_ref, yb_ref, wr_ref, wg_ref, ga_ref, gb_ref, o_ref):
    ma = _mm(ya_ref[0], wr_ref[...])
    mb = _mm(yb_ref[0], wg_ref[...])
    o_ref[0] = (_sigmoid(ga_ref[0]) * ma + _sigmoid(gb_ref[0]) * mb).astype(o_ref.dtype)


def _merge(ya, yb, w_r, w_g, p_gate, tm=512, tn=1024):
    b, t, w = ya.shape
    d = w_r.shape[1]
    nj = d // tn
    return pl.pallas_call(
        _merge_kernel,
        grid=(b, t // tm, nj),
        in_specs=[pl.BlockSpec((1, tm, w), lambda bi, i, j: (bi, i, 0)),
                  pl.BlockSpec((1, tm, w), lambda bi, i, j: (bi, i, 0)),
                  pl.BlockSpec((w, tn), lambda bi, i, j: (0, j)),
                  pl.BlockSpec((w, tn), lambda bi, i, j: (0, j)),
                  pl.BlockSpec((1, tm, tn), lambda bi, i, j: (bi, i, j)),
                  pl.BlockSpec((1, tm, tn), lambda bi, i, j: (bi, i, j + nj))],
        out_specs=pl.BlockSpec((1, tm, tn), lambda bi, i, j: (bi, i, j)),
        out_shape=jax.ShapeDtypeStruct((b, t, d), BF16),
        compiler_params=_cparams(("arbitrary", "arbitrary", "arbitrary")),
        name="merge_branches",
    )(ya, yb, w_r, w_g, p_gate, p_gate)


def _mix_out_kernel(m_ref, w_ref, x_ref, gate_ref, npost_ref, npre_ref, sh_ref, sc_ref, x1_ref, h_ref):
    z = _mm(m_ref[0], w_ref[...])
    z = z * lax.rsqrt(jnp.mean(z * z, axis=-1, keepdims=True) + NORM_EPS) * npost_ref[...]
    x1 = x_ref[0] + gate_ref[0] * z
    x1_ref[0] = x1
    y = x1 * lax.rsqrt(jnp.mean(x1 * x1, axis=-1, keepdims=True) + NORM_EPS) * npre_ref[...]
    h_ref[0] = (y * (1.0 + sc_ref[0]) + sh_ref[0]).astype(h_ref.dtype)


def _mix_out(m, w_out, x, gate, n_post, n_pre, shift, scale, tm=256):
    b, t, d = x.shape
    row = lambda bi, i: (bi, i, 0)
    per_b = lambda bi, i: (bi, 0, 0)
    vec = lambda bi, i: (0, 0)
    return pl.pallas_call(
        _mix_out_kernel,
        grid=(b, t // tm),
        in_specs=[pl.BlockSpec((1, tm, d), row),
                  pl.BlockSpec((d, d), vec),
                  pl.BlockSpec((1, tm, d), row),
                  pl.BlockSpec((1, 1, d), per_b),
                  pl.BlockSpec((1, d), vec), pl.BlockSpec((1, d), vec),
                  pl.BlockSpec((1, 1, d), per_b), pl.BlockSpec((1, 1, d), per_b)],
        out_specs=[pl.BlockSpec((1, tm, d), row), pl.BlockSpec((1, tm, d), row)],
        out_shape=[jax.ShapeDtypeStruct((b, t, d), F32), jax.ShapeDtypeStruct((b, t, d), BF16)],
        compiler_params=_cparams(("arbitrary", "arbitrary")),
        name="mix_out",
    )(m, w_out, x, gate, n_post, n_pre, shift, scale)


def _ffn_up_kernel(h_ref, wg_ref, wu_ref, o_ref):
    h = h_ref[0]
    a = _mm(h, wg_ref[...])
    u = _mm(h, wu_ref[...])
    o_ref[0] = (a * _sigmoid(a) * u).astype(o_ref.dtype)


def _ffn_up(h, w_gate, w_up, tm=1024, tn=512):
    b, t, d = h.shape
    f = w_gate.shape[1]
    return pl.pallas_call(
        _ffn_up_kernel,
        grid=(b, t // tm, f // tn),
        in_specs=[pl.BlockSpec((1, tm, d), lambda bi, i, j: (bi, i, 0)),
                  pl.BlockSpec((d, tn), lambda bi, i, j: (0, j)),
                  pl.BlockSpec((d, tn), lambda bi, i, j: (0, j))],
        out_specs=pl.BlockSpec((1, tm, tn), lambda bi, i, j: (bi, i, j)),
        out_shape=jax.ShapeDtypeStruct((b, t, f), BF16),
        compiler_params=_cparams(("arbitrary", "arbitrary", "arbitrary")),
        name="ffn_up",
    )(h, w_gate, w_up)


def _ffn_down_kernel(h_ref, w_ref, x_ref, gate_ref, npost_ref, o_ref, acc):
    kk = pl.program_id(2)

    @pl.when(kk == 0)
    def _():
        acc[...] = jnp.zeros_like(acc)

    acc[...] += _mm(h_ref[0], w_ref[...])

    @pl.when(kk == pl.num_programs(2) - 1)
    def _():
        z = acc[...]
        z = z * lax.rsqrt(jnp.mean(z * z, axis=-1, keepdims=True) + NORM_EPS) * npost_ref[...]
        o_ref[0] = x_ref[0] + gate_ref[0] * z


def _ffn_down(h, w_down, x1, gate, n_post, tm=512, tk=1408):
    b, t, f = h.shape
    d = w_down.shape[1]
    return pl.pallas_call(
        _ffn_down_kernel,
        grid=(b, t // tm, f // tk),
        in_specs=[pl.BlockSpec((1, tm, tk), lambda bi, i, k: (bi, i, k)),
                  pl.BlockSpec((tk, d), lambda bi, i, k: (k, 0)),
                  pl.BlockSpec((1, tm, d), lambda bi, i, k: (bi, i, 0)),
                  pl.BlockSpec((1, 1, d), lambda bi, i, k: (bi, 0, 0)),
                  pl.BlockSpec((1, d), lambda bi, i, k: (0, 0))],
        out_specs=pl.BlockSpec((1, tm, d), lambda bi, i, k: (bi, i, 0)),
        out_shape=jax.ShapeDtypeStruct((b, t, d), F32),
        scratch_shapes=[pltpu.VMEM((tm, d), F32)],
        compiler_params=_cparams(("arbitrary", "arbitrary", "arbitrary")),
        name="ffn_down",
    )(h, w_down, x1, gate, n_post)


def _pad_cols(w, n):
    return jnp.pad(w, ((0, 0), (0, n - w.shape[1])))


def kernel(x, c, ctx, c_ctx, ada_w, ada_b, norm_pre_mix, norm_post_mix, norm_pre_ffn, norm_post_ffn, w_in, shift_mu, rwkv_w0, rwkv_w2, rwkv_a0, rwkv_a2, rwkv_g2, rwkv_k_k, rwkv_k_a, rwkv_r_k, rwkv_ln_g, rwkv_ln_b, w_rwkv_up, gla_alpha_up, gla_alpha_b, gla_norm_g, w_gla_up, w_out, ffn_w_gate, ffn_w_up, ffn_w_down):
    assert ada_w.shape[0] == 1, "single trunk layer"
    bsz, seq, d = x.shape
    n_ctx = ctx.shape[1]
    prec = BF16

    cvecs = jnp.concatenate([c, c_ctx[None, :], jnp.zeros((8 - bsz - 1, d), F32)], 0)
    mod = _modulation(cvecs, ada_w[0], ada_b[0])
    mod_x = mod[:bsz].reshape(bsz, 6, 1, d)
    shx1, scx1, gx1, shx2, scx2, gx2 = (mod_x[:, i] for i in range(6))
    mod_c = jnp.broadcast_to(mod[bsz].reshape(1, 6, 1, d), (bsz, 6, 1, d))
    shc1, scc1 = mod_c[:, 0], mod_c[:, 1]

    w_all = w_in[0]
    mix_in = RWKV_IN + GLA_IN
    w_rwkv = _pad_cols(w_all[:, :RWKV_IN], RWKV_PAD).astype(BF16)
    w_gla = _pad_cols(w_all[:, RWKV_IN:mix_in], GLA_PAD).astype(BF16)
    w_gate = w_all[:, mix_in:].astype(BF16)
    mu = _pad_cols(shift_mu, RWKV_PAD)
    n_pre = norm_pre_mix

    px_rwkv = _project_shift(x, shx1, scx1, n_pre, w_rwkv, mu, 1024, 512, "proj_rwkv")
    px_gla = _project_colmajor(x, shx1, scx1, n_pre, w_gla, 16, 512, "proj_gla")
    px_gate = _project(x, shx1, scx1, n_pre, w_gate, 1024, 1024, "proj_gate")
    pc_rwkv = _project_shift(ctx, shc1, scc1, n_pre, w_rwkv, mu, n_ctx, 512, "proj_rwkv_ctx")
    pc_gla = _project(ctx, shc1, scc1, n_pre, w_gla, n_ctx, 512, "proj_gla_ctx")

    y_dir = []
    for dr in range(2):
        y_dir.append(_rwkv_scan(px_rwkv, pc_rwkv, rwkv_w2[0, dr], rwkv_w0[0, dr][None, :], rwkv_a2[0, dr],
                                rwkv_a0[0, dr][None, :], rwkv_k_k, rwkv_k_a, dr == 1, prec))
    ya = _rwkv_post(y_dir[0], y_dir[1], px_rwkv, rwkv_a2[0], rwkv_a0[0][:, None, :], rwkv_g2[0], rwkv_k_a,
                    rwkv_r_k.reshape(1, RWKV_WIDTH), rwkv_ln_g, rwkv_ln_b)

    o_dir = [_gla_scan(px_gla, pc_gla, gla_alpha_up[0, dr], gla_alpha_b[0, dr][None, :], dr == 1, prec) for dr in range(2)]
    yb = _gla_post(o_dir[0], o_dir[1], px_gla, gla_norm_g)

    m = _merge(ya, yb, w_rwkv_up[0].astype(BF16), w_gla_up[0].astype(BF16), px_gate)
    x1, h2 = _mix_out(m, w_out[0].astype(BF16), x, gx1, norm_post_mix, norm_pre_ffn, shx2, scx2)
    hf = _ffn_up(h2, ffn_w_gate[0].astype(BF16), ffn_w_up[0].astype(BF16))
    return _ffn_down(hf, ffn_w_down[0].astype(BF16), x1, gx2, norm_post_ffn)
```

```python
import functools

import jax
import jax.numpy as jnp
from jax import lax
from jax.experimental import pallas as pl
from jax.experimental.pallas import tpu as pltpu

F32 = jnp.float32
BF16 = jnp.bfloat16
HIGHEST = lax.Precision.HIGHEST

LANES = 128
VMEM_LIMIT_BYTES = 56 * 1024 * 1024

GRID_W = 64
CHUNK = 64
RWKV_HEADS, RWKV_HEAD_DIM = 16, 64
RWKV_WIDTH = RWKV_HEADS * RWKV_HEAD_DIM
DECAY_LORA = ICL_LORA = 96
GATE_LORA = 64
RWKV_GN_EPS = 64e-5
GLA_HEADS, GLA_KEY_DIM, GLA_VAL_DIM = 4, 128, 256
GLA_QK_WIDTH = GLA_HEADS * GLA_KEY_DIM
GLA_V_WIDTH = GLA_HEADS * GLA_VAL_DIM
GLA_GATE_LORA = 16
GLA_TAU = 16.0
GLA_NORM_EPS = 1e-5
GLA_SUB = 16
NORM_EPS = 1e-6
PROLOGUE_ROWS = 256

RWKV_IN = 3 * RWKV_WIDTH + 2 * DECAY_LORA + 2 * ICL_LORA + GATE_LORA
RWKV_PAD = 3584
OFF_WD = 3 * RWKV_WIDTH
OFF_AD = OFF_WD + 2 * DECAY_LORA
OFF_GD = OFF_AD + 2 * ICL_LORA
GLA_IN = 2 * GLA_QK_WIDTH + 2 * GLA_V_WIDTH + 2 * GLA_GATE_LORA
GLA_BLK = 512
GLA_PAD = 7 * GLA_BLK
OFF_GLA_AD = 6 * GLA_BLK

PAIR = 2 * RWKV_HEAD_DIM
N_PAIRS = RWKV_HEADS // 2


def _cparams(semantics):
    return pltpu.CompilerParams(dimension_semantics=semantics, vmem_limit_bytes=VMEM_LIMIT_BYTES)


def _sigmoid(z):
    return 1.0 / (1.0 + jnp.exp(-z))


def _softplus(z):
    return jnp.maximum(z, 0.0) + jnp.log(1.0 + jnp.exp(-jnp.abs(z)))


def _dot(a, b, dims, precision):
    if precision is BF16:
        a, b, precision = a.astype(BF16), b.astype(BF16), None
    return lax.dot_general(a, b, (dims, ((), ())), precision=precision, preferred_element_type=F32)


def _mm(a, b, precision=None):
    return _dot(a, b, ((1,), (0,)), precision)


def _mm_nt(a, b, precision=None):
    return _dot(a, b, ((1,), (1,)), precision)


def _mm_tn(a, b, precision=None):
    return _dot(a, b, ((0,), (0,)), precision)


def _split2(z):
    hi = z.astype(BF16)
    return hi, (z - hi.astype(F32)).astype(BF16)


def _mm_split(a, b):
    ah, al = _split2(a)
    bh, bl = _split2(b)
    return (_mm(al, bh) + _mm(ah, bl)) + _mm(ah, bh)


def _mm_exact_lhs(e, b):
    hi = b.astype(BF16)
    r = b - hi.astype(F32)
    mid = r.astype(BF16)
    lo = (r - mid.astype(F32)).astype(BF16)
    e = e.astype(BF16)
    return (_mm(e, lo) + _mm(e, mid)) + _mm(e, hi)


def _mod_kernel(c_ref, w_ref, b_ref, o_ref):
    s = c_ref[...]
    s = s * _sigmoid(s)
    o_ref[...] = _mm_split(s, w_ref[...]) + b_ref[...]


def _modulation(cvecs, ada_w, ada_b, tn=1024):
    m, d = cvecs.shape
    n = ada_w.shape[1]
    return pl.pallas_call(
        _mod_kernel,
        grid=(n // tn,),
        in_specs=[pl.BlockSpec((m, d), lambda j: (0, 0)),
                  pl.BlockSpec((d, tn), lambda j: (0, j)),
                  pl.BlockSpec((1, tn), lambda j: (0, j))],
        out_specs=pl.BlockSpec((m, tn), lambda j: (0, j)),
        out_shape=jax.ShapeDtypeStruct((m, n), F32),
        compiler_params=_cparams(("arbitrary",)),
        name="adaln_mod",
    )(cvecs, ada_w, ada_b.reshape(1, n))


def _proj_kernel(x_ref, sh_ref, sc_ref, g_ref, w_ref, o_ref, h_scr):
    @pl.when(pl.program_id(2) == 0)
    def _():
        rows = min(PROLOGUE_ROWS, h_scr.shape[0])

        def body(rb, carry):
            sl = pl.ds(pl.multiple_of(rb * rows, rows), rows)
            x = x_ref[0, sl, :]
            ms = jnp.mean(x * x, axis=-1, keepdims=True)
            y = x * lax.rsqrt(ms + NORM_EPS) * g_ref[...]
            h_scr[sl, :] = (y * (1.0 + sc_ref[0]) + sh_ref[0]).astype(BF16)
            return carry

        lax.fori_loop(0, h_scr.shape[0] // rows, body, 0)

    o_ref[0] = _mm(h_scr[...], w_ref[...])


def _project(x, shift, scale, gain, w, tm, tn, name):
    b, t, d = x.shape
    n = w.shape[1]
    return pl.pallas_call(
        _proj_kernel,
        grid=(b, t // tm, n // tn),
        in_specs=[pl.BlockSpec((1, tm, d), lambda bi, i, j: (bi, i, 0)),
                  pl.BlockSpec((1, 1, d), lambda bi, i, j: (bi, 0, 0)),
                  pl.BlockSpec((1, 1, d), lambda bi, i, j: (bi, 0, 0)),
                  pl.BlockSpec((1, d), lambda bi, i, j: (0, 0)),
                  pl.BlockSpec((d, tn), lambda bi, i, j: (0, j))],
        out_specs=pl.BlockSpec((1, tm, tn), lambda bi, i, j: (bi, i, j)),
        out_shape=jax.ShapeDtypeStruct((b, t, n), F32),
        scratch_shapes=[pltpu.VMEM((tm, d), BF16)],
        compiler_params=_cparams(("arbitrary", "arbitrary", "arbitrary")),
        name=name,
    )(x, shift, scale, gain, w)


def _colmajor_perm(n_rows, n_cols):
    dst = jnp.arange(n_rows * n_cols)
    src = (dst % n_rows) * n_cols + dst // n_rows
    return (src[:, None] == dst[None, :]).astype(BF16)


def _proj_cm_kernel(x_ref, sh_ref, sc_ref, g_ref, perm_ref, w_ref, o_ref, hr_scr, h_scr):
    @pl.when(pl.program_id(2) == 0)
    def _():
        n_r, n_c, d = x_ref.shape[1:]
        rb = PROLOGUE_ROWS // n_c
        rows = rb * n_c

        def body(kb, carry):
            x = x_ref[0, pl.ds(pl.multiple_of(kb * rb, rb), rb), :, :].reshape(rows, d)
            ms = jnp.mean(x * x, axis=-1, keepdims=True)
            y = x * lax.rsqrt(ms + NORM_EPS) * g_ref[...]
            hr_scr[pl.ds(pl.multiple_of(kb * rows, rows), rows), :] = (y * (1.0 + sc_ref[0]) + sh_ref[0]).astype(BF16)
            return carry

        lax.fori_loop(0, n_r // rb, body, 0)
        for cb in range(d // GLA_BLK):
            cs = slice(cb * GLA_BLK, (cb + 1) * GLA_BLK)
            h_scr[:, cs] = _mm(perm_ref[...], hr_scr[:, cs]).astype(BF16)

    o_ref[0] = _mm(h_scr[...], w_ref[...])


def _project_colmajor(x, shift, scale, gain, w, n_cols, tn, name):
    b, t, d = x.shape
    n = w.shape[1]
    n_r = t // GRID_W
    tm = n_r * n_cols
    x4 = x.reshape(b, n_r, GRID_W, d)
    perm = _colmajor_perm(n_r, n_cols)
    return pl.pallas_call(
        _proj_cm_kernel,
        grid=(b, GRID_W // n_cols, n // tn),
        in_specs=[pl.BlockSpec((1, n_r, n_cols, d), lambda bi, i, j: (bi, 0, i, 0)),
                  pl.BlockSpec((1, 1, d), lambda bi, i, j: (bi, 0, 0)),
                  pl.BlockSpec((1, 1, d), lambda bi, i, j: (bi, 0, 0)),
                  pl.BlockSpec((1, d), lambda bi, i, j: (0, 0)),
                  pl.BlockSpec((tm, tm), lambda bi, i, j: (0, 0)),
                  pl.BlockSpec((d, tn), lambda bi, i, j: (0, j))],
        out_specs=pl.BlockSpec((1, tm, tn), lambda bi, i, j: (bi, i, j)),
        out_shape=jax.ShapeDtypeStruct((b, t, n), F32),
        scratch_shapes=[pltpu.VMEM((tm, d), BF16), pltpu.VMEM((tm, d), BF16)],
        compiler_params=_cparams(("arbitrary", "arbitrary", "arbitrary")),
        name=name,
    )(x4, shift, scale, gain, perm, w)


def _norm_modulate(x, g_ref, sh_ref, sc_ref):
    ms = jnp.mean(x * x, axis=-1, keepdims=True)
    y = x * lax.rsqrt(ms + NORM_EPS) * g_ref[...]
    return (y * (1.0 + sc_ref[0]) + sh_ref[0]).astype(BF16)


def _proj_shift_kernel(x_ref, xp_ref, xn_ref, sh_ref, sc_ref, g_ref, w_ref, mu_ref, o_ref, h_scr, halo_scr):
    i = pl.program_id(1)
    tm = h_scr.shape[0]

    @pl.when(pl.program_id(2) == 0)
    def _():
        rows = min(PROLOGUE_ROWS, tm)

        def body(rb, carry):
            sl = pl.ds(pl.multiple_of(rb * rows, rows), rows)
            h_scr[sl, :] = _norm_modulate(x_ref[0, sl, :], g_ref, sh_ref, sc_ref)
            return carry

        lax.fori_loop(0, tm // rows, body, 0)
        halo_scr[...] = _norm_modulate(jnp.concatenate([xp_ref[0], xn_ref[0]], 0), g_ref, sh_ref, sc_ref)

    p = _mm(h_scr[...], w_ref[...])
    ph = _mm(halo_scr[...], w_ref[...])
    before = jnp.where(i == 0, 0.0, ph[7:8])
    after = jnp.where(i == pl.num_programs(1) - 1, 0.0, ph[8:9])
    row = lax.broadcasted_iota(jnp.int32, p.shape, 0)
    prev = jnp.where(row == 0, before, pltpu.roll(p, 1, 0))
    nxt = jnp.where(row == tm - 1, after, pltpu.roll(p, tm - 1, 0))
    o_ref[0] = p + mu_ref[...] * (0.5 * (prev + nxt) - p)


def _project_shift(x, shift, scale, gain, w, mu, tm, tn, name):
    b, t, d = x.shape
    n = w.shape[1]
    r8, n8 = tm // 8, t // 8
    return pl.pallas_call(
        _proj_shift_kernel,
        grid=(b, t // tm, n // tn),
        in_specs=[pl.BlockSpec((1, tm, d), lambda bi, i, j: (bi, i, 0)),
                  pl.BlockSpec((1, 8, d), lambda bi, i, j: (bi, jnp.maximum(i * r8 - 1, 0), 0)),
                  pl.BlockSpec((1, 8, d), lambda bi, i, j: (bi, jnp.minimum((i + 1) * r8, n8 - 1), 0)),
                  pl.BlockSpec((1, 1, d), lambda bi, i, j: (bi, 0, 0)),
                  pl.BlockSpec((1, 1, d), lambda bi, i, j: (bi, 0, 0)),
                  pl.BlockSpec((1, d), lambda bi, i, j: (0, 0)),
                  pl.BlockSpec((d, tn), lambda bi, i, j: (0, j)),
                  pl.BlockSpec((1, tn), lambda bi, i, j: (0, j))],
        out_specs=pl.BlockSpec((1, tm, tn), lambda bi, i, j: (bi, i, j)),
        out_shape=jax.ShapeDtypeStruct((b, t, n), F32),
        scratch_shapes=[pltpu.VMEM((tm, d), BF16), pltpu.VMEM((16, d), BF16)],
        compiler_params=_cparams(("arbitrary", "arbitrary", "arbitrary")),
        name=name,
    )(x, x, x, shift, scale, gain, w, mu)


def _head_sum_matrix():
    r = lax.broadcasted_iota(jnp.int32, (PAIR, PAIR), 0)
    c = lax.broadcasted_iota(jnp.int32, (PAIR, PAIR), 1)
    return jnp.where((r >> 6) == (c >> 6), 1.0, 0.0).astype(BF16)


def _head_sum(z, hsum):
    hi = z.astype(BF16)
    lo = (z - hi.astype(F32)).astype(BF16)
    return _mm(hi, hsum) + _mm(lo, hsum)


def _rwkv_kernel(prec, n_ctx_chunks, plf_ref, pcf_ref, plb_ref, pcb_ref, w2_ref, w0_ref, a2_ref, a0_ref, kk_ref, ka_ref,
                 yf_ref, yb_ref, st_scr):
    @pl.when(pl.program_id(1) == 0)
    def _():
        st_scr[...] = jnp.zeros_like(st_scr)

    is_ctx = pl.program_id(1) < n_ctx_chunks
    for dr, (pl_ref, pc_ref, y_ref) in enumerate(((plf_ref, pcf_ref, yf_ref), (plb_ref, pcb_ref, yb_ref))):
        _rwkv_chunk(dr == 1, prec, is_ctx, pl_ref, pc_ref, w2_ref.at[dr], w0_ref.at[dr], a2_ref.at[dr], a0_ref.at[dr],
                    kk_ref, ka_ref, y_ref, st_scr.at[dr])


def _rwkv_chunk(rev, prec, is_ctx, pl_ref, pc_ref, w2_ref, w0_ref, a2_ref, a0_ref, kk_ref, ka_ref, y_ref, st_scr):
    hd = RWKV_HEAD_DIM
    c2 = 2 * CHUNK
    p = jnp.where(is_ctx, pc_ref[0], pl_ref[0])
    r = p[:, 0:RWKV_WIDTH]
    k = p[:, RWKV_WIDTH:2 * RWKV_WIDTH]
    v = p[:, 2 * RWKV_WIDTH:3 * RWKV_WIDTH]
    d_off = DECAY_LORA if rev else 0
    wd = p[:, OFF_WD + d_off:OFF_WD + d_off + DECAY_LORA]
    ad = p[:, OFF_AD + d_off:OFF_AD + d_off + ICL_LORA]

    w_log = -_softplus(-(w0_ref[...] + _mm_split(jnp.tanh(wd), w2_ref[...]))) - 0.5
    lw = -jnp.exp(w_log)
    a = _sigmoid(a0_ref[...] + _mm(ad, a2_ref[...], BF16))
    kk_raw = k * kk_ref[...]
    kd = k * (1.0 + (a - 1.0) * ka_ref[...])

    ri = lax.broadcasted_iota(jnp.int32, (c2, PAIR), 0)
    li = lax.broadcasted_iota(jnp.int32, (c2, PAIR), 1)
    rt, lt = ri & (CHUNK - 1), li & (hd - 1)
    same_head = (ri >> 6) == (li >> 6)
    strict = (lt > rt) if rev else (lt < rt)
    mask_n = jnp.logical_and(same_head, strict)
    mask_k = jnp.logical_and(jnp.logical_not(same_head), strict)
    ident = ri == li
    eye = jnp.where(ident, 1.0, 0.0).astype(F32)
    rc = lax.broadcasted_iota(jnp.int32, (CHUNK, PAIR), 0)
    lc = lax.broadcasted_iota(jnp.int32, (CHUNK, PAIR), 1)
    incl_c = ((lc & (hd - 1)) >= rc) if rev else ((lc & (hd - 1)) <= rc)
    lane_e = lc < hd
    hsum = _head_sum_matrix()

    ci = lax.broadcasted_iota(jnp.int32, (CHUNK, CHUNK), 0)
    cj = lax.broadcasted_iota(jnp.int32, (CHUNK, CHUNK), 1)
    tri = jnp.where((cj >= ci) if rev else (cj <= ci), 1.0, 0.0).astype(F32)
    cum = _mm_exact_lhs(tri, lw)
    total = cum[0:1] if rev else cum[CHUNK - 1:CHUNK]
    e_prev = jnp.exp(cum - lw)
    e_neg = jnp.exp(-cum)
    e_pos = jnp.exp(cum)
    e_rest = jnp.exp(total - cum)
    p_end = jnp.exp(total)

    def split(z):
        ze = jnp.where(lane_e, z, 0.0)
        return ze, z - ze

    pairs = range(N_PAIRS)
    sls = [slice(pr * PAIR, (pr + 1) * PAIR) for pr in pairs]
    cat0 = lambda *z: jnp.concatenate(z, 0)

    kkr = [kk_raw[:, sl] for sl in sls]
    nrm2 = [_head_sum(z * z, hsum) for z in kkr]
    kk = [z / jnp.maximum(jnp.sqrt(n), 1e-12) for z, n in zip(kkr, nrm2)]
    bb = [z * a[:, sl] for z, sl in zip(kk, sls)]
    at = [split(z * e_prev[:, sl]) for z, sl in zip(kk, sls)]
    bt = [z * e_neg[:, sl] for z, sl in zip(bb, sls)]
    kt = [kd[:, sl] * e_neg[:, sl] for sl in sls]
    rt_ = [r[:, sl] * e_pos[:, sl] for sl in sls]
    rs = [split(z) for z in rt_]
    vs = [split(v[:, sl]) for sl in sls]
    bh = [cat0(*split(z * e_rest[:, sl])) for z, sl in zip(bb, sls)]
    kh = [cat0(*split(kd[:, sl] * e_rest[:, sl])) for sl in sls]
    v_swap = [cat0(vo, ve) for ve, vo in vs]
    v_stack = [cat0(ve, vo) for ve, vo in vs]

    g_e = [_mm_nt(cat0(at[i][0], rs[i][0]), cat0(bt[i], kt[i]), prec) for i in pairs]
    g_o = [_mm_nt(cat0(at[i][1], rs[i][1]), cat0(kt[i], bt[i]), prec) for i in pairs]
    g_top = [cat0(g_e[i][0:CHUNK], g_o[i][0:CHUNK]) for i in pairs]
    nbd = [jnp.where(mask_n, z, 0.0) for z in g_top]
    aak = [jnp.where(mask_k, z, 0.0) for z in g_top]
    rab = [jnp.where(incl_c, jnp.where(lane_e, g_e[i][CHUNK:c2], g_o[i][CHUNK:c2]), 0.0) for i in pairs]
    rak = [jnp.where(incl_c, jnp.where(lane_e, g_o[i][CHUNK:c2], g_e[i][CHUNK:c2]), 0.0) for i in pairs]

    x = [_mm(aak[i], v_swap[i], prec) for i in pairs]
    n2 = [_mm(z, z, prec) for z in nbd]
    y0b = [_mm(rak[i], v_swap[i], prec) for i in pairs]
    n4 = [_mm(z, z, prec) for z in n2]
    imn = [eye - z for z in nbd]
    p1 = [imn[i] + _mm(imn[i], n2[i], prec) for i in pairs]
    n8 = [_mm(z, z, prec) for z in n4]
    nb = [_mm_tn(kh[i], v_stack[i], prec) for i in pairs]
    n16 = [_mm(z, z, prec) for z in n8]
    p2 = [eye + n4[i] + n8[i] + _mm(n4[i], n8[i], prec) for i in pairs]
    n32 = [_mm(z, z, prec) for z in n16]
    p12 = [_mm(p1[i], p2[i], prec) for i in pairs]
    p3 = [eye + n16[i] + n32[i] + _mm(n16[i], n32[i], prec) for i in pairs]
    tinv = [_mm(p12[i], p3[i], prec) for i in pairs]
    wu = [-_mm(tinv[i], jnp.concatenate([cat0(*at[i]), x[i]], 1), prec) for i in pairs]
    qy = [_mm(rab[i], wu[i], prec) for i in pairs]
    mn = [_mm_tn(bh[i], wu[i], prec) for i in pairs]
    q = [rt_[i] + qy[i][:, 0:PAIR] for i in pairs]
    m = [jnp.where(ident, p_end[:, sls[i]], 0.0) + mn[i][:, 0:PAIR] for i in pairs]
    qm = [_mm(cat0(q[i], m[i]), st_scr[i], prec) for i in pairs]
    for i in pairs:
        y_ref[0, :, sls[i]] = qm[i][0:CHUNK] + qy[i][:, PAIR:2 * PAIR] + y0b[i]
        st_scr[i] = qm[i][CHUNK:CHUNK + PAIR] + mn[i][:, PAIR:2 * PAIR] + nb[i]


def _scan_order(n_ctx_chunks, n_lat):
    n_steps = n_ctx_chunks + n_lat
    lat_f = lambda i: jnp.maximum(i - n_ctx_chunks, 0)
    lat_b = lambda i: jnp.where(i < n_ctx_chunks, n_lat - 1, n_steps - 1 - i)
    ctx_f = lambda i: jnp.minimum(i, n_ctx_chunks - 1)
    ctx_b = lambda i: jnp.maximum(n_ctx_chunks - 1 - i, 0)
    return n_steps, (lat_f, lat_b), (ctx_f, ctx_b)


def _rwkv_scan(p_lat, p_ctx, w2, w0, a2, a0, k_k, k_a, prec):
    b, t, w = p_lat.shape
    n_lat = t // CHUNK
    n_ctx_chunks = p_ctx.shape[1] // CHUNK
    n_steps, lat_of, ctx_of = _scan_order(n_ctx_chunks, n_lat)
    chunk = lambda f: pl.BlockSpec((1, CHUNK, w), lambda bi, i: (bi, f(i), 0))
    out = lambda f: pl.BlockSpec((1, CHUNK, RWKV_WIDTH), lambda bi, i: (bi, f(i), 0))
    vec = lambda bi, i: (0, 0)
    vec3 = lambda bi, i: (0, 0, 0)
    y_shape = jax.ShapeDtypeStruct((b, t, RWKV_WIDTH), F32)
    return pl.pallas_call(
        functools.partial(_rwkv_kernel, prec, n_ctx_chunks),
        grid=(b, n_steps),
        in_specs=[chunk(lat_of[0]), chunk(ctx_of[0]), chunk(lat_of[1]), chunk(ctx_of[1]),
                  pl.BlockSpec((2, DECAY_LORA, RWKV_WIDTH), vec3),
                  pl.BlockSpec((2, 1, RWKV_WIDTH), vec3),
                  pl.BlockSpec((2, ICL_LORA, RWKV_WIDTH), vec3),
                  pl.BlockSpec((2, 1, RWKV_WIDTH), vec3),
                  pl.BlockSpec((1, RWKV_WIDTH), vec),
                  pl.BlockSpec((1, RWKV_WIDTH), vec)],
        out_specs=[out(lat_of[0]), out(lat_of[1])],
        out_shape=[y_shape, y_shape],
        scratch_shapes=[pltpu.VMEM((2, N_PAIRS, PAIR, PAIR), F32)],
        compiler_params=_cparams(("arbitrary", "arbitrary")),
        name="rwkv7_scan",
    )(p_lat, p_ctx, p_lat, p_ctx, w2, w0, a2, a0, k_k, k_a)


def _gla_kernel(prec, n_ctx_chunks, *refs):
    lat_f, lat_b = refs[0:6], refs[6:12]
    aup_ref, ab_ref, of_ref, ob_ref, st_scr = refs[12:17]

    @pl.when(pl.program_id(1) == 0)
    def _():
        st_scr[...] = jnp.zeros_like(st_scr)

    is_ctx = pl.program_id(1) < n_ctx_chunks
    for dr, (blocks, o_ref) in enumerate(((lat_f, of_ref), (lat_b, ob_ref))):
        _gla_chunk(dr == 1, prec, is_ctx, *blocks, aup_ref.at[dr], ab_ref.at[dr], o_ref, st_scr.at[dr])


def _gla_chunk(rev, prec, is_ctx, q_ref, k_ref, v0_ref, v1_ref, ad_ref, ctx_ref, aup_ref, ab_ref, o_ref, st_scr):
    pc = ctx_ref[0]
    q = jnp.where(is_ctx, pc[:, 0:GLA_BLK], q_ref[0])
    k = jnp.where(is_ctx, pc[:, GLA_BLK:2 * GLA_BLK], k_ref[0])
    v = jnp.concatenate([jnp.where(is_ctx, pc[:, 2 * GLA_BLK:3 * GLA_BLK], v0_ref[0]),
                         jnp.where(is_ctx, pc[:, 3 * GLA_BLK:4 * GLA_BLK], v1_ref[0])], 1)
    d_off = GLA_GATE_LORA if rev else 0
    ad = jnp.where(is_ctx, pc[:, OFF_GLA_AD:OFF_GLA_AD + LANES], ad_ref[0][:, 0:LANES])
    ad = ad[:, d_off:d_off + GLA_GATE_LORA]

    la = -_softplus(-(_mm_split(ad, aup_ref[...]) + ab_ref[...])) * (1.0 / GLA_TAU)
    ci = lax.broadcasted_iota(jnp.int32, (CHUNK, CHUNK), 0)
    cj = lax.broadcasted_iota(jnp.int32, (CHUNK, CHUNK), 1)
    tri = jnp.where((cj >= ci) if rev else (cj <= ci), 1.0, 0.0).astype(F32)
    cum = _mm_exact_lhs(tri, la)
    total = cum[0:1] if rev else cum[CHUNK - 1:CHUNK]

    dk, dv, sb = GLA_KEY_DIM, GLA_VAL_DIM, GLA_SUB
    n_sb = CHUNK // sb
    row = lax.broadcasted_iota(jnp.int32, (CHUNK, dk), 0)
    row_in = row & (sb - 1)
    arow = lax.broadcasted_iota(jnp.int32, (CHUNK, CHUNK), 0)
    acol = lax.broadcasted_iota(jnp.int32, (CHUNK, CHUNK), 1)
    scale = GLA_KEY_DIM ** -0.5

    heads = range(GLA_HEADS)
    ksl = [slice(h * dk, (h + 1) * dk) for h in heads]
    qh = [q[:, s_] * scale for s_ in ksl]
    kh = [k[:, s_] for s_ in ksl]
    bh = [cum[:, s_] for s_ in ksl]
    lah = [la[:, s_] for s_ in ksl]
    toth = [total[:, s_] for s_ in ksl]
    vh = [v[:, h * dv:(h + 1) * dv] for h in heads]
    st = [st_scr[h] for h in heads]

    o_inter = [_mm_nt(qh[h] * jnp.exp(bh[h]), st[h], prec) for h in heads]
    st_new = [st[h] * jnp.exp(toth[h]) + _mm_tn(vh[h], kh[h] * jnp.exp(toth[h] - bh[h]), prec) for h in heads]

    off_rows = [[] for _ in heads]
    for blk in range(n_sb):
        rs = slice(blk * sb, (blk + 1) * sb)
        first = blk * sb + (sb - 1 if rev else 0)
        is_first_blk = (blk == n_sb - 1) if rev else (blk == 0)
        before = (row >= (blk + 1) * sb) if rev else (row < blk * sb)
        for h in heads:
            if is_first_blk:
                off_rows[h].append(jnp.zeros((sb, CHUNK), F32))
                continue
            beta = bh[h][first:first + 1] - lah[h][first:first + 1]
            qs = qh[h][rs] * jnp.exp(bh[h][rs] - beta)
            ksc = jnp.where(before, kh[h] * jnp.exp(jnp.minimum(beta - bh[h], 0.0)), 0.0)
            off_rows[h].append(_mm_nt(qs, ksc, prec))
    att = [jnp.concatenate(off_rows[h], 0) for h in heads]

    for s in range(sb):
        pick = lambda z: jnp.concatenate(
            [jnp.broadcast_to(z[blk * sb + s:blk * sb + s + 1], (sb, dk)) for blk in range(n_sb)], 0)
        ok = (row_in <= s) if rev else (row_in >= s)
        tgt = jnp.logical_and(acol == (arow & ~(sb - 1)) + s,
                              ((arow & (sb - 1)) <= s) if rev else ((arow & (sb - 1)) >= s))
        for h in heads:
            e = jnp.exp(jnp.where(ok, bh[h] - pick(bh[h]), 0.0))
            col = jnp.sum(jnp.where(ok, qh[h] * pick(kh[h]) * e, 0.0), axis=-1, keepdims=True)
            att[h] = jnp.where(tgt, col, att[h])

    for h in heads:
        o_ref[0, :, h * dv:(h + 1) * dv] = o_inter[h] + _mm(att[h], vh[h], prec)
        st_scr[h] = st_new[h]


def _gla_scan(p_lat, p_ctx, alpha_up, alpha_b, prec):
    b, t, w = p_lat.shape
    assert w == GLA_PAD and t == GRID_W * CHUNK
    n_ctx_chunks = p_ctx.shape[1] // CHUNK
    n_steps, col_of, ctx_of = _scan_order(n_ctx_chunks, GRID_W)

    def blocks(dr):
        lat = lambda m: pl.BlockSpec((1, CHUNK, GLA_BLK), lambda bi, i: (bi, col_of[dr](i), m))
        return [lat(0), lat(1), lat(2), lat(3), lat(6), pl.BlockSpec((1, CHUNK, w), lambda bi, i: (bi, ctx_of[dr](i), 0))]

    out = lambda dr: pl.BlockSpec((1, CHUNK, GLA_V_WIDTH), lambda bi, i: (bi, col_of[dr](i), 0))
    vec3 = lambda bi, i: (0, 0, 0)
    o_shape = jax.ShapeDtypeStruct((b, t, GLA_V_WIDTH), F32)
    return pl.pallas_call(
        functools.partial(_gla_kernel, prec, n_ctx_chunks),
        grid=(b, n_steps),
        in_specs=blocks(0) + blocks(1) + [pl.BlockSpec((2, GLA_GATE_LORA, GLA_QK_WIDTH), vec3),
                                          pl.BlockSpec((2, 1, GLA_QK_WIDTH), vec3)],
        out_specs=[out(0), out(1)],
        out_shape=[o_shape, o_shape],
        scratch_shapes=[pltpu.VMEM((2, GLA_HEADS, GLA_VAL_DIM, GLA_KEY_DIM), F32)],
        compiler_params=_cparams(("arbitrary", "arbitrary")),
        name="gla_scan",
    )(*([p_lat] * 5 + [p_ctx]) * 2, alpha_up, alpha_b)


def _rwkv_post_kernel(yf_ref, yb_ref, pm_ref, a2_ref, a0_ref, g2_ref, ka_ref, rk_ref, lng_ref, lnb_ref, o_ref):
    p = pm_ref[0]
    r = p[:, 0:RWKV_WIDTH]
    k = p[:, RWKV_WIDTH:2 * RWKV_WIDTH]
    v = p[:, 2 * RWKV_WIDTH:3 * RWKV_WIDTH]
    ad_f = p[:, OFF_AD:OFF_AD + ICL_LORA]
    ad_b = p[:, OFF_AD + ICL_LORA:OFF_AD + 2 * ICL_LORA]
    gd = p[:, OFF_GD:OFF_GD + GATE_LORA]
    ka = ka_ref[...]
    a_f = _sigmoid(a0_ref[0] + _mm(ad_f, a2_ref[0], BF16))
    a_b = _sigmoid(a0_ref[1] + _mm(ad_b, a2_ref[1], BF16))
    kd_sum = k * (1.0 + (a_f - 1.0) * ka) + k * (1.0 + (a_b - 1.0) * ka)
    gate = _mm(_sigmoid(gd), g2_ref[...], BF16)
    rkk = r * kd_sum * rk_ref[...]
    ysum = yf_ref[0] + yb_ref[0]
    hsum = _head_sum_matrix()
    inv_n = 1.0 / RWKV_HEAD_DIM
    for pr in range(N_PAIRS):
        sl = slice(pr * PAIR, (pr + 1) * PAIR)
        ys = ysum[:, sl]
        mean = _head_sum(ys, hsum) * inv_n
        dlt = ys - mean
        var = _head_sum(dlt * dlt, hsum) * inv_n
        gn = dlt * lax.rsqrt(var + RWKV_GN_EPS) * lng_ref[:, sl] + lnb_ref[:, sl]
        bonus = _head_sum(rkk[:, sl], hsum) * v[:, sl]
        o_ref[0, :, sl] = ((gn + bonus) * gate[:, sl]).astype(o_ref.dtype)


def _rwkv_post(y_f, y_b, pmix, a2, a0, g2, k_a, r_k, ln_g, ln_b, tm=256):
    b, t, w = y_f.shape
    row = lambda bi, i: (bi, i, 0)
    vec = lambda bi, i: (0, 0)
    vec3 = lambda bi, i: (0, 0, 0)
    return pl.pallas_call(
        _rwkv_post_kernel,
        grid=(b, t // tm),
        in_specs=[pl.BlockSpec((1, tm, w), row),
                  pl.BlockSpec((1, tm, w), row),
                  pl.BlockSpec((1, tm, pmix.shape[2]), row),
                  pl.BlockSpec((2, ICL_LORA, w), vec3),
                  pl.BlockSpec((2, 1, w), vec3),
                  pl.BlockSpec((GATE_LORA, w), vec),
                  pl.BlockSpec((1, w), vec), pl.BlockSpec((1, w), vec),
                  pl.BlockSpec((1, w), vec), pl.BlockSpec((1, w), vec)],
        out_specs=pl.BlockSpec((1, tm, w), row),
        out_shape=jax.ShapeDtypeStruct((b, t, w), BF16),
        compiler_params=_cparams(("arbitrary", "arbitrary")),
        name="rwkv_post",
    )(y_f, y_b, pmix, a2, a0, g2, k_a, r_k, ln_g, ln_b)


def _gla_post_kernel(of_ref, ob_ref, g_ref, ng_ref, perm_ref, o_ref):
    n_r, n_c = o_ref.shape[1:3]
    dv = GLA_VAL_DIM
    for h in range(GLA_HEADS):
        sl = slice(h * dv, (h + 1) * dv)
        oh = of_ref[0, :, sl] + ob_ref[0, :, sl]
        oh = oh * lax.rsqrt(jnp.mean(oh * oh, axis=-1, keepdims=True) + GLA_NORM_EPS) * ng_ref[:, sl]
        gh = g_ref[0, :, sl]
        y = (oh * (gh * _sigmoid(gh))).astype(BF16)
        o_ref[0, :, :, sl] = _mm(perm_ref[...], y).reshape(n_r, n_c, dv).astype(o_ref.dtype)


def _gla_post(o_f, o_b, p_gla, norm_g, n_cols=16):
    b, t, w = o_f.shape
    n_r = t // GRID_W
    tm = n_r * n_cols
    perm_t = _colmajor_perm(n_r, n_cols).T
    row = lambda bi, i: (bi, i, 0)
    out = pl.pallas_call(
        _gla_post_kernel,
        grid=(b, GRID_W // n_cols),
        in_specs=[pl.BlockSpec((1, tm, w), row),
                  pl.BlockSpec((1, tm, w), row),
                  pl.BlockSpec((1, tm, w), lambda bi, i: (bi, i, 2)),
                  pl.BlockSpec((1, w), lambda bi, i: (0, 0)),
                  pl.BlockSpec((tm, tm), lambda bi, i: (0, 0))],
        out_specs=pl.BlockSpec((1, n_r, n_cols, w), lambda bi, i: (bi, 0, i, 0)),
        out_shape=jax.ShapeDtypeStruct((b, n_r, GRID_W, w), BF16),
        compiler_params=_cparams(("arbitrary", "arbitrary")),
        name="gla_post",
    )(o_f, o_b, p_gla, norm_g, perm_t)
    return out.reshape(b, t, w)


def _merge_kernel(ya_ref, yb_ref, wr_ref, wg_ref, ga_ref, gb_ref, o_ref):
    ma = _mm(ya_ref[0], wr_ref[...])
    mb = _mm(yb_ref[0], wg_ref[...])
    o_ref[0] = (_sigmoid(ga_ref[0]) * ma + _sigmoid(gb_ref[0]) * mb).astype(o_ref.dtype)


def _merge(ya, yb, w_r, w_g, p_gate, tm=512, tn=1024):
    b, t, w = ya.shape
    d = w_r.shape[1]
    nj = d // tn
    return pl.pallas_call(
        _merge_kernel,
        grid=(b, t // tm, nj),
        in_specs=[pl.BlockSpec((1, tm, w), lambda bi, i, j: (bi, i, 0)),
                  pl.BlockSpec((1, tm, w), lambda bi, i, j: (bi, i, 0)),
                  pl.BlockSpec((w, tn), lambda bi, i, j: (0, j)),
                  pl.BlockSpec((w, tn), lambda bi, i, j: (0, j)),
                  pl.BlockSpec((1, tm, tn), lambda bi, i, j: (bi, i, j)),
                  pl.BlockSpec((1, tm, tn), lambda bi, i, j: (bi, i, j + nj))],
        out_specs=pl.BlockSpec((1, tm, tn), lambda bi, i, j: (bi, i, j)),
        out_shape=jax.ShapeDtypeStruct((b, t, d), BF16),
        compiler_params=_cparams(("arbitrary", "arbitrary", "arbitrary")),
        name="merge_branches",
    )(ya, yb, w_r, w_g, p_gate, p_gate)


def _mix_out_kernel(m_ref, w_ref, x_ref, gate_ref, npost_ref, npre_ref, sh_ref, sc_ref, x1_ref, h_ref):
    z = _mm(m_ref[0], w_ref[...])
    z = z * lax.rsqrt(jnp.mean(z * z, axis=-1, keepdims=True) + NORM_EPS) * npost_ref[...]
    x1 = x_ref[0] + gate_ref[0] * z
    x1_ref[0] = x1
    y = x1 * lax.rsqrt(jnp.mean(x1 * x1, axis=-1, keepdims=True) + NORM_EPS) * npre_ref[...]
    h_ref[0] = (y * (1.0 + sc_ref[0]) + sh_ref[0]).astype(h_ref.dtype)


def _mix_out(m, w_out, x, gate, n_post, n_pre, shift, scale, tm=256):
    b, t, d = x.shape
    row = lambda bi, i: (bi, i, 0)
    per_b = lambda bi, i: (bi, 0, 0)
    vec = lambda bi, i: (0, 0)
    return pl.pallas_call(
        _mix_out_kernel,
        grid=(b, t // tm),
        in_specs=[pl.BlockSpec((1, tm, d), row),
                  pl.BlockSpec((d, d), vec),
                  pl.BlockSpec((1, tm, d), row),
                  pl.BlockSpec((1, 1, d), per_b),
                  pl.BlockSpec((1, d), vec), pl.BlockSpec((1, d), vec),
                  pl.BlockSpec((1, 1, d), per_b), pl.BlockSpec((1, 1, d), per_b)],
        out_specs=[pl.BlockSpec((1, tm, d), row), pl.BlockSpec((1, tm, d), row)],
        out_shape=[jax.ShapeDtypeStruct((b, t, d), F32), jax.ShapeDtypeStruct((b, t, d), BF16)],
        compiler_params=_cparams(("arbitrary", "arbitrary")),
        name="mix_out",
    )(m, w_out, x, gate, n_post, n_pre, shift, scale)


def _ffn_up_kernel(h_ref, wg_ref, wu_ref, o_ref):
    h = h_ref[0]
    a = _mm(h, wg_ref[...])
    u = _mm(h, wu_ref[...])
    o_ref[0] = (a * _sigmoid(a) * u).astype(o_ref.dtype)


def _ffn_up(h, w_gate, w_up, tm=1024, tn=512):
    b, t, d = h.shape
    f = w_gate.shape[1]
    return pl.pallas_call(
        _ffn_up_kernel,
        grid=(b, t // tm, f // tn),
        in_specs=[pl.BlockSpec((1, tm, d), lambda bi, i, j: (bi, i, 0)),
                  pl.BlockSpec((d, tn), lambda bi, i, j: (0, j)),
                  pl.BlockSpec((d, tn), lambda bi, i, j: (0, j))],
        out_specs=pl.BlockSpec((1, tm, tn), lambda bi, i, j: (bi, i, j)),
        out_shape=jax.ShapeDtypeStruct((b, t, f), BF16),
        compiler_params=_cparams(("arbitrary", "arbitrary", "arbitrary")),
        name="ffn_up",
    )(h, w_gate, w_up)


def _ffn_down_kernel(h_ref, w_ref, x_ref, gate_ref, npost_ref, o_ref, z_scr):
    j = pl.program_id(2)
    n_j = z_scr.shape[0]
    z_scr[j] = _mm(h_ref[0], w_ref[...])

    @pl.when(j == n_j - 1)
    def _():
        z = jnp.concatenate([z_scr[t] for t in range(n_j)], 1)
        z = z * lax.rsqrt(jnp.mean(z * z, axis=-1, keepdims=True) + NORM_EPS) * npost_ref[...]
        o_ref[0] = x_ref[0] + gate_ref[0] * z


def _ffn_down(h, w_down, x1, gate, n_post, tm=512, tn=512):
    b, t, f = h.shape
    d = w_down.shape[1]
    return pl.pallas_call(
        _ffn_down_kernel,
        grid=(b, t // tm, d // tn),
        in_specs=[pl.BlockSpec((1, tm, f), lambda bi, i, j: (bi, i, 0)),
                  pl.BlockSpec((f, tn), lambda bi, i, j: (0, j)),
                  pl.BlockSpec((1, tm, d), lambda bi, i, j: (bi, i, 0)),
                  pl.BlockSpec((1, 1, d), lambda bi, i, j: (bi, 0, 0)),
                  pl.BlockSpec((1, d), lambda bi, i, j: (0, 0))],
        out_specs=pl.BlockSpec((1, tm, d), lambda bi, i, j: (bi, i, 0)),
        out_shape=jax.ShapeDtypeStruct((b, t, d), F32),
        scratch_shapes=[pltpu.VMEM((d // tn, tm, tn), F32)],
        compiler_params=_cparams(("arbitrary", "arbitrary", "arbitrary")),
        name="ffn_down",
    )(h, w_down, x1, gate, n_post)


def _pad_cols(w, n):
    return jnp.pad(w, ((0, 0), (0, n - w.shape[1])))


def kernel(x, c, ctx, c_ctx, ada_w, ada_b, norm_pre_mix, norm_post_mix, norm_pre_ffn, norm_post_ffn, w_in, shift_mu, rwkv_w0, rwkv_w2, rwkv_a0, rwkv_a2, rwkv_g2, rwkv_k_k, rwkv_k_a, rwkv_r_k, rwkv_ln_g, rwkv_ln_b, w_rwkv_up, gla_alpha_up, gla_alpha_b, gla_norm_g, w_gla_up, w_out, ffn_w_gate, ffn_w_up, ffn_w_down):
    assert ada_w.shape[0] == 1, "single trunk layer"
    bsz, seq, d = x.shape
    n_ctx = ctx.shape[1]
    prec = BF16

    cvecs = jnp.concatenate([c, c_ctx[None, :], jnp.zeros((8 - bsz - 1, d), F32)], 0)
    mod = _modulation(cvecs, ada_w[0], ada_b[0])
    mod_x = mod[:bsz].reshape(bsz, 6, 1, d)
    shx1, scx1, gx1, shx2, scx2, gx2 = (mod_x[:, i] for i in range(6))
    mod_c = jnp.broadcast_to(mod[bsz].reshape(1, 6, 1, d), (bsz, 6, 1, d))
    shc1, scc1 = mod_c[:, 0], mod_c[:, 1]

    w_all = w_in[0]
    mix_in = RWKV_IN + GLA_IN
    w_rwkv = _pad_cols(w_all[:, :RWKV_IN], RWKV_PAD).astype(BF16)
    w_gla = _pad_cols(w_all[:, RWKV_IN:mix_in], GLA_PAD).astype(BF16)
    w_gate = w_all[:, mix_in:].astype(BF16)
    mu = _pad_cols(shift_mu, RWKV_PAD)
    n_pre = norm_pre_mix

    px_rwkv = _project_shift(x, shx1, scx1, n_pre, w_rwkv, mu, 1024, 512, "proj_rwkv")
    px_gla = _project_colmajor(x, shx1, scx1, n_pre, w_gla, 16, 512, "proj_gla")
    px_gate = _project(x, shx1, scx1, n_pre, w_gate, 1024, 1024, "proj_gate")
    pc_rwkv = _project_shift(ctx, shc1, scc1, n_pre, w_rwkv, mu, n_ctx, 512, "proj_rwkv_ctx")
    pc_gla = _project(ctx, shc1, scc1, n_pre, w_gla, n_ctx, 512, "proj_gla_ctx")

    y_f, y_b = _rwkv_scan(px_rwkv, pc_rwkv, rwkv_w2[0], rwkv_w0[0][:, None, :], rwkv_a2[0], rwkv_a0[0][:, None, :],
                          rwkv_k_k, rwkv_k_a, prec)
    ya = _rwkv_post(y_f, y_b, px_rwkv, rwkv_a2[0], rwkv_a0[0][:, None, :], rwkv_g2[0], rwkv_k_a,
                    rwkv_r_k.reshape(1, RWKV_WIDTH), rwkv_ln_g, rwkv_ln_b)

    o_f, o_b = _gla_scan(px_gla, pc_gla, gla_alpha_up[0], gla_alpha_b[0][:, None, :], prec)
    yb = _gla_post(o_f, o_b, px_gla, gla_norm_g)

    m = _merge(ya, yb, w_rwkv_up[0].astype(BF16), w_gla_up[0].astype(BF16), px_gate)
    x1, h2 = _mix_out(m, w_out[0].astype(BF16), x, gx1, norm_post_mix, norm_pre_ffn, shx2, scx2)
    hf = _ffn_up(h2, ffn_w_gate[0].astype(BF16), ffn_w_up[0].astype(BF16))
    return _ffn_down(hf, ffn_w_down[0].astype(BF16), x1, gx2, norm_post_ffn)
```

```python
import functools

import jax
import jax.numpy as jnp
from jax import lax
from jax.experimental import pallas as pl
from jax.experimental.pallas import tpu as pltpu

F32 = jnp.float32
BF16 = jnp.bfloat16
HIGHEST = lax.Precision.HIGHEST

LANES = 128
VMEM_LIMIT_BYTES = 56 * 1024 * 1024

GRID_W = 64
CHUNK = 64
RWKV_HEADS, RWKV_HEAD_DIM = 16, 64
RWKV_WIDTH = RWKV_HEADS * RWKV_HEAD_DIM
DECAY_LORA = ICL_LORA = 96
GATE_LORA = 64
RWKV_GN_EPS = 64e-5
GLA_HEADS, GLA_KEY_DIM, GLA_VAL_DIM = 4, 128, 256
GLA_QK_WIDTH = GLA_HEADS * GLA_KEY_DIM
GLA_V_WIDTH = GLA_HEADS * GLA_VAL_DIM
GLA_GATE_LORA = 16
GLA_TAU = 16.0
GLA_NORM_EPS = 1e-5
GLA_SUB = 16
NORM_EPS = 1e-6
PROLOGUE_ROWS = 256

RWKV_IN = 3 * RWKV_WIDTH + 2 * DECAY_LORA + 2 * ICL_LORA + GATE_LORA
RWKV_PAD = 3584
OFF_WD = 3 * RWKV_WIDTH
OFF_AD = OFF_WD + 2 * DECAY_LORA
OFF_GD = OFF_AD + 2 * ICL_LORA
GLA_IN = 2 * GLA_QK_WIDTH + 2 * GLA_V_WIDTH + 2 * GLA_GATE_LORA
GLA_BLK = 512
GLA_PAD = 7 * GLA_BLK
OFF_GLA_AD = 6 * GLA_BLK

PAIR = 2 * RWKV_HEAD_DIM
N_PAIRS = RWKV_HEADS // 2


def _cparams(semantics):
    return pltpu.CompilerParams(dimension_semantics=semantics, vmem_limit_bytes=VMEM_LIMIT_BYTES)


def _sigmoid(z):
    return 1.0 / (1.0 + jnp.exp(-z))


def _softplus(z):
    return jnp.maximum(z, 0.0) + jnp.log(1.0 + jnp.exp(-jnp.abs(z)))


def _dot(a, b, dims, precision):
    if precision is BF16:
        a, b, precision = a.astype(BF16), b.astype(BF16), None
    return lax.dot_general(a, b, (dims, ((), ())), precision=precision, preferred_element_type=F32)


def _mm(a, b, precision=None):
    return _dot(a, b, ((1,), (0,)), precision)


def _mm_nt(a, b, precision=None):
    return _dot(a, b, ((1,), (1,)), precision)


def _mm_tn(a, b, precision=None):
    return _dot(a, b, ((0,), (0,)), precision)


def _split2(z):
    hi = z.astype(BF16)
    return hi, (z - hi.astype(F32)).astype(BF16)


def _mm_split(a, b):
    ah, al = _split2(a)
    bh, bl = _split2(b)
    return (_mm(al, bh) + _mm(ah, bl)) + _mm(ah, bh)


def _mm_exact_lhs(e, b):
    hi = b.astype(BF16)
    r = b - hi.astype(F32)
    mid = r.astype(BF16)
    lo = (r - mid.astype(F32)).astype(BF16)
    e = e.astype(BF16)
    return (_mm(e, lo) + _mm(e, mid)) + _mm(e, hi)


def _mod_kernel(c_ref, w_ref, b_ref, o_ref):
    s = c_ref[...]
    s = s * _sigmoid(s)
    o_ref[...] = _mm_split(s, w_ref[...]) + b_ref[...]


def _modulation(cvecs, ada_w, ada_b, tn=1024):
    m, d = cvecs.shape
    n = ada_w.shape[1]
    return pl.pallas_call(
        _mod_kernel,
        grid=(n // tn,),
        in_specs=[pl.BlockSpec((m, d), lambda j: (0, 0)),
                  pl.BlockSpec((d, tn), lambda j: (0, j)),
                  pl.BlockSpec((1, tn), lambda j: (0, j))],
        out_specs=pl.BlockSpec((m, tn), lambda j: (0, j)),
        out_shape=jax.ShapeDtypeStruct((m, n), F32),
        compiler_params=_cparams(("arbitrary",)),
        name="adaln_mod",
    )(cvecs, ada_w, ada_b.reshape(1, n))


def _proj_kernel(gate_out, x_ref, sh_ref, sc_ref, g_ref, w_ref, o_ref, h_scr):
    @pl.when(pl.program_id(2) == 0)
    def _():
        rows = min(PROLOGUE_ROWS, h_scr.shape[0])

        def body(rb, carry):
            sl = pl.ds(pl.multiple_of(rb * rows, rows), rows)
            x = x_ref[0, sl, :]
            ms = jnp.mean(x * x, axis=-1, keepdims=True)
            y = x * lax.rsqrt(ms + NORM_EPS) * g_ref[...]
            h_scr[sl, :] = (y * (1.0 + sc_ref[0]) + sh_ref[0]).astype(BF16)
            return carry

        lax.fori_loop(0, h_scr.shape[0] // rows, body, 0)

    p = _mm(h_scr[...], w_ref[...])
    o_ref[0] = (_sigmoid(p) if gate_out else p).astype(o_ref.dtype)


def _project(x, shift, scale, gain, w, tm, tn, name, gate_out=False):
    b, t, d = x.shape
    n = w.shape[1]
    return pl.pallas_call(
        functools.partial(_proj_kernel, gate_out),
        grid=(b, t // tm, n // tn),
        in_specs=[pl.BlockSpec((1, tm, d), lambda bi, i, j: (bi, i, 0)),
                  pl.BlockSpec((1, 1, d), lambda bi, i, j: (bi, 0, 0)),
                  pl.BlockSpec((1, 1, d), lambda bi, i, j: (bi, 0, 0)),
                  pl.BlockSpec((1, d), lambda bi, i, j: (0, 0)),
                  pl.BlockSpec((d, tn), lambda bi, i, j: (0, j))],
        out_specs=pl.BlockSpec((1, tm, tn), lambda bi, i, j: (bi, i, j)),
        out_shape=jax.ShapeDtypeStruct((b, t, n), BF16 if gate_out else F32),
        scratch_shapes=[pltpu.VMEM((tm, d), BF16)],
        compiler_params=_cparams(("arbitrary", "arbitrary", "arbitrary")),
        name=name,
    )(x, shift, scale, gain, w)


def _colmajor_perm(n_rows, n_cols):
    dst = jnp.arange(n_rows * n_cols)
    src = (dst % n_rows) * n_cols + dst // n_rows
    return (src[:, None] == dst[None, :]).astype(BF16)


def _proj_cm_kernel(x_ref, sh_ref, sc_ref, g_ref, perm_ref, w_ref, o_ref, hr_scr, h_scr):
    @pl.when(pl.program_id(2) == 0)
    def _():
        n_r, n_c, d = x_ref.shape[1:]
        rb = PROLOGUE_ROWS // n_c
        rows = rb * n_c

        def body(kb, carry):
            x = x_ref[0, pl.ds(pl.multiple_of(kb * rb, rb), rb), :, :].reshape(rows, d)
            ms = jnp.mean(x * x, axis=-1, keepdims=True)
            y = x * lax.rsqrt(ms + NORM_EPS) * g_ref[...]
            hr_scr[pl.ds(pl.multiple_of(kb * rows, rows), rows), :] = (y * (1.0 + sc_ref[0]) + sh_ref[0]).astype(BF16)
            return carry

        lax.fori_loop(0, n_r // rb, body, 0)
        for cb in range(d // GLA_BLK):
            cs = slice(cb * GLA_BLK, (cb + 1) * GLA_BLK)
            h_scr[:, cs] = _mm(perm_ref[...], hr_scr[:, cs]).astype(BF16)

    o_ref[0] = _mm(h_scr[...], w_ref[...])


def _project_colmajor(x, shift, scale, gain, w, n_cols, tn, name):
    b, t, d = x.shape
    n = w.shape[1]
    n_r = t // GRID_W
    tm = n_r * n_cols
    x4 = x.reshape(b, n_r, GRID_W, d)
    perm = _colmajor_perm(n_r, n_cols)
    return pl.pallas_call(
        _proj_cm_kernel,
        grid=(b, GRID_W // n_cols, n // tn),
        in_specs=[pl.BlockSpec((1, n_r, n_cols, d), lambda bi, i, j: (bi, 0, i, 0)),
                  pl.BlockSpec((1, 1, d), lambda bi, i, j: (bi, 0, 0)),
                  pl.BlockSpec((1, 1, d), lambda bi, i, j: (bi, 0, 0)),
                  pl.BlockSpec((1, d), lambda bi, i, j: (0, 0)),
                  pl.BlockSpec((tm, tm), lambda bi, i, j: (0, 0)),
                  pl.BlockSpec((d, tn), lambda bi, i, j: (0, j))],
        out_specs=pl.BlockSpec((1, tm, tn), lambda bi, i, j: (bi, i, j)),
        out_shape=jax.ShapeDtypeStruct((b, t, n), F32),
        scratch_shapes=[pltpu.VMEM((tm, d), BF16), pltpu.VMEM((tm, d), BF16)],
        compiler_params=_cparams(("arbitrary", "arbitrary", "arbitrary")),
        name=name,
    )(x4, shift, scale, gain, perm, w)


def _norm_modulate(x, g_ref, sh_ref, sc_ref):
    ms = jnp.mean(x * x, axis=-1, keepdims=True)
    y = x * lax.rsqrt(ms + NORM_EPS) * g_ref[...]
    return (y * (1.0 + sc_ref[0]) + sh_ref[0]).astype(BF16)


def _proj_shift_kernel(x_ref, xp_ref, xn_ref, sh_ref, sc_ref, g_ref, w_ref, mu_ref, o_ref, h_scr, halo_scr):
    i = pl.program_id(1)
    tm = h_scr.shape[0]

    @pl.when(pl.program_id(2) == 0)
    def _():
        rows = min(PROLOGUE_ROWS, tm)

        def body(rb, carry):
            sl = pl.ds(pl.multiple_of(rb * rows, rows), rows)
            h_scr[sl, :] = _norm_modulate(x_ref[0, sl, :], g_ref, sh_ref, sc_ref)
            return carry

        lax.fori_loop(0, tm // rows, body, 0)
        halo_scr[...] = _norm_modulate(jnp.concatenate([xp_ref[0], xn_ref[0]], 0), g_ref, sh_ref, sc_ref)

    p = _mm(h_scr[...], w_ref[...])
    ph = _mm(halo_scr[...], w_ref[...])
    before = jnp.where(i == 0, 0.0, ph[7:8])
    after = jnp.where(i == pl.num_programs(1) - 1, 0.0, ph[8:9])
    row = lax.broadcasted_iota(jnp.int32, p.shape, 0)
    prev = jnp.where(row == 0, before, pltpu.roll(p, 1, 0))
    nxt = jnp.where(row == tm - 1, after, pltpu.roll(p, tm - 1, 0))
    o_ref[0] = p + mu_ref[...] * (0.5 * (prev + nxt) - p)


def _project_shift(x, shift, scale, gain, w, mu, tm, tn, name):
    b, t, d = x.shape
    n = w.shape[1]
    r8, n8 = tm // 8, t // 8
    return pl.pallas_call(
        _proj_shift_kernel,
        grid=(b, t // tm, n // tn),
        in_specs=[pl.BlockSpec((1, tm, d), lambda bi, i, j: (bi, i, 0)),
                  pl.BlockSpec((1, 8, d), lambda bi, i, j: (bi, jnp.maximum(i * r8 - 1, 0), 0)),
                  pl.BlockSpec((1, 8, d), lambda bi, i, j: (bi, jnp.minimum((i + 1) * r8, n8 - 1), 0)),
                  pl.BlockSpec((1, 1, d), lambda bi, i, j: (bi, 0, 0)),
                  pl.BlockSpec((1, 1, d), lambda bi, i, j: (bi, 0, 0)),
                  pl.BlockSpec((1, d), lambda bi, i, j: (0, 0)),
                  pl.BlockSpec((d, tn), lambda bi, i, j: (0, j)),
                  pl.BlockSpec((1, tn), lambda bi, i, j: (0, j))],
        out_specs=pl.BlockSpec((1, tm, tn), lambda bi, i, j: (bi, i, j)),
        out_shape=jax.ShapeDtypeStruct((b, t, n), F32),
        scratch_shapes=[pltpu.VMEM((tm, d), BF16), pltpu.VMEM((16, d), BF16)],
        compiler_params=_cparams(("arbitrary", "arbitrary", "arbitrary")),
        name=name,
    )(x, x, x, shift, scale, gain, w, mu)


def _head_sum_matrix():
    r = lax.broadcasted_iota(jnp.int32, (PAIR, PAIR), 0)
    c = lax.broadcasted_iota(jnp.int32, (PAIR, PAIR), 1)
    return jnp.where((r >> 6) == (c >> 6), 1.0, 0.0).astype(BF16)


def _head_sum(z, hsum):
    hi = z.astype(BF16)
    lo = (z - hi.astype(F32)).astype(BF16)
    return _mm(hi, hsum) + _mm(lo, hsum)


def _rwkv_kernel(prec, n_ctx_chunks, plf_ref, pcf_ref, plb_ref, pcb_ref, w2_ref, w0_ref, a2_ref, a0_ref, kk_ref, ka_ref,
                 yf_ref, yb_ref, st_scr):
    @pl.when(pl.program_id(1) == 0)
    def _():
        st_scr[...] = jnp.zeros_like(st_scr)

    is_ctx = pl.program_id(1) < n_ctx_chunks
    for dr, (pl_ref, pc_ref, y_ref) in enumerate(((plf_ref, pcf_ref, yf_ref), (plb_ref, pcb_ref, yb_ref))):
        _rwkv_chunk(dr == 1, prec, is_ctx, pl_ref, pc_ref, w2_ref.at[dr], w0_ref.at[dr], a2_ref.at[dr], a0_ref.at[dr],
                    kk_ref, ka_ref, y_ref, st_scr.at[dr])


def _rwkv_chunk(rev, prec, is_ctx, pl_ref, pc_ref, w2_ref, w0_ref, a2_ref, a0_ref, kk_ref, ka_ref, y_ref, st_scr):
    hd = RWKV_HEAD_DIM
    c2 = 2 * CHUNK
    p = jnp.where(is_ctx, pc_ref[0], pl_ref[0])
    r = p[:, 0:RWKV_WIDTH]
    k = p[:, RWKV_WIDTH:2 * RWKV_WIDTH]
    v = p[:, 2 * RWKV_WIDTH:3 * RWKV_WIDTH]
    d_off = DECAY_LORA if rev else 0
    wd = p[:, OFF_WD + d_off:OFF_WD + d_off + DECAY_LORA]
    ad = p[:, OFF_AD + d_off:OFF_AD + d_off + ICL_LORA]

    w_log = -_softplus(-(w0_ref[...] + _mm_split(jnp.tanh(wd), w2_ref[...]))) - 0.5
    lw = -jnp.exp(w_log)
    a = _sigmoid(a0_ref[...] + _mm(ad, a2_ref[...], BF16))
    kk_raw = k * kk_ref[...]
    kd = k * (1.0 + (a - 1.0) * ka_ref[...])

    ri = lax.broadcasted_iota(jnp.int32, (c2, PAIR), 0)
    li = lax.broadcasted_iota(jnp.int32, (c2, PAIR), 1)
    rt, lt = ri & (CHUNK - 1), li & (hd - 1)
    same_head = (ri >> 6) == (li >> 6)
    strict = (lt > rt) if rev else (lt < rt)
    mask_n = jnp.logical_and(same_head, strict)
    mask_k = jnp.logical_and(jnp.logical_not(same_head), strict)
    ident = ri == li
    eye = jnp.where(ident, 1.0, 0.0).astype(F32)
    rc = lax.broadcasted_iota(jnp.int32, (CHUNK, PAIR), 0)
    lc = lax.broadcasted_iota(jnp.int32, (CHUNK, PAIR), 1)
    incl_c = ((lc & (hd - 1)) >= rc) if rev else ((lc & (hd - 1)) <= rc)
    lane_e = lc < hd
    hsum = _head_sum_matrix()

    ci = lax.broadcasted_iota(jnp.int32, (CHUNK, CHUNK), 0)
    cj = lax.broadcasted_iota(jnp.int32, (CHUNK, CHUNK), 1)
    tri = jnp.where((cj >= ci) if rev else (cj <= ci), 1.0, 0.0).astype(F32)
    cum = _mm_exact_lhs(tri, lw)
    total = cum[0:1] if rev else cum[CHUNK - 1:CHUNK]
    e_prev = jnp.exp(cum - lw)
    e_neg = jnp.exp(-cum)
    e_pos = jnp.exp(cum)
    e_rest = jnp.exp(total - cum)
    p_end = jnp.exp(total)

    def split(z):
        ze = jnp.where(lane_e, z, 0.0)
        return ze, z - ze

    pairs = range(N_PAIRS)
    sls = [slice(pr * PAIR, (pr + 1) * PAIR) for pr in pairs]
    cat0 = lambda *z: jnp.concatenate(z, 0)

    kkr = [kk_raw[:, sl] for sl in sls]
    nrm2 = [_head_sum(z * z, hsum) for z in kkr]
    kk = [z / jnp.maximum(jnp.sqrt(n), 1e-12) for z, n in zip(kkr, nrm2)]
    bb = [z * a[:, sl] for z, sl in zip(kk, sls)]
    at = [split(z * e_prev[:, sl]) for z, sl in zip(kk, sls)]
    bt = [z * e_neg[:, sl] for z, sl in zip(bb, sls)]
    kt = [kd[:, sl] * e_neg[:, sl] for sl in sls]
    rt_ = [r[:, sl] * e_pos[:, sl] for sl in sls]
    rs = [split(z) for z in rt_]
    vs = [split(v[:, sl]) for sl in sls]
    bh = [cat0(*split(z * e_rest[:, sl])) for z, sl in zip(bb, sls)]
    kh = [cat0(*split(kd[:, sl] * e_rest[:, sl])) for sl in sls]
    v_swap = [cat0(vo, ve) for ve, vo in vs]
    v_stack = [cat0(ve, vo) for ve, vo in vs]

    g_e = [_mm_nt(cat0(at[i][0], rs[i][0]), cat0(bt[i], kt[i]), prec) for i in pairs]
    g_o = [_mm_nt(cat0(at[i][1], rs[i][1]), cat0(kt[i], bt[i]), prec) for i in pairs]
    g_top = [cat0(g_e[i][0:CHUNK], g_o[i][0:CHUNK]) for i in pairs]
    nbd = [jnp.where(mask_n, z, 0.0) for z in g_top]
    aak = [jnp.where(mask_k, z, 0.0) for z in g_top]
    rab = [jnp.where(incl_c, jnp.where(lane_e, g_e[i][CHUNK:c2], g_o[i][CHUNK:c2]), 0.0) for i in pairs]
    rak = [jnp.where(incl_c, jnp.where(lane_e, g_o[i][CHUNK:c2], g_e[i][CHUNK:c2]), 0.0) for i in pairs]

    x = [_mm(aak[i], v_swap[i], prec) for i in pairs]
    n2 = [_mm(z, z, prec) for z in nbd]
    y0b = [_mm(rak[i], v_swap[i], prec) for i in pairs]
    n4 = [_mm(z, z, prec) for z in n2]
    imn = [eye - z for z in nbd]
    p1 = [imn[i] + _mm(imn[i], n2[i], prec) for i in pairs]
    n8 = [_mm(z, z, prec) for z in n4]
    nb = [_mm_tn(kh[i], v_stack[i], prec) for i in pairs]
    n16 = [_mm(z, z, prec) for z in n8]
    p2 = [eye + n4[i] + n8[i] + _mm(n4[i], n8[i], prec) for i in pairs]
    n32 = [_mm(z, z, prec) for z in n16]
    p12 = [_mm(p1[i], p2[i], prec) for i in pairs]
    p3 = [eye + n16[i] + n32[i] + _mm(n16[i], n32[i], prec) for i in pairs]
    tinv = [_mm(p12[i], p3[i], prec) for i in pairs]
    wu = [-_mm(tinv[i], jnp.concatenate([cat0(*at[i]), x[i]], 1), prec) for i in pairs]
    qy = [_mm(rab[i], wu[i], prec) for i in pairs]
    mn = [_mm_tn(bh[i], wu[i], prec) for i in pairs]
    q = [rt_[i] + qy[i][:, 0:PAIR] for i in pairs]
    m = [jnp.where(ident, p_end[:, sls[i]], 0.0) + mn[i][:, 0:PAIR] for i in pairs]
    qm = [_mm(cat0(q[i], m[i]), st_scr[i], prec) for i in pairs]
    for i in pairs:
        y_ref[0, :, sls[i]] = qm[i][0:CHUNK] + qy[i][:, PAIR:2 * PAIR] + y0b[i]
        st_scr[i] = qm[i][CHUNK:CHUNK + PAIR] + mn[i][:, PAIR:2 * PAIR] + nb[i]


def _scan_order(n_ctx_chunks, n_lat):
    n_steps = n_ctx_chunks + n_lat
    lat_f = lambda i: jnp.maximum(i - n_ctx_chunks, 0)
    lat_b = lambda i: jnp.where(i < n_ctx_chunks, n_lat - 1, n_steps - 1 - i)
    ctx_f = lambda i: jnp.minimum(i, n_ctx_chunks - 1)
    ctx_b = lambda i: jnp.maximum(n_ctx_chunks - 1 - i, 0)
    return n_steps, (lat_f, lat_b), (ctx_f, ctx_b)


def _rwkv_scan(p_lat, p_ctx, w2, w0, a2, a0, k_k, k_a, prec):
    b, t, w = p_lat.shape
    n_lat = t // CHUNK
    n_ctx_chunks = p_ctx.shape[1] // CHUNK
    n_steps, lat_of, ctx_of = _scan_order(n_ctx_chunks, n_lat)
    chunk = lambda f: pl.BlockSpec((1, CHUNK, w), lambda bi, i: (bi, f(i), 0))
    out = lambda f: pl.BlockSpec((1, CHUNK, RWKV_WIDTH), lambda bi, i: (bi, f(i), 0))
    vec = lambda bi, i: (0, 0)
    vec3 = lambda bi, i: (0, 0, 0)
    y_shape = jax.ShapeDtypeStruct((b, t, RWKV_WIDTH), F32)
    return pl.pallas_call(
        functools.partial(_rwkv_kernel, prec, n_ctx_chunks),
        grid=(b, n_steps),
        in_specs=[chunk(lat_of[0]), chunk(ctx_of[0]), chunk(lat_of[1]), chunk(ctx_of[1]),
                  pl.BlockSpec((2, DECAY_LORA, RWKV_WIDTH), vec3),
                  pl.BlockSpec((2, 1, RWKV_WIDTH), vec3),
                  pl.BlockSpec((2, ICL_LORA, RWKV_WIDTH), vec3),
                  pl.BlockSpec((2, 1, RWKV_WIDTH), vec3),
                  pl.BlockSpec((1, RWKV_WIDTH), vec),
                  pl.BlockSpec((1, RWKV_WIDTH), vec)],
        out_specs=[out(lat_of[0]), out(lat_of[1])],
        out_shape=[y_shape, y_shape],
        scratch_shapes=[pltpu.VMEM((2, N_PAIRS, PAIR, PAIR), F32)],
        compiler_params=_cparams(("arbitrary", "arbitrary")),
        name="rwkv7_scan",
    )(p_lat, p_ctx, p_lat, p_ctx, w2, w0, a2, a0, k_k, k_a)


def _gla_kernel(prec, n_ctx_chunks, *refs):
    lat_f, lat_b = refs[0:6], refs[6:12]
    aup_ref, ab_ref, of_ref, ob_ref, st_scr = refs[12:17]

    @pl.when(pl.program_id(1) == 0)
    def _():
        st_scr[...] = jnp.zeros_like(st_scr)

    is_ctx = pl.program_id(1) < n_ctx_chunks
    for dr, (blocks, o_ref) in enumerate(((lat_f, of_ref), (lat_b, ob_ref))):
        _gla_chunk(dr == 1, prec, is_ctx, *blocks, aup_ref.at[dr], ab_ref.at[dr], o_ref, st_scr.at[dr])


def _gla_chunk(rev, prec, is_ctx, q_ref, k_ref, v0_ref, v1_ref, ad_ref, ctx_ref, aup_ref, ab_ref, o_ref, st_scr):
    pc = ctx_ref[0]
    q = jnp.where(is_ctx, pc[:, 0:GLA_BLK], q_ref[0])
    k = jnp.where(is_ctx, pc[:, GLA_BLK:2 * GLA_BLK], k_ref[0])
    v = jnp.concatenate([jnp.where(is_ctx, pc[:, 2 * GLA_BLK:3 * GLA_BLK], v0_ref[0]),
                         jnp.where(is_ctx, pc[:, 3 * GLA_BLK:4 * GLA_BLK], v1_ref[0])], 1)
    d_off = GLA_GATE_LORA if rev else 0
    ad = jnp.where(is_ctx, pc[:, OFF_GLA_AD:OFF_GLA_AD + LANES], ad_ref[0][:, 0:LANES])
    ad = ad[:, d_off:d_off + GLA_GATE_LORA]

    la = -_softplus(-(_mm_split(ad, aup_ref[...]) + ab_ref[...])) * (1.0 / GLA_TAU)
    ci = lax.broadcasted_iota(jnp.int32, (CHUNK, CHUNK), 0)
    cj = lax.broadcasted_iota(jnp.int32, (CHUNK, CHUNK), 1)
    tri = jnp.where((cj >= ci) if rev else (cj <= ci), 1.0, 0.0).astype(F32)
    cum = _mm_exact_lhs(tri, la)
    total = cum[0:1] if rev else cum[CHUNK - 1:CHUNK]

    dk, dv, sb = GLA_KEY_DIM, GLA_VAL_DIM, GLA_SUB
    n_sb = CHUNK // sb
    row = lax.broadcasted_iota(jnp.int32, (CHUNK, dk), 0)
    row_in = row & (sb - 1)
    arow = lax.broadcasted_iota(jnp.int32, (CHUNK, CHUNK), 0)
    acol = lax.broadcasted_iota(jnp.int32, (CHUNK, CHUNK), 1)
    scale = GLA_KEY_DIM ** -0.5

    heads = range(GLA_HEADS)
    ksl = [slice(h * dk, (h + 1) * dk) for h in heads]
    qh = [q[:, s_] * scale for s_ in ksl]
    kh = [k[:, s_] for s_ in ksl]
    bh = [cum[:, s_] for s_ in ksl]
    lah = [la[:, s_] for s_ in ksl]
    toth = [total[:, s_] for s_ in ksl]
    vh = [v[:, h * dv:(h + 1) * dv] for h in heads]
    st = [st_scr[h] for h in heads]

    o_inter = [_mm_nt(qh[h] * jnp.exp(bh[h]), st[h], prec) for h in heads]
    st_new = [st[h] * jnp.exp(toth[h]) + _mm_tn(vh[h], kh[h] * jnp.exp(toth[h] - bh[h]), prec) for h in heads]

    off_rows = [[] for _ in heads]
    for blk in range(n_sb):
        rs = slice(blk * sb, (blk + 1) * sb)
        first = blk * sb + (sb - 1 if rev else 0)
        is_first_blk = (blk == n_sb - 1) if rev else (blk == 0)
        before = (row >= (blk + 1) * sb) if rev else (row < blk * sb)
        for h in heads:
            if is_first_blk:
                off_rows[h].append(jnp.zeros((sb, CHUNK), F32))
                continue
            beta = bh[h][first:first + 1] - lah[h][first:first + 1]
            qs = qh[h][rs] * jnp.exp(bh[h][rs] - beta)
            ksc = jnp.where(before, kh[h] * jnp.exp(jnp.minimum(beta - bh[h], 0.0)), 0.0)
            off_rows[h].append(_mm_nt(qs, ksc, prec))
    att = [jnp.concatenate(off_rows[h], 0) for h in heads]

    for s in range(sb):
        pick = lambda z: jnp.concatenate(
            [jnp.broadcast_to(z[blk * sb + s:blk * sb + s + 1], (sb, dk)) for blk in range(n_sb)], 0)
        ok = (row_in <= s) if rev else (row_in >= s)
        tgt = jnp.logical_and(acol == (arow & ~(sb - 1)) + s,
                              ((arow & (sb - 1)) <= s) if rev else ((arow & (sb - 1)) >= s))
        for h in heads:
            e = jnp.exp(jnp.where(ok, bh[h] - pick(bh[h]), 0.0))
            col = jnp.sum(jnp.where(ok, qh[h] * pick(kh[h]) * e, 0.0), axis=-1, keepdims=True)
            att[h] = jnp.where(tgt, col, att[h])

    for h in heads:
        o_ref[0, :, h * dv:(h + 1) * dv] = o_inter[h] + _mm(att[h], vh[h], prec)
        st_scr[h] = st_new[h]


def _gla_scan(p_lat, p_ctx, alpha_up, alpha_b, prec):
    b, t, w = p_lat.shape
    assert w == GLA_PAD and t == GRID_W * CHUNK
    n_ctx_chunks = p_ctx.shape[1] // CHUNK
    n_steps, col_of, ctx_of = _scan_order(n_ctx_chunks, GRID_W)

    def blocks(dr):
        lat = lambda m: pl.BlockSpec((1, CHUNK, GLA_BLK), lambda bi, i: (bi, col_of[dr](i), m))
        return [lat(0), lat(1), lat(2), lat(3), lat(6), pl.BlockSpec((1, CHUNK, w), lambda bi, i: (bi, ctx_of[dr](i), 0))]

    out = lambda dr: pl.BlockSpec((1, CHUNK, GLA_V_WIDTH), lambda bi, i: (bi, col_of[dr](i), 0))
    vec3 = lambda bi, i: (0, 0, 0)
    o_shape = jax.ShapeDtypeStruct((b, t, GLA_V_WIDTH), F32)
    return pl.pallas_call(
        functools.partial(_gla_kernel, prec, n_ctx_chunks),
        grid=(b, n_steps),
        in_specs=blocks(0) + blocks(1) + [pl.BlockSpec((2, GLA_GATE_LORA, GLA_QK_WIDTH), vec3),
                                          pl.BlockSpec((2, 1, GLA_QK_WIDTH), vec3)],
        out_specs=[out(0), out(1)],
        out_shape=[o_shape, o_shape],
        scratch_shapes=[pltpu.VMEM((2, GLA_HEADS, GLA_VAL_DIM, GLA_KEY_DIM), F32)],
        compiler_params=_cparams(("arbitrary", "arbitrary")),
        name="gla_scan",
    )(*([p_lat] * 5 + [p_ctx]) * 2, alpha_up, alpha_b)


def _rwkv_post_kernel(yf_ref, yb_ref, pm_ref, a2_ref, a0_ref, g2_ref, ka_ref, rk_ref, lng_ref, lnb_ref, o_ref):
    p = pm_ref[0]
    r = p[:, 0:RWKV_WIDTH]
    k = p[:, RWKV_WIDTH:2 * RWKV_WIDTH]
    v = p[:, 2 * RWKV_WIDTH:3 * RWKV_WIDTH]
    ad_f = p[:, OFF_AD:OFF_AD + ICL_LORA]
    ad_b = p[:, OFF_AD + ICL_LORA:OFF_AD + 2 * ICL_LORA]
    gd = p[:, OFF_GD:OFF_GD + GATE_LORA]
    ka = ka_ref[...]
    a_f = _sigmoid(a0_ref[0] + _mm(ad_f, a2_ref[0], BF16))
    a_b = _sigmoid(a0_ref[1] + _mm(ad_b, a2_ref[1], BF16))
    kd_sum = k * (1.0 + (a_f - 1.0) * ka) + k * (1.0 + (a_b - 1.0) * ka)
    gate = _mm(_sigmoid(gd), g2_ref[...], BF16)
    rkk = r * kd_sum * rk_ref[...]
    ysum = yf_ref[0] + yb_ref[0]
    hsum = _head_sum_matrix()
    inv_n = 1.0 / RWKV_HEAD_DIM
    for pr in range(N_PAIRS):
        sl = slice(pr * PAIR, (pr + 1) * PAIR)
        ys = ysum[:, sl]
        mean = _head_sum(ys, hsum) * inv_n
        dlt = ys - mean
        var = _head_sum(dlt * dlt, hsum) * inv_n
        gn = dlt * lax.rsqrt(var + RWKV_GN_EPS) * lng_ref[:, sl] + lnb_ref[:, sl]
        bonus = _head_sum(rkk[:, sl], hsum) * v[:, sl]
        o_ref[0, :, sl] = ((gn + bonus) * gate[:, sl]).astype(o_ref.dtype)


def _rwkv_post(y_f, y_b, pmix, a2, a0, g2, k_a, r_k, ln_g, ln_b, tm=256):
    b, t, w = y_f.shape
    row = lambda bi, i: (bi, i, 0)
    vec = lambda bi, i: (0, 0)
    vec3 = lambda bi, i: (0, 0, 0)
    return pl.pallas_call(
        _rwkv_post_kernel,
        grid=(b, t // tm),
        in_specs=[pl.BlockSpec((1, tm, w), row),
                  pl.BlockSpec((1, tm, w), row),
                  pl.BlockSpec((1, tm, pmix.shape[2]), row),
                  pl.BlockSpec((2, ICL_LORA, w), vec3),
                  pl.BlockSpec((2, 1, w), vec3),
                  pl.BlockSpec((GATE_LORA, w), vec),
                  pl.BlockSpec((1, w), vec), pl.BlockSpec((1, w), vec),
                  pl.BlockSpec((1, w), vec), pl.BlockSpec((1, w), vec)],
        out_specs=pl.BlockSpec((1, tm, w), row),
        out_shape=jax.ShapeDtypeStruct((b, t, w), BF16),
        compiler_params=_cparams(("arbitrary", "arbitrary")),
        name="rwkv_post",
    )(y_f, y_b, pmix, a2, a0, g2, k_a, r_k, ln_g, ln_b)


def _gla_post_kernel(of_ref, ob_ref, g_ref, ng_ref, perm_ref, o_ref):
    n_r, n_c = o_ref.shape[1:3]
    dv = GLA_VAL_DIM
    for h in range(GLA_HEADS):
        sl = slice(h * dv, (h + 1) * dv)
        oh = of_ref[0, :, sl] + ob_ref[0, :, sl]
        oh = oh * lax.rsqrt(jnp.mean(oh * oh, axis=-1, keepdims=True) + GLA_NORM_EPS) * ng_ref[:, sl]
        gh = g_ref[0, :, sl]
        y = (oh * (gh * _sigmoid(gh))).astype(BF16)
        o_ref[0, :, :, sl] = _mm(perm_ref[...], y).reshape(n_r, n_c, dv).astype(o_ref.dtype)


def _gla_post(o_f, o_b, p_gla, norm_g, n_cols=16):
    b, t, w = o_f.shape
    n_r = t // GRID_W
    tm = n_r * n_cols
    perm_t = _colmajor_perm(n_r, n_cols).T
    row = lambda bi, i: (bi, i, 0)
    out = pl.pallas_call(
        _gla_post_kernel,
        grid=(b, GRID_W // n_cols),
        in_specs=[pl.BlockSpec((1, tm, w), row),
                  pl.BlockSpec((1, tm, w), row),
                  pl.BlockSpec((1, tm, w), lambda bi, i: (bi, i, 2)),
                  pl.BlockSpec((1, w), lambda bi, i: (0, 0)),
                  pl.BlockSpec((tm, tm), lambda bi, i: (0, 0))],
        out_specs=pl.BlockSpec((1, n_r, n_cols, w), lambda bi, i: (bi, 0, i, 0)),
        out_shape=jax.ShapeDtypeStruct((b, n_r, GRID_W, w), BF16),
        compiler_params=_cparams(("arbitrary", "arbitrary")),
        name="gla_post",
    )(o_f, o_b, p_gla, norm_g, perm_t)
    return out.reshape(b, t, w)


def _merge_kernel(ya_ref, yb_ref, wr_ref, wg_ref, ga_ref, gb_ref, o_ref):
    ma = _mm(ya_ref[0], wr_ref[...])
    mb = _mm(yb_ref[0], wg_ref[...])
    o_ref[0] = (ga_ref[0].astype(F32) * ma + gb_ref[0].astype(F32) * mb).astype(o_ref.dtype)


def _merge(ya, yb, w_r, w_g, p_gate, tm=1024, tn=1024):
    b, t, w = ya.shape
    d = w_r.shape[1]
    nj = d // tn
    return pl.pallas_call(
        _merge_kernel,
        grid=(b, t // tm, nj),
        in_specs=[pl.BlockSpec((1, tm, w), lambda bi, i, j: (bi, i, 0)),
                  pl.BlockSpec((1, tm, w), lambda bi, i, j: (bi, i, 0)),
                  pl.BlockSpec((w, tn), lambda bi, i, j: (0, j)),
                  pl.BlockSpec((w, tn), lambda bi, i, j: (0, j)),
                  pl.BlockSpec((1, tm, tn), lambda bi, i, j: (bi, i, j)),
                  pl.BlockSpec((1, tm, tn), lambda bi, i, j: (bi, i, j + nj))],
        out_specs=pl.BlockSpec((1, tm, tn), lambda bi, i, j: (bi, i, j)),
        out_shape=jax.ShapeDtypeStruct((b, t, d), BF16),
        compiler_params=_cparams(("arbitrary", "arbitrary", "arbitrary")),
        name="merge_branches",
    )(ya, yb, w_r, w_g, p_gate, p_gate)


def _mix_out_kernel(m_ref, w_ref, x_ref, gate_ref, npost_ref, npre_ref, sh_ref, sc_ref, x1_ref, h_ref):
    half = m_ref.shape[1] // 2
    for rs in (slice(0, half), slice(half, 2 * half)):
        z = _mm(m_ref[0, rs, :], w_ref[...])
        z = z * lax.rsqrt(jnp.mean(z * z, axis=-1, keepdims=True) + NORM_EPS) * npost_ref[...]
        x1 = x_ref[0, rs, :] + gate_ref[0] * z
        x1_ref[0, rs, :] = x1
        y = x1 * lax.rsqrt(jnp.mean(x1 * x1, axis=-1, keepdims=True) + NORM_EPS) * npre_ref[...]
        h_ref[0, rs, :] = (y * (1.0 + sc_ref[0]) + sh_ref[0]).astype(h_ref.dtype)


def _mix_out(m, w_out, x, gate, n_post, n_pre, shift, scale, tm=256):
    b, t, d = x.shape
    row = lambda bi, i: (bi, i, 0)
    per_b = lambda bi, i: (bi, 0, 0)
    vec = lambda bi, i: (0, 0)
    return pl.pallas_call(
        _mix_out_kernel,
        grid=(b, t // tm),
        in_specs=[pl.BlockSpec((1, tm, d), row),
                  pl.BlockSpec((d, d), vec),
                  pl.BlockSpec((1, tm, d), row),
                  pl.BlockSpec((1, 1, d), per_b),
                  pl.BlockSpec((1, d), vec), pl.BlockSpec((1, d), vec),
                  pl.BlockSpec((1, 1, d), per_b), pl.BlockSpec((1, 1, d), per_b)],
        out_specs=[pl.BlockSpec((1, tm, d), row), pl.BlockSpec((1, tm, d), row)],
        out_shape=[jax.ShapeDtypeStruct((b, t, d), F32), jax.ShapeDtypeStruct((b, t, d), BF16)],
        compiler_params=_cparams(("arbitrary", "arbitrary")),
        name="mix_out",
    )(m, w_out, x, gate, n_post, n_pre, shift, scale)


def _ffn_up_kernel(h_ref, wg_ref, wu_ref, o_ref):
    h = h_ref[0]
    a = _mm(h, wg_ref[...])
    u = _mm(h, wu_ref[...])
    o_ref[0] = (a * _sigmoid(a) * u).astype(o_ref.dtype)


def _ffn_up(h, w_gate, w_up, tm=1024, tn=512):
    b, t, d = h.shape
    f = w_gate.shape[1]
    return pl.pallas_call(
        _ffn_up_kernel,
        grid=(b, t // tm, f // tn),
        in_specs=[pl.BlockSpec((1, tm, d), lambda bi, i, j: (bi, i, 0)),
                  pl.BlockSpec((d, tn), lambda bi, i, j: (0, j)),
                  pl.BlockSpec((d, tn), lambda bi, i, j: (0, j))],
        out_specs=pl.BlockSpec((1, tm, tn), lambda bi, i, j: (bi, i, j)),
        out_shape=jax.ShapeDtypeStruct((b, t, f), BF16),
        compiler_params=_cparams(("arbitrary", "arbitrary", "arbitrary")),
        name="ffn_up",
    )(h, w_gate, w_up)


def _ffn_down_kernel(h_ref, w_ref, x_ref, gate_ref, npost_ref, o_ref, z_scr):
    j = pl.program_id(2)
    n_j = z_scr.shape[0]
    z_scr[j] = _mm(h_ref[0], w_ref[...])

    @pl.when(j == n_j - 1)
    def _():
        z = jnp.concatenate([z_scr[t] for t in range(n_j)], 1)
        z = z * lax.rsqrt(jnp.mean(z * z, axis=-1, keepdims=True) + NORM_EPS) * npost_ref[...]
        o_ref[0] = x_ref[0] + gate_ref[0] * z


def _ffn_down(h, w_down, x1, gate, n_post, tm=512, tn=512):
    b, t, f = h.shape
    d = w_down.shape[1]
    return pl.pallas_call(
        _ffn_down_kernel,
        grid=(b, t // tm, d // tn),
        in_specs=[pl.BlockSpec((1, tm, f), lambda bi, i, j: (bi, i, 0)),
                  pl.BlockSpec((f, tn), lambda bi, i, j: (0, j)),
                  pl.BlockSpec((1, tm, d), lambda bi, i, j: (bi, i, 0)),
                  pl.BlockSpec((1, 1, d), lambda bi, i, j: (bi, 0, 0)),
                  pl.BlockSpec((1, d), lambda bi, i, j: (0, 0))],
        out_specs=pl.BlockSpec((1, tm, d), lambda bi, i, j: (bi, i, 0)),
        out_shape=jax.ShapeDtypeStruct((b, t, d), F32),
        scratch_shapes=[pltpu.VMEM((d // tn, tm, tn), F32)],
        compiler_params=_cparams(("arbitrary", "arbitrary", "arbitrary")),
        name="ffn_down",
    )(h, w_down, x1, gate, n_post)


def _pad_cols(w, n):
    return jnp.pad(w, ((0, 0), (0, n - w.shape[1])))


def kernel(x, c, ctx, c_ctx, ada_w, ada_b, norm_pre_mix, norm_post_mix, norm_pre_ffn, norm_post_ffn, w_in, shift_mu, rwkv_w0, rwkv_w2, rwkv_a0, rwkv_a2, rwkv_g2, rwkv_k_k, rwkv_k_a, rwkv_r_k, rwkv_ln_g, rwkv_ln_b, w_rwkv_up, gla_alpha_up, gla_alpha_b, gla_norm_g, w_gla_up, w_out, ffn_w_gate, ffn_w_up, ffn_w_down):
    assert ada_w.shape[0] == 1, "single trunk layer"
    bsz, seq, d = x.shape
    n_ctx = ctx.shape[1]
    prec = BF16

    cvecs = jnp.concatenate([c, c_ctx[None, :], jnp.zeros((8 - bsz - 1, d), F32)], 0)
    mod = _modulation(cvecs, ada_w[0], ada_b[0])
    mod_x = mod[:bsz].reshape(bsz, 6, 1, d)
    shx1, scx1, gx1, shx2, scx2, gx2 = (mod_x[:, i] for i in range(6))
    mod_c = jnp.broadcast_to(mod[bsz].reshape(1, 6, 1, d), (bsz, 6, 1, d))
    shc1, scc1 = mod_c[:, 0], mod_c[:, 1]

    w_all = w_in[0]
    mix_in = RWKV_IN + GLA_IN
    w_rwkv = _pad_cols(w_all[:, :RWKV_IN], RWKV_PAD).astype(BF16)
    w_gla = _pad_cols(w_all[:, RWKV_IN:mix_in], GLA_PAD).astype(BF16)
    w_gate = w_all[:, mix_in:].astype(BF16)
    mu = _pad_cols(shift_mu, RWKV_PAD)
    n_pre = norm_pre_mix

    px_rwkv = _project_shift(x, shx1, scx1, n_pre, w_rwkv, mu, 1024, 512, "proj_rwkv")
    px_gla = _project_colmajor(x, shx1, scx1, n_pre, w_gla, 16, 512, "proj_gla")
    px_gate = _project(x, shx1, scx1, n_pre, w_gate, 1024, 1024, "proj_gate", gate_out=True)
    pc_rwkv = _project_shift(ctx, shc1, scc1, n_pre, w_rwkv, mu, n_ctx, 512, "proj_rwkv_ctx")
    pc_gla = _project(ctx, shc1, scc1, n_pre, w_gla, n_ctx, 512, "proj_gla_ctx")

    y_f, y_b = _rwkv_scan(px_rwkv, pc_rwkv, rwkv_w2[0], rwkv_w0[0][:, None, :], rwkv_a2[0], rwkv_a0[0][:, None, :],
                          rwkv_k_k, rwkv_k_a, prec)
    ya = _rwkv_post(y_f, y_b, px_rwkv, rwkv_a2[0], rwkv_a0[0][:, None, :], rwkv_g2[0], rwkv_k_a,
                    rwkv_r_k.reshape(1, RWKV_WIDTH), rwkv_ln_g, rwkv_ln_b)

    o_f, o_b = _gla_scan(px_gla, pc_gla, gla_alpha_up[0], gla_alpha_b[0][:, None, :], prec)
    yb = _gla_post(o_f, o_b, px_gla, gla_norm_g)

    m = _merge(ya, yb, w_rwkv_up[0].astype(BF16), w_gla_up[0].astype(BF16), px_gate)
    x1, h2 = _mix_out(m, w_out[0].astype(BF16), x, gx1, norm_post_mix, norm_pre_ffn, shx2, scx2)
    hf = _ffn_up(h2, ffn_w_gate[0].astype(BF16), ffn_w_up[0].astype(BF16))
    return _ffn_down(hf, ffn_w_down[0].astype(BF16), x1, gx2, norm_post_ffn)
```

```python
import functools

import jax
import jax.numpy as jnp
from jax import lax
from jax.experimental import pallas as pl
from jax.experimental.pallas import tpu as pltpu

F32 = jnp.float32
BF16 = jnp.bfloat16
HIGHEST = lax.Precision.HIGHEST

LANES = 128
VMEM_LIMIT_BYTES = 56 * 1024 * 1024

GRID_W = 64
CHUNK = 64
RWKV_HEADS, RWKV_HEAD_DIM = 16, 64
RWKV_WIDTH = RWKV_HEADS * RWKV_HEAD_DIM
DECAY_LORA = ICL_LORA = 96
GATE_LORA = 64
RWKV_GN_EPS = 64e-5
GLA_HEADS, GLA_KEY_DIM, GLA_VAL_DIM = 4, 128, 256
GLA_QK_WIDTH = GLA_HEADS * GLA_KEY_DIM
GLA_V_WIDTH = GLA_HEADS * GLA_VAL_DIM
GLA_GATE_LORA = 16
GLA_TAU = 16.0
GLA_NORM_EPS = 1e-5
GLA_SUB = 16
NORM_EPS = 1e-6

RWKV_IN = 3 * RWKV_WIDTH + 2 * DECAY_LORA + 2 * ICL_LORA + GATE_LORA
RWKV_PAD = 3584
OFF_WD = 3 * RWKV_WIDTH
OFF_AD = OFF_WD + 2 * DECAY_LORA
OFF_GD = OFF_AD + 2 * ICL_LORA
GLA_IN = 2 * GLA_QK_WIDTH + 2 * GLA_V_WIDTH + 2 * GLA_GATE_LORA
GLA_BLK = 512
GLA_PAD = 7 * GLA_BLK
OFF_GLA_AD = 6 * GLA_BLK

PAIR = 2 * RWKV_HEAD_DIM
N_PAIRS = RWKV_HEADS // 2


def _cparams(semantics):
    return pltpu.CompilerParams(dimension_semantics=semantics, vmem_limit_bytes=VMEM_LIMIT_BYTES)


def _sigmoid(z):
    return 1.0 / (1.0 + jnp.exp(-z))


def _softplus(z):
    return jnp.maximum(z, 0.0) + jnp.log(1.0 + jnp.exp(-jnp.abs(z)))


def _dot(a, b, dims, precision):
    if precision is BF16:
        a, b, precision = a.astype(BF16), b.astype(BF16), None
    return lax.dot_general(a, b, (dims, ((), ())), precision=precision, preferred_element_type=F32)


def _mm(a, b, precision=None):
    return _dot(a, b, ((1,), (0,)), precision)


def _mm_nt(a, b, precision=None):
    return _dot(a, b, ((1,), (1,)), precision)


def _mm_tn(a, b, precision=None):
    return _dot(a, b, ((0,), (0,)), precision)


def _split2(z):
    hi = z.astype(BF16)
    return hi, (z - hi.astype(F32)).astype(BF16)


def _mm_split(a, b):
    ah, al = _split2(a)
    bh, bl = _split2(b)
    return (_mm(al, bh) + _mm(ah, bl)) + _mm(ah, bh)


def _mm_exact_lhs(e, b):
    hi = b.astype(BF16)
    r = b - hi.astype(F32)
    mid = r.astype(BF16)
    lo = (r - mid.astype(F32)).astype(BF16)
    e = e.astype(BF16)
    return (_mm(e, lo) + _mm(e, mid)) + _mm(e, hi)


def _mod_kernel(c_ref, w_ref, b_ref, o_ref):
    s = c_ref[...]
    s = s * _sigmoid(s)
    o_ref[...] = _mm_split(s, w_ref[...]) + b_ref[...]


def _modulation(cvecs, ada_w, ada_b, tn=1024):
    m, d = cvecs.shape
    n = ada_w.shape[1]
    return pl.pallas_call(
        _mod_kernel,
        grid=(n // tn,),
        in_specs=[pl.BlockSpec((m, d), lambda j: (0, 0)),
                  pl.BlockSpec((d, tn), lambda j: (0, j)),
                  pl.BlockSpec((1, tn), lambda j: (0, j))],
        out_specs=pl.BlockSpec((m, tn), lambda j: (0, j)),
        out_shape=jax.ShapeDtypeStruct((m, n), F32),
        compiler_params=_cparams(("arbitrary",)),
        name="adaln_mod",
    )(cvecs, ada_w, ada_b.reshape(1, n))


def _prenorm_kernel(x_ref, sh_ref, sc_ref, g_ref, o_ref):
    x = x_ref[0]
    ms = jnp.mean(x * x, axis=-1, keepdims=True)
    y = x * lax.rsqrt(ms + NORM_EPS) * g_ref[...]
    o_ref[0] = (y * (1.0 + sc_ref[0]) + sh_ref[0]).astype(o_ref.dtype)


def _prenorm(x, shift, scale, gain, tm=256):
    b, t, d = x.shape
    row = lambda bi, i: (bi, i, 0)
    per_b = lambda bi, i: (bi, 0, 0)
    return pl.pallas_call(
        _prenorm_kernel,
        grid=(b, t // tm),
        in_specs=[pl.BlockSpec((1, tm, d), row), pl.BlockSpec((1, 1, d), per_b), pl.BlockSpec((1, 1, d), per_b),
                  pl.BlockSpec((1, d), lambda bi, i: (0, 0))],
        out_specs=pl.BlockSpec((1, tm, d), row),
        out_shape=jax.ShapeDtypeStruct((b, t, d), BF16),
        compiler_params=_cparams(("arbitrary", "arbitrary")),
        name="prenorm",
    )(x, shift, scale, gain)


def _proj_kernel(gate_out, h_ref, w_ref, o_ref):
    p = _mm(h_ref[0], w_ref[...])
    o_ref[0] = (_sigmoid(p) if gate_out else p).astype(o_ref.dtype)


def _project(h, w, tm, tn, name, gate_out=False):
    b, t, d = h.shape
    n = w.shape[1]
    return pl.pallas_call(
        functools.partial(_proj_kernel, gate_out),
        grid=(b, t // tm, n // tn),
        in_specs=[pl.BlockSpec((1, tm, d), lambda bi, i, j: (bi, i, 0)),
                  pl.BlockSpec((d, tn), lambda bi, i, j: (0, j))],
        out_specs=pl.BlockSpec((1, tm, tn), lambda bi, i, j: (bi, i, j)),
        out_shape=jax.ShapeDtypeStruct((b, t, n), BF16 if gate_out else F32),
        compiler_params=_cparams(("arbitrary", "arbitrary", "arbitrary")),
        name=name,
    )(h, w)


def _colmajor_perm(n_rows, n_cols):
    dst = jnp.arange(n_rows * n_cols)
    src = (dst % n_rows) * n_cols + dst // n_rows
    return (src[:, None] == dst[None, :]).astype(BF16)


def _proj_cm_kernel(h_ref, perm_ref, w_ref, o_ref, h_scr):
    @pl.when(pl.program_id(2) == 0)
    def _():
        n_r, n_c, d = h_ref.shape[1:]
        for cb in range(d // GLA_BLK):
            cs = slice(cb * GLA_BLK, (cb + 1) * GLA_BLK)
            hr = h_ref[0, :, :, cs].astype(F32).reshape(n_r * n_c, GLA_BLK).astype(BF16)
            h_scr[:, cs] = _mm(perm_ref[...], hr).astype(BF16)

    o_ref[0] = _mm(h_scr[...], w_ref[...])


def _project_colmajor(h, w, n_cols, tn, name):
    b, t, d = h.shape
    n = w.shape[1]
    n_r = t // GRID_W
    tm = n_r * n_cols
    h4 = h.reshape(b, n_r, GRID_W, d)
    perm = _colmajor_perm(n_r, n_cols)
    return pl.pallas_call(
        _proj_cm_kernel,
        grid=(b, GRID_W // n_cols, n // tn),
        in_specs=[pl.BlockSpec((1, n_r, n_cols, d), lambda bi, i, j: (bi, 0, i, 0)),
                  pl.BlockSpec((tm, tm), lambda bi, i, j: (0, 0)),
                  pl.BlockSpec((d, tn), lambda bi, i, j: (0, j))],
        out_specs=pl.BlockSpec((1, tm, tn), lambda bi, i, j: (bi, i, j)),
        out_shape=jax.ShapeDtypeStruct((b, t, n), F32),
        scratch_shapes=[pltpu.VMEM((tm, d), BF16)],
        compiler_params=_cparams(("arbitrary", "arbitrary", "arbitrary")),
        name=name,
    )(h4, perm, w)


HALO = 16


def _proj_shift_kernel(h_ref, hp_ref, hn_ref, w_ref, mu_ref, o_ref):
    i = pl.program_id(1)
    tm = h_ref.shape[1]
    p = _mm(h_ref[0], w_ref[...])
    ph = _mm(jnp.concatenate([hp_ref[0], hn_ref[0]], 0), w_ref[...])
    before = jnp.where(i == 0, 0.0, ph[HALO - 1:HALO])
    after = jnp.where(i == pl.num_programs(1) - 1, 0.0, ph[HALO:HALO + 1])
    row = lax.broadcasted_iota(jnp.int32, p.shape, 0)
    prev = jnp.where(row == 0, before, pltpu.roll(p, 1, 0))
    nxt = jnp.where(row == tm - 1, after, pltpu.roll(p, tm - 1, 0))
    o_ref[0] = p + mu_ref[...] * (0.5 * (prev + nxt) - p)


def _project_shift(h, w, mu, tm, tn, name):
    b, t, d = h.shape
    n = w.shape[1]
    rh, nh = tm // HALO, t // HALO
    return pl.pallas_call(
        _proj_shift_kernel,
        grid=(b, t // tm, n // tn),
        in_specs=[pl.BlockSpec((1, tm, d), lambda bi, i, j: (bi, i, 0)),
                  pl.BlockSpec((1, HALO, d), lambda bi, i, j: (bi, jnp.maximum(i * rh - 1, 0), 0)),
                  pl.BlockSpec((1, HALO, d), lambda bi, i, j: (bi, jnp.minimum((i + 1) * rh, nh - 1), 0)),
                  pl.BlockSpec((d, tn), lambda bi, i, j: (0, j)),
                  pl.BlockSpec((1, tn), lambda bi, i, j: (0, j))],
        out_specs=pl.BlockSpec((1, tm, tn), lambda bi, i, j: (bi, i, j)),
        out_shape=jax.ShapeDtypeStruct((b, t, n), F32),
        compiler_params=_cparams(("arbitrary", "arbitrary", "arbitrary")),
        name=name,
    )(h, h, h, w, mu)


def _head_sum_matrix():
    r = lax.broadcasted_iota(jnp.int32, (PAIR, PAIR), 0)
    c = lax.broadcasted_iota(jnp.int32, (PAIR, PAIR), 1)
    return jnp.where((r >> 6) == (c >> 6), 1.0, 0.0).astype(BF16)


def _head_sum(z, hsum):
    hi = z.astype(BF16)
    lo = (z - hi.astype(F32)).astype(BF16)
    return _mm(hi, hsum) + _mm(lo, hsum)


def _rwkv_kernel(prec, n_ctx_chunks, plf_ref, pcf_ref, plb_ref, pcb_ref, w2_ref, w0_ref, a2_ref, a0_ref, kk_ref, ka_ref,
                 yf_ref, yb_ref, st_scr):
    @pl.when(pl.program_id(1) == 0)
    def _():
        st_scr[...] = jnp.zeros_like(st_scr)

    is_ctx = pl.program_id(1) < n_ctx_chunks
    for dr, (pl_ref, pc_ref, y_ref) in enumerate(((plf_ref, pcf_ref, yf_ref), (plb_ref, pcb_ref, yb_ref))):
        _rwkv_chunk(dr == 1, prec, is_ctx, pl_ref, pc_ref, w2_ref.at[dr], w0_ref.at[dr], a2_ref.at[dr], a0_ref.at[dr],
                    kk_ref, ka_ref, y_ref, st_scr.at[dr])


def _rwkv_chunk(rev, prec, is_ctx, pl_ref, pc_ref, w2_ref, w0_ref, a2_ref, a0_ref, kk_ref, ka_ref, y_ref, st_scr):
    hd = RWKV_HEAD_DIM
    c2 = 2 * CHUNK
    p = jnp.where(is_ctx, pc_ref[0], pl_ref[0])
    r = p[:, 0:RWKV_WIDTH]
    k = p[:, RWKV_WIDTH:2 * RWKV_WIDTH]
    v = p[:, 2 * RWKV_WIDTH:3 * RWKV_WIDTH]
    d_off = DECAY_LORA if rev else 0
    wd = p[:, OFF_WD + d_off:OFF_WD + d_off + DECAY_LORA]
    ad = p[:, OFF_AD + d_off:OFF_AD + d_off + ICL_LORA]

    w_log = -_softplus(-(w0_ref[...] + _mm_split(jnp.tanh(wd), w2_ref[...]))) - 0.5
    lw = -jnp.exp(w_log)
    a = _sigmoid(a0_ref[...] + _mm(ad, a2_ref[...], BF16))
    kk_raw = k * kk_ref[...]
    kd = k * (1.0 + (a - 1.0) * ka_ref[...])

    ri = lax.broadcasted_iota(jnp.int32, (c2, PAIR), 0)
    li = lax.broadcasted_iota(jnp.int32, (c2, PAIR), 1)
    rt, lt = ri & (CHUNK - 1), li & (hd - 1)
    same_head = (ri >> 6) == (li >> 6)
    strict = (lt > rt) if rev else (lt < rt)
    mask_n = jnp.logical_and(same_head, strict)
    mask_k = jnp.logical_and(jnp.logical_not(same_head), strict)
    ident = ri == li
    eye = jnp.where(ident, 1.0, 0.0).astype(F32)
    rc = lax.broadcasted_iota(jnp.int32, (CHUNK, PAIR), 0)
    lc = lax.broadcasted_iota(jnp.int32, (CHUNK, PAIR), 1)
    incl_c = ((lc & (hd - 1)) >= rc) if rev else ((lc & (hd - 1)) <= rc)
    lane_e = lc < hd
    hsum = _head_sum_matrix()

    ci = lax.broadcasted_iota(jnp.int32, (CHUNK, CHUNK), 0)
    cj = lax.broadcasted_iota(jnp.int32, (CHUNK, CHUNK), 1)
    tri = jnp.where((cj >= ci) if rev else (cj <= ci), 1.0, 0.0).astype(F32)
    cum = _mm_exact_lhs(tri, lw)
    total = cum[0:1] if rev else cum[CHUNK - 1:CHUNK]
    e_prev = jnp.exp(cum - lw)
    e_neg = jnp.exp(-cum)
    e_pos = jnp.exp(cum)
    e_rest = jnp.exp(total - cum)
    p_end = jnp.exp(total)

    def split(z):
        ze = jnp.where(lane_e, z, 0.0)
        return ze, z - ze

    pairs = range(N_PAIRS)
    sls = [slice(pr * PAIR, (pr + 1) * PAIR) for pr in pairs]
    cat0 = lambda *z: jnp.concatenate(z, 0)

    kkr = [kk_raw[:, sl] for sl in sls]
    nrm2 = [_head_sum(z * z, hsum) for z in kkr]
    kk = [z / jnp.maximum(jnp.sqrt(n), 1e-12) for z, n in zip(kkr, nrm2)]
    bb = [z * a[:, sl] for z, sl in zip(kk, sls)]
    at = [split(z * e_prev[:, sl]) for z, sl in zip(kk, sls)]
    bt = [z * e_neg[:, sl] for z, sl in zip(bb, sls)]
    kt = [kd[:, sl] * e_neg[:, sl] for sl in sls]
    rt_ = [r[:, sl] * e_pos[:, sl] for sl in sls]
    rs = [split(z) for z in rt_]
    vs = [split(v[:, sl]) for sl in sls]
    bh = [cat0(*split(z * e_rest[:, sl])) for z, sl in zip(bb, sls)]
    kh = [cat0(*split(kd[:, sl] * e_rest[:, sl])) for sl in sls]
    v_swap = [cat0(vo, ve) for ve, vo in vs]
    v_stack = [cat0(ve, vo) for ve, vo in vs]

    g_e = [_mm_nt(cat0(at[i][0], rs[i][0]), cat0(bt[i], kt[i]), prec) for i in pairs]
    g_o = [_mm_nt(cat0(at[i][1], rs[i][1]), cat0(kt[i], bt[i]), prec) for i in pairs]
    g_top = [cat0(g_e[i][0:CHUNK], g_o[i][0:CHUNK]) for i in pairs]
    nbd = [jnp.where(mask_n, z, 0.0) for z in g_top]
    aak = [jnp.where(mask_k, z, 0.0) for z in g_top]
    rab = [jnp.where(incl_c, jnp.where(lane_e, g_e[i][CHUNK:c2], g_o[i][CHUNK:c2]), 0.0) for i in pairs]
    rak = [jnp.where(incl_c, jnp.where(lane_e, g_o[i][CHUNK:c2], g_e[i][CHUNK:c2]), 0.0) for i in pairs]

    x = [_mm(aak[i], v_swap[i], prec) for i in pairs]
    n2 = [_mm(z, z, prec) for z in nbd]
    y0b = [_mm(rak[i], v_swap[i], prec) for i in pairs]
    n4 = [_mm(z, z, prec) for z in n2]
    imn = [eye - z for z in nbd]
    p1 = [imn[i] + _mm(imn[i], n2[i], prec) for i in pairs]
    n8 = [_mm(z, z, prec) for z in n4]
    nb = [_mm_tn(kh[i], v_stack[i], prec) for i in pairs]
    n16 = [_mm(z, z, prec) for z in n8]
    p2 = [eye + n4[i] + n8[i] + _mm(n4[i], n8[i], prec) for i in pairs]
    n32 = [_mm(z, z, prec) for z in n16]
    p12 = [_mm(p1[i], p2[i], prec) for i in pairs]
    p3 = [eye + n16[i] + n32[i] + _mm(n16[i], n32[i], prec) for i in pairs]
    tinv = [_mm(p12[i], p3[i], prec) for i in pairs]
    wu = [-_mm(tinv[i], jnp.concatenate([cat0(*at[i]), x[i]], 1), prec) for i in pairs]
    qy = [_mm(rab[i], wu[i], prec) for i in pairs]
    mn = [_mm_tn(bh[i], wu[i], prec) for i in pairs]
    q = [rt_[i] + qy[i][:, 0:PAIR] for i in pairs]
    m = [jnp.where(ident, p_end[:, sls[i]], 0.0) + mn[i][:, 0:PAIR] for i in pairs]
    qm = [_mm(cat0(q[i], m[i]), st_scr[i], prec) for i in pairs]
    for i in pairs:
        y_ref[0, :, sls[i]] = qm[i][0:CHUNK] + qy[i][:, PAIR:2 * PAIR] + y0b[i]
        st_scr[i] = qm[i][CHUNK:CHUNK + PAIR] + mn[i][:, PAIR:2 * PAIR] + nb[i]


def _scan_order(n_ctx_chunks, n_lat):
    n_steps = n_ctx_chunks + n_lat
    lat_f = lambda i: jnp.maximum(i - n_ctx_chunks, 0)
    lat_b = lambda i: jnp.where(i < n_ctx_chunks, n_lat - 1, n_steps - 1 - i)
    ctx_f = lambda i: jnp.minimum(i, n_ctx_chunks - 1)
    ctx_b = lambda i: jnp.maximum(n_ctx_chunks - 1 - i, 0)
    return n_steps, (lat_f, lat_b), (ctx_f, ctx_b)


def _rwkv_scan(p_lat, p_ctx, w2, w0, a2, a0, k_k, k_a, prec):
    b, t, w = p_lat.shape
    n_lat = t // CHUNK
    n_ctx_chunks = p_ctx.shape[1] // CHUNK
    n_steps, lat_of, ctx_of = _scan_order(n_ctx_chunks, n_lat)
    chunk = lambda f: pl.BlockSpec((1, CHUNK, w), lambda bi, i: (bi, f(i), 0))
    out = lambda f: pl.BlockSpec((1, CHUNK, RWKV_WIDTH), lambda bi, i: (bi, f(i), 0))
    vec = lambda bi, i: (0, 0)
    vec3 = lambda bi, i: (0, 0, 0)
    y_shape = jax.ShapeDtypeStruct((b, t, RWKV_WIDTH), F32)
    return pl.pallas_call(
        functools.partial(_rwkv_kernel, prec, n_ctx_chunks),
        grid=(b, n_steps),
        in_specs=[chunk(lat_of[0]), chunk(ctx_of[0]), chunk(lat_of[1]), chunk(ctx_of[1]),
                  pl.BlockSpec((2, DECAY_LORA, RWKV_WIDTH), vec3),
                  pl.BlockSpec((2, 1, RWKV_WIDTH), vec3),
                  pl.BlockSpec((2, ICL_LORA, RWKV_WIDTH), vec3),
                  pl.BlockSpec((2, 1, RWKV_WIDTH), vec3),
                  pl.BlockSpec((1, RWKV_WIDTH), vec),
                  pl.BlockSpec((1, RWKV_WIDTH), vec)],
        out_specs=[out(lat_of[0]), out(lat_of[1])],
        out_shape=[y_shape, y_shape],
        scratch_shapes=[pltpu.VMEM((2, N_PAIRS, PAIR, PAIR), F32)],
        compiler_params=_cparams(("arbitrary", "arbitrary")),
        name="rwkv7_scan",
    )(p_lat, p_ctx, p_lat, p_ctx, w2, w0, a2, a0, k_k, k_a)


def _gla_kernel(prec, n_ctx_chunks, *refs):
    lat_f, lat_b = refs[0:6], refs[6:12]
    aup_ref, ab_ref, of_ref, ob_ref, st_scr = refs[12:17]

    @pl.when(pl.program_id(1) == 0)
    def _():
        st_scr[...] = jnp.zeros_like(st_scr)

    is_ctx = pl.program_id(1) < n_ctx_chunks
    for dr, (blocks, o_ref) in enumerate(((lat_f, of_ref), (lat_b, ob_ref))):
        _gla_chunk(dr == 1, prec, is_ctx, *blocks, aup_ref.at[dr], ab_ref.at[dr], o_ref, st_scr.at[dr])


def _gla_chunk(rev, prec, is_ctx, q_ref, k_ref, v0_ref, v1_ref, ad_ref, ctx_ref, aup_ref, ab_ref, o_ref, st_scr):
    pc = ctx_ref[0]
    q = jnp.where(is_ctx, pc[:, 0:GLA_BLK], q_ref[0])
    k = jnp.where(is_ctx, pc[:, GLA_BLK:2 * GLA_BLK], k_ref[0])
    v = jnp.concatenate([jnp.where(is_ctx, pc[:, 2 * GLA_BLK:3 * GLA_BLK], v0_ref[0]),
                         jnp.where(is_ctx, pc[:, 3 * GLA_BLK:4 * GLA_BLK], v1_ref[0])], 1)
    d_off = GLA_GATE_LORA if rev else 0
    ad = jnp.where(is_ctx, pc[:, OFF_GLA_AD:OFF_GLA_AD + LANES], ad_ref[0][:, 0:LANES])
    ad = ad[:, d_off:d_off + GLA_GATE_LORA]

    la = -_softplus(-(_mm_split(ad, aup_ref[...]) + ab_ref[...])) * (1.0 / GLA_TAU)
    ci = lax.broadcasted_iota(jnp.int32, (CHUNK, CHUNK), 0)
    cj = lax.broadcasted_iota(jnp.int32, (CHUNK, CHUNK), 1)
    tri = jnp.where((cj >= ci) if rev else (cj <= ci), 1.0, 0.0).astype(F32)
    cum = _mm_exact_lhs(tri, la)
    total = cum[0:1] if rev else cum[CHUNK - 1:CHUNK]

    dk, dv, sb = GLA_KEY_DIM, GLA_VAL_DIM, GLA_SUB
    n_sb = CHUNK // sb
    row = lax.broadcasted_iota(jnp.int32, (CHUNK, dk), 0)
    row_in = row & (sb - 1)
    arow = lax.broadcasted_iota(jnp.int32, (CHUNK, CHUNK), 0)
    acol = lax.broadcasted_iota(jnp.int32, (CHUNK, CHUNK), 1)
    scale = GLA_KEY_DIM ** -0.5

    heads = range(GLA_HEADS)
    ksl = [slice(h * dk, (h + 1) * dk) for h in heads]
    qh = [q[:, s_] * scale for s_ in ksl]
    kh = [k[:, s_] for s_ in ksl]
    bh = [cum[:, s_] for s_ in ksl]
    lah = [la[:, s_] for s_ in ksl]
    toth = [total[:, s_] for s_ in ksl]
    vh = [v[:, h * dv:(h + 1) * dv] for h in heads]
    st = [st_scr[h] for h in heads]

    o_inter = [_mm_nt(qh[h] * jnp.exp(bh[h]), st[h], prec) for h in heads]
    st_new = [st[h] * jnp.exp(toth[h]) + _mm_tn(vh[h], kh[h] * jnp.exp(toth[h] - bh[h]), prec) for h in heads]

    off_rows = [[] for _ in heads]
    for blk in range(n_sb):
        rs = slice(blk * sb, (blk + 1) * sb)
        first = blk * sb + (sb - 1 if rev else 0)
        is_first_blk = (blk == n_sb - 1) if rev else (blk == 0)
        before = (row >= (blk + 1) * sb) if rev else (row < blk * sb)
        for h in heads:
            if is_first_blk:
                off_rows[h].append(jnp.zeros((sb, CHUNK), F32))
                continue
            beta = bh[h][first:first + 1] - lah[h][first:first + 1]
            qs = qh[h][rs] * jnp.exp(bh[h][rs] - beta)
            ksc = jnp.where(before, kh[h] * jnp.exp(jnp.minimum(beta - bh[h], 0.0)), 0.0)
            off_rows[h].append(_mm_nt(qs, ksc, prec))
    att = [jnp.concatenate(off_rows[h], 0) for h in heads]

    for s in range(sb):
        pick = lambda z: jnp.concatenate(
            [jnp.broadcast_to(z[blk * sb + s:blk * sb + s + 1], (sb, dk)) for blk in range(n_sb)], 0)
        ok = (row_in <= s) if rev else (row_in >= s)
        tgt = jnp.logical_and(acol == (arow & ~(sb - 1)) + s,
                              ((arow & (sb - 1)) <= s) if rev else ((arow & (sb - 1)) >= s))
        for h in heads:
            e = jnp.exp(jnp.where(ok, bh[h] - pick(bh[h]), 0.0))
            col = jnp.sum(jnp.where(ok, qh[h] * pick(kh[h]) * e, 0.0), axis=-1, keepdims=True)
            att[h] = jnp.where(tgt, col, att[h])

    for h in heads:
        o_ref[0, :, h * dv:(h + 1) * dv] = o_inter[h] + _mm(att[h], vh[h], prec)
        st_scr[h] = st_new[h]


def _gla_scan(p_lat, p_ctx, alpha_up, alpha_b, prec):
    b, t, w = p_lat.shape
    assert w == GLA_PAD and t == GRID_W * CHUNK
    n_ctx_chunks = p_ctx.shape[1] // CHUNK
    n_steps, col_of, ctx_of = _scan_order(n_ctx_chunks, GRID_W)

    def blocks(dr):
        lat = lambda m: pl.BlockSpec((1, CHUNK, GLA_BLK), lambda bi, i: (bi, col_of[dr](i), m))
        return [lat(0), lat(1), lat(2), lat(3), lat(6), pl.BlockSpec((1, CHUNK, w), lambda bi, i: (bi, ctx_of[dr](i), 0))]

    out = lambda dr: pl.BlockSpec((1, CHUNK, GLA_V_WIDTH), lambda bi, i: (bi, col_of[dr](i), 0))
    vec3 = lambda bi, i: (0, 0, 0)
    o_shape = jax.ShapeDtypeStruct((b, t, GLA_V_WIDTH), F32)
    return pl.pallas_call(
        functools.partial(_gla_kernel, prec, n_ctx_chunks),
        grid=(b, n_steps),
        in_specs=blocks(0) + blocks(1) + [pl.BlockSpec((2, GLA_GATE_LORA, GLA_QK_WIDTH), vec3),
                                          pl.BlockSpec((2, 1, GLA_QK_WIDTH), vec3)],
        out_specs=[out(0), out(1)],
        out_shape=[o_shape, o_shape],
        scratch_shapes=[pltpu.VMEM((2, GLA_HEADS, GLA_VAL_DIM, GLA_KEY_DIM), F32)],
        compiler_params=_cparams(("arbitrary", "arbitrary")),
        name="gla_scan",
    )(*([p_lat] * 5 + [p_ctx]) * 2, alpha_up, alpha_b)


def _rwkv_post_kernel(yf_ref, yb_ref, pm_ref, a2_ref, a0_ref, g2_ref, ka_ref, rk_ref, lng_ref, lnb_ref, o_ref):
    p = pm_ref[0]
    r = p[:, 0:RWKV_WIDTH]
    k = p[:, RWKV_WIDTH:2 * RWKV_WIDTH]
    v = p[:, 2 * RWKV_WIDTH:3 * RWKV_WIDTH]
    ad_f = p[:, OFF_AD:OFF_AD + ICL_LORA]
    ad_b = p[:, OFF_AD + ICL_LORA:OFF_AD + 2 * ICL_LORA]
    gd = p[:, OFF_GD:OFF_GD + GATE_LORA]
    ka = ka_ref[...]
    a_f = _sigmoid(a0_ref[0] + _mm(ad_f, a2_ref[0], BF16))
    a_b = _sigmoid(a0_ref[1] + _mm(ad_b, a2_ref[1], BF16))
    kd_sum = k * (1.0 + (a_f - 1.0) * ka) + k * (1.0 + (a_b - 1.0) * ka)
    gate = _mm(_sigmoid(gd), g2_ref[...], BF16)
    rkk = r * kd_sum * rk_ref[...]
    ysum = yf_ref[0] + yb_ref[0]
    hsum = _head_sum_matrix()
    inv_n = 1.0 / RWKV_HEAD_DIM
    for pr in range(N_PAIRS):
        sl = slice(pr * PAIR, (pr + 1) * PAIR)
        ys = ysum[:, sl]
        mean = _head_sum(ys, hsum) * inv_n
        dlt = ys - mean
        var = _head_sum(dlt * dlt, hsum) * inv_n
        gn = dlt * lax.rsqrt(var + RWKV_GN_EPS) * lng_ref[:, sl] + lnb_ref[:, sl]
        bonus = _head_sum(rkk[:, sl], hsum) * v[:, sl]
        o_ref[0, :, sl] = ((gn + bonus) * gate[:, sl]).astype(o_ref.dtype)


def _rwkv_post(y_f, y_b, pmix, a2, a0, g2, k_a, r_k, ln_g, ln_b, tm=256):
    b, t, w = y_f.shape
    row = lambda bi, i: (bi, i, 0)
    vec = lambda bi, i: (0, 0)
    vec3 = lambda bi, i: (0, 0, 0)
    return pl.pallas_call(
        _rwkv_post_kernel,
        grid=(b, t // tm),
        in_specs=[pl.BlockSpec((1, tm, w), row),
                  pl.BlockSpec((1, tm, w), row),
                  pl.BlockSpec((1, tm, pmix.shape[2]), row),
                  pl.BlockSpec((2, ICL_LORA, w), vec3),
                  pl.BlockSpec((2, 1, w), vec3),
                  pl.BlockSpec((GATE_LORA, w), vec),
                  pl.BlockSpec((1, w), vec), pl.BlockSpec((1, w), vec),
                  pl.BlockSpec((1, w), vec), pl.BlockSpec((1, w), vec)],
        out_specs=pl.BlockSpec((1, tm, w), row),
        out_shape=jax.ShapeDtypeStruct((b, t, w), BF16),
        compiler_params=_cparams(("arbitrary", "arbitrary")),
        name="rwkv_post",
    )(y_f, y_b, pmix, a2, a0, g2, k_a, r_k, ln_g, ln_b)


def _gla_post_kernel(of_ref, ob_ref, g_ref, ng_ref, perm_ref, o_ref):
    n_r, n_c = o_ref.shape[1:3]
    dv = GLA_VAL_DIM
    for h in range(GLA_HEADS):
        sl = slice(h * dv, (h + 1) * dv)
        oh = of_ref[0, :, sl] + ob_ref[0, :, sl]
        oh = oh * lax.rsqrt(jnp.mean(oh * oh, axis=-1, keepdims=True) + GLA_NORM_EPS) * ng_ref[:, sl]
        gh = g_ref[0, :, sl]
        y = (oh * (gh * _sigmoid(gh))).astype(BF16)
        o_ref[0, :, :, sl] = _mm(perm_ref[...], y).reshape(n_r, n_c, dv).astype(o_ref.dtype)


def _gla_post(o_f, o_b, p_gla, norm_g, n_cols=16):
    b, t, w = o_f.shape
    n_r = t // GRID_W
    tm = n_r * n_cols
    perm_t = _colmajor_perm(n_r, n_cols).T
    row = lambda bi, i: (bi, i, 0)
    out = pl.pallas_call(
        _gla_post_kernel,
        grid=(b, GRID_W // n_cols),
        in_specs=[pl.BlockSpec((1, tm, w), row),
                  pl.BlockSpec((1, tm, w), row),
                  pl.BlockSpec((1, tm, w), lambda bi, i: (bi, i, 2)),
                  pl.BlockSpec((1, w), lambda bi, i: (0, 0)),
                  pl.BlockSpec((tm, tm), lambda bi, i: (0, 0))],
        out_specs=pl.BlockSpec((1, n_r, n_cols, w), lambda bi, i: (bi, 0, i, 0)),
        out_shape=jax.ShapeDtypeStruct((b, n_r, GRID_W, w), BF16),
        compiler_params=_cparams(("arbitrary", "arbitrary")),
        name="gla_post",
    )(o_f, o_b, p_gla, norm_g, perm_t)
    return out.reshape(b, t, w)


def _merge_kernel(ya_ref, yb_ref, wr_ref, wg_ref, ga_ref, gb_ref, o_ref):
    ma = _mm(ya_ref[0], wr_ref[...])
    mb = _mm(yb_ref[0], wg_ref[...])
    o_ref[0] = (ga_ref[0].astype(F32) * ma + gb_ref[0].astype(F32) * mb).astype(o_ref.dtype)


def _merge(ya, yb, w_r, w_g, p_gate, tm=1024, tn=1024):
    b, t, w = ya.shape
    d = w_r.shape[1]
    nj = d // tn
    return pl.pallas_call(
        _merge_kernel,
        grid=(b, t // tm, nj),
        in_specs=[pl.BlockSpec((1, tm, w), lambda bi, i, j: (bi, i, 0)),
                  pl.BlockSpec((1, tm, w), lambda bi, i, j: (bi, i, 0)),
                  pl.BlockSpec((w, tn), lambda bi, i, j: (0, j)),
                  pl.BlockSpec((w, tn), lambda bi, i, j: (0, j)),
                  pl.BlockSpec((1, tm, tn), lambda bi, i, j: (bi, i, j)),
                  pl.BlockSpec((1, tm, tn), lambda bi, i, j: (bi, i, j + nj))],
        out_specs=pl.BlockSpec((1, tm, tn), lambda bi, i, j: (bi, i, j)),
        out_shape=jax.ShapeDtypeStruct((b, t, d), BF16),
        compiler_params=_cparams(("arbitrary", "arbitrary", "arbitrary")),
        name="merge_branches",
    )(ya, yb, w_r, w_g, p_gate, p_gate)


def _mix_out_kernel(m_ref, w_ref, x_ref, gate_ref, npost_ref, npre_ref, sh_ref, sc_ref, x1_ref, h_ref):
    half = m_ref.shape[1] // 2
    for rs in (slice(0, half), slice(half, 2 * half)):
        z = _mm(m_ref[0, rs, :], w_ref[...])
        z = z * lax.rsqrt(jnp.mean(z * z, axis=-1, keepdims=True) + NORM_EPS) * npost_ref[...]
        x1 = x_ref[0, rs, :] + gate_ref[0] * z
        x1_ref[0, rs, :] = x1
        y = x1 * lax.rsqrt(jnp.mean(x1 * x1, axis=-1, keepdims=True) + NORM_EPS) * npre_ref[...]
        h_ref[0, rs, :] = (y * (1.0 + sc_ref[0]) + sh_ref[0]).astype(h_ref.dtype)


def _mix_out(m, w_out, x, gate, n_post, n_pre, shift, scale, tm=512):
    b, t, d = x.shape
    row = lambda bi, i: (bi, i, 0)
    per_b = lambda bi, i: (bi, 0, 0)
    vec = lambda bi, i: (0, 0)
    return pl.pallas_call(
        _mix_out_kernel,
        grid=(b, t // tm),
        in_specs=[pl.BlockSpec((1, tm, d), row),
                  pl.BlockSpec((d, d), vec),
                  pl.BlockSpec((1, tm, d), row),
                  pl.BlockSpec((1, 1, d), per_b),
                  pl.BlockSpec((1, d), vec), pl.BlockSpec((1, d), vec),
                  pl.BlockSpec((1, 1, d), per_b), pl.BlockSpec((1, 1, d), per_b)],
        out_specs=[pl.BlockSpec((1, tm, d), row), pl.BlockSpec((1, tm, d), row)],
        out_shape=[jax.ShapeDtypeStruct((b, t, d), F32), jax.ShapeDtypeStruct((b, t, d), BF16)],
        compiler_params=_cparams(("arbitrary", "arbitrary")),
        name="mix_out",
    )(m, w_out, x, gate, n_post, n_pre, shift, scale)


def _ffn_up_kernel(h_ref, wg_ref, wu_ref, o_ref):
    h = h_ref[0]
    a = _mm(h, wg_ref[...])
    u = _mm(h, wu_ref[...])
    o_ref[0] = (a * _sigmoid(a) * u).astype(o_ref.dtype)


def _ffn_up(h, w_gate, w_up, tm=1024, tn=512):
    b, t, d = h.shape
    f = w_gate.shape[1]
    return pl.pallas_call(
        _ffn_up_kernel,
        grid=(b, t // tm, f // tn),
        in_specs=[pl.BlockSpec((1, tm, d), lambda bi, i, j: (bi, i, 0)),
                  pl.BlockSpec((d, tn), lambda bi, i, j: (0, j)),
                  pl.BlockSpec((d, tn), lambda bi, i, j: (0, j))],
        out_specs=pl.BlockSpec((1, tm, tn), lambda bi, i, j: (bi, i, j)),
        out_shape=jax.ShapeDtypeStruct((b, t, f), BF16),
        compiler_params=_cparams(("arbitrary", "arbitrary", "arbitrary")),
        name="ffn_up",
    )(h, w_gate, w_up)


def _ffn_down_kernel(h_ref, w_ref, x_ref, gate_ref, npost_ref, o_ref, z_scr):
    j = pl.program_id(2)
    n_j = z_scr.shape[0]
    z_scr[j] = _mm(h_ref[0], w_ref[...])

    @pl.when(j == n_j - 1)
    def _():
        z = jnp.concatenate([z_scr[t] for t in range(n_j)], 1)
        z = z * lax.rsqrt(jnp.mean(z * z, axis=-1, keepdims=True) + NORM_EPS) * npost_ref[...]
        o_ref[0] = x_ref[0] + gate_ref[0] * z


def _ffn_down(h, w_down, x1, gate, n_post, tm=512, tn=512):
    b, t, f = h.shape
    d = w_down.shape[1]
    return pl.pallas_call(
        _ffn_down_kernel,
        grid=(b, t // tm, d // tn),
        in_specs=[pl.BlockSpec((1, tm, f), lambda bi, i, j: (bi, i, 0)),
                  pl.BlockSpec((f, tn), lambda bi, i, j: (0, j)),
                  pl.BlockSpec((1, tm, d), lambda bi, i, j: (bi, i, 0)),
                  pl.BlockSpec((1, 1, d), lambda bi, i, j: (bi, 0, 0)),
                  pl.BlockSpec((1, d), lambda bi, i, j: (0, 0))],
        out_specs=pl.BlockSpec((1, tm, d), lambda bi, i, j: (bi, i, 0)),
        out_shape=jax.ShapeDtypeStruct((b, t, d), F32),
        scratch_shapes=[pltpu.VMEM((d // tn, tm, tn), F32)],
        compiler_params=_cparams(("arbitrary", "arbitrary", "arbitrary")),
        name="ffn_down",
    )(h, w_down, x1, gate, n_post)


def _pad_cols(w, n):
    return jnp.pad(w, ((0, 0), (0, n - w.shape[1])))


def kernel(x, c, ctx, c_ctx, ada_w, ada_b, norm_pre_mix, norm_post_mix, norm_pre_ffn, norm_post_ffn, w_in, shift_mu, rwkv_w0, rwkv_w2, rwkv_a0, rwkv_a2, rwkv_g2, rwkv_k_k, rwkv_k_a, rwkv_r_k, rwkv_ln_g, rwkv_ln_b, w_rwkv_up, gla_alpha_up, gla_alpha_b, gla_norm_g, w_gla_up, w_out, ffn_w_gate, ffn_w_up, ffn_w_down):
    assert ada_w.shape[0] == 1, "single trunk layer"
    bsz, seq, d = x.shape
    n_ctx = ctx.shape[1]
    prec = BF16

    cvecs = jnp.concatenate([c, c_ctx[None, :], jnp.zeros((8 - bsz - 1, d), F32)], 0)
    mod = _modulation(cvecs, ada_w[0], ada_b[0])
    mod_x = mod[:bsz].reshape(bsz, 6, 1, d)
    shx1, scx1, gx1, shx2, scx2, gx2 = (mod_x[:, i] for i in range(6))
    mod_c = jnp.broadcast_to(mod[bsz].reshape(1, 6, 1, d), (bsz, 6, 1, d))
    shc1, scc1 = mod_c[:, 0], mod_c[:, 1]

    w_all = w_in[0]
    mix_in = RWKV_IN + GLA_IN
    w_rwkv = _pad_cols(w_all[:, :RWKV_IN], RWKV_PAD).astype(BF16)
    w_gla = _pad_cols(w_all[:, RWKV_IN:mix_in], GLA_PAD).astype(BF16)
    w_gate = w_all[:, mix_in:].astype(BF16)
    mu = _pad_cols(shift_mu, RWKV_PAD)
    n_pre = norm_pre_mix

    hx = _prenorm(x, shx1, scx1, n_pre)
    hc = _prenorm(ctx, shc1, scc1, n_pre)
    px_rwkv = _project_shift(hx, w_rwkv, mu, 1024, 512, "proj_rwkv")
    px_gla = _project_colmajor(hx, w_gla, 16, 512, "proj_gla")
    px_gate = _project(hx, w_gate, 1024, 1024, "proj_gate", gate_out=True)
    pc_rwkv = _project_shift(hc, w_rwkv, mu, n_ctx, 512, "proj_rwkv_ctx")
    pc_gla = _project(hc, w_gla, n_ctx, 512, "proj_gla_ctx")

    y_f, y_b = _rwkv_scan(px_rwkv, pc_rwkv, rwkv_w2[0], rwkv_w0[0][:, None, :], rwkv_a2[0], rwkv_a0[0][:, None, :],
                          rwkv_k_k, rwkv_k_a, prec)
    ya = _rwkv_post(y_f, y_b, px_rwkv, rwkv_a2[0], rwkv_a0[0][:, None, :], rwkv_g2[0], rwkv_k_a,
                    rwkv_r_k.reshape(1, RWKV_WIDTH), rwkv_ln_g, rwkv_ln_b)

    o_f, o_b = _gla_scan(px_gla, pc_gla, gla_alpha_up[0], gla_alpha_b[0][:, None, :], prec)
    yb = _gla_post(o_f, o_b, px_gla, gla_norm_g)

    m = _merge(ya, yb, w_rwkv_up[0].astype(BF16), w_gla_up[0].astype(BF16), px_gate)
    x1, h2 = _mix_out(m, w_out[0].astype(BF16), x, gx1, norm_post_mix, norm_pre_ffn, shx2, scx2)
    hf = _ffn_up(h2, ffn_w_gate[0].astype(BF16), ffn_w_up[0].astype(BF16))
    return _ffn_down(hf, ffn_w_down[0].astype(BF16), x1, gx2, norm_post_ffn)
```

```python
import functools

import jax
import jax.numpy as jnp
from jax import lax
from jax.experimental import pallas as pl
from jax.experimental.pallas import tpu as pltpu

F32 = jnp.float32
BF16 = jnp.bfloat16
HIGHEST = lax.Precision.HIGHEST

LANES = 128
VMEM_LIMIT_BYTES = 56 * 1024 * 1024

GRID_W = 64
CHUNK = 64
RWKV_HEADS, RWKV_HEAD_DIM = 16, 64
RWKV_WIDTH = RWKV_HEADS * RWKV_HEAD_DIM
DECAY_LORA = ICL_LORA = 96
GATE_LORA = 64
RWKV_GN_EPS = 64e-5
GLA_HEADS, GLA_KEY_DIM, GLA_VAL_DIM = 4, 128, 256
GLA_QK_WIDTH = GLA_HEADS * GLA_KEY_DIM
GLA_V_WIDTH = GLA_HEADS * GLA_VAL_DIM
GLA_GATE_LORA = 16
GLA_TAU = 16.0
GLA_NORM_EPS = 1e-5
GLA_SUB = 8
NORM_EPS = 1e-6

RWKV_IN = 3 * RWKV_WIDTH + 2 * DECAY_LORA + 2 * ICL_LORA + GATE_LORA
RWKV_PAD = 3584
OFF_WD = 3 * RWKV_WIDTH
OFF_AD = OFF_WD + 2 * DECAY_LORA
OFF_GD = OFF_AD + 2 * ICL_LORA
GLA_IN = 2 * GLA_QK_WIDTH + 2 * GLA_V_WIDTH + 2 * GLA_GATE_LORA
GLA_BLK = 512
GLA_PAD = 7 * GLA_BLK
OFF_GLA_AD = 6 * GLA_BLK

PAIR = 2 * RWKV_HEAD_DIM
N_PAIRS = RWKV_HEADS // 2


def _cparams(semantics):
    return pltpu.CompilerParams(dimension_semantics=semantics, vmem_limit_bytes=VMEM_LIMIT_BYTES)


def _sigmoid(z):
    return 1.0 / (1.0 + jnp.exp(-z))


def _softplus(z):
    return jnp.maximum(z, 0.0) + jnp.log(1.0 + jnp.exp(-jnp.abs(z)))


def _dot(a, b, dims, precision):
    if precision is BF16:
        a, b, precision = a.astype(BF16), b.astype(BF16), None
    return lax.dot_general(a, b, (dims, ((), ())), precision=precision, preferred_element_type=F32)


def _mm(a, b, precision=None):
    return _dot(a, b, ((1,), (0,)), precision)


def _mm_nt(a, b, precision=None):
    return _dot(a, b, ((1,), (1,)), precision)


def _mm_tn(a, b, precision=None):
    return _dot(a, b, ((0,), (0,)), precision)


def _split2(z):
    hi = z.astype(BF16)
    return hi, (z - hi.astype(F32)).astype(BF16)


def _mm_split(a, b):
    ah, al = _split2(a)
    bh, bl = _split2(b)
    return (_mm(al, bh) + _mm(ah, bl)) + _mm(ah, bh)


def _mm_exact_lhs(e, b):
    hi = b.astype(BF16)
    r = b - hi.astype(F32)
    mid = r.astype(BF16)
    lo = (r - mid.astype(F32)).astype(BF16)
    e = e.astype(BF16)
    return (_mm(e, lo) + _mm(e, mid)) + _mm(e, hi)


def _mod_kernel(c_ref, w_ref, b_ref, o_ref):
    s = c_ref[...]
    s = s * _sigmoid(s)
    o_ref[...] = _mm_split(s, w_ref[...]) + b_ref[...]


def _modulation(cvecs, ada_w, ada_b, tn=1024):
    m, d = cvecs.shape
    n = ada_w.shape[1]
    return pl.pallas_call(
        _mod_kernel,
        grid=(n // tn,),
        in_specs=[pl.BlockSpec((m, d), lambda j: (0, 0)),
                  pl.BlockSpec((d, tn), lambda j: (0, j)),
                  pl.BlockSpec((1, tn), lambda j: (0, j))],
        out_specs=pl.BlockSpec((m, tn), lambda j: (0, j)),
        out_shape=jax.ShapeDtypeStruct((m, n), F32),
        compiler_params=_cparams(("arbitrary",)),
        name="adaln_mod",
    )(cvecs, ada_w, ada_b.reshape(1, n))


def _prenorm_kernel(x_ref, sh_ref, sc_ref, g_ref, o_ref):
    x = x_ref[0]
    ms = jnp.mean(x * x, axis=-1, keepdims=True)
    y = x * lax.rsqrt(ms + NORM_EPS) * g_ref[...]
    o_ref[0] = (y * (1.0 + sc_ref[0]) + sh_ref[0]).astype(o_ref.dtype)


def _prenorm(x, shift, scale, gain, tm=256):
    b, t, d = x.shape
    row = lambda bi, i: (bi, i, 0)
    per_b = lambda bi, i: (bi, 0, 0)
    return pl.pallas_call(
        _prenorm_kernel,
        grid=(b, t // tm),
        in_specs=[pl.BlockSpec((1, tm, d), row), pl.BlockSpec((1, 1, d), per_b), pl.BlockSpec((1, 1, d), per_b),
                  pl.BlockSpec((1, d), lambda bi, i: (0, 0))],
        out_specs=pl.BlockSpec((1, tm, d), row),
        out_shape=jax.ShapeDtypeStruct((b, t, d), BF16),
        compiler_params=_cparams(("arbitrary", "arbitrary")),
        name="prenorm",
    )(x, shift, scale, gain)


def _proj_kernel(gate_out, h_ref, w_ref, o_ref):
    p = _mm(h_ref[0], w_ref[...])
    o_ref[0] = (_sigmoid(p) if gate_out else p).astype(o_ref.dtype)


def _project(h, w, tm, tn, name, gate_out=False):
    b, t, d = h.shape
    n = w.shape[1]
    return pl.pallas_call(
        functools.partial(_proj_kernel, gate_out),
        grid=(b, t // tm, n // tn),
        in_specs=[pl.BlockSpec((1, tm, d), lambda bi, i, j: (bi, i, 0)),
                  pl.BlockSpec((d, tn), lambda bi, i, j: (0, j))],
        out_specs=pl.BlockSpec((1, tm, tn), lambda bi, i, j: (bi, i, j)),
        out_shape=jax.ShapeDtypeStruct((b, t, n), BF16 if gate_out else F32),
        compiler_params=_cparams(("arbitrary", "arbitrary", "arbitrary")),
        name=name,
    )(h, w)


def _colmajor_perm(n_rows, n_cols):
    dst = jnp.arange(n_rows * n_cols)
    src = (dst % n_rows) * n_cols + dst // n_rows
    return (src[:, None] == dst[None, :]).astype(BF16)


def _proj_cm_kernel(h_ref, perm_ref, w_ref, o_ref, h_scr):
    @pl.when(pl.program_id(2) == 0)
    def _():
        n_r, n_c, d = h_ref.shape[1:]
        for cb in range(d // GLA_BLK):
            cs = slice(cb * GLA_BLK, (cb + 1) * GLA_BLK)
            hr = h_ref[0, :, :, cs].astype(F32).reshape(n_r * n_c, GLA_BLK).astype(BF16)
            h_scr[:, cs] = _mm(perm_ref[...], hr).astype(BF16)

    o_ref[0] = _mm(h_scr[...], w_ref[...])


def _project_colmajor(h, w, n_cols, tn, name):
    b, t, d = h.shape
    n = w.shape[1]
    n_r = t // GRID_W
    tm = n_r * n_cols
    h4 = h.reshape(b, n_r, GRID_W, d)
    perm = _colmajor_perm(n_r, n_cols)
    return pl.pallas_call(
        _proj_cm_kernel,
        grid=(b, GRID_W // n_cols, n // tn),
        in_specs=[pl.BlockSpec((1, n_r, n_cols, d), lambda bi, i, j: (bi, 0, i, 0)),
                  pl.BlockSpec((tm, tm), lambda bi, i, j: (0, 0)),
                  pl.BlockSpec((d, tn), lambda bi, i, j: (0, j))],
        out_specs=pl.BlockSpec((1, tm, tn), lambda bi, i, j: (bi, i, j)),
        out_shape=jax.ShapeDtypeStruct((b, t, n), F32),
        scratch_shapes=[pltpu.VMEM((tm, d), BF16)],
        compiler_params=_cparams(("arbitrary", "arbitrary", "arbitrary")),
        name=name,
    )(h4, perm, w)


HALO = 16


def _proj_shift_kernel(h_ref, hp_ref, hn_ref, w_ref, mu_ref, o_ref):
    i = pl.program_id(1)
    tm = h_ref.shape[1]
    p = _mm(h_ref[0], w_ref[...])
    ph = _mm(jnp.concatenate([hp_ref[0], hn_ref[0]], 0), w_ref[...])
    before = jnp.where(i == 0, 0.0, ph[HALO - 1:HALO])
    after = jnp.where(i == pl.num_programs(1) - 1, 0.0, ph[HALO:HALO + 1])
    row = lax.broadcasted_iota(jnp.int32, p.shape, 0)
    prev = jnp.where(row == 0, before, pltpu.roll(p, 1, 0))
    nxt = jnp.where(row == tm - 1, after, pltpu.roll(p, tm - 1, 0))
    o_ref[0] = p + mu_ref[...] * (0.5 * (prev + nxt) - p)


def _project_shift(h, w, mu, tm, tn, name):
    b, t, d = h.shape
    n = w.shape[1]
    rh, nh = tm // HALO, t // HALO
    return pl.pallas_call(
        _proj_shift_kernel,
        grid=(b, t // tm, n // tn),
        in_specs=[pl.BlockSpec((1, tm, d), lambda bi, i, j: (bi, i, 0)),
                  pl.BlockSpec((1, HALO, d), lambda bi, i, j: (bi, jnp.maximum(i * rh - 1, 0), 0)),
                  pl.BlockSpec((1, HALO, d), lambda bi, i, j: (bi, jnp.minimum((i + 1) * rh, nh - 1), 0)),
                  pl.BlockSpec((d, tn), lambda bi, i, j: (0, j)),
                  pl.BlockSpec((1, tn), lambda bi, i, j: (0, j))],
        out_specs=pl.BlockSpec((1, tm, tn), lambda bi, i, j: (bi, i, j)),
        out_shape=jax.ShapeDtypeStruct((b, t, n), F32),
        compiler_params=_cparams(("arbitrary", "arbitrary", "arbitrary")),
        name=name,
    )(h, h, h, w, mu)


def _head_sum_matrix():
    r = lax.broadcasted_iota(jnp.int32, (PAIR, PAIR), 0)
    c = lax.broadcasted_iota(jnp.int32, (PAIR, PAIR), 1)
    return jnp.where((r >> 6) == (c >> 6), 1.0, 0.0).astype(BF16)


def _head_sum(z, hsum):
    hi = z.astype(BF16)
    lo = (z - hi.astype(F32)).astype(BF16)
    return _mm(hi, hsum) + _mm(lo, hsum)


def _rwkv_kernel(prec, n_ctx_chunks, plf_ref, pcf_ref, plb_ref, pcb_ref, w2_ref, w0_ref, a2_ref, a0_ref, kk_ref, ka_ref,
                 yf_ref, yb_ref, st_scr):
    @pl.when(pl.program_id(1) == 0)
    def _():
        st_scr[...] = jnp.zeros_like(st_scr)

    is_ctx = pl.program_id(1) < n_ctx_chunks
    y_refs = (yf_ref, yb_ref)
    prep = [_rwkv_prep(dr == 1, jnp.where(is_ctx, pc_ref[0], pl_ref[0]), w2_ref.at[dr], w0_ref.at[dr], a2_ref.at[dr],
                       a0_ref.at[dr], kk_ref, ka_ref)
            for dr, (pl_ref, pc_ref) in enumerate(((plf_ref, pcf_ref), (plb_ref, pcb_ref)))]

    items = [(dr, pr) for pr in range(N_PAIRS) for dr in range(2)]
    n = range(len(items))
    per_pair = lambda name: [prep[dr][name][pr] for dr, pr in items]
    per_dir = lambda name: [prep[dr][name] for dr, _ in items]
    at, bt, kt, rt_, rs, bh, kh, v_swap, v_stack, p_end = (per_pair(k) for k in (
        "at", "bt", "kt", "rt", "rs", "bh", "kh", "v_swap", "v_stack", "p_end"))
    mask_n, mask_k, incl_c = per_dir("mask_n"), per_dir("mask_k"), per_dir("incl_c")

    c2 = 2 * CHUNK
    ri = lax.broadcasted_iota(jnp.int32, (c2, PAIR), 0)
    li = lax.broadcasted_iota(jnp.int32, (c2, PAIR), 1)
    ident = ri == li
    eye = jnp.where(ident, 1.0, 0.0).astype(F32)
    lane_e = lax.broadcasted_iota(jnp.int32, (CHUNK, PAIR), 1) < RWKV_HEAD_DIM
    cat0 = lambda *z: jnp.concatenate(z, 0)

    g_e = [_mm_nt(cat0(at[i][0], rs[i][0]), cat0(bt[i], kt[i]), prec) for i in n]
    g_o = [_mm_nt(cat0(at[i][1], rs[i][1]), cat0(kt[i], bt[i]), prec) for i in n]
    g_top = [cat0(g_e[i][0:CHUNK], g_o[i][0:CHUNK]) for i in n]
    nbd = [jnp.where(mask_n[i], g_top[i], 0.0) for i in n]
    aak = [jnp.where(mask_k[i], g_top[i], 0.0) for i in n]
    rab = [jnp.where(incl_c[i], jnp.where(lane_e, g_e[i][CHUNK:c2], g_o[i][CHUNK:c2]), 0.0) for i in n]
    rak = [jnp.where(incl_c[i], jnp.where(lane_e, g_o[i][CHUNK:c2], g_e[i][CHUNK:c2]), 0.0) for i in n]

    x = [_mm(aak[i], v_swap[i], prec) for i in n]
    n2 = [_mm(z, z, prec) for z in nbd]
    y0b = [_mm(rak[i], v_swap[i], prec) for i in n]
    n4 = [_mm(z, z, prec) for z in n2]
    imn = [eye - z for z in nbd]
    p1 = [imn[i] + _mm(imn[i], n2[i], prec) for i in n]
    n8 = [_mm(z, z, prec) for z in n4]
    nb = [_mm_tn(kh[i], v_stack[i], prec) for i in n]
    n16 = [_mm(z, z, prec) for z in n8]
    p2 = [eye + n4[i] + n8[i] + _mm(n4[i], n8[i], prec) for i in n]
    n32 = [_mm(z, z, prec) for z in n16]
    p12 = [_mm(p1[i], p2[i], prec) for i in n]
    p3 = [eye + n16[i] + n32[i] + _mm(n16[i], n32[i], prec) for i in n]
    tinv = [_mm(p12[i], p3[i], prec) for i in n]
    wu = [-_mm(tinv[i], jnp.concatenate([cat0(*at[i]), x[i]], 1), prec) for i in n]
    qy = [_mm(rab[i], wu[i], prec) for i in n]
    mn = [_mm_tn(bh[i], wu[i], prec) for i in n]
    q = [rt_[i] + qy[i][:, 0:PAIR] for i in n]
    m = [jnp.where(ident, p_end[i], 0.0) + mn[i][:, 0:PAIR] for i in n]
    qm = [_mm(cat0(q[i], m[i]), st_scr[dr, pr], prec) for i, (dr, pr) in enumerate(items)]
    for i, (dr, pr) in enumerate(items):
        y_refs[dr][0, :, pr * PAIR:(pr + 1) * PAIR] = qm[i][0:CHUNK] + qy[i][:, PAIR:2 * PAIR] + y0b[i]
        st_scr[dr, pr] = qm[i][CHUNK:CHUNK + PAIR] + mn[i][:, PAIR:2 * PAIR] + nb[i]


def _rwkv_prep(rev, p, w2_ref, w0_ref, a2_ref, a0_ref, kk_ref, ka_ref):
    hd = RWKV_HEAD_DIM
    c2 = 2 * CHUNK
    r = p[:, 0:RWKV_WIDTH]
    k = p[:, RWKV_WIDTH:2 * RWKV_WIDTH]
    v = p[:, 2 * RWKV_WIDTH:3 * RWKV_WIDTH]
    d_off = DECAY_LORA if rev else 0
    wd = p[:, OFF_WD + d_off:OFF_WD + d_off + DECAY_LORA]
    ad = p[:, OFF_AD + d_off:OFF_AD + d_off + ICL_LORA]

    w_log = -_softplus(-(w0_ref[...] + _mm_split(jnp.tanh(wd), w2_ref[...]))) - 0.5
    lw = -jnp.exp(w_log)
    a = _sigmoid(a0_ref[...] + _mm(ad, a2_ref[...], BF16))
    kk_raw = k * kk_ref[...]
    kd = k * (1.0 + (a - 1.0) * ka_ref[...])

    ri = lax.broadcasted_iota(jnp.int32, (c2, PAIR), 0)
    li = lax.broadcasted_iota(jnp.int32, (c2, PAIR), 1)
    rt, lt = ri & (CHUNK - 1), li & (hd - 1)
    same_head = (ri >> 6) == (li >> 6)
    strict = (lt > rt) if rev else (lt < rt)
    rc = lax.broadcasted_iota(jnp.int32, (CHUNK, PAIR), 0)
    lc = lax.broadcasted_iota(jnp.int32, (CHUNK, PAIR), 1)
    lane_e = lc < hd
    hsum = _head_sum_matrix()

    ci = lax.broadcasted_iota(jnp.int32, (CHUNK, CHUNK), 0)
    cj = lax.broadcasted_iota(jnp.int32, (CHUNK, CHUNK), 1)
    tri = jnp.where((cj >= ci) if rev else (cj <= ci), 1.0, 0.0).astype(F32)
    cum = _mm_exact_lhs(tri, lw)
    total = cum[0:1] if rev else cum[CHUNK - 1:CHUNK]
    e_prev = jnp.exp(cum - lw)
    e_neg = jnp.exp(-cum)
    e_pos = jnp.exp(cum)
    e_rest = jnp.exp(total - cum)
    p_end = jnp.exp(total)

    def split(z):
        ze = jnp.where(lane_e, z, 0.0)
        return ze, z - ze

    sls = [slice(pr * PAIR, (pr + 1) * PAIR) for pr in range(N_PAIRS)]
    cat0 = lambda *z: jnp.concatenate(z, 0)
    kkr = [kk_raw[:, sl] for sl in sls]
    nrm2 = [_head_sum(z * z, hsum) for z in kkr]
    kk = [z / jnp.maximum(jnp.sqrt(n), 1e-12) for z, n in zip(kkr, nrm2)]
    bb = [z * a[:, sl] for z, sl in zip(kk, sls)]
    rt_ = [r[:, sl] * e_pos[:, sl] for sl in sls]
    vs = [split(v[:, sl]) for sl in sls]
    return dict(
        mask_n=jnp.logical_and(same_head, strict),
        mask_k=jnp.logical_and(jnp.logical_not(same_head), strict),
        incl_c=((lc & (hd - 1)) >= rc) if rev else ((lc & (hd - 1)) <= rc),
        at=[split(z * e_prev[:, sl]) for z, sl in zip(kk, sls)],
        bt=[z * e_neg[:, sl] for z, sl in zip(bb, sls)],
        kt=[kd[:, sl] * e_neg[:, sl] for sl in sls],
        rt=rt_,
        rs=[split(z) for z in rt_],
        bh=[cat0(*split(z * e_rest[:, sl])) for z, sl in zip(bb, sls)],
        kh=[cat0(*split(kd[:, sl] * e_rest[:, sl])) for sl in sls],
        v_swap=[cat0(vo, ve) for ve, vo in vs],
        v_stack=[cat0(ve, vo) for ve, vo in vs],
        p_end=[p_end[:, sl] for sl in sls],
    )


def _scan_order(n_ctx_chunks, n_lat):
    n_steps = n_ctx_chunks + n_lat
    lat_f = lambda i: jnp.maximum(i - n_ctx_chunks, 0)
    lat_b = lambda i: jnp.where(i < n_ctx_chunks, n_lat - 1, n_steps - 1 - i)
    ctx_f = lambda i: jnp.minimum(i, n_ctx_chunks - 1)
    ctx_b = lambda i: jnp.maximum(n_ctx_chunks - 1 - i, 0)
    return n_steps, (lat_f, lat_b), (ctx_f, ctx_b)


def _rwkv_scan(p_lat, p_ctx, w2, w0, a2, a0, k_k, k_a, prec):
    b, t, w = p_lat.shape
    n_lat = t // CHUNK
    n_ctx_chunks = p_ctx.shape[1] // CHUNK
    n_steps, lat_of, ctx_of = _scan_order(n_ctx_chunks, n_lat)
    chunk = lambda f: pl.BlockSpec((1, CHUNK, w), lambda bi, i: (bi, f(i), 0))
    out = lambda f: pl.BlockSpec((1, CHUNK, RWKV_WIDTH), lambda bi, i: (bi, f(i), 0))
    vec = lambda bi, i: (0, 0)
    vec3 = lambda bi, i: (0, 0, 0)
    y_shape = jax.ShapeDtypeStruct((b, t, RWKV_WIDTH), F32)
    return pl.pallas_call(
        functools.partial(_rwkv_kernel, prec, n_ctx_chunks),
        grid=(b, n_steps),
        in_specs=[chunk(lat_of[0]), chunk(ctx_of[0]), chunk(lat_of[1]), chunk(ctx_of[1]),
                  pl.BlockSpec((2, DECAY_LORA, RWKV_WIDTH), vec3),
                  pl.BlockSpec((2, 1, RWKV_WIDTH), vec3),
                  pl.BlockSpec((2, ICL_LORA, RWKV_WIDTH), vec3),
                  pl.BlockSpec((2, 1, RWKV_WIDTH), vec3),
                  pl.BlockSpec((1, RWKV_WIDTH), vec),
                  pl.BlockSpec((1, RWKV_WIDTH), vec)],
        out_specs=[out(lat_of[0]), out(lat_of[1])],
        out_shape=[y_shape, y_shape],
        scratch_shapes=[pltpu.VMEM((2, N_PAIRS, PAIR, PAIR), F32)],
        compiler_params=_cparams(("arbitrary", "arbitrary")),
        name="rwkv7_scan",
    )(p_lat, p_ctx, p_lat, p_ctx, w2, w0, a2, a0, k_k, k_a)


def _gla_kernel(prec, n_ctx_chunks, *refs):
    lat = (refs[0:6], refs[6:12])
    aup_ref, ab_ref, of_ref, ob_ref, st_scr = refs[12:17]
    o_refs = (of_ref, ob_ref)

    @pl.when(pl.program_id(1) == 0)
    def _():
        st_scr[...] = jnp.zeros_like(st_scr)

    is_ctx = pl.program_id(1) < n_ctx_chunks
    dk, dv, sb = GLA_KEY_DIM, GLA_VAL_DIM, GLA_SUB
    n_sb = CHUNK // sb
    prep = [_gla_prep(dr == 1, is_ctx, *lat[dr], aup_ref.at[dr], ab_ref.at[dr]) for dr in range(2)]

    items = [(dr, h) for h in range(GLA_HEADS) for dr in range(2)]
    n = range(len(items))
    revs = [dr == 1 for dr, _ in items]
    qh, kh, bh, lah, toth, vh = ([prep[dr][name][h] for dr, h in items] for name in ("q", "k", "b", "la", "tot", "v"))
    st = [st_scr[dr, h] for dr, h in items]

    o_inter = [_mm_nt(qh[i] * jnp.exp(bh[i]), st[i], prec) for i in n]
    st_new = [st[i] * jnp.exp(toth[i]) + _mm_tn(vh[i], kh[i] * jnp.exp(toth[i] - bh[i]), prec) for i in n]

    arow = lax.broadcasted_iota(jnp.int32, (CHUNK, CHUNK), 0)
    acol = lax.broadcasted_iota(jnp.int32, (CHUNK, CHUNK), 1)
    bcol = lax.broadcasted_iota(jnp.int32, (sb, CHUNK), 1)
    off_rows = [[] for _ in n]
    for blk in range(n_sb):
        rs = slice(blk * sb, (blk + 1) * sb)
        for i in n:
            rev = revs[i]
            if (blk == n_sb - 1) if rev else (blk == 0):
                off_rows[i].append(jnp.zeros((sb, CHUNK), F32))
                continue
            first = blk * sb + (sb - 1 if rev else 0)
            before = (bcol >= (blk + 1) * sb) if rev else (bcol < blk * sb)
            beta = bh[i][first:first + 1] - lah[i][first:first + 1]
            qs = qh[i][rs] * jnp.exp(bh[i][rs] - beta)
            ksc = kh[i] * jnp.exp(jnp.minimum(beta - bh[i], 0.0))
            off_rows[i].append(jnp.where(before, _mm_nt(qs, ksc, prec), 0.0))
    att = [jnp.concatenate(off_rows[i], 0) for i in n]

    in_blk = arow & (sb - 1)
    for s in range(sb):
        pick = lambda z: jnp.concatenate(
            [jnp.broadcast_to(z[blk * sb + s:blk * sb + s + 1], (sb, dk)) for blk in range(n_sb)], 0)
        on_col = acol == (arow & ~(sb - 1)) + s
        tgt = (jnp.logical_and(on_col, in_blk >= s), jnp.logical_and(on_col, in_blk <= s))
        for i in n:
            e = jnp.exp(jnp.minimum(bh[i] - pick(bh[i]), 0.0))
            col = jnp.sum(qh[i] * pick(kh[i]) * e, axis=-1, keepdims=True)
            att[i] = jnp.where(tgt[revs[i]], col, att[i])

    for i, (dr, h) in enumerate(items):
        o_refs[dr][0, :, h * dv:(h + 1) * dv] = o_inter[i] + _mm(att[i], vh[i], prec)
        st_scr[dr, h] = st_new[i]


def _gla_prep(rev, is_ctx, q_ref, k_ref, v0_ref, v1_ref, ad_ref, ctx_ref, aup_ref, ab_ref):
    pc = ctx_ref[0]
    q = jnp.where(is_ctx, pc[:, 0:GLA_BLK], q_ref[0])
    k = jnp.where(is_ctx, pc[:, GLA_BLK:2 * GLA_BLK], k_ref[0])
    v = jnp.concatenate([jnp.where(is_ctx, pc[:, 2 * GLA_BLK:3 * GLA_BLK], v0_ref[0]),
                         jnp.where(is_ctx, pc[:, 3 * GLA_BLK:4 * GLA_BLK], v1_ref[0])], 1)
    d_off = GLA_GATE_LORA if rev else 0
    ad = jnp.where(is_ctx, pc[:, OFF_GLA_AD:OFF_GLA_AD + LANES], ad_ref[0][:, 0:LANES])
    ad = ad[:, d_off:d_off + GLA_GATE_LORA]

    la = -_softplus(-(_mm_split(ad, aup_ref[...]) + ab_ref[...])) * (1.0 / GLA_TAU)
    ci = lax.broadcasted_iota(jnp.int32, (CHUNK, CHUNK), 0)
    cj = lax.broadcasted_iota(jnp.int32, (CHUNK, CHUNK), 1)
    tri = jnp.where((cj >= ci) if rev else (cj <= ci), 1.0, 0.0).astype(F32)
    cum = _mm_exact_lhs(tri, la)
    total = cum[0:1] if rev else cum[CHUNK - 1:CHUNK]
    dk, dv = GLA_KEY_DIM, GLA_VAL_DIM
    ksl = [slice(h * dk, (h + 1) * dk) for h in range(GLA_HEADS)]
    scale = GLA_KEY_DIM ** -0.5
    return dict(q=[q[:, s_] * scale for s_ in ksl], k=[k[:, s_] for s_ in ksl], b=[cum[:, s_] for s_ in ksl],
                la=[la[:, s_] for s_ in ksl], tot=[total[:, s_] for s_ in ksl],
                v=[v[:, h * dv:(h + 1) * dv] for h in range(GLA_HEADS)])


def _gla_scan(p_lat, p_ctx, alpha_up, alpha_b, prec):
    b, t, w = p_lat.shape
    assert w == GLA_PAD and t == GRID_W * CHUNK
    n_ctx_chunks = p_ctx.shape[1] // CHUNK
    n_steps, col_of, ctx_of = _scan_order(n_ctx_chunks, GRID_W)

    def blocks(dr):
        lat = lambda m: pl.BlockSpec((1, CHUNK, GLA_BLK), lambda bi, i: (bi, col_of[dr](i), m))
        return [lat(0), lat(1), lat(2), lat(3), lat(6), pl.BlockSpec((1, CHUNK, w), lambda bi, i: (bi, ctx_of[dr](i), 0))]

    out = lambda dr: pl.BlockSpec((1, CHUNK, GLA_V_WIDTH), lambda bi, i: (bi, col_of[dr](i), 0))
    vec3 = lambda bi, i: (0, 0, 0)
    o_shape = jax.ShapeDtypeStruct((b, t, GLA_V_WIDTH), F32)
    return pl.pallas_call(
        functools.partial(_gla_kernel, prec, n_ctx_chunks),
        grid=(b, n_steps),
        in_specs=blocks(0) + blocks(1) + [pl.BlockSpec((2, GLA_GATE_LORA, GLA_QK_WIDTH), vec3),
                                          pl.BlockSpec((2, 1, GLA_QK_WIDTH), vec3)],
        out_specs=[out(0), out(1)],
        out_shape=[o_shape, o_shape],
        scratch_shapes=[pltpu.VMEM((2, GLA_HEADS, GLA_VAL_DIM, GLA_KEY_DIM), F32)],
        compiler_params=_cparams(("arbitrary", "arbitrary")),
        name="gla_scan",
    )(*([p_lat] * 5 + [p_ctx]) * 2, alpha_up, alpha_b)


def _rwkv_post_kernel(yf_ref, yb_ref, pm_ref, a2_ref, a0_ref, g2_ref, ka_ref, rk_ref, lng_ref, lnb_ref, o_ref):
    p = pm_ref[0]
    r = p[:, 0:RWKV_WIDTH]
    k = p[:, RWKV_WIDTH:2 * RWKV_WIDTH]
    v = p[:, 2 * RWKV_WIDTH:3 * RWKV_WIDTH]
    ad_f = p[:, OFF_AD:OFF_AD + ICL_LORA]
    ad_b = p[:, OFF_AD + ICL_LORA:OFF_AD + 2 * ICL_LORA]
    gd = p[:, OFF_GD:OFF_GD + GATE_LORA]
    ka = ka_ref[...]
    a_f = _sigmoid(a0_ref[0] + _mm(ad_f, a2_ref[0], BF16))
    a_b = _sigmoid(a0_ref[1] + _mm(ad_b, a2_ref[1], BF16))
    kd_sum = k * (1.0 + (a_f - 1.0) * ka) + k * (1.0 + (a_b - 1.0) * ka)
    gate = _mm(_sigmoid(gd), g2_ref[...], BF16)
    rkk = r * kd_sum * rk_ref[...]
    ysum = yf_ref[0] + yb_ref[0]
    hsum = _head_sum_matrix()
    inv_n = 1.0 / RWKV_HEAD_DIM
    for pr in range(N_PAIRS):
        sl = slice(pr * PAIR, (pr + 1) * PAIR)
        ys = ysum[:, sl]
        mean = _head_sum(ys, hsum) * inv_n
        dlt = ys - mean
        var = _head_sum(dlt * dlt, hsum) * inv_n
        gn = dlt * lax.rsqrt(var + RWKV_GN_EPS) * lng_ref[:, sl] + lnb_ref[:, sl]
        bonus = _head_sum(rkk[:, sl], hsum) * v[:, sl]
        o_ref[0, :, sl] = ((gn + bonus) * gate[:, sl]).astype(o_ref.dtype)


def _rwkv_post(y_f, y_b, pmix, a2, a0, g2, k_a, r_k, ln_g, ln_b, tm=256):
    b, t, w = y_f.shape
    row = lambda bi, i: (bi, i, 0)
    vec = lambda bi, i: (0, 0)
    vec3 = lambda bi, i: (0, 0, 0)
    return pl.pallas_call(
        _rwkv_post_kernel,
        grid=(b, t // tm),
        in_specs=[pl.BlockSpec((1, tm, w), row),
                  pl.BlockSpec((1, tm, w), row),
                  pl.BlockSpec((1, tm, pmix.shape[2]), row),
                  pl.BlockSpec((2, ICL_LORA, w), vec3),
                  pl.BlockSpec((2, 1, w), vec3),
                  pl.BlockSpec((GATE_LORA, w), vec),
                  pl.BlockSpec((1, w), vec), pl.BlockSpec((1, w), vec),
                  pl.BlockSpec((1, w), vec), pl.BlockSpec((1, w), vec)],
        out_specs=pl.BlockSpec((1, tm, w), row),
        out_shape=jax.ShapeDtypeStruct((b, t, w), BF16),
        compiler_params=_cparams(("arbitrary", "arbitrary")),
        name="rwkv_post",
    )(y_f, y_b, pmix, a2, a0, g2, k_a, r_k, ln_g, ln_b)


def _gla_post_kernel(of_ref, ob_ref, g_ref, ng_ref, perm_ref, o_ref):
    n_r, n_c = o_ref.shape[1:3]
    dv = GLA_VAL_DIM
    for h in range(GLA_HEADS):
        sl = slice(h * dv, (h + 1) * dv)
        oh = of_ref[0, :, sl] + ob_ref[0, :, sl]
        oh = oh * lax.rsqrt(jnp.mean(oh * oh, axis=-1, keepdims=True) + GLA_NORM_EPS) * ng_ref[:, sl]
        gh = g_ref[0, :, sl]
        y = (oh * (gh * _sigmoid(gh))).astype(BF16)
        o_ref[0, :, :, sl] = _mm(perm_ref[...], y).reshape(n_r, n_c, dv).astype(o_ref.dtype)


def _gla_post(o_f, o_b, p_gla, norm_g, n_cols=16):
    b, t, w = o_f.shape
    n_r = t // GRID_W
    tm = n_r * n_cols
    perm_t = _colmajor_perm(n_r, n_cols).T
    row = lambda bi, i: (bi, i, 0)
    out = pl.pallas_call(
        _gla_post_kernel,
        grid=(b, GRID_W // n_cols),
        in_specs=[pl.BlockSpec((1, tm, w), row),
                  pl.BlockSpec((1, tm, w), row),
                  pl.BlockSpec((1, tm, w), lambda bi, i: (bi, i, 2)),
                  pl.BlockSpec((1, w), lambda bi, i: (0, 0)),
                  pl.BlockSpec((tm, tm), lambda bi, i: (0, 0))],
        out_specs=pl.BlockSpec((1, n_r, n_cols, w), lambda bi, i: (bi, 0, i, 0)),
        out_shape=jax.ShapeDtypeStruct((b, n_r, GRID_W, w), BF16),
        compiler_params=_cparams(("arbitrary", "arbitrary")),
        name="gla_post",
    )(o_f, o_b, p_gla, norm_g, perm_t)
    return out.reshape(b, t, w)


def _merge_kernel(ya_ref, yb_ref, wr_ref, wg_ref, ga_ref, gb_ref, o_ref):
    ma = _mm(ya_ref[0], wr_ref[...])
    mb = _mm(yb_ref[0], wg_ref[...])
    o_ref[0] = (ga_ref[0].astype(F32) * ma + gb_ref[0].astype(F32) * mb).astype(o_ref.dtype)


def _merge(ya, yb, w_r, w_g, p_gate, tm=1024, tn=1024):
    b, t, w = ya.shape
    d = w_r.shape[1]
    nj = d // tn
    return pl.pallas_call(
        _merge_kernel,
        grid=(b, t // tm, nj),
        in_specs=[pl.BlockSpec((1, tm, w), lambda bi, i, j: (bi, i, 0)),
                  pl.BlockSpec((1, tm, w), lambda bi, i, j: (bi, i, 0)),
                  pl.BlockSpec((w, tn), lambda bi, i, j: (0, j)),
                  pl.BlockSpec((w, tn), lambda bi, i, j: (0, j)),
                  pl.BlockSpec((1, tm, tn), lambda bi, i, j: (bi, i, j)),
                  pl.BlockSpec((1, tm, tn), lambda bi, i, j: (bi, i, j + nj))],
        out_specs=pl.BlockSpec((1, tm, tn), lambda bi, i, j: (bi, i, j)),
        out_shape=jax.ShapeDtypeStruct((b, t, d), BF16),
        compiler_params=_cparams(("arbitrary", "arbitrary", "arbitrary")),
        name="merge_branches",
    )(ya, yb, w_r, w_g, p_gate, p_gate)


def _mix_out_kernel(m_ref, w_ref, x_ref, gate_ref, npost_ref, npre_ref, sh_ref, sc_ref, x1_ref, h_ref):
    half = m_ref.shape[1] // 2
    for rs in (slice(0, half), slice(half, 2 * half)):
        z = _mm(m_ref[0, rs, :], w_ref[...])
        z = z * lax.rsqrt(jnp.mean(z * z, axis=-1, keepdims=True) + NORM_EPS) * npost_ref[...]
        x1 = x_ref[0, rs, :] + gate_ref[0] * z
        x1_ref[0, rs, :] = x1
        y = x1 * lax.rsqrt(jnp.mean(x1 * x1, axis=-1, keepdims=True) + NORM_EPS) * npre_ref[...]
        h_ref[0, rs, :] = (y * (1.0 + sc_ref[0]) + sh_ref[0]).astype(h_ref.dtype)


def _mix_out(m, w_out, x, gate, n_post, n_pre, shift, scale, tm=512):
    b, t, d = x.shape
    row = lambda bi, i: (bi, i, 0)
    per_b = lambda bi, i: (bi, 0, 0)
    vec = lambda bi, i: (0, 0)
    return pl.pallas_call(
        _mix_out_kernel,
        grid=(b, t // tm),
        in_specs=[pl.BlockSpec((1, tm, d), row),
                  pl.BlockSpec((d, d), vec),
                  pl.BlockSpec((1, tm, d), row),
                  pl.BlockSpec((1, 1, d), per_b),
                  pl.BlockSpec((1, d), vec), pl.BlockSpec((1, d), vec),
                  pl.BlockSpec((1, 1, d), per_b), pl.BlockSpec((1, 1, d), per_b)],
        out_specs=[pl.BlockSpec((1, tm, d), row), pl.BlockSpec((1, tm, d), row)],
        out_shape=[jax.ShapeDtypeStruct((b, t, d), F32), jax.ShapeDtypeStruct((b, t, d), BF16)],
        compiler_params=_cparams(("arbitrary", "arbitrary")),
        name="mix_out",
    )(m, w_out, x, gate, n_post, n_pre, shift, scale)


def _ffn_up_kernel(h_ref, wg_ref, wu_ref, o_ref):
    h = h_ref[0]
    a = _mm(h, wg_ref[...])
    u = _mm(h, wu_ref[...])
    o_ref[0] = (a * _sigmoid(a) * u).astype(o_ref.dtype)


def _ffn_up(h, w_gate, w_up, tm=1024, tn=512):
    b, t, d = h.shape
    f = w_gate.shape[1]
    return pl.pallas_call(
        _ffn_up_kernel,
        grid=(b, t // tm, f // tn),
        in_specs=[pl.BlockSpec((1, tm, d), lambda bi, i, j: (bi, i, 0)),
                  pl.BlockSpec((d, tn), lambda bi, i, j: (0, j)),
                  pl.BlockSpec((d, tn), lambda bi, i, j: (0, j))],
        out_specs=pl.BlockSpec((1, tm, tn), lambda bi, i, j: (bi, i, j)),
        out_shape=jax.ShapeDtypeStruct((b, t, f), BF16),
        compiler_params=_cparams(("arbitrary", "arbitrary", "arbitrary")),
        name="ffn_up",
    )(h, w_gate, w_up)


def _ffn_down_kernel(h_ref, w_ref, x_ref, gate_ref, npost_ref, o_ref, z_scr):
    j = pl.program_id(2)
    n_j = z_scr.shape[0]
    z_scr[j] = _mm(h_ref[0], w_ref[...])

    @pl.when(j == n_j - 1)
    def _():
        z = jnp.concatenate([z_scr[t] for t in range(n_j)], 1)
        z = z * lax.rsqrt(jnp.mean(z * z, axis=-1, keepdims=True) + NORM_EPS) * npost_ref[...]
        o_ref[0] = x_ref[0] + gate_ref[0] * z


def _ffn_down(h, w_down, x1, gate, n_post, tm=512, tn=512):
    b, t, f = h.shape
    d = w_down.shape[1]
    return pl.pallas_call(
        _ffn_down_kernel,
        grid=(b, t // tm, d // tn),
        in_specs=[pl.BlockSpec((1, tm, f), lambda bi, i, j: (bi, i, 0)),
                  pl.BlockSpec((f, tn), lambda bi, i, j: (0, j)),
                  pl.BlockSpec((1, tm, d), lambda bi, i, j: (bi, i, 0)),
                  pl.BlockSpec((1, 1, d), lambda bi, i, j: (bi, 0, 0)),
                  pl.BlockSpec((1, d), lambda bi, i, j: (0, 0))],
        out_specs=pl.BlockSpec((1, tm, d), lambda bi, i, j: (bi, i, 0)),
        out_shape=jax.ShapeDtypeStruct((b, t, d), F32),
        scratch_shapes=[pltpu.VMEM((d // tn, tm, tn), F32)],
        compiler_params=_cparams(("arbitrary", "arbitrary", "arbitrary")),
        name="ffn_down",
    )(h, w_down, x1, gate, n_post)


def _pad_cols(w, n):
    return jnp.pad(w, ((0, 0), (0, n - w.shape[1])))


def kernel(x, c, ctx, c_ctx, ada_w, ada_b, norm_pre_mix, norm_post_mix, norm_pre_ffn, norm_post_ffn, w_in, shift_mu, rwkv_w0, rwkv_w2, rwkv_a0, rwkv_a2, rwkv_g2, rwkv_k_k, rwkv_k_a, rwkv_r_k, rwkv_ln_g, rwkv_ln_b, w_rwkv_up, gla_alpha_up, gla_alpha_b, gla_norm_g, w_gla_up, w_out, ffn_w_gate, ffn_w_up, ffn_w_down):
    assert ada_w.shape[0] == 1, "single trunk layer"
    bsz, seq, d = x.shape
    n_ctx = ctx.shape[1]
    prec = BF16

    cvecs = jnp.concatenate([c, c_ctx[None, :], jnp.zeros((8 - bsz - 1, d), F32)], 0)
    mod = _modulation(cvecs, ada_w[0], ada_b[0])
    mod_x = mod[:bsz].reshape(bsz, 6, 1, d)
    shx1, scx1, gx1, shx2, scx2, gx2 = (mod_x[:, i] for i in range(6))
    mod_c = jnp.broadcast_to(mod[bsz].reshape(1, 6, 1, d), (bsz, 6, 1, d))
    shc1, scc1 = mod_c[:, 0], mod_c[:, 1]

    w_all = w_in[0]
    mix_in = RWKV_IN + GLA_IN
    w_rwkv = _pad_cols(w_all[:, :RWKV_IN], RWKV_PAD).astype(BF16)
    w_gla = _pad_cols(w_all[:, RWKV_IN:mix_in], GLA_PAD).astype(BF16)
    w_gate = w_all[:, mix_in:].astype(BF16)
    mu = _pad_cols(shift_mu, RWKV_PAD)
    n_pre = norm_pre_mix

    hx = _prenorm(x, shx1, scx1, n_pre)
    hc = _prenorm(ctx, shc1, scc1, n_pre)
    px_rwkv = _project_shift(hx, w_rwkv, mu, 1024, 512, "proj_rwkv")
    px_gla = _project_colmajor(hx, w_gla, 16, 512, "proj_gla")
    px_gate = _project(hx, w_gate, 1024, 1024, "proj_gate", gate_out=True)
    pc_rwkv = _project_shift(hc, w_rwkv, mu, n_ctx, 512, "proj_rwkv_ctx")
    pc_gla = _project(hc, w_gla, n_ctx, 512, "proj_gla_ctx")

    y_f, y_b = _rwkv_scan(px_rwkv, pc_rwkv, rwkv_w2[0], rwkv_w0[0][:, None, :], rwkv_a2[0], rwkv_a0[0][:, None, :],
                          rwkv_k_k, rwkv_k_a, prec)
    ya = _rwkv_post(y_f, y_b, px_rwkv, rwkv_a2[0], rwkv_a0[0][:, None, :], rwkv_g2[0], rwkv_k_a,
                    rwkv_r_k.reshape(1, RWKV_WIDTH), rwkv_ln_g, rwkv_ln_b)

    o_f, o_b = _gla_scan(px_gla, pc_gla, gla_alpha_up[0], gla_alpha_b[0][:, None, :], prec)
    yb = _gla_post(o_f, o_b, px_gla, gla_norm_g)

    m = _merge(ya, yb, w_rwkv_up[0].astype(BF16), w_gla_up[0].astype(BF16), px_gate)
    x1, h2 = _mix_out(m, w_out[0].astype(BF16), x, gx1, norm_post_mix, norm_pre_ffn, shx2, scx2)
    hf = _ffn_up(h2, ffn_w_gate[0].astype(BF16), ffn_w_up[0].astype(BF16))
    return _ffn_down(hf, ffn_w_down[0].astype(BF16), x1, gx2, norm_post_ffn)
```

```python
import functools

import jax
import jax.numpy as jnp
from jax import lax
from jax.experimental import pallas as pl
from jax.experimental.pallas import tpu as pltpu

F32 = jnp.float32
BF16 = jnp.bfloat16
HIGHEST = lax.Precision.HIGHEST

LANES = 128
VMEM_LIMIT_BYTES = 56 * 1024 * 1024

GRID_W = 64
CHUNK = 64
RWKV_HEADS, RWKV_HEAD_DIM = 16, 64
RWKV_WIDTH = RWKV_HEADS * RWKV_HEAD_DIM
DECAY_LORA = ICL_LORA = 96
GATE_LORA = 64
RWKV_GN_EPS = 64e-5
GLA_HEADS, GLA_KEY_DIM, GLA_VAL_DIM = 4, 128, 256
GLA_QK_WIDTH = GLA_HEADS * GLA_KEY_DIM
GLA_V_WIDTH = GLA_HEADS * GLA_VAL_DIM
GLA_GATE_LORA = 16
GLA_TAU = 16.0
GLA_NORM_EPS = 1e-5
GLA_SUB = 8
NORM_EPS = 1e-6

RWKV_IN = 3 * RWKV_WIDTH + 2 * DECAY_LORA + 2 * ICL_LORA + GATE_LORA
RWKV_PAD = 3584
OFF_WD = 3 * RWKV_WIDTH
OFF_AD = OFF_WD + 2 * DECAY_LORA
OFF_GD = OFF_AD + 2 * ICL_LORA
GLA_IN = 2 * GLA_QK_WIDTH + 2 * GLA_V_WIDTH + 2 * GLA_GATE_LORA
GLA_BLK = 512
GLA_PAD = 7 * GLA_BLK
OFF_GLA_AD = 6 * GLA_BLK

PAIR = 2 * RWKV_HEAD_DIM
N_PAIRS = RWKV_HEADS // 2


def _cparams(semantics):
    return pltpu.CompilerParams(dimension_semantics=semantics, vmem_limit_bytes=VMEM_LIMIT_BYTES)


def _sigmoid(z):
    return 1.0 / (1.0 + jnp.exp(-z))


def _softplus(z):
    return jnp.maximum(z, 0.0) + jnp.log(1.0 + jnp.exp(-jnp.abs(z)))


def _dot(a, b, dims, precision):
    if precision is BF16:
        a, b, precision = a.astype(BF16), b.astype(BF16), None
    return lax.dot_general(a, b, (dims, ((), ())), precision=precision, preferred_element_type=F32)


def _mm(a, b, precision=None):
    return _dot(a, b, ((1,), (0,)), precision)


def _mm_nt(a, b, precision=None):
    return _dot(a, b, ((1,), (1,)), precision)


def _mm_tn(a, b, precision=None):
    return _dot(a, b, ((0,), (0,)), precision)


def _split2(z):
    hi = z.astype(BF16)
    return hi, (z - hi.astype(F32)).astype(BF16)


def _mm_split(a, b):
    ah, al = _split2(a)
    bh, bl = _split2(b)
    return (_mm(al, bh) + _mm(ah, bl)) + _mm(ah, bh)


def _mm_exact_lhs(e, b):
    hi = b.astype(BF16)
    r = b - hi.astype(F32)
    mid = r.astype(BF16)
    lo = (r - mid.astype(F32)).astype(BF16)
    e = e.astype(BF16)
    return (_mm(e, lo) + _mm(e, mid)) + _mm(e, hi)


def _mod_kernel(c_ref, w_ref, b_ref, o_ref):
    s = c_ref[...]
    s = s * _sigmoid(s)
    o_ref[...] = _mm_split(s, w_ref[...]) + b_ref[...]


def _modulation(cvecs, ada_w, ada_b, tn=1024):
    m, d = cvecs.shape
    n = ada_w.shape[1]
    return pl.pallas_call(
        _mod_kernel,
        grid=(n // tn,),
        in_specs=[pl.BlockSpec((m, d), lambda j: (0, 0)),
                  pl.BlockSpec((d, tn), lambda j: (0, j)),
                  pl.BlockSpec((1, tn), lambda j: (0, j))],
        out_specs=pl.BlockSpec((m, tn), lambda j: (0, j)),
        out_shape=jax.ShapeDtypeStruct((m, n), F32),
        compiler_params=_cparams(("arbitrary",)),
        name="adaln_mod",
    )(cvecs, ada_w, ada_b.reshape(1, n))


def _prenorm_kernel(x_ref, sh_ref, sc_ref, g_ref, o_ref):
    x = x_ref[0]
    ms = jnp.mean(x * x, axis=-1, keepdims=True)
    y = x * lax.rsqrt(ms + NORM_EPS) * g_ref[...]
    o_ref[0] = (y * (1.0 + sc_ref[0]) + sh_ref[0]).astype(o_ref.dtype)


def _prenorm(x, shift, scale, gain, tm=256):
    b, t, d = x.shape
    row = lambda bi, i: (bi, i, 0)
    per_b = lambda bi, i: (bi, 0, 0)
    return pl.pallas_call(
        _prenorm_kernel,
        grid=(b, t // tm),
        in_specs=[pl.BlockSpec((1, tm, d), row), pl.BlockSpec((1, 1, d), per_b), pl.BlockSpec((1, 1, d), per_b),
                  pl.BlockSpec((1, d), lambda bi, i: (0, 0))],
        out_specs=pl.BlockSpec((1, tm, d), row),
        out_shape=jax.ShapeDtypeStruct((b, t, d), BF16),
        compiler_params=_cparams(("arbitrary", "arbitrary")),
        name="prenorm",
    )(x, shift, scale, gain)


def _proj_kernel(gate_out, h_ref, w_ref, o_ref):
    p = _mm(h_ref[0], w_ref[...])
    o_ref[0] = (_sigmoid(p) if gate_out else p).astype(o_ref.dtype)


def _project(h, w, tm, tn, name, gate_out=False):
    b, t, d = h.shape
    n = w.shape[1]
    return pl.pallas_call(
        functools.partial(_proj_kernel, gate_out),
        grid=(b, t // tm, n // tn),
        in_specs=[pl.BlockSpec((1, tm, d), lambda bi, i, j: (bi, i, 0)),
                  pl.BlockSpec((d, tn), lambda bi, i, j: (0, j))],
        out_specs=pl.BlockSpec((1, tm, tn), lambda bi, i, j: (bi, i, j)),
        out_shape=jax.ShapeDtypeStruct((b, t, n), BF16 if gate_out else F32),
        compiler_params=_cparams(("arbitrary", "arbitrary", "arbitrary")),
        name=name,
    )(h, w)


def _colmajor_perm(n_rows, n_cols):
    dst = jnp.arange(n_rows * n_cols)
    src = (dst % n_rows) * n_cols + dst // n_rows
    return (src[:, None] == dst[None, :]).astype(BF16)


def _proj_cm_kernel(h_ref, perm_ref, w_ref, o_ref, h_scr):
    @pl.when(pl.program_id(2) == 0)
    def _():
        n_r, n_c, d = h_ref.shape[1:]
        for cb in range(d // GLA_BLK):
            cs = slice(cb * GLA_BLK, (cb + 1) * GLA_BLK)
            hr = h_ref[0, :, :, cs].astype(F32).reshape(n_r * n_c, GLA_BLK).astype(BF16)
            h_scr[:, cs] = _mm(perm_ref[...], hr).astype(BF16)

    o_ref[0] = _mm(h_scr[...], w_ref[...])


def _project_colmajor(h, w, n_cols, tn, name):
    b, t, d = h.shape
    n = w.shape[1]
    n_r = t // GRID_W
    tm = n_r * n_cols
    h4 = h.reshape(b, n_r, GRID_W, d)
    perm = _colmajor_perm(n_r, n_cols)
    return pl.pallas_call(
        _proj_cm_kernel,
        grid=(b, GRID_W // n_cols, n // tn),
        in_specs=[pl.BlockSpec((1, n_r, n_cols, d), lambda bi, i, j: (bi, 0, i, 0)),
                  pl.BlockSpec((tm, tm), lambda bi, i, j: (0, 0)),
                  pl.BlockSpec((d, tn), lambda bi, i, j: (0, j))],
        out_specs=pl.BlockSpec((1, tm, tn), lambda bi, i, j: (bi, i, j)),
        out_shape=jax.ShapeDtypeStruct((b, t, n), F32),
        scratch_shapes=[pltpu.VMEM((tm, d), BF16)],
        compiler_params=_cparams(("arbitrary", "arbitrary", "arbitrary")),
        name=name,
    )(h4, perm, w)


HALO = 16


def _proj_shift_kernel(h_ref, hp_ref, hn_ref, w_ref, mu_ref, o_ref):
    i = pl.program_id(1)
    tm = h_ref.shape[1]
    p = _mm(h_ref[0], w_ref[...])
    ph = _mm(jnp.concatenate([hp_ref[0], hn_ref[0]], 0), w_ref[...])
    before = jnp.where(i == 0, 0.0, ph[HALO - 1:HALO])
    after = jnp.where(i == pl.num_programs(1) - 1, 0.0, ph[HALO:HALO + 1])
    row = lax.broadcasted_iota(jnp.int32, p.shape, 0)
    prev = jnp.where(row == 0, before, pltpu.roll(p, 1, 0))
    nxt = jnp.where(row == tm - 1, after, pltpu.roll(p, tm - 1, 0))
    o_ref[0] = p + mu_ref[...] * (0.5 * (prev + nxt) - p)


def _project_shift(h, w, mu, tm, tn, name):
    b, t, d = h.shape
    n = w.shape[1]
    rh, nh = tm // HALO, t // HALO
    return pl.pallas_call(
        _proj_shift_kernel,
        grid=(b, t // tm, n // tn),
        in_specs=[pl.BlockSpec((1, tm, d), lambda bi, i, j: (bi, i, 0)),
                  pl.BlockSpec((1, HALO, d), lambda bi, i, j: (bi, jnp.maximum(i * rh - 1, 0), 0)),
                  pl.BlockSpec((1, HALO, d), lambda bi, i, j: (bi, jnp.minimum((i + 1) * rh, nh - 1), 0)),
                  pl.BlockSpec((d, tn), lambda bi, i, j: (0, j)),
                  pl.BlockSpec((1, tn), lambda bi, i, j: (0, j))],
        out_specs=pl.BlockSpec((1, tm, tn), lambda bi, i, j: (bi, i, j)),
        out_shape=jax.ShapeDtypeStruct((b, t, n), F32),
        compiler_params=_cparams(("arbitrary", "arbitrary", "arbitrary")),
        name=name,
    )(h, h, h, w, mu)


def _head_sum_matrix():
    r = lax.broadcasted_iota(jnp.int32, (PAIR, PAIR), 0)
    c = lax.broadcasted_iota(jnp.int32, (PAIR, PAIR), 1)
    return jnp.where((r >> 6) == (c >> 6), 1.0, 0.0).astype(BF16)


def _head_sum(z, hsum):
    hi = z.astype(BF16)
    lo = (z - hi.astype(F32)).astype(BF16)
    return _mm(hi, hsum) + _mm(lo, hsum)


def _rwkv_kernel(prec, n_ctx_chunks, plf_ref, pcf_ref, plb_ref, pcb_ref, w2_ref, w0_ref, a2_ref, a0_ref, kk_ref, ka_ref,
                 yf_ref, yb_ref, st_scr):
    @pl.when(pl.program_id(1) == 0)
    def _():
        st_scr[...] = jnp.zeros_like(st_scr)

    is_ctx = pl.program_id(1) < n_ctx_chunks
    y_refs = (yf_ref, yb_ref)
    prep = [_rwkv_prep(dr == 1, jnp.where(is_ctx, pc_ref[0], pl_ref[0]), w2_ref.at[dr], w0_ref.at[dr], a2_ref.at[dr],
                       a0_ref.at[dr], kk_ref, ka_ref)
            for dr, (pl_ref, pc_ref) in enumerate(((plf_ref, pcf_ref), (plb_ref, pcb_ref)))]

    items = [(dr, pr) for pr in range(N_PAIRS) for dr in range(2)]
    n = range(len(items))
    per_pair = lambda name: [prep[dr][name][pr] for dr, pr in items]
    per_dir = lambda name: [prep[dr][name] for dr, _ in items]
    at, bt, kt, rt_, rs, bh, kh, v_swap, v_stack, p_end = (per_pair(k) for k in (
        "at", "bt", "kt", "rt", "rs", "bh", "kh", "v_swap", "v_stack", "p_end"))
    mask_n, mask_k, incl_c = per_dir("mask_n"), per_dir("mask_k"), per_dir("incl_c")

    c2 = 2 * CHUNK
    ri = lax.broadcasted_iota(jnp.int32, (c2, PAIR), 0)
    li = lax.broadcasted_iota(jnp.int32, (c2, PAIR), 1)
    ident = ri == li
    eye = jnp.where(ident, 1.0, 0.0).astype(F32)
    lane_e = lax.broadcasted_iota(jnp.int32, (CHUNK, PAIR), 1) < RWKV_HEAD_DIM
    cat0 = lambda *z: jnp.concatenate(z, 0)

    g_e = [_mm_nt(cat0(at[i][0], rs[i][0]), cat0(bt[i], kt[i]), prec) for i in n]
    g_o = [_mm_nt(cat0(at[i][1], rs[i][1]), cat0(kt[i], bt[i]), prec) for i in n]
    g_top = [cat0(g_e[i][0:CHUNK], g_o[i][0:CHUNK]) for i in n]
    nbd = [jnp.where(mask_n[i], g_top[i], 0.0) for i in n]
    aak = [jnp.where(mask_k[i], g_top[i], 0.0) for i in n]
    rab = [jnp.where(incl_c[i], jnp.where(lane_e, g_e[i][CHUNK:c2], g_o[i][CHUNK:c2]), 0.0) for i in n]
    rak = [jnp.where(incl_c[i], jnp.where(lane_e, g_o[i][CHUNK:c2], g_e[i][CHUNK:c2]), 0.0) for i in n]

    x = [_mm(aak[i], v_swap[i], prec) for i in n]
    n2 = [_mm(z, z, prec) for z in nbd]
    y0b = [_mm(rak[i], v_swap[i], prec) for i in n]
    n4 = [_mm(z, z, prec) for z in n2]
    imn = [eye - z for z in nbd]
    p1 = [imn[i] + _mm(imn[i], n2[i], prec) for i in n]
    n8 = [_mm(z, z, prec) for z in n4]
    nb = [_mm_tn(kh[i], v_stack[i], prec) for i in n]
    n16 = [_mm(z, z, prec) for z in n8]
    p2 = [eye + n4[i] + n8[i] + _mm(n4[i], n8[i], prec) for i in n]
    n32 = [_mm(z, z, prec) for z in n16]
    p12 = [_mm(p1[i], p2[i], prec) for i in n]
    p3 = [eye + n16[i] + n32[i] + _mm(n16[i], n32[i], prec) for i in n]
    tinv = [_mm(p12[i], p3[i], prec) for i in n]
    wu = [-_mm(tinv[i], jnp.concatenate([cat0(*at[i]), x[i]], 1), prec) for i in n]
    qy = [_mm(rab[i], wu[i], prec) for i in n]
    mn = [_mm_tn(bh[i], wu[i], prec) for i in n]
    q = [rt_[i] + qy[i][:, 0:PAIR] for i in n]
    m = [jnp.where(ident, p_end[i], 0.0) + mn[i][:, 0:PAIR] for i in n]
    qm = [_mm(cat0(q[i], m[i]), st_scr[dr, pr], prec) for i, (dr, pr) in enumerate(items)]
    for i, (dr, pr) in enumerate(items):
        y_refs[dr][0, :, pr * PAIR:(pr + 1) * PAIR] = qm[i][0:CHUNK] + qy[i][:, PAIR:2 * PAIR] + y0b[i]
        st_scr[dr, pr] = qm[i][CHUNK:CHUNK + PAIR] + mn[i][:, PAIR:2 * PAIR] + nb[i]


def _rwkv_prep(rev, p, w2_ref, w0_ref, a2_ref, a0_ref, kk_ref, ka_ref):
    hd = RWKV_HEAD_DIM
    c2 = 2 * CHUNK
    r = p[:, 0:RWKV_WIDTH]
    k = p[:, RWKV_WIDTH:2 * RWKV_WIDTH]
    v = p[:, 2 * RWKV_WIDTH:3 * RWKV_WIDTH]
    d_off = DECAY_LORA if rev else 0
    wd = p[:, OFF_WD + d_off:OFF_WD + d_off + DECAY_LORA]
    ad = p[:, OFF_AD + d_off:OFF_AD + d_off + ICL_LORA]

    w_log = -_softplus(-(w0_ref[...] + _mm_split(jnp.tanh(wd), w2_ref[...]))) - 0.5
    lw = -jnp.exp(w_log)
    a = _sigmoid(a0_ref[...] + _mm(ad, a2_ref[...], BF16))
    kk_raw = k * kk_ref[...]
    kd = k * (1.0 + (a - 1.0) * ka_ref[...])

    ri = lax.broadcasted_iota(jnp.int32, (c2, PAIR), 0)
    li = lax.broadcasted_iota(jnp.int32, (c2, PAIR), 1)
    rt, lt = ri & (CHUNK - 1), li & (hd - 1)
    same_head = (ri >> 6) == (li >> 6)
    strict = (lt > rt) if rev else (lt < rt)
    rc = lax.broadcasted_iota(jnp.int32, (CHUNK, PAIR), 0)
    lc = lax.broadcasted_iota(jnp.int32, (CHUNK, PAIR), 1)
    lane_e = lc < hd
    hsum = _head_sum_matrix()

    ci = lax.broadcasted_iota(jnp.int32, (CHUNK, CHUNK), 0)
    cj = lax.broadcasted_iota(jnp.int32, (CHUNK, CHUNK), 1)
    tri = jnp.where((cj >= ci) if rev else (cj <= ci), 1.0, 0.0).astype(F32)
    cum = _mm_exact_lhs(tri, lw)
    total = cum[0:1] if rev else cum[CHUNK - 1:CHUNK]
    e_prev = jnp.exp(cum - lw)
    e_neg = jnp.exp(-cum)
    e_pos = jnp.exp(cum)
    e_rest = jnp.exp(total - cum)
    p_end = jnp.exp(total)

    def split(z):
        ze = jnp.where(lane_e, z, 0.0)
        return ze, z - ze

    sls = [slice(pr * PAIR, (pr + 1) * PAIR) for pr in range(N_PAIRS)]
    cat0 = lambda *z: jnp.concatenate(z, 0)
    kkr = [kk_raw[:, sl] for sl in sls]
    nrm2 = [_head_sum(z * z, hsum) for z in kkr]
    kk = [z / jnp.maximum(jnp.sqrt(n), 1e-12) for z, n in zip(kkr, nrm2)]
    bb = [z * a[:, sl] for z, sl in zip(kk, sls)]
    rt_ = [r[:, sl] * e_pos[:, sl] for sl in sls]
    vs = [split(v[:, sl]) for sl in sls]
    return dict(
        mask_n=jnp.logical_and(same_head, strict),
        mask_k=jnp.logical_and(jnp.logical_not(same_head), strict),
        incl_c=((lc & (hd - 1)) >= rc) if rev else ((lc & (hd - 1)) <= rc),
        at=[split(z * e_prev[:, sl]) for z, sl in zip(kk, sls)],
        bt=[z * e_neg[:, sl] for z, sl in zip(bb, sls)],
        kt=[kd[:, sl] * e_neg[:, sl] for sl in sls],
        rt=rt_,
        rs=[split(z) for z in rt_],
        bh=[cat0(*split(z * e_rest[:, sl])) for z, sl in zip(bb, sls)],
        kh=[cat0(*split(kd[:, sl] * e_rest[:, sl])) for sl in sls],
        v_swap=[cat0(vo, ve) for ve, vo in vs],
        v_stack=[cat0(ve, vo) for ve, vo in vs],
        p_end=[p_end[:, sl] for sl in sls],
    )


def _scan_order(n_ctx_chunks, n_lat):
    n_steps = n_ctx_chunks + n_lat
    lat_f = lambda i: jnp.maximum(i - n_ctx_chunks, 0)
    lat_b = lambda i: jnp.where(i < n_ctx_chunks, n_lat - 1, n_steps - 1 - i)
    ctx_f = lambda i: jnp.minimum(i, n_ctx_chunks - 1)
    ctx_b = lambda i: jnp.maximum(n_ctx_chunks - 1 - i, 0)
    return n_steps, (lat_f, lat_b), (ctx_f, ctx_b)


def _rwkv_scan(p_lat, p_ctx, w2, w0, a2, a0, k_k, k_a, prec):
    b, t, w = p_lat.shape
    n_lat = t // CHUNK
    n_ctx_chunks = p_ctx.shape[1] // CHUNK
    n_steps, lat_of, ctx_of = _scan_order(n_ctx_chunks, n_lat)
    chunk = lambda f: pl.BlockSpec((1, CHUNK, w), lambda bi, i: (bi, f(i), 0))
    out = lambda f: pl.BlockSpec((1, CHUNK, RWKV_WIDTH), lambda bi, i: (bi, f(i), 0))
    vec = lambda bi, i: (0, 0)
    vec3 = lambda bi, i: (0, 0, 0)
    y_shape = jax.ShapeDtypeStruct((b, t, RWKV_WIDTH), F32)
    return pl.pallas_call(
        functools.partial(_rwkv_kernel, prec, n_ctx_chunks),
        grid=(b, n_steps),
        in_specs=[chunk(lat_of[0]), chunk(ctx_of[0]), chunk(lat_of[1]), chunk(ctx_of[1]),
                  pl.BlockSpec((2, DECAY_LORA, RWKV_WIDTH), vec3),
                  pl.BlockSpec((2, 1, RWKV_WIDTH), vec3),
                  pl.BlockSpec((2, ICL_LORA, RWKV_WIDTH), vec3),
                  pl.BlockSpec((2, 1, RWKV_WIDTH), vec3),
                  pl.BlockSpec((1, RWKV_WIDTH), vec),
                  pl.BlockSpec((1, RWKV_WIDTH), vec)],
        out_specs=[out(lat_of[0]), out(lat_of[1])],
        out_shape=[y_shape, y_shape],
        scratch_shapes=[pltpu.VMEM((2, N_PAIRS, PAIR, PAIR), F32)],
        compiler_params=_cparams(("arbitrary", "arbitrary")),
        name="rwkv7_scan",
    )(p_lat, p_ctx, p_lat, p_ctx, w2, w0, a2, a0, k_k, k_a)


def _gla_kernel(prec, n_ctx_chunks, *refs):
    lat = (refs[0:6], refs[6:12])
    aup_ref, ab_ref, of_ref, ob_ref, st_scr = refs[12:17]
    o_refs = (of_ref, ob_ref)

    @pl.when(pl.program_id(1) == 0)
    def _():
        st_scr[...] = jnp.zeros_like(st_scr)

    is_ctx = pl.program_id(1) < n_ctx_chunks
    dk, dv, sb = GLA_KEY_DIM, GLA_VAL_DIM, GLA_SUB
    n_sb = CHUNK // sb
    prep = [_gla_prep(dr == 1, is_ctx, *lat[dr], aup_ref.at[dr], ab_ref.at[dr]) for dr in range(2)]

    items = [(dr, h) for h in range(GLA_HEADS) for dr in range(2)]
    n = range(len(items))
    revs = [dr == 1 for dr, _ in items]
    qh, kh, bh, lah, toth, vh = ([prep[dr][name][h] for dr, h in items] for name in ("q", "k", "b", "la", "tot", "v"))
    st = [st_scr[dr, h] for dr, h in items]

    o_inter = [_mm_nt(qh[i] * jnp.exp(bh[i]), st[i], prec) for i in n]
    st_new = [st[i] * jnp.exp(toth[i]) + _mm_tn(vh[i], kh[i] * jnp.exp(toth[i] - bh[i]), prec) for i in n]

    arow = lax.broadcasted_iota(jnp.int32, (CHUNK, CHUNK), 0)
    acol = lax.broadcasted_iota(jnp.int32, (CHUNK, CHUNK), 1)
    bcol = lax.broadcasted_iota(jnp.int32, (sb, CHUNK), 1)
    off_rows = [[] for _ in n]
    for blk in range(n_sb):
        rs = slice(blk * sb, (blk + 1) * sb)
        for i in n:
            rev = revs[i]
            if (blk == n_sb - 1) if rev else (blk == 0):
                off_rows[i].append(jnp.zeros((sb, CHUNK), F32))
                continue
            first = blk * sb + (sb - 1 if rev else 0)
            before = (bcol >= (blk + 1) * sb) if rev else (bcol < blk * sb)
            beta = bh[i][first:first + 1] - lah[i][first:first + 1]
            qs = qh[i][rs] * jnp.exp(bh[i][rs] - beta)
            ksc = kh[i] * jnp.exp(jnp.minimum(beta - bh[i], 0.0))
            off_rows[i].append(jnp.where(before, _mm_nt(qs, ksc, prec), 0.0))
    att = [jnp.concatenate(off_rows[i], 0) for i in n]

    in_blk = arow & (sb - 1)
    for s in range(sb):
        pick = lambda z: jnp.concatenate(
            [jnp.broadcast_to(z[blk * sb + s:blk * sb + s + 1], (sb, dk)) for blk in range(n_sb)], 0)
        on_col = acol == (arow & ~(sb - 1)) + s
        tgt = (jnp.logical_and(on_col, in_blk >= s), jnp.logical_and(on_col, in_blk <= s))
        for i in n:
            e = jnp.exp(jnp.minimum(bh[i] - pick(bh[i]), 0.0))
            col = jnp.sum(qh[i] * pick(kh[i]) * e, axis=-1, keepdims=True)
            att[i] = jnp.where(tgt[revs[i]], col, att[i])

    for i, (dr, h) in enumerate(items):
        o_refs[dr][0, :, h * dv:(h + 1) * dv] = o_inter[i] + _mm(att[i], vh[i], prec)
        st_scr[dr, h] = st_new[i]


def _gla_prep(rev, is_ctx, q_ref, k_ref, v0_ref, v1_ref, ad_ref, ctx_ref, aup_ref, ab_ref):
    pc = ctx_ref[0]
    q = jnp.where(is_ctx, pc[:, 0:GLA_BLK], q_ref[0])
    k = jnp.where(is_ctx, pc[:, GLA_BLK:2 * GLA_BLK], k_ref[0])
    v = jnp.concatenate([jnp.where(is_ctx, pc[:, 2 * GLA_BLK:3 * GLA_BLK], v0_ref[0]),
                         jnp.where(is_ctx, pc[:, 3 * GLA_BLK:4 * GLA_BLK], v1_ref[0])], 1)
    d_off = GLA_GATE_LORA if rev else 0
    ad = jnp.where(is_ctx, pc[:, OFF_GLA_AD:OFF_GLA_AD + LANES], ad_ref[0][:, 0:LANES])
    ad = ad[:, d_off:d_off + GLA_GATE_LORA]

    la = -_softplus(-(_mm_split(ad, aup_ref[...]) + ab_ref[...])) * (1.0 / GLA_TAU)
    ci = lax.broadcasted_iota(jnp.int32, (CHUNK, CHUNK), 0)
    cj = lax.broadcasted_iota(jnp.int32, (CHUNK, CHUNK), 1)
    tri = jnp.where((cj >= ci) if rev else (cj <= ci), 1.0, 0.0).astype(F32)
    cum = _mm_exact_lhs(tri, la)
    total = cum[0:1] if rev else cum[CHUNK - 1:CHUNK]
    dk, dv = GLA_KEY_DIM, GLA_VAL_DIM
    ksl = [slice(h * dk, (h + 1) * dk) for h in range(GLA_HEADS)]
    scale = GLA_KEY_DIM ** -0.5
    return dict(q=[q[:, s_] * scale for s_ in ksl], k=[k[:, s_] for s_ in ksl], b=[cum[:, s_] for s_ in ksl],
                la=[la[:, s_] for s_ in ksl], tot=[total[:, s_] for s_ in ksl],
                v=[v[:, h * dv:(h + 1) * dv] for h in range(GLA_HEADS)])


def _gla_scan(p_lat, p_ctx, alpha_up, alpha_b, prec):
    b, t, w = p_lat.shape
    assert w == GLA_PAD and t == GRID_W * CHUNK
    n_ctx_chunks = p_ctx.shape[1] // CHUNK
    n_steps, col_of, ctx_of = _scan_order(n_ctx_chunks, GRID_W)

    def blocks(dr):
        lat = lambda m: pl.BlockSpec((1, CHUNK, GLA_BLK), lambda bi, i: (bi, col_of[dr](i), m))
        return [lat(0), lat(1), lat(2), lat(3), lat(6), pl.BlockSpec((1, CHUNK, w), lambda bi, i: (bi, ctx_of[dr](i), 0))]

    out = lambda dr: pl.BlockSpec((1, CHUNK, GLA_V_WIDTH), lambda bi, i: (bi, col_of[dr](i), 0))
    vec3 = lambda bi, i: (0, 0, 0)
    o_shape = jax.ShapeDtypeStruct((b, t, GLA_V_WIDTH), F32)
    return pl.pallas_call(
        functools.partial(_gla_kernel, prec, n_ctx_chunks),
        grid=(b, n_steps),
        in_specs=blocks(0) + blocks(1) + [pl.BlockSpec((2, GLA_GATE_LORA, GLA_QK_WIDTH), vec3),
                                          pl.BlockSpec((2, 1, GLA_QK_WIDTH), vec3)],
        out_specs=[out(0), out(1)],
        out_shape=[o_shape, o_shape],
        scratch_shapes=[pltpu.VMEM((2, GLA_HEADS, GLA_VAL_DIM, GLA_KEY_DIM), F32)],
        compiler_params=_cparams(("arbitrary", "arbitrary")),
        name="gla_scan",
    )(*([p_lat] * 5 + [p_ctx]) * 2, alpha_up, alpha_b)


def _rwkv_post_kernel(yf_ref, yb_ref, pm_ref, a2_ref, a0_ref, g2_ref, ka_ref, rk_ref, lng_ref, lnb_ref, o_ref):
    p = pm_ref[0]
    r = p[:, 0:RWKV_WIDTH]
    k = p[:, RWKV_WIDTH:2 * RWKV_WIDTH]
    v = p[:, 2 * RWKV_WIDTH:3 * RWKV_WIDTH]
    ad_f = p[:, OFF_AD:OFF_AD + ICL_LORA]
    ad_b = p[:, OFF_AD + ICL_LORA:OFF_AD + 2 * ICL_LORA]
    gd = p[:, OFF_GD:OFF_GD + GATE_LORA]
    ka = ka_ref[...]
    a_f = _sigmoid(a0_ref[0] + _mm(ad_f, a2_ref[0], BF16))
    a_b = _sigmoid(a0_ref[1] + _mm(ad_b, a2_ref[1], BF16))
    kd_sum = k * (1.0 + (a_f - 1.0) * ka) + k * (1.0 + (a_b - 1.0) * ka)
    gate = _mm(_sigmoid(gd), g2_ref[...], BF16)
    rkk = r * kd_sum * rk_ref[...]
    ysum = yf_ref[0] + yb_ref[0]
    hsum = _head_sum_matrix()
    inv_n = 1.0 / RWKV_HEAD_DIM
    for pr in range(N_PAIRS):
        sl = slice(pr * PAIR, (pr + 1) * PAIR)
        ys = ysum[:, sl]
        mean = _head_sum(ys, hsum) * inv_n
        dlt = ys - mean
        var = _head_sum(dlt * dlt, hsum) * inv_n
        gn = dlt * lax.rsqrt(var + RWKV_GN_EPS) * lng_ref[:, sl] + lnb_ref[:, sl]
        bonus = _head_sum(rkk[:, sl], hsum) * v[:, sl]
        o_ref[0, :, sl] = ((gn + bonus) * gate[:, sl]).astype(o_ref.dtype)


def _rwkv_post(y_f, y_b, pmix, a2, a0, g2, k_a, r_k, ln_g, ln_b, tm=256):
    b, t, w = y_f.shape
    row = lambda bi, i: (bi, i, 0)
    vec = lambda bi, i: (0, 0)
    vec3 = lambda bi, i: (0, 0, 0)
    return pl.pallas_call(
        _rwkv_post_kernel,
        grid=(b, t // tm),
        in_specs=[pl.BlockSpec((1, tm, w), row),
                  pl.BlockSpec((1, tm, w), row),
                  pl.BlockSpec((1, tm, pmix.shape[2]), row),
                  pl.BlockSpec((2, ICL_LORA, w), vec3),
                  pl.BlockSpec((2, 1, w), vec3),
                  pl.BlockSpec((GATE_LORA, w), vec),
                  pl.BlockSpec((1, w), vec), pl.BlockSpec((1, w), vec),
                  pl.BlockSpec((1, w), vec), pl.BlockSpec((1, w), vec)],
        out_specs=pl.BlockSpec((1, tm, w), row),
        out_shape=jax.ShapeDtypeStruct((b, t, w), BF16),
        compiler_params=_cparams(("arbitrary", "arbitrary")),
        name="rwkv_post",
    )(y_f, y_b, pmix, a2, a0, g2, k_a, r_k, ln_g, ln_b)


def _gla_post_kernel(of_ref, ob_ref, g_ref, ng_ref, perm_ref, o_ref):
    n_r, n_c = o_ref.shape[1:3]
    dv = GLA_VAL_DIM
    for h in range(GLA_HEADS):
        sl = slice(h * dv, (h + 1) * dv)
        oh = of_ref[0, :, sl] + ob_ref[0, :, sl]
        oh = oh * lax.rsqrt(jnp.mean(oh * oh, axis=-1, keepdims=True) + GLA_NORM_EPS) * ng_ref[:, sl]
        gh = g_ref[0, :, sl]
        y = (oh * (gh * _sigmoid(gh))).astype(BF16)
        o_ref[0, :, :, sl] = _mm(perm_ref[...], y).reshape(n_r, n_c, dv).astype(o_ref.dtype)


def _gla_post(o_f, o_b, p_gla, norm_g, n_cols=16):
    b, t, w = o_f.shape
    n_r = t // GRID_W
    tm = n_r * n_cols
    perm_t = _colmajor_perm(n_r, n_cols).T
    row = lambda bi, i: (bi, i, 0)
    out = pl.pallas_call(
        _gla_post_kernel,
        grid=(b, GRID_W // n_cols),
        in_specs=[pl.BlockSpec((1, tm, w), row),
                  pl.BlockSpec((1, tm, w), row),
                  pl.BlockSpec((1, tm, w), lambda bi, i: (bi, i, 2)),
                  pl.BlockSpec((1, w), lambda bi, i: (0, 0)),
                  pl.BlockSpec((tm, tm), lambda bi, i: (0, 0))],
        out_specs=pl.BlockSpec((1, n_r, n_cols, w), lambda bi, i: (bi, 0, i, 0)),
        out_shape=jax.ShapeDtypeStruct((b, n_r, GRID_W, w), BF16),
        compiler_params=_cparams(("arbitrary", "arbitrary")),
        name="gla_post",
    )(o_f, o_b, p_gla, norm_g, perm_t)
    return out.reshape(b, t, w)


def _merge_kernel(ya_ref, yb_ref, wr_ref, wg_ref, ga_ref, gb_ref, o_ref, wr_scr, wg_scr):
    @pl.when(jnp.logical_and(pl.program_id(1) == 0, pl.program_id(2) == 0))
    def _():
        wr_scr[...] = wr_ref[...].astype(BF16)
        wg_scr[...] = wg_ref[...].astype(BF16)

    ma = _mm(ya_ref[0], wr_scr[...])
    mb = _mm(yb_ref[0], wg_scr[...])
    o_ref[0] = (ga_ref[0].astype(F32) * ma + gb_ref[0].astype(F32) * mb).astype(o_ref.dtype)


def _merge(ya, yb, w_r, w_g, p_gate, tm=1024, tn=1024):
    b, t, w = ya.shape
    d = w_r.shape[1]
    nj = d // tn
    return pl.pallas_call(
        _merge_kernel,
        grid=(nj, b, t // tm),
        in_specs=[pl.BlockSpec((1, tm, w), lambda j, bi, i: (bi, i, 0)),
                  pl.BlockSpec((1, tm, w), lambda j, bi, i: (bi, i, 0)),
                  pl.BlockSpec((w, tn), lambda j, bi, i: (0, j)),
                  pl.BlockSpec((w, tn), lambda j, bi, i: (0, j)),
                  pl.BlockSpec((1, tm, tn), lambda j, bi, i: (bi, i, j)),
                  pl.BlockSpec((1, tm, tn), lambda j, bi, i: (bi, i, j + nj))],
        out_specs=pl.BlockSpec((1, tm, tn), lambda j, bi, i: (bi, i, j)),
        out_shape=jax.ShapeDtypeStruct((b, t, d), BF16),
        scratch_shapes=[pltpu.VMEM((w, tn), BF16), pltpu.VMEM((w, tn), BF16)],
        compiler_params=_cparams(("arbitrary", "arbitrary", "arbitrary")),
        name="merge_branches",
    )(ya, yb, w_r, w_g, p_gate, p_gate)


def _mix_out_kernel(m_ref, w_ref, x_ref, gate_ref, npost_ref, npre_ref, sh_ref, sc_ref, x1_ref, h_ref):
    half = m_ref.shape[1] // 2
    for rs in (slice(0, half), slice(half, 2 * half)):
        z = _mm(m_ref[0, rs, :], w_ref[...])
        z = z * lax.rsqrt(jnp.mean(z * z, axis=-1, keepdims=True) + NORM_EPS) * npost_ref[...]
        x1 = x_ref[0, rs, :] + gate_ref[0] * z
        x1_ref[0, rs, :] = x1
        y = x1 * lax.rsqrt(jnp.mean(x1 * x1, axis=-1, keepdims=True) + NORM_EPS) * npre_ref[...]
        h_ref[0, rs, :] = (y * (1.0 + sc_ref[0]) + sh_ref[0]).astype(h_ref.dtype)


def _mix_out(m, w_out, x, gate, n_post, n_pre, shift, scale, tm=512):
    b, t, d = x.shape
    row = lambda bi, i: (bi, i, 0)
    per_b = lambda bi, i: (bi, 0, 0)
    vec = lambda bi, i: (0, 0)
    return pl.pallas_call(
        _mix_out_kernel,
        grid=(b, t // tm),
        in_specs=[pl.BlockSpec((1, tm, d), row),
                  pl.BlockSpec((d, d), vec),
                  pl.BlockSpec((1, tm, d), row),
                  pl.BlockSpec((1, 1, d), per_b),
                  pl.BlockSpec((1, d), vec), pl.BlockSpec((1, d), vec),
                  pl.BlockSpec((1, 1, d), per_b), pl.BlockSpec((1, 1, d), per_b)],
        out_specs=[pl.BlockSpec((1, tm, d), row), pl.BlockSpec((1, tm, d), row)],
        out_shape=[jax.ShapeDtypeStruct((b, t, d), F32), jax.ShapeDtypeStruct((b, t, d), BF16)],
        compiler_params=_cparams(("arbitrary", "arbitrary")),
        name="mix_out",
    )(m, w_out, x, gate, n_post, n_pre, shift, scale)


def _ffn_up_kernel(h_ref, wg_ref, wu_ref, o_ref, wg_scr, wu_scr):
    @pl.when(jnp.logical_and(pl.program_id(1) == 0, pl.program_id(2) == 0))
    def _():
        wg_scr[...] = wg_ref[...].astype(BF16)
        wu_scr[...] = wu_ref[...].astype(BF16)

    h = h_ref[0]
    a = _mm(h, wg_scr[...])
    u = _mm(h, wu_scr[...])
    o_ref[0] = (a * _sigmoid(a) * u).astype(o_ref.dtype)


def _ffn_up(h, w_gate, w_up, tm=1024, tn=512):
    b, t, d = h.shape
    f = w_gate.shape[1]
    return pl.pallas_call(
        _ffn_up_kernel,
        grid=(f // tn, b, t // tm),
        in_specs=[pl.BlockSpec((1, tm, d), lambda j, bi, i: (bi, i, 0)),
                  pl.BlockSpec((d, tn), lambda j, bi, i: (0, j)),
                  pl.BlockSpec((d, tn), lambda j, bi, i: (0, j))],
        out_specs=pl.BlockSpec((1, tm, tn), lambda j, bi, i: (bi, i, j)),
        out_shape=jax.ShapeDtypeStruct((b, t, f), BF16),
        scratch_shapes=[pltpu.VMEM((d, tn), BF16), pltpu.VMEM((d, tn), BF16)],
        compiler_params=_cparams(("arbitrary", "arbitrary", "arbitrary")),
        name="ffn_up",
    )(h, w_gate, w_up)


def _ffn_down_kernel(h_ref, w_ref, x_ref, gate_ref, npost_ref, o_ref, z_scr):
    j = pl.program_id(2)
    n_j = z_scr.shape[0]
    z_scr[j] = _mm(h_ref[0], w_ref[...])

    @pl.when(j == n_j - 1)
    def _():
        z = jnp.concatenate([z_scr[t] for t in range(n_j)], 1)
        z = z * lax.rsqrt(jnp.mean(z * z, axis=-1, keepdims=True) + NORM_EPS) * npost_ref[...]
        o_ref[0] = x_ref[0] + gate_ref[0] * z


def _ffn_down(h, w_down, x1, gate, n_post, tm=512, tn=512):
    b, t, f = h.shape
    d = w_down.shape[1]
    return pl.pallas_call(
        _ffn_down_kernel,
        grid=(b, t // tm, d // tn),
        in_specs=[pl.BlockSpec((1, tm, f), lambda bi, i, j: (bi, i, 0)),
                  pl.BlockSpec((f, tn), lambda bi, i, j: (0, j)),
                  pl.BlockSpec((1, tm, d), lambda bi, i, j: (bi, i, 0)),
                  pl.BlockSpec((1, 1, d), lambda bi, i, j: (bi, 0, 0)),
                  pl.BlockSpec((1, d), lambda bi, i, j: (0, 0))],
        out_specs=pl.BlockSpec((1, tm, d), lambda bi, i, j: (bi, i, 0)),
        out_shape=jax.ShapeDtypeStruct((b, t, d), F32),
        scratch_shapes=[pltpu.VMEM((d // tn, tm, tn), F32)],
        compiler_params=_cparams(("arbitrary", "arbitrary", "arbitrary")),
        name="ffn_down",
    )(h, w_down, x1, gate, n_post)


def _pad_cols(w, n):
    return jnp.pad(w, ((0, 0), (0, n - w.shape[1])))


def kernel(x, c, ctx, c_ctx, ada_w, ada_b, norm_pre_mix, norm_post_mix, norm_pre_ffn, norm_post_ffn, w_in, shift_mu, rwkv_w0, rwkv_w2, rwkv_a0, rwkv_a2, rwkv_g2, rwkv_k_k, rwkv_k_a, rwkv_r_k, rwkv_ln_g, rwkv_ln_b, w_rwkv_up, gla_alpha_up, gla_alpha_b, gla_norm_g, w_gla_up, w_out, ffn_w_gate, ffn_w_up, ffn_w_down):
    assert ada_w.shape[0] == 1, "single trunk layer"
    bsz, seq, d = x.shape
    n_ctx = ctx.shape[1]
    prec = BF16

    cvecs = jnp.concatenate([c, c_ctx[None, :], jnp.zeros((8 - bsz - 1, d), F32)], 0)
    mod = _modulation(cvecs, ada_w[0], ada_b[0])
    mod_x = mod[:bsz].reshape(bsz, 6, 1, d)
    shx1, scx1, gx1, shx2, scx2, gx2 = (mod_x[:, i] for i in range(6))
    mod_c = jnp.broadcast_to(mod[bsz].reshape(1, 6, 1, d), (bsz, 6, 1, d))
    shc1, scc1 = mod_c[:, 0], mod_c[:, 1]

    w_all = w_in[0]
    mix_in = RWKV_IN + GLA_IN
    w_rwkv = _pad_cols(w_all[:, :RWKV_IN], RWKV_PAD).astype(BF16)
    w_gla = _pad_cols(w_all[:, RWKV_IN:mix_in], GLA_PAD).astype(BF16)
    w_gate = w_all[:, mix_in:].astype(BF16)
    mu = _pad_cols(shift_mu, RWKV_PAD)
    n_pre = norm_pre_mix

    hx = _prenorm(x, shx1, scx1, n_pre)
    hc = _prenorm(ctx, shc1, scc1, n_pre)
    px_rwkv = _project_shift(hx, w_rwkv, mu, 1024, 512, "proj_rwkv")
    px_gla = _project_colmajor(hx, w_gla, 16, 512, "proj_gla")
    px_gate = _project(hx, w_gate, 1024, 1024, "proj_gate", gate_out=True)
    pc_rwkv = _project_shift(hc, w_rwkv, mu, n_ctx, 512, "proj_rwkv_ctx")
    pc_gla = _project(hc, w_gla, n_ctx, 512, "proj_gla_ctx")

    y_f, y_b = _rwkv_scan(px_rwkv, pc_rwkv, rwkv_w2[0], rwkv_w0[0][:, None, :], rwkv_a2[0], rwkv_a0[0][:, None, :],
                          rwkv_k_k, rwkv_k_a, prec)
    ya = _rwkv_post(y_f, y_b, px_rwkv, rwkv_a2[0], rwkv_a0[0][:, None, :], rwkv_g2[0], rwkv_k_a,
                    rwkv_r_k.reshape(1, RWKV_WIDTH), rwkv_ln_g, rwkv_ln_b)

    o_f, o_b = _gla_scan(px_gla, pc_gla, gla_alpha_up[0], gla_alpha_b[0][:, None, :], prec)
    yb = _gla_post(o_f, o_b, px_gla, gla_norm_g)

    m = _merge(ya, yb, w_rwkv_up[0], w_gla_up[0], px_gate)
    x1, h2 = _mix_out(m, w_out[0].astype(BF16), x, gx1, norm_post_mix, norm_pre_ffn, shx2, scx2)
    hf = _ffn_up(h2, ffn_w_gate[0], ffn_w_up[0])
    return _ffn_down(hf, ffn_w_down[0].astype(BF16), x1, gx2, norm_post_ffn)
```

```python
import functools

import jax
import jax.numpy as jnp
from jax import lax
from jax.experimental import pallas as pl
from jax.experimental.pallas import tpu as pltpu

F32 = jnp.float32
BF16 = jnp.bfloat16
HIGHEST = lax.Precision.HIGHEST

LANES = 128
VMEM_LIMIT_BYTES = 56 * 1024 * 1024

GRID_W = 64
CHUNK = 64
RWKV_HEADS, RWKV_HEAD_DIM = 16, 64
RWKV_WIDTH = RWKV_HEADS * RWKV_HEAD_DIM
DECAY_LORA = ICL_LORA = 96
GATE_LORA = 64
RWKV_GN_EPS = 64e-5
GLA_HEADS, GLA_KEY_DIM, GLA_VAL_DIM = 4, 128, 256
GLA_QK_WIDTH = GLA_HEADS * GLA_KEY_DIM
GLA_V_WIDTH = GLA_HEADS * GLA_VAL_DIM
GLA_GATE_LORA = 16
GLA_TAU = 16.0
GLA_NORM_EPS = 1e-5
GLA_SUB = 8
NORM_EPS = 1e-6

RWKV_IN = 3 * RWKV_WIDTH + 2 * DECAY_LORA + 2 * ICL_LORA + GATE_LORA
RWKV_PAD = 3584
OFF_WD = 3 * RWKV_WIDTH
OFF_AD = OFF_WD + 2 * DECAY_LORA
OFF_GD = OFF_AD + 2 * ICL_LORA
GLA_IN = 2 * GLA_QK_WIDTH + 2 * GLA_V_WIDTH + 2 * GLA_GATE_LORA
GLA_BLK = 512
GLA_PAD = 7 * GLA_BLK
OFF_GLA_AD = 6 * GLA_BLK

PAIR = 2 * RWKV_HEAD_DIM
N_PAIRS = RWKV_HEADS // 2


def _cparams(semantics):
    return pltpu.CompilerParams(dimension_semantics=semantics, vmem_limit_bytes=VMEM_LIMIT_BYTES)


def _sigmoid(z):
    return 1.0 / (1.0 + jnp.exp(-z))


def _softplus(z):
    return jnp.maximum(z, 0.0) + jnp.log(1.0 + jnp.exp(-jnp.abs(z)))


def _dot(a, b, dims, precision):
    if precision is BF16:
        a, b, precision = a.astype(BF16), b.astype(BF16), None
    return lax.dot_general(a, b, (dims, ((), ())), precision=precision, preferred_element_type=F32)


def _mm(a, b, precision=None):
    return _dot(a, b, ((1,), (0,)), precision)


def _mm_nt(a, b, precision=None):
    return _dot(a, b, ((1,), (1,)), precision)


def _mm_tn(a, b, precision=None):
    return _dot(a, b, ((0,), (0,)), precision)


def _split2(z):
    hi = z.astype(BF16)
    return hi, (z - hi.astype(F32)).astype(BF16)


def _mm_split(a, b):
    ah, al = _split2(a)
    bh, bl = _split2(b)
    return (_mm(al, bh) + _mm(ah, bl)) + _mm(ah, bh)


def _mm_exact_lhs(e, b):
    hi = b.astype(BF16)
    r = b - hi.astype(F32)
    mid = r.astype(BF16)
    lo = (r - mid.astype(F32)).astype(BF16)
    e = e.astype(BF16)
    return (_mm(e, lo) + _mm(e, mid)) + _mm(e, hi)


def _mod_kernel(c_ref, w_ref, b_ref, o_ref):
    s = c_ref[...]
    s = s * _sigmoid(s)
    o_ref[...] = _mm_split(s, w_ref[...]) + b_ref[...]


def _modulation(cvecs, ada_w, ada_b, tn=1024):
    m, d = cvecs.shape
    n = ada_w.shape[1]
    return pl.pallas_call(
        _mod_kernel,
        grid=(n // tn,),
        in_specs=[pl.BlockSpec((m, d), lambda j: (0, 0)),
                  pl.BlockSpec((d, tn), lambda j: (0, j)),
                  pl.BlockSpec((1, tn), lambda j: (0, j))],
        out_specs=pl.BlockSpec((m, tn), lambda j: (0, j)),
        out_shape=jax.ShapeDtypeStruct((m, n), F32),
        compiler_params=_cparams(("arbitrary",)),
        name="adaln_mod",
    )(cvecs, ada_w, ada_b.reshape(1, n))


PRENORM_ROWS = 64


def _prenorm_kernel(x_ref, sh_ref, sc_ref, g_ref, o_ref):
    def body(rb, carry):
        sl = pl.ds(pl.multiple_of(rb * PRENORM_ROWS, PRENORM_ROWS), PRENORM_ROWS)
        x = x_ref[0, sl, :]
        ms = jnp.mean(x * x, axis=-1, keepdims=True)
        y = x * lax.rsqrt(ms + NORM_EPS) * g_ref[...]
        o_ref[0, sl, :] = (y * (1.0 + sc_ref[0]) + sh_ref[0]).astype(o_ref.dtype)
        return carry

    lax.fori_loop(0, x_ref.shape[1] // PRENORM_ROWS, body, 0)


def _prenorm(x, shift, scale, gain, tm=256):
    b, t, d = x.shape
    row = lambda bi, i: (bi, i, 0)
    per_b = lambda bi, i: (bi, 0, 0)
    return pl.pallas_call(
        _prenorm_kernel,
        grid=(b, t // tm),
        in_specs=[pl.BlockSpec((1, tm, d), row), pl.BlockSpec((1, 1, d), per_b), pl.BlockSpec((1, 1, d), per_b),
                  pl.BlockSpec((1, d), lambda bi, i: (0, 0))],
        out_specs=pl.BlockSpec((1, tm, d), row),
        out_shape=jax.ShapeDtypeStruct((b, t, d), BF16),
        compiler_params=_cparams(("arbitrary", "arbitrary")),
        name="prenorm",
    )(x, shift, scale, gain)


def _proj_kernel(gate_out, h_ref, w_ref, o_ref):
    p = _mm(h_ref[0], w_ref[...])
    o_ref[0] = (_sigmoid(p) if gate_out else p).astype(o_ref.dtype)


def _col_block_offset(cols, tn):
    start, width = cols
    assert start % tn == 0 and width % tn == 0
    return start // tn, width


def _project(h, w, cols, tm, tn, name, gate_out=False):
    b, t, d = h.shape
    off, n = _col_block_offset(cols, tn)
    return pl.pallas_call(
        functools.partial(_proj_kernel, gate_out),
        grid=(b, t // tm, n // tn),
        in_specs=[pl.BlockSpec((1, tm, d), lambda bi, i, j: (bi, i, 0)),
                  pl.BlockSpec((d, tn), lambda bi, i, j: (0, off + j))],
        out_specs=pl.BlockSpec((1, tm, tn), lambda bi, i, j: (bi, i, j)),
        out_shape=jax.ShapeDtypeStruct((b, t, n), BF16 if gate_out else F32),
        compiler_params=_cparams(("arbitrary", "arbitrary", "arbitrary")),
        name=name,
    )(h, w)


def _colmajor_perm(n_rows, n_cols):
    dst = jnp.arange(n_rows * n_cols)
    src = (dst % n_rows) * n_cols + dst // n_rows
    return (src[:, None] == dst[None, :]).astype(BF16)


def _proj_cm_kernel(h_ref, perm_ref, w_ref, o_ref, h_scr):
    @pl.when(pl.program_id(2) == 0)
    def _():
        n_r, n_c, d = h_ref.shape[1:]
        for cb in range(d // GLA_BLK):
            cs = slice(cb * GLA_BLK, (cb + 1) * GLA_BLK)
            hr = h_ref[0, :, :, cs].astype(F32).reshape(n_r * n_c, GLA_BLK).astype(BF16)
            h_scr[:, cs] = _mm(perm_ref[...], hr).astype(BF16)

    o_ref[0] = _mm(h_scr[...], w_ref[...])


def _project_colmajor(h, w, cols, n_cols, tn, name):
    b, t, d = h.shape
    off, n = _col_block_offset(cols, tn)
    n_r = t // GRID_W
    tm = n_r * n_cols
    h4 = h.reshape(b, n_r, GRID_W, d)
    perm = _colmajor_perm(n_r, n_cols)
    return pl.pallas_call(
        _proj_cm_kernel,
        grid=(b, GRID_W // n_cols, n // tn),
        in_specs=[pl.BlockSpec((1, n_r, n_cols, d), lambda bi, i, j: (bi, 0, i, 0)),
                  pl.BlockSpec((tm, tm), lambda bi, i, j: (0, 0)),
                  pl.BlockSpec((d, tn), lambda bi, i, j: (0, off + j))],
        out_specs=pl.BlockSpec((1, tm, tn), lambda bi, i, j: (bi, i, j)),
        out_shape=jax.ShapeDtypeStruct((b, t, n), F32),
        scratch_shapes=[pltpu.VMEM((tm, d), BF16)],
        compiler_params=_cparams(("arbitrary", "arbitrary", "arbitrary")),
        name=name,
    )(h4, perm, w)


HALO = 16


def _proj_shift_kernel(h_ref, hp_ref, hn_ref, w_ref, mu_ref, o_ref):
    i = pl.program_id(1)
    tm = h_ref.shape[1]
    p = _mm(h_ref[0], w_ref[...])
    ph = _mm(jnp.concatenate([hp_ref[0], hn_ref[0]], 0), w_ref[...])
    before = jnp.where(i == 0, 0.0, ph[HALO - 1:HALO])
    after = jnp.where(i == pl.num_programs(1) - 1, 0.0, ph[HALO:HALO + 1])
    row = lax.broadcasted_iota(jnp.int32, p.shape, 0)
    prev = jnp.where(row == 0, before, pltpu.roll(p, 1, 0))
    nxt = jnp.where(row == tm - 1, after, pltpu.roll(p, tm - 1, 0))
    o_ref[0] = p + mu_ref[...] * (0.5 * (prev + nxt) - p)


def _project_shift(h, w, cols, mu, tm, tn, name):
    b, t, d = h.shape
    off, n = _col_block_offset(cols, tn)
    rh, nh = tm // HALO, t // HALO
    return pl.pallas_call(
        _proj_shift_kernel,
        grid=(b, t // tm, n // tn),
        in_specs=[pl.BlockSpec((1, tm, d), lambda bi, i, j: (bi, i, 0)),
                  pl.BlockSpec((1, HALO, d), lambda bi, i, j: (bi, jnp.maximum(i * rh - 1, 0), 0)),
                  pl.BlockSpec((1, HALO, d), lambda bi, i, j: (bi, jnp.minimum((i + 1) * rh, nh - 1), 0)),
                  pl.BlockSpec((d, tn), lambda bi, i, j: (0, off + j)),
                  pl.BlockSpec((1, tn), lambda bi, i, j: (0, j))],
        out_specs=pl.BlockSpec((1, tm, tn), lambda bi, i, j: (bi, i, j)),
        out_shape=jax.ShapeDtypeStruct((b, t, n), F32),
        compiler_params=_cparams(("arbitrary", "arbitrary", "arbitrary")),
        name=name,
    )(h, h, h, w, mu)


def _head_sum_matrix():
    r = lax.broadcasted_iota(jnp.int32, (PAIR, PAIR), 0)
    c = lax.broadcasted_iota(jnp.int32, (PAIR, PAIR), 1)
    return jnp.where((r >> 6) == (c >> 6), 1.0, 0.0).astype(BF16)


def _head_sum(z, hsum):
    hi = z.astype(BF16)
    lo = (z - hi.astype(F32)).astype(BF16)
    return _mm(hi, hsum) + _mm(lo, hsum)


def _rwkv_kernel(prec, n_ctx_chunks, plf_ref, pcf_ref, plb_ref, pcb_ref, w2_ref, w0_ref, a2_ref, a0_ref, kk_ref, ka_ref,
                 yf_ref, yb_ref, st_scr):
    @pl.when(pl.program_id(1) == 0)
    def _():
        st_scr[...] = jnp.zeros_like(st_scr)

    is_ctx = pl.program_id(1) < n_ctx_chunks
    y_refs = (yf_ref, yb_ref)
    prep = [_rwkv_prep(dr == 1, jnp.where(is_ctx, pc_ref[0], pl_ref[0]), w2_ref.at[dr], w0_ref.at[dr], a2_ref.at[dr],
                       a0_ref.at[dr], kk_ref, ka_ref)
            for dr, (pl_ref, pc_ref) in enumerate(((plf_ref, pcf_ref), (plb_ref, pcb_ref)))]

    items = [(dr, pr) for pr in range(N_PAIRS) for dr in range(2)]
    n = range(len(items))
    per_pair = lambda name: [prep[dr][name][pr] for dr, pr in items]
    per_dir = lambda name: [prep[dr][name] for dr, _ in items]
    at, bt, kt, rt_, rs, bh, kh, v_swap, v_stack, p_end = (per_pair(k) for k in (
        "at", "bt", "kt", "rt", "rs", "bh", "kh", "v_swap", "v_stack", "p_end"))
    mask_n, mask_k, incl_c = per_dir("mask_n"), per_dir("mask_k"), per_dir("incl_c")

    c2 = 2 * CHUNK
    ri = lax.broadcasted_iota(jnp.int32, (c2, PAIR), 0)
    li = lax.broadcasted_iota(jnp.int32, (c2, PAIR), 1)
    ident = ri == li
    eye = jnp.where(ident, 1.0, 0.0).astype(F32)
    lane_e = lax.broadcasted_iota(jnp.int32, (CHUNK, PAIR), 1) < RWKV_HEAD_DIM
    cat0 = lambda *z: jnp.concatenate(z, 0)

    g_e = [_mm_nt(cat0(at[i][0], rs[i][0]), cat0(bt[i], kt[i]), prec) for i in n]
    g_o = [_mm_nt(cat0(at[i][1], rs[i][1]), cat0(kt[i], bt[i]), prec) for i in n]
    g_top = [cat0(g_e[i][0:CHUNK], g_o[i][0:CHUNK]) for i in n]
    nbd = [jnp.where(mask_n[i], g_top[i], 0.0) for i in n]
    aak = [jnp.where(mask_k[i], g_top[i], 0.0) for i in n]
    rab = [jnp.where(incl_c[i], jnp.where(lane_e, g_e[i][CHUNK:c2], g_o[i][CHUNK:c2]), 0.0) for i in n]
    rak = [jnp.where(incl_c[i], jnp.where(lane_e, g_o[i][CHUNK:c2], g_e[i][CHUNK:c2]), 0.0) for i in n]

    x = [_mm(aak[i], v_swap[i], prec) for i in n]
    n2 = [_mm(z, z, prec) for z in nbd]
    y0b = [_mm(rak[i], v_swap[i], prec) for i in n]
    n4 = [_mm(z, z, prec) for z in n2]
    imn = [eye - z for z in nbd]
    p1 = [imn[i] + _mm(imn[i], n2[i], prec) for i in n]
    n8 = [_mm(z, z, prec) for z in n4]
    nb = [_mm_tn(kh[i], v_stack[i], prec) for i in n]
    n16 = [_mm(z, z, prec) for z in n8]
    p2 = [eye + n4[i] + n8[i] + _mm(n4[i], n8[i], prec) for i in n]
    n32 = [_mm(z, z, prec) for z in n16]
    p12 = [_mm(p1[i], p2[i], prec) for i in n]
    p3 = [eye + n16[i] + n32[i] + _mm(n16[i], n32[i], prec) for i in n]
    tinv = [_mm(p12[i], p3[i], prec) for i in n]
    wu = [-_mm(tinv[i], jnp.concatenate([cat0(*at[i]), x[i]], 1), prec) for i in n]
    qy = [_mm(rab[i], wu[i], prec) for i in n]
    mn = [_mm_tn(bh[i], wu[i], prec) for i in n]
    q = [rt_[i] + qy[i][:, 0:PAIR] for i in n]
    m = [jnp.where(ident, p_end[i], 0.0) + mn[i][:, 0:PAIR] for i in n]
    qm = [_mm(cat0(q[i], m[i]), st_scr[dr, pr], prec) for i, (dr, pr) in enumerate(items)]
    for i, (dr, pr) in enumerate(items):
        y_refs[dr][0, :, pr * PAIR:(pr + 1) * PAIR] = qm[i][0:CHUNK] + qy[i][:, PAIR:2 * PAIR] + y0b[i]
        st_scr[dr, pr] = qm[i][CHUNK:CHUNK + PAIR] + mn[i][:, PAIR:2 * PAIR] + nb[i]


def _rwkv_prep(rev, p, w2_ref, w0_ref, a2_ref, a0_ref, kk_ref, ka_ref):
    hd = RWKV_HEAD_DIM
    c2 = 2 * CHUNK
    r = p[:, 0:RWKV_WIDTH]
    k = p[:, RWKV_WIDTH:2 * RWKV_WIDTH]
    v = p[:, 2 * RWKV_WIDTH:3 * RWKV_WIDTH]
    d_off = DECAY_LORA if rev else 0
    wd = p[:, OFF_WD + d_off:OFF_WD + d_off + DECAY_LORA]
    ad = p[:, OFF_AD + d_off:OFF_AD + d_off + ICL_LORA]

    w_log = -_softplus(-(w0_ref[...] + _mm_split(jnp.tanh(wd), w2_ref[...]))) - 0.5
    lw = -jnp.exp(w_log)
    a = _sigmoid(a0_ref[...] + _mm(ad, a2_ref[...], BF16))
    kk_raw = k * kk_ref[...]
    kd = k * (1.0 + (a - 1.0) * ka_ref[...])

    ri = lax.broadcasted_iota(jnp.int32, (c2, PAIR), 0)
    li = lax.broadcasted_iota(jnp.int32, (c2, PAIR), 1)
    rt, lt = ri & (CHUNK - 1), li & (hd - 1)
    same_head = (ri >> 6) == (li >> 6)
    strict = (lt > rt) if rev else (lt < rt)
    rc = lax.broadcasted_iota(jnp.int32, (CHUNK, PAIR), 0)
    lc = lax.broadcasted_iota(jnp.int32, (CHUNK, PAIR), 1)
    lane_e = lc < hd
    hsum = _head_sum_matrix()

    ci = lax.broadcasted_iota(jnp.int32, (CHUNK, CHUNK), 0)
    cj = lax.broadcasted_iota(jnp.int32, (CHUNK, CHUNK), 1)
    tri = jnp.where((cj >= ci) if rev else (cj <= ci), 1.0, 0.0).astype(F32)
    cum = _mm_exact_lhs(tri, lw)
    total = cum[0:1] if rev else cum[CHUNK - 1:CHUNK]
    e_prev = jnp.exp(cum - lw)
    e_neg = jnp.exp(-cum)
    e_pos = jnp.exp(cum)
    e_rest = jnp.exp(total - cum)
    p_end = jnp.exp(total)

    def split(z):
        ze = jnp.where(lane_e, z, 0.0)
        return ze, z - ze

    sls = [slice(pr * PAIR, (pr + 1) * PAIR) for pr in range(N_PAIRS)]
    cat0 = lambda *z: jnp.concatenate(z, 0)
    kkr = [kk_raw[:, sl] for sl in sls]
    nrm2 = [_head_sum(z * z, hsum) for z in kkr]
    kk = [z / jnp.maximum(jnp.sqrt(n), 1e-12) for z, n in zip(kkr, nrm2)]
    bb = [z * a[:, sl] for z, sl in zip(kk, sls)]
    rt_ = [r[:, sl] * e_pos[:, sl] for sl in sls]
    vs = [split(v[:, sl]) for sl in sls]
    return dict(
        mask_n=jnp.logical_and(same_head, strict),
        mask_k=jnp.logical_and(jnp.logical_not(same_head), strict),
        incl_c=((lc & (hd - 1)) >= rc) if rev else ((lc & (hd - 1)) <= rc),
        at=[split(z * e_prev[:, sl]) for z, sl in zip(kk, sls)],
        bt=[z * e_neg[:, sl] for z, sl in zip(bb, sls)],
        kt=[kd[:, sl] * e_neg[:, sl] for sl in sls],
        rt=rt_,
        rs=[split(z) for z in rt_],
        bh=[cat0(*split(z * e_rest[:, sl])) for z, sl in zip(bb, sls)],
        kh=[cat0(*split(kd[:, sl] * e_rest[:, sl])) for sl in sls],
        v_swap=[cat0(vo, ve) for ve, vo in vs],
        v_stack=[cat0(ve, vo) for ve, vo in vs],
        p_end=[p_end[:, sl] for sl in sls],
    )


def _scan_order(n_ctx_chunks, n_lat):
    n_steps = n_ctx_chunks + n_lat
    lat_f = lambda i: jnp.maximum(i - n_ctx_chunks, 0)
    lat_b = lambda i: jnp.where(i < n_ctx_chunks, n_lat - 1, n_steps - 1 - i)
    ctx_f = lambda i: jnp.minimum(i, n_ctx_chunks - 1)
    ctx_b = lambda i: jnp.maximum(n_ctx_chunks - 1 - i, 0)
    return n_steps, (lat_f, lat_b), (ctx_f, ctx_b)


def _rwkv_scan(p_lat, p_ctx, w2, w0, a2, a0, k_k, k_a, prec):
    b, t, w = p_lat.shape
    n_lat = t // CHUNK
    n_ctx_chunks = p_ctx.shape[1] // CHUNK
    n_steps, lat_of, ctx_of = _scan_order(n_ctx_chunks, n_lat)
    chunk = lambda f: pl.BlockSpec((1, CHUNK, w), lambda bi, i: (bi, f(i), 0))
    out = lambda f: pl.BlockSpec((1, CHUNK, RWKV_WIDTH), lambda bi, i: (bi, f(i), 0))
    vec = lambda bi, i: (0, 0)
    vec3 = lambda bi, i: (0, 0, 0)
    y_shape = jax.ShapeDtypeStruct((b, t, RWKV_WIDTH), F32)
    return pl.pallas_call(
        functools.partial(_rwkv_kernel, prec, n_ctx_chunks),
        grid=(b, n_steps),
        in_specs=[chunk(lat_of[0]), chunk(ctx_of[0]), chunk(lat_of[1]), chunk(ctx_of[1]),
                  pl.BlockSpec((2, DECAY_LORA, RWKV_WIDTH), vec3),
                  pl.BlockSpec((2, 1, RWKV_WIDTH), vec3),
                  pl.BlockSpec((2, ICL_LORA, RWKV_WIDTH), vec3),
                  pl.BlockSpec((2, 1, RWKV_WIDTH), vec3),
                  pl.BlockSpec((1, RWKV_WIDTH), vec),
                  pl.BlockSpec((1, RWKV_WIDTH), vec)],
        out_specs=[out(lat_of[0]), out(lat_of[1])],
        out_shape=[y_shape, y_shape],
        scratch_shapes=[pltpu.VMEM((2, N_PAIRS, PAIR, PAIR), F32)],
        compiler_params=_cparams(("arbitrary", "arbitrary")),
        name="rwkv7_scan",
    )(p_lat, p_ctx, p_lat, p_ctx, w2, w0, a2, a0, k_k, k_a)


def _gla_kernel(prec, n_ctx_chunks, *refs):
    lat = (refs[0:6], refs[6:12])
    aup_ref, ab_ref, of_ref, ob_ref, st_scr = refs[12:17]
    o_refs = (of_ref, ob_ref)

    @pl.when(pl.program_id(1) == 0)
    def _():
        st_scr[...] = jnp.zeros_like(st_scr)

    is_ctx = pl.program_id(1) < n_ctx_chunks
    dk, dv, sb = GLA_KEY_DIM, GLA_VAL_DIM, GLA_SUB
    n_sb = CHUNK // sb
    prep = [_gla_prep(dr == 1, is_ctx, *lat[dr], aup_ref.at[dr], ab_ref.at[dr]) for dr in range(2)]

    items = [(dr, h) for h in range(GLA_HEADS) for dr in range(2)]
    n = range(len(items))
    revs = [dr == 1 for dr, _ in items]
    qh, kh, bh, lah, toth, vh = ([prep[dr][name][h] for dr, h in items] for name in ("q", "k", "b", "la", "tot", "v"))
    st = [st_scr[dr, h] for dr, h in items]

    o_inter = [_mm_nt(qh[i] * jnp.exp(bh[i]), st[i], prec) for i in n]
    st_new = [st[i] * jnp.exp(toth[i]) + _mm_tn(vh[i], kh[i] * jnp.exp(toth[i] - bh[i]), prec) for i in n]

    arow = lax.broadcasted_iota(jnp.int32, (CHUNK, CHUNK), 0)
    acol = lax.broadcasted_iota(jnp.int32, (CHUNK, CHUNK), 1)
    bcol = lax.broadcasted_iota(jnp.int32, (sb, CHUNK), 1)
    off_rows = [[] for _ in n]
    for blk in range(n_sb):
        rs = slice(blk * sb, (blk + 1) * sb)
        for i in n:
            rev = revs[i]
            if (blk == n_sb - 1) if rev else (blk == 0):
                off_rows[i].append(jnp.zeros((sb, CHUNK), F32))
                continue
            first = blk * sb + (sb - 1 if rev else 0)
            before = (bcol >= (blk + 1) * sb) if rev else (bcol < blk * sb)
            beta = bh[i][first:first + 1] - lah[i][first:first + 1]
            qs = qh[i][rs] * jnp.exp(bh[i][rs] - beta)
            ksc = kh[i] * jnp.exp(jnp.minimum(beta - bh[i], 0.0))
            off_rows[i].append(jnp.where(before, _mm_nt(qs, ksc, prec), 0.0))
    att = [jnp.concatenate(off_rows[i], 0) for i in n]

    in_blk = arow & (sb - 1)
    for s in range(sb):
        pick = lambda z: jnp.concatenate(
            [jnp.broadcast_to(z[blk * sb + s:blk * sb + s + 1], (sb, dk)) for blk in range(n_sb)], 0)
        on_col = acol == (arow & ~(sb - 1)) + s
        tgt = (jnp.logical_and(on_col, in_blk >= s), jnp.logical_and(on_col, in_blk <= s))
        for i in n:
            e = jnp.exp(jnp.minimum(bh[i] - pick(bh[i]), 0.0))
            col = jnp.sum(qh[i] * pick(kh[i]) * e, axis=-1, keepdims=True)
            att[i] = jnp.where(tgt[revs[i]], col, att[i])

    for i, (dr, h) in enumerate(items):
        o_refs[dr][0, :, h * dv:(h + 1) * dv] = o_inter[i] + _mm(att[i], vh[i], prec)
        st_scr[dr, h] = st_new[i]


def _gla_prep(rev, is_ctx, q_ref, k_ref, v0_ref, v1_ref, ad_ref, ctx_ref, aup_ref, ab_ref):
    pc = ctx_ref[0]
    q = jnp.where(is_ctx, pc[:, 0:GLA_BLK], q_ref[0])
    k = jnp.where(is_ctx, pc[:, GLA_BLK:2 * GLA_BLK], k_ref[0])
    v = jnp.concatenate([jnp.where(is_ctx, pc[:, 2 * GLA_BLK:3 * GLA_BLK], v0_ref[0]),
                         jnp.where(is_ctx, pc[:, 3 * GLA_BLK:4 * GLA_BLK], v1_ref[0])], 1)
    d_off = GLA_GATE_LORA if rev else 0
    ad = jnp.where(is_ctx, pc[:, OFF_GLA_AD:OFF_GLA_AD + LANES], ad_ref[0][:, 0:LANES])
    ad = ad[:, d_off:d_off + GLA_GATE_LORA]

    la = -_softplus(-(_mm_split(ad, aup_ref[...]) + ab_ref[...])) * (1.0 / GLA_TAU)
    ci = lax.broadcasted_iota(jnp.int32, (CHUNK, CHUNK), 0)
    cj = lax.broadcasted_iota(jnp.int32, (CHUNK, CHUNK), 1)
    tri = jnp.where((cj >= ci) if rev else (cj <= ci), 1.0, 0.0).astype(F32)
    cum = _mm_exact_lhs(tri, la)
    total = cum[0:1] if rev else cum[CHUNK - 1:CHUNK]
    dk, dv = GLA_KEY_DIM, GLA_VAL_DIM
    ksl = [slice(h * dk, (h + 1) * dk) for h in range(GLA_HEADS)]
    scale = GLA_KEY_DIM ** -0.5
    return dict(q=[q[:, s_] * scale for s_ in ksl], k=[k[:, s_] for s_ in ksl], b=[cum[:, s_] for s_ in ksl],
                la=[la[:, s_] for s_ in ksl], tot=[total[:, s_] for s_ in ksl],
                v=[v[:, h * dv:(h + 1) * dv] for h in range(GLA_HEADS)])


def _gla_scan(p_lat, p_ctx, alpha_up, alpha_b, prec):
    b, t, w = p_lat.shape
    assert w == GLA_PAD and t == GRID_W * CHUNK
    n_ctx_chunks = p_ctx.shape[1] // CHUNK
    n_steps, col_of, ctx_of = _scan_order(n_ctx_chunks, GRID_W)

    def blocks(dr):
        lat = lambda m: pl.BlockSpec((1, CHUNK, GLA_BLK), lambda bi, i: (bi, col_of[dr](i), m))
        return [lat(0), lat(1), lat(2), lat(3), lat(6), pl.BlockSpec((1, CHUNK, w), lambda bi, i: (bi, ctx_of[dr](i), 0))]

    out = lambda dr: pl.BlockSpec((1, CHUNK, GLA_V_WIDTH), lambda bi, i: (bi, col_of[dr](i), 0))
    vec3 = lambda bi, i: (0, 0, 0)
    o_shape = jax.ShapeDtypeStruct((b, t, GLA_V_WIDTH), F32)
    return pl.pallas_call(
        functools.partial(_gla_kernel, prec, n_ctx_chunks),
        grid=(b, n_steps),
        in_specs=blocks(0) + blocks(1) + [pl.BlockSpec((2, GLA_GATE_LORA, GLA_QK_WIDTH), vec3),
                                          pl.BlockSpec((2, 1, GLA_QK_WIDTH), vec3)],
        out_specs=[out(0), out(1)],
        out_shape=[o_shape, o_shape],
        scratch_shapes=[pltpu.VMEM((2, GLA_HEADS, GLA_VAL_DIM, GLA_KEY_DIM), F32)],
        compiler_params=_cparams(("arbitrary", "arbitrary")),
        name="gla_scan",
    )(*([p_lat] * 5 + [p_ctx]) * 2, alpha_up, alpha_b)


def _rwkv_post_kernel(yf_ref, yb_ref, pm_ref, a2_ref, a0_ref, g2_ref, ka_ref, rk_ref, lng_ref, lnb_ref, o_ref):
    p = pm_ref[0]
    r = p[:, 0:RWKV_WIDTH]
    k = p[:, RWKV_WIDTH:2 * RWKV_WIDTH]
    v = p[:, 2 * RWKV_WIDTH:3 * RWKV_WIDTH]
    ad_f = p[:, OFF_AD:OFF_AD + ICL_LORA]
    ad_b = p[:, OFF_AD + ICL_LORA:OFF_AD + 2 * ICL_LORA]
    gd = p[:, OFF_GD:OFF_GD + GATE_LORA]
    ka = ka_ref[...]
    a_f = _sigmoid(a0_ref[0] + _mm(ad_f, a2_ref[0], BF16))
    a_b = _sigmoid(a0_ref[1] + _mm(ad_b, a2_ref[1], BF16))
    kd_sum = k * (1.0 + (a_f - 1.0) * ka) + k * (1.0 + (a_b - 1.0) * ka)
    gate = _mm(_sigmoid(gd), g2_ref[...], BF16)
    rkk = r * kd_sum * rk_ref[...]
    ysum = yf_ref[0] + yb_ref[0]
    hsum = _head_sum_matrix()
    inv_n = 1.0 / RWKV_HEAD_DIM
    for pr in range(N_PAIRS):
        sl = slice(pr * PAIR, (pr + 1) * PAIR)
        ys = ysum[:, sl]
        mean = _head_sum(ys, hsum) * inv_n
        dlt = ys - mean
        var = _head_sum(dlt * dlt, hsum) * inv_n
        gn = dlt * lax.rsqrt(var + RWKV_GN_EPS) * lng_ref[:, sl] + lnb_ref[:, sl]
        bonus = _head_sum(rkk[:, sl], hsum) * v[:, sl]
        o_ref[0, :, sl] = ((gn + bonus) * gate[:, sl]).astype(o_ref.dtype)


def _rwkv_post(y_f, y_b, pmix, a2, a0, g2, k_a, r_k, ln_g, ln_b, tm=256):
    b, t, w = y_f.shape
    row = lambda bi, i: (bi, i, 0)
    vec = lambda bi, i: (0, 0)
    vec3 = lambda bi, i: (0, 0, 0)
    return pl.pallas_call(
        _rwkv_post_kernel,
        grid=(b, t // tm),
        in_specs=[pl.BlockSpec((1, tm, w), row),
                  pl.BlockSpec((1, tm, w), row),
                  pl.BlockSpec((1, tm, pmix.shape[2]), row),
                  pl.BlockSpec((2, ICL_LORA, w), vec3),
                  pl.BlockSpec((2, 1, w), vec3),
                  pl.BlockSpec((GATE_LORA, w), vec),
                  pl.BlockSpec((1, w), vec), pl.BlockSpec((1, w), vec),
                  pl.BlockSpec((1, w), vec), pl.BlockSpec((1, w), vec)],
        out_specs=pl.BlockSpec((1, tm, w), row),
        out_shape=jax.ShapeDtypeStruct((b, t, w), BF16),
        compiler_params=_cparams(("arbitrary", "arbitrary")),
        name="rwkv_post",
    )(y_f, y_b, pmix, a2, a0, g2, k_a, r_k, ln_g, ln_b)


def _gla_post_kernel(of_ref, ob_ref, g_ref, ng_ref, perm_ref, o_ref):
    n_r, n_c = o_ref.shape[1:3]
    dv = GLA_VAL_DIM
    for h in range(GLA_HEADS):
        sl = slice(h * dv, (h + 1) * dv)
        oh = of_ref[0, :, sl] + ob_ref[0, :, sl]
        oh = oh * lax.rsqrt(jnp.mean(oh * oh, axis=-1, keepdims=True) + GLA_NORM_EPS) * ng_ref[:, sl]
        gh = g_ref[0, :, sl]
        y = (oh * (gh * _sigmoid(gh))).astype(BF16)
        o_ref[0, :, :, sl] = _mm(perm_ref[...], y).reshape(n_r, n_c, dv).astype(o_ref.dtype)


def _gla_post(o_f, o_b, p_gla, norm_g, n_cols=16):
    b, t, w = o_f.shape
    n_r = t // GRID_W
    tm = n_r * n_cols
    perm_t = _colmajor_perm(n_r, n_cols).T
    row = lambda bi, i: (bi, i, 0)
    out = pl.pallas_call(
        _gla_post_kernel,
        grid=(b, GRID_W // n_cols),
        in_specs=[pl.BlockSpec((1, tm, w), row),
                  pl.BlockSpec((1, tm, w), row),
                  pl.BlockSpec((1, tm, w), lambda bi, i: (bi, i, 2)),
                  pl.BlockSpec((1, w), lambda bi, i: (0, 0)),
                  pl.BlockSpec((tm, tm), lambda bi, i: (0, 0))],
        out_specs=pl.BlockSpec((1, n_r, n_cols, w), lambda bi, i: (bi, 0, i, 0)),
        out_shape=jax.ShapeDtypeStruct((b, n_r, GRID_W, w), BF16),
        compiler_params=_cparams(("arbitrary", "arbitrary")),
        name="gla_post",
    )(o_f, o_b, p_gla, norm_g, perm_t)
    return out.reshape(b, t, w)


def _merge_kernel(ya_ref, yb_ref, wr_ref, wg_ref, ga_ref, gb_ref, o_ref, wr_scr, wg_scr):
    @pl.when(jnp.logical_and(pl.program_id(1) == 0, pl.program_id(2) == 0))
    def _():
        wr_scr[...] = wr_ref[...].astype(BF16)
        wg_scr[...] = wg_ref[...].astype(BF16)

    ma = _mm(ya_ref[0], wr_scr[...])
    mb = _mm(yb_ref[0], wg_scr[...])
    o_ref[0] = (ga_ref[0].astype(F32) * ma + gb_ref[0].astype(F32) * mb).astype(o_ref.dtype)


def _merge(ya, yb, w_r, w_g, p_gate, tm=1024, tn=1024):
    b, t, w = ya.shape
    d = w_r.shape[1]
    nj = d // tn
    return pl.pallas_call(
        _merge_kernel,
        grid=(nj, b, t // tm),
        in_specs=[pl.BlockSpec((1, tm, w), lambda j, bi, i: (bi, i, 0)),
                  pl.BlockSpec((1, tm, w), lambda j, bi, i: (bi, i, 0)),
                  pl.BlockSpec((w, tn), lambda j, bi, i: (0, j)),
                  pl.BlockSpec((w, tn), lambda j, bi, i: (0, j)),
                  pl.BlockSpec((1, tm, tn), lambda j, bi, i: (bi, i, j)),
                  pl.BlockSpec((1, tm, tn), lambda j, bi, i: (bi, i, j + nj))],
        out_specs=pl.BlockSpec((1, tm, tn), lambda j, bi, i: (bi, i, j)),
        out_shape=jax.ShapeDtypeStruct((b, t, d), BF16),
        scratch_shapes=[pltpu.VMEM((w, tn), BF16), pltpu.VMEM((w, tn), BF16)],
        compiler_params=_cparams(("arbitrary", "arbitrary", "arbitrary")),
        name="merge_branches",
    )(ya, yb, w_r, w_g, p_gate, p_gate)


def _mix_out_kernel(m_ref, w_ref, x_ref, gate_ref, npost_ref, npre_ref, sh_ref, sc_ref, x1_ref, h_ref):
    half = m_ref.shape[1] // 2
    for rs in (slice(0, half), slice(half, 2 * half)):
        z = _mm(m_ref[0, rs, :], w_ref[...])
        z = z * lax.rsqrt(jnp.mean(z * z, axis=-1, keepdims=True) + NORM_EPS) * npost_ref[...]
        x1 = x_ref[0, rs, :] + gate_ref[0] * z
        x1_ref[0, rs, :] = x1
        y = x1 * lax.rsqrt(jnp.mean(x1 * x1, axis=-1, keepdims=True) + NORM_EPS) * npre_ref[...]
        h_ref[0, rs, :] = (y * (1.0 + sc_ref[0]) + sh_ref[0]).astype(h_ref.dtype)


def _mix_out(m, w_out, x, gate, n_post, n_pre, shift, scale, tm=512):
    b, t, d = x.shape
    row = lambda bi, i: (bi, i, 0)
    per_b = lambda bi, i: (bi, 0, 0)
    vec = lambda bi, i: (0, 0)
    return pl.pallas_call(
        _mix_out_kernel,
        grid=(b, t // tm),
        in_specs=[pl.BlockSpec((1, tm, d), row),
                  pl.BlockSpec((d, d), vec),
                  pl.BlockSpec((1, tm, d), row),
                  pl.BlockSpec((1, 1, d), per_b),
                  pl.BlockSpec((1, d), vec), pl.BlockSpec((1, d), vec),
                  pl.BlockSpec((1, 1, d), per_b), pl.BlockSpec((1, 1, d), per_b)],
        out_specs=[pl.BlockSpec((1, tm, d), row), pl.BlockSpec((1, tm, d), row)],
        out_shape=[jax.ShapeDtypeStruct((b, t, d), F32), jax.ShapeDtypeStruct((b, t, d), BF16)],
        compiler_params=_cparams(("arbitrary", "arbitrary")),
        name="mix_out",
    )(m, w_out, x, gate, n_post, n_pre, shift, scale)


def _ffn_up_kernel(h_ref, wg_ref, wu_ref, o_ref, wg_scr, wu_scr):
    @pl.when(jnp.logical_and(pl.program_id(1) == 0, pl.program_id(2) == 0))
    def _():
        wg_scr[...] = wg_ref[...].astype(BF16)
        wu_scr[...] = wu_ref[...].astype(BF16)

    h = h_ref[0]
    a = _mm(h, wg_scr[...])
    u = _mm(h, wu_scr[...])
    o_ref[0] = (a * _sigmoid(a) * u).astype(o_ref.dtype)


def _ffn_up(h, w_gate, w_up, tm=1024, tn=512):
    b, t, d = h.shape
    f = w_gate.shape[1]
    return pl.pallas_call(
        _ffn_up_kernel,
        grid=(f // tn, b, t // tm),
        in_specs=[pl.BlockSpec((1, tm, d), lambda j, bi, i: (bi, i, 0)),
                  pl.BlockSpec((d, tn), lambda j, bi, i: (0, j)),
                  pl.BlockSpec((d, tn), lambda j, bi, i: (0, j))],
        out_specs=pl.BlockSpec((1, tm, tn), lambda j, bi, i: (bi, i, j)),
        out_shape=jax.ShapeDtypeStruct((b, t, f), BF16),
        scratch_shapes=[pltpu.VMEM((d, tn), BF16), pltpu.VMEM((d, tn), BF16)],
        compiler_params=_cparams(("arbitrary", "arbitrary", "arbitrary")),
        name="ffn_up",
    )(h, w_gate, w_up)


def _ffn_down_kernel(h_ref, w_ref, x_ref, gate_ref, npost_ref, o_ref, z_scr):
    j = pl.program_id(2)
    n_j = z_scr.shape[0]
    z_scr[j] = _mm(h_ref[0], w_ref[...])

    @pl.when(j == n_j - 1)
    def _():
        z = jnp.concatenate([z_scr[t] for t in range(n_j)], 1)
        z = z * lax.rsqrt(jnp.mean(z * z, axis=-1, keepdims=True) + NORM_EPS) * npost_ref[...]
        o_ref[0] = x_ref[0] + gate_ref[0] * z


def _ffn_down(h, w_down, x1, gate, n_post, tm=512, tn=512):
    b, t, f = h.shape
    d = w_down.shape[1]
    return pl.pallas_call(
        _ffn_down_kernel,
        grid=(b, t // tm, d // tn),
        in_specs=[pl.BlockSpec((1, tm, f), lambda bi, i, j: (bi, i, 0)),
                  pl.BlockSpec((f, tn), lambda bi, i, j: (0, j)),
                  pl.BlockSpec((1, tm, d), lambda bi, i, j: (bi, i, 0)),
                  pl.BlockSpec((1, 1, d), lambda bi, i, j: (bi, 0, 0)),
                  pl.BlockSpec((1, d), lambda bi, i, j: (0, 0))],
        out_specs=pl.BlockSpec((1, tm, d), lambda bi, i, j: (bi, i, 0)),
        out_shape=jax.ShapeDtypeStruct((b, t, d), F32),
        scratch_shapes=[pltpu.VMEM((d // tn, tm, tn), F32)],
        compiler_params=_cparams(("arbitrary", "arbitrary", "arbitrary")),
        name="ffn_down",
    )(h, w_down, x1, gate, n_post)


def _pad_cols(w, n):
    return jnp.pad(w, ((0, 0), (0, n - w.shape[1])))


def kernel(x, c, ctx, c_ctx, ada_w, ada_b, norm_pre_mix, norm_post_mix, norm_pre_ffn, norm_post_ffn, w_in, shift_mu, rwkv_w0, rwkv_w2, rwkv_a0, rwkv_a2, rwkv_g2, rwkv_k_k, rwkv_k_a, rwkv_r_k, rwkv_ln_g, rwkv_ln_b, w_rwkv_up, gla_alpha_up, gla_alpha_b, gla_norm_g, w_gla_up, w_out, ffn_w_gate, ffn_w_up, ffn_w_down):
    assert ada_w.shape[0] == 1, "single trunk layer"
    bsz, seq, d = x.shape
    n_ctx = ctx.shape[1]
    prec = BF16

    cvecs = jnp.concatenate([c, c_ctx[None, :], jnp.zeros((8 - bsz - 1, d), F32)], 0)
    mod = _modulation(cvecs, ada_w[0], ada_b[0])
    mod_x = mod[:bsz].reshape(bsz, 6, 1, d)
    shx1, scx1, gx1, shx2, scx2, gx2 = (mod_x[:, i] for i in range(6))
    mod_c = jnp.broadcast_to(mod[bsz].reshape(1, 6, 1, d), (bsz, 6, 1, d))
    shc1, scc1 = mod_c[:, 0], mod_c[:, 1]

    w_all = w_in[0]
    mix_in = RWKV_IN + GLA_IN
    zeros = lambda n: jnp.zeros((d, n), F32)
    w_cat = jnp.concatenate([w_all[:, :RWKV_IN], zeros(RWKV_PAD - RWKV_IN), w_all[:, RWKV_IN:mix_in],
                             zeros(GLA_PAD - GLA_IN), w_all[:, mix_in:]], 1).astype(BF16)
    c_rwkv, c_gla, c_gate = (0, RWKV_PAD), (RWKV_PAD, GLA_PAD), (RWKV_PAD + GLA_PAD, w_all.shape[1] - mix_in)
    mu = _pad_cols(shift_mu, RWKV_PAD)
    n_pre = norm_pre_mix

    hx = _prenorm(x, shx1, scx1, n_pre)
    hc = _prenorm(ctx, shc1, scc1, n_pre)
    px_rwkv = _project_shift(hx, w_cat, c_rwkv, mu, 1024, 512, "proj_rwkv")
    px_gla = _project_colmajor(hx, w_cat, c_gla, 16, 512, "proj_gla")
    px_gate = _project(hx, w_cat, c_gate, 1024, 1024, "proj_gate", gate_out=True)
    pc_rwkv = _project_shift(hc, w_cat, c_rwkv, mu, n_ctx, 512, "proj_rwkv_ctx")
    pc_gla = _project(hc, w_cat, c_gla, n_ctx, 512, "proj_gla_ctx")

    y_f, y_b = _rwkv_scan(px_rwkv, pc_rwkv, rwkv_w2[0], rwkv_w0[0][:, None, :], rwkv_a2[0], rwkv_a0[0][:, None, :],
                          rwkv_k_k, rwkv_k_a, prec)
    ya = _rwkv_post(y_f, y_b, px_rwkv, rwkv_a2[0], rwkv_a0[0][:, None, :], rwkv_g2[0], rwkv_k_a,
                    rwkv_r_k.reshape(1, RWKV_WIDTH), rwkv_ln_g, rwkv_ln_b)

    o_f, o_b = _gla_scan(px_gla, pc_gla, gla_alpha_up[0], gla_alpha_b[0][:, None, :], prec)
    yb = _gla_post(o_f, o_b, px_gla, gla_norm_g)

    m = _merge(ya, yb, w_rwkv_up[0], w_gla_up[0], px_gate)
    x1, h2 = _mix_out(m, w_out[0].astype(BF16), x, gx1, norm_post_mix, norm_pre_ffn, shx2, scx2)
    hf = _ffn_up(h2, ffn_w_gate[0], ffn_w_up[0])
    return _ffn_down(hf, ffn_w_down[0].astype(BF16), x1, gx2, norm_post_ffn)
```

```python
import functools

import jax
import jax.numpy as jnp
from jax import lax
from jax.experimental import pallas as pl
from jax.experimental.pallas import tpu as pltpu

F32 = jnp.float32
BF16 = jnp.bfloat16
HIGHEST = lax.Precision.HIGHEST

LANES = 128
VMEM_LIMIT_BYTES = 56 * 1024 * 1024

GRID_W = 64
CHUNK = 64
RWKV_HEADS, RWKV_HEAD_DIM = 16, 64
RWKV_WIDTH = RWKV_HEADS * RWKV_HEAD_DIM
DECAY_LORA = ICL_LORA = 96
GATE_LORA = 64
RWKV_GN_EPS = 64e-5
GLA_HEADS, GLA_KEY_DIM, GLA_VAL_DIM = 4, 128, 256
GLA_QK_WIDTH = GLA_HEADS * GLA_KEY_DIM
GLA_V_WIDTH = GLA_HEADS * GLA_VAL_DIM
GLA_GATE_LORA = 16
GLA_TAU = 16.0
LOG2_E = 1.4426950408889634
GLA_NORM_EPS = 1e-5
GLA_SUB = 8
NORM_EPS = 1e-6

RWKV_IN = 3 * RWKV_WIDTH + 2 * DECAY_LORA + 2 * ICL_LORA + GATE_LORA
RWKV_PAD = 3584
OFF_WD = 3 * RWKV_WIDTH
OFF_AD = OFF_WD + 2 * DECAY_LORA
OFF_GD = OFF_AD + 2 * ICL_LORA
GLA_IN = 2 * GLA_QK_WIDTH + 2 * GLA_V_WIDTH + 2 * GLA_GATE_LORA
GLA_BLK = 512
GLA_PAD = 7 * GLA_BLK
OFF_GLA_AD = 6 * GLA_BLK

PAIR = 2 * RWKV_HEAD_DIM
N_PAIRS = RWKV_HEADS // 2


def _cparams(semantics):
    return pltpu.CompilerParams(dimension_semantics=semantics, vmem_limit_bytes=VMEM_LIMIT_BYTES)


def _sigmoid(z):
    return 1.0 / (1.0 + jnp.exp(-z))


def _softplus(z):
    return jnp.maximum(z, 0.0) + jnp.log(1.0 + jnp.exp(-jnp.abs(z)))


def _dot(a, b, dims, precision):
    if precision is BF16:
        a, b, precision = a.astype(BF16), b.astype(BF16), None
    return lax.dot_general(a, b, (dims, ((), ())), precision=precision, preferred_element_type=F32)


def _mm(a, b, precision=None):
    return _dot(a, b, ((1,), (0,)), precision)


def _mm_nt(a, b, precision=None):
    return _dot(a, b, ((1,), (1,)), precision)


def _mm_tn(a, b, precision=None):
    return _dot(a, b, ((0,), (0,)), precision)


def _split2(z):
    hi = z.astype(BF16)
    return hi, (z - hi.astype(F32)).astype(BF16)


def _mm_split(a, b):
    ah, al = _split2(a)
    bh, bl = _split2(b)
    return (_mm(al, bh) + _mm(ah, bl)) + _mm(ah, bh)


def _mm_exact_lhs(e, b):
    hi = b.astype(BF16)
    r = b - hi.astype(F32)
    mid = r.astype(BF16)
    lo = (r - mid.astype(F32)).astype(BF16)
    e = e.astype(BF16)
    return (_mm(e, lo) + _mm(e, mid)) + _mm(e, hi)


def _mod_kernel(c_ref, w_ref, b_ref, o_ref):
    s = c_ref[...]
    s = s * _sigmoid(s)
    o_ref[...] = _mm_split(s, w_ref[...]) + b_ref[...]


def _modulation(cvecs, ada_w, ada_b, tn=1024):
    m, d = cvecs.shape
    n = ada_w.shape[1]
    return pl.pallas_call(
        _mod_kernel,
        grid=(n // tn,),
        in_specs=[pl.BlockSpec((m, d), lambda j: (0, 0)),
                  pl.BlockSpec((d, tn), lambda j: (0, j)),
                  pl.BlockSpec((1, tn), lambda j: (0, j))],
        out_specs=pl.BlockSpec((m, tn), lambda j: (0, j)),
        out_shape=jax.ShapeDtypeStruct((m, n), F32),
        compiler_params=_cparams(("arbitrary",)),
        name="adaln_mod",
    )(cvecs, ada_w, ada_b.reshape(1, n))


def _prenorm_kernel(x_ref, sh_ref, sc_ref, g_ref, o_ref):
    x = x_ref[0]
    ms = jnp.mean(x * x, axis=-1, keepdims=True)
    y = x * lax.rsqrt(ms + NORM_EPS) * g_ref[...]
    o_ref[0] = (y * (1.0 + sc_ref[0]) + sh_ref[0]).astype(o_ref.dtype)


def _prenorm(x, shift, scale, gain, tm=256):
    b, t, d = x.shape
    row = lambda bi, i: (bi, i, 0)
    per_b = lambda bi, i: (bi, 0, 0)
    return pl.pallas_call(
        _prenorm_kernel,
        grid=(b, t // tm),
        in_specs=[pl.BlockSpec((1, tm, d), row), pl.BlockSpec((1, 1, d), per_b), pl.BlockSpec((1, 1, d), per_b),
                  pl.BlockSpec((1, d), lambda bi, i: (0, 0))],
        out_specs=pl.BlockSpec((1, tm, d), row),
        out_shape=jax.ShapeDtypeStruct((b, t, d), BF16),
        compiler_params=_cparams(("arbitrary", "arbitrary")),
        name="prenorm",
    )(x, shift, scale, gain)


def _proj_kernel(gate_out, h_ref, w_ref, o_ref):
    p = _mm(h_ref[0], w_ref[...])
    o_ref[0] = (_sigmoid(p) if gate_out else p).astype(o_ref.dtype)


def _project(h, w, tm, tn, name, gate_out=False):
    b, t, d = h.shape
    n = w.shape[1]
    return pl.pallas_call(
        functools.partial(_proj_kernel, gate_out),
        grid=(b, t // tm, n // tn),
        in_specs=[pl.BlockSpec((1, tm, d), lambda bi, i, j: (bi, i, 0)),
                  pl.BlockSpec((d, tn), lambda bi, i, j: (0, j))],
        out_specs=pl.BlockSpec((1, tm, tn), lambda bi, i, j: (bi, i, j)),
        out_shape=jax.ShapeDtypeStruct((b, t, n), BF16 if gate_out else F32),
        compiler_params=_cparams(("arbitrary", "arbitrary", "arbitrary")),
        name=name,
    )(h, w)


def _colmajor_perm(n_rows, n_cols):
    dst = jnp.arange(n_rows * n_cols)
    src = (dst % n_rows) * n_cols + dst // n_rows
    return (src[:, None] == dst[None, :]).astype(BF16)


def _proj_cm_kernel(h_ref, perm_ref, w_ref, o_ref, h_scr):
    @pl.when(pl.program_id(2) == 0)
    def _():
        n_r, n_c, d = h_ref.shape[1:]
        for cb in range(d // GLA_BLK):
            cs = slice(cb * GLA_BLK, (cb + 1) * GLA_BLK)
            hr = h_ref[0, :, :, cs].astype(F32).reshape(n_r * n_c, GLA_BLK).astype(BF16)
            h_scr[:, cs] = _mm(perm_ref[...], hr).astype(BF16)

    o_ref[0] = _mm(h_scr[...], w_ref[...])


def _project_colmajor(h, w, n_cols, tn, name):
    b, t, d = h.shape
    n = w.shape[1]
    n_r = t // GRID_W
    tm = n_r * n_cols
    h4 = h.reshape(b, n_r, GRID_W, d)
    perm = _colmajor_perm(n_r, n_cols)
    return pl.pallas_call(
        _proj_cm_kernel,
        grid=(b, GRID_W // n_cols, n // tn),
        in_specs=[pl.BlockSpec((1, n_r, n_cols, d), lambda bi, i, j: (bi, 0, i, 0)),
                  pl.BlockSpec((tm, tm), lambda bi, i, j: (0, 0)),
                  pl.BlockSpec((d, tn), lambda bi, i, j: (0, j))],
        out_specs=pl.BlockSpec((1, tm, tn), lambda bi, i, j: (bi, i, j)),
        out_shape=jax.ShapeDtypeStruct((b, t, n), F32),
        scratch_shapes=[pltpu.VMEM((tm, d), BF16)],
        compiler_params=_cparams(("arbitrary", "arbitrary", "arbitrary")),
        name=name,
    )(h4, perm, w)


HALO = 16


def _proj_shift_kernel(h_ref, hp_ref, hn_ref, w_ref, mu_ref, o_ref):
    i = pl.program_id(1)
    tm = h_ref.shape[1]
    p = _mm(h_ref[0], w_ref[...])
    ph = _mm(jnp.concatenate([hp_ref[0], hn_ref[0]], 0), w_ref[...])
    before = jnp.where(i == 0, 0.0, ph[HALO - 1:HALO])
    after = jnp.where(i == pl.num_programs(1) - 1, 0.0, ph[HALO:HALO + 1])
    row = lax.broadcasted_iota(jnp.int32, p.shape, 0)
    prev = jnp.where(row == 0, before, pltpu.roll(p, 1, 0))
    nxt = jnp.where(row == tm - 1, after, pltpu.roll(p, tm - 1, 0))
    o_ref[0] = p + mu_ref[...] * (0.5 * (prev + nxt) - p)


def _project_shift(h, w, mu, tm, tn, name):
    b, t, d = h.shape
    n = w.shape[1]
    rh, nh = tm // HALO, t // HALO
    return pl.pallas_call(
        _proj_shift_kernel,
        grid=(b, t // tm, n // tn),
        in_specs=[pl.BlockSpec((1, tm, d), lambda bi, i, j: (bi, i, 0)),
                  pl.BlockSpec((1, HALO, d), lambda bi, i, j: (bi, jnp.maximum(i * rh - 1, 0), 0)),
                  pl.BlockSpec((1, HALO, d), lambda bi, i, j: (bi, jnp.minimum((i + 1) * rh, nh - 1), 0)),
                  pl.BlockSpec((d, tn), lambda bi, i, j: (0, j)),
                  pl.BlockSpec((1, tn), lambda bi, i, j: (0, j))],
        out_specs=pl.BlockSpec((1, tm, tn), lambda bi, i, j: (bi, i, j)),
        out_shape=jax.ShapeDtypeStruct((b, t, n), F32),
        compiler_params=_cparams(("arbitrary", "arbitrary", "arbitrary")),
        name=name,
    )(h, h, h, w, mu)


def _head_sum_matrix():
    r = lax.broadcasted_iota(jnp.int32, (PAIR, PAIR), 0)
    c = lax.broadcasted_iota(jnp.int32, (PAIR, PAIR), 1)
    return jnp.where((r >> 6) == (c >> 6), 1.0, 0.0).astype(BF16)


def _head_sum(z, hsum):
    hi = z.astype(BF16)
    lo = (z - hi.astype(F32)).astype(BF16)
    return _mm(hi, hsum) + _mm(lo, hsum)


def _rwkv_kernel(prec, n_ctx_chunks, plf_ref, pcf_ref, plb_ref, pcb_ref, w2_ref, w0_ref, a2_ref, a0_ref, kk_ref, ka_ref,
                 yf_ref, yb_ref, st_scr):
    @pl.when(pl.program_id(1) == 0)
    def _():
        st_scr[...] = jnp.zeros_like(st_scr)

    is_ctx = pl.program_id(1) < n_ctx_chunks
    y_refs = (yf_ref, yb_ref)
    prep = [_rwkv_prep(dr == 1, jnp.where(is_ctx, pc_ref[0], pl_ref[0]), w2_ref.at[dr], w0_ref.at[dr], a2_ref.at[dr],
                       a0_ref.at[dr], kk_ref, ka_ref)
            for dr, (pl_ref, pc_ref) in enumerate(((plf_ref, pcf_ref), (plb_ref, pcb_ref)))]

    items = [(dr, pr) for pr in range(N_PAIRS) for dr in range(2)]
    n = range(len(items))
    per_pair = lambda name: [prep[dr][name][pr] for dr, pr in items]
    per_dir = lambda name: [prep[dr][name] for dr, _ in items]
    at, bt, kt, rt_, rs, bh, kh, v_swap, v_stack, p_end = (per_pair(k) for k in (
        "at", "bt", "kt", "rt", "rs", "bh", "kh", "v_swap", "v_stack", "p_end"))
    mask_n, mask_k, incl_c = per_dir("mask_n"), per_dir("mask_k"), per_dir("incl_c")

    c2 = 2 * CHUNK
    ri = lax.broadcasted_iota(jnp.int32, (c2, PAIR), 0)
    li = lax.broadcasted_iota(jnp.int32, (c2, PAIR), 1)
    ident = ri == li
    eye = jnp.where(ident, 1.0, 0.0).astype(F32)
    lane_e = lax.broadcasted_iota(jnp.int32, (CHUNK, PAIR), 1) < RWKV_HEAD_DIM
    cat0 = lambda *z: jnp.concatenate(z, 0)

    g_e = [_mm_nt(cat0(at[i][0], rs[i][0]), cat0(bt[i], kt[i]), prec) for i in n]
    g_o = [_mm_nt(cat0(at[i][1], rs[i][1]), cat0(kt[i], bt[i]), prec) for i in n]
    g_top = [cat0(g_e[i][0:CHUNK], g_o[i][0:CHUNK]) for i in n]
    nbd = [jnp.where(mask_n[i], g_top[i], 0.0) for i in n]
    aak = [jnp.where(mask_k[i], g_top[i], 0.0) for i in n]
    rab = [jnp.where(incl_c[i], jnp.where(lane_e, g_e[i][CHUNK:c2], g_o[i][CHUNK:c2]), 0.0) for i in n]
    rak = [jnp.where(incl_c[i], jnp.where(lane_e, g_o[i][CHUNK:c2], g_e[i][CHUNK:c2]), 0.0) for i in n]

    x = [_mm(aak[i], v_swap[i], prec) for i in n]
    n2 = [_mm(z, z, prec) for z in nbd]
    y0b = [_mm(rak[i], v_swap[i], prec) for i in n]
    n4 = [_mm(z, z, prec) for z in n2]
    imn = [eye - z for z in nbd]
    p1 = [imn[i] + _mm(imn[i], n2[i], prec) for i in n]
    n8 = [_mm(z, z, prec) for z in n4]
    nb = [_mm_tn(kh[i], v_stack[i], prec) for i in n]
    n16 = [_mm(z, z, prec) for z in n8]
    p2 = [eye + n4[i] + n8[i] + _mm(n4[i], n8[i], prec) for i in n]
    n32 = [_mm(z, z, prec) for z in n16]
    p12 = [_mm(p1[i], p2[i], prec) for i in n]
    p3 = [eye + n16[i] + n32[i] + _mm(n16[i], n32[i], prec) for i in n]
    tinv = [_mm(p12[i], p3[i], prec) for i in n]
    wu = [-_mm(tinv[i], jnp.concatenate([cat0(*at[i]), x[i]], 1), prec) for i in n]
    qy = [_mm(rab[i], wu[i], prec) for i in n]
    mn = [_mm_tn(bh[i], wu[i], prec) for i in n]
    q = [rt_[i] + qy[i][:, 0:PAIR] for i in n]
    m = [jnp.where(ident, p_end[i], 0.0) + mn[i][:, 0:PAIR] for i in n]
    qm = [_mm(cat0(q[i], m[i]), st_scr[dr, pr], prec) for i, (dr, pr) in enumerate(items)]
    for i, (dr, pr) in enumerate(items):
        y_refs[dr][0, :, pr * PAIR:(pr + 1) * PAIR] = qm[i][0:CHUNK] + qy[i][:, PAIR:2 * PAIR] + y0b[i]
        st_scr[dr, pr] = qm[i][CHUNK:CHUNK + PAIR] + mn[i][:, PAIR:2 * PAIR] + nb[i]


def _rwkv_prep(rev, p, w2_ref, w0_ref, a2_ref, a0_ref, kk_ref, ka_ref):
    hd = RWKV_HEAD_DIM
    c2 = 2 * CHUNK
    r = p[:, 0:RWKV_WIDTH]
    k = p[:, RWKV_WIDTH:2 * RWKV_WIDTH]
    v = p[:, 2 * RWKV_WIDTH:3 * RWKV_WIDTH]
    d_off = DECAY_LORA if rev else 0
    wd = p[:, OFF_WD + d_off:OFF_WD + d_off + DECAY_LORA]
    ad = p[:, OFF_AD + d_off:OFF_AD + d_off + ICL_LORA]

    w_log = -_softplus(-(w0_ref[...] + _mm_split(jnp.tanh(wd), w2_ref[...]))) - 0.5
    lw = -jnp.exp(w_log)
    a = _sigmoid(a0_ref[...] + _mm(ad, a2_ref[...], BF16))
    kk_raw = k * kk_ref[...]
    kd = k * (1.0 + (a - 1.0) * ka_ref[...])

    ri = lax.broadcasted_iota(jnp.int32, (c2, PAIR), 0)
    li = lax.broadcasted_iota(jnp.int32, (c2, PAIR), 1)
    rt, lt = ri & (CHUNK - 1), li & (hd - 1)
    same_head = (ri >> 6) == (li >> 6)
    strict = (lt > rt) if rev else (lt < rt)
    rc = lax.broadcasted_iota(jnp.int32, (CHUNK, PAIR), 0)
    lc = lax.broadcasted_iota(jnp.int32, (CHUNK, PAIR), 1)
    lane_e = lc < hd
    hsum = _head_sum_matrix()

    ci = lax.broadcasted_iota(jnp.int32, (CHUNK, CHUNK), 0)
    cj = lax.broadcasted_iota(jnp.int32, (CHUNK, CHUNK), 1)
    tri = jnp.where((cj >= ci) if rev else (cj <= ci), 1.0, 0.0).astype(F32)
    cum = _mm_exact_lhs(tri, lw)
    total = cum[0:1] if rev else cum[CHUNK - 1:CHUNK]
    e_prev = jnp.exp(cum - lw)
    e_neg = jnp.exp(-cum)
    e_pos = jnp.exp(cum)
    e_rest = jnp.exp(total - cum)
    p_end = jnp.exp(total)

    def split(z):
        ze = jnp.where(lane_e, z, 0.0)
        return ze, z - ze

    sls = [slice(pr * PAIR, (pr + 1) * PAIR) for pr in range(N_PAIRS)]
    cat0 = lambda *z: jnp.concatenate(z, 0)
    kkr = [kk_raw[:, sl] for sl in sls]
    nrm2 = [_head_sum(z * z, hsum) for z in kkr]
    kk = [z / jnp.maximum(jnp.sqrt(n), 1e-12) for z, n in zip(kkr, nrm2)]
    bb = [z * a[:, sl] for z, sl in zip(kk, sls)]
    rt_ = [r[:, sl] * e_pos[:, sl] for sl in sls]
    vs = [split(v[:, sl]) for sl in sls]
    return dict(
        mask_n=jnp.logical_and(same_head, strict),
        mask_k=jnp.logical_and(jnp.logical_not(same_head), strict),
        incl_c=((lc & (hd - 1)) >= rc) if rev else ((lc & (hd - 1)) <= rc),
        at=[split(z * e_prev[:, sl]) for z, sl in zip(kk, sls)],
        bt=[z * e_neg[:, sl] for z, sl in zip(bb, sls)],
        kt=[kd[:, sl] * e_neg[:, sl] for sl in sls],
        rt=rt_,
        rs=[split(z) for z in rt_],
        bh=[cat0(*split(z * e_rest[:, sl])) for z, sl in zip(bb, sls)],
        kh=[cat0(*split(kd[:, sl] * e_rest[:, sl])) for sl in sls],
        v_swap=[cat0(vo, ve) for ve, vo in vs],
        v_stack=[cat0(ve, vo) for ve, vo in vs],
        p_end=[p_end[:, sl] for sl in sls],
    )


def _scan_order(n_ctx_chunks, n_lat):
    n_steps = n_ctx_chunks + n_lat
    lat_f = lambda i: jnp.maximum(i - n_ctx_chunks, 0)
    lat_b = lambda i: jnp.where(i < n_ctx_chunks, n_lat - 1, n_steps - 1 - i)
    ctx_f = lambda i: jnp.minimum(i, n_ctx_chunks - 1)
    ctx_b = lambda i: jnp.maximum(n_ctx_chunks - 1 - i, 0)
    return n_steps, (lat_f, lat_b), (ctx_f, ctx_b)


def _rwkv_scan(p_lat, p_ctx, w2, w0, a2, a0, k_k, k_a, prec):
    b, t, w = p_lat.shape
    n_lat = t // CHUNK
    n_ctx_chunks = p_ctx.shape[1] // CHUNK
    n_steps, lat_of, ctx_of = _scan_order(n_ctx_chunks, n_lat)
    chunk = lambda f: pl.BlockSpec((1, CHUNK, w), lambda bi, i: (bi, f(i), 0))
    out = lambda f: pl.BlockSpec((1, CHUNK, RWKV_WIDTH), lambda bi, i: (bi, f(i), 0))
    vec = lambda bi, i: (0, 0)
    vec3 = lambda bi, i: (0, 0, 0)
    y_shape = jax.ShapeDtypeStruct((b, t, RWKV_WIDTH), F32)
    return pl.pallas_call(
        functools.partial(_rwkv_kernel, prec, n_ctx_chunks),
        grid=(b, n_steps),
        in_specs=[chunk(lat_of[0]), chunk(ctx_of[0]), chunk(lat_of[1]), chunk(ctx_of[1]),
                  pl.BlockSpec((2, DECAY_LORA, RWKV_WIDTH), vec3),
                  pl.BlockSpec((2, 1, RWKV_WIDTH), vec3),
                  pl.BlockSpec((2, ICL_LORA, RWKV_WIDTH), vec3),
                  pl.BlockSpec((2, 1, RWKV_WIDTH), vec3),
                  pl.BlockSpec((1, RWKV_WIDTH), vec),
                  pl.BlockSpec((1, RWKV_WIDTH), vec)],
        out_specs=[out(lat_of[0]), out(lat_of[1])],
        out_shape=[y_shape, y_shape],
        scratch_shapes=[pltpu.VMEM((2, N_PAIRS, PAIR, PAIR), F32)],
        compiler_params=_cparams(("arbitrary", "arbitrary")),
        name="rwkv7_scan",
    )(p_lat, p_ctx, p_lat, p_ctx, w2, w0, a2, a0, k_k, k_a)


def _gla_kernel(prec, n_ctx_chunks, *refs):
    lat = (refs[0:6], refs[6:12])
    aup_ref, ab_ref, of_ref, ob_ref, st_scr = refs[12:17]
    o_refs = (of_ref, ob_ref)

    @pl.when(pl.program_id(1) == 0)
    def _():
        st_scr[...] = jnp.zeros_like(st_scr)

    is_ctx = pl.program_id(1) < n_ctx_chunks
    dk, dv, sb = GLA_KEY_DIM, GLA_VAL_DIM, GLA_SUB
    n_sb = CHUNK // sb
    prep = [_gla_prep(dr == 1, is_ctx, *lat[dr], aup_ref.at[dr], ab_ref.at[dr]) for dr in range(2)]

    items = [(dr, h) for h in range(GLA_HEADS) for dr in range(2)]
    n = range(len(items))
    revs = [dr == 1 for dr, _ in items]
    qh, kh, bh, lah, toth, vh = ([prep[dr][name][h] for dr, h in items] for name in ("q", "k", "b", "la", "tot", "v"))
    st = [st_scr[dr, h] for dr, h in items]

    o_inter = [_mm_nt(qh[i] * jnp.exp2(bh[i]), st[i], prec) for i in n]
    st_new = [st[i] * jnp.exp2(toth[i]) + _mm_tn(vh[i], kh[i] * jnp.exp2(toth[i] - bh[i]), prec) for i in n]

    arow = lax.broadcasted_iota(jnp.int32, (CHUNK, CHUNK), 0)
    acol = lax.broadcasted_iota(jnp.int32, (CHUNK, CHUNK), 1)
    bcol = lax.broadcasted_iota(jnp.int32, (sb, CHUNK), 1)
    off_rows = [[] for _ in n]
    for blk in range(n_sb):
        rs = slice(blk * sb, (blk + 1) * sb)
        for i in n:
            rev = revs[i]
            if (blk == n_sb - 1) if rev else (blk == 0):
                off_rows[i].append(jnp.zeros((sb, CHUNK), F32))
                continue
            first = blk * sb + (sb - 1 if rev else 0)
            before = (bcol >= (blk + 1) * sb) if rev else (bcol < blk * sb)
            beta = bh[i][first:first + 1] - lah[i][first:first + 1]
            qs = qh[i][rs] * jnp.exp2(bh[i][rs] - beta)
            ksc = kh[i] * jnp.exp2(jnp.minimum(beta - bh[i], 0.0))
            off_rows[i].append(jnp.where(before, _mm_nt(qs, ksc, prec), 0.0))
    att = [jnp.concatenate(off_rows[i], 0) for i in n]

    in_blk = arow & (sb - 1)
    for s in range(sb):
        pick = lambda z: jnp.concatenate(
            [jnp.broadcast_to(z[blk * sb + s:blk * sb + s + 1], (sb, dk)) for blk in range(n_sb)], 0)
        on_col = acol == (arow & ~(sb - 1)) + s
        tgt = (jnp.logical_and(on_col, in_blk >= s), jnp.logical_and(on_col, in_blk <= s))
        for i in n:
            e = jnp.exp2(jnp.minimum(bh[i] - pick(bh[i]), 0.0))
            col = jnp.sum(qh[i] * pick(kh[i]) * e, axis=-1, keepdims=True)
            att[i] = jnp.where(tgt[revs[i]], col, att[i])

    for i, (dr, h) in enumerate(items):
        o_refs[dr][0, :, h * dv:(h + 1) * dv] = o_inter[i] + _mm(att[i], vh[i], prec)
        st_scr[dr, h] = st_new[i]


def _gla_prep(rev, is_ctx, q_ref, k_ref, v0_ref, v1_ref, ad_ref, ctx_ref, aup_ref, ab_ref):
    pc = ctx_ref[0]
    q = jnp.where(is_ctx, pc[:, 0:GLA_BLK], q_ref[0])
    k = jnp.where(is_ctx, pc[:, GLA_BLK:2 * GLA_BLK], k_ref[0])
    v = jnp.concatenate([jnp.where(is_ctx, pc[:, 2 * GLA_BLK:3 * GLA_BLK], v0_ref[0]),
                         jnp.where(is_ctx, pc[:, 3 * GLA_BLK:4 * GLA_BLK], v1_ref[0])], 1)
    d_off = GLA_GATE_LORA if rev else 0
    ad = jnp.where(is_ctx, pc[:, OFF_GLA_AD:OFF_GLA_AD + LANES], ad_ref[0][:, 0:LANES])
    ad = ad[:, d_off:d_off + GLA_GATE_LORA]

    la = -_softplus(-(_mm_split(ad, aup_ref[...]) + ab_ref[...])) * (LOG2_E / GLA_TAU)
    ci = lax.broadcasted_iota(jnp.int32, (CHUNK, CHUNK), 0)
    cj = lax.broadcasted_iota(jnp.int32, (CHUNK, CHUNK), 1)
    tri = jnp.where((cj >= ci) if rev else (cj <= ci), 1.0, 0.0).astype(F32)
    cum = _mm_exact_lhs(tri, la)
    total = cum[0:1] if rev else cum[CHUNK - 1:CHUNK]
    dk, dv = GLA_KEY_DIM, GLA_VAL_DIM
    ksl = [slice(h * dk, (h + 1) * dk) for h in range(GLA_HEADS)]
    scale = GLA_KEY_DIM ** -0.5
    return dict(q=[q[:, s_] * scale for s_ in ksl], k=[k[:, s_] for s_ in ksl], b=[cum[:, s_] for s_ in ksl],
                la=[la[:, s_] for s_ in ksl], tot=[total[:, s_] for s_ in ksl],
                v=[v[:, h * dv:(h + 1) * dv] for h in range(GLA_HEADS)])


def _gla_scan(p_lat, p_ctx, alpha_up, alpha_b, prec):
    b, t, w = p_lat.shape
    assert w == GLA_PAD and t == GRID_W * CHUNK
    n_ctx_chunks = p_ctx.shape[1] // CHUNK
    n_steps, col_of, ctx_of = _scan_order(n_ctx_chunks, GRID_W)

    def blocks(dr):
        lat = lambda m: pl.BlockSpec((1, CHUNK, GLA_BLK), lambda bi, i: (bi, col_of[dr](i), m))
        return [lat(0), lat(1), lat(2), lat(3), lat(6), pl.BlockSpec((1, CHUNK, w), lambda bi, i: (bi, ctx_of[dr](i), 0))]

    out = lambda dr: pl.BlockSpec((1, CHUNK, GLA_V_WIDTH), lambda bi, i: (bi, col_of[dr](i), 0))
    vec3 = lambda bi, i: (0, 0, 0)
    o_shape = jax.ShapeDtypeStruct((b, t, GLA_V_WIDTH), F32)
    return pl.pallas_call(
        functools.partial(_gla_kernel, prec, n_ctx_chunks),
        grid=(b, n_steps),
        in_specs=blocks(0) + blocks(1) + [pl.BlockSpec((2, GLA_GATE_LORA, GLA_QK_WIDTH), vec3),
                                          pl.BlockSpec((2, 1, GLA_QK_WIDTH), vec3)],
        out_specs=[out(0), out(1)],
        out_shape=[o_shape, o_shape],
        scratch_shapes=[pltpu.VMEM((2, GLA_HEADS, GLA_VAL_DIM, GLA_KEY_DIM), F32)],
        compiler_params=_cparams(("arbitrary", "arbitrary")),
        name="gla_scan",
    )(*([p_lat] * 5 + [p_ctx]) * 2, alpha_up, alpha_b)


def _rwkv_post_kernel(yf_ref, yb_ref, pm_ref, a2_ref, a0_ref, g2_ref, ka_ref, rk_ref, lng_ref, lnb_ref, o_ref):
    p = pm_ref[0]
    r = p[:, 0:RWKV_WIDTH]
    k = p[:, RWKV_WIDTH:2 * RWKV_WIDTH]
    v = p[:, 2 * RWKV_WIDTH:3 * RWKV_WIDTH]
    ad_f = p[:, OFF_AD:OFF_AD + ICL_LORA]
    ad_b = p[:, OFF_AD + ICL_LORA:OFF_AD + 2 * ICL_LORA]
    gd = p[:, OFF_GD:OFF_GD + GATE_LORA]
    ka = ka_ref[...]
    a_f = _sigmoid(a0_ref[0] + _mm(ad_f, a2_ref[0], BF16))
    a_b = _sigmoid(a0_ref[1] + _mm(ad_b, a2_ref[1], BF16))
    kd_sum = k * (1.0 + (a_f - 1.0) * ka) + k * (1.0 + (a_b - 1.0) * ka)
    gate = _mm(_sigmoid(gd), g2_ref[...], BF16)
    rkk = r * kd_sum * rk_ref[...]
    ysum = yf_ref[0] + yb_ref[0]
    hsum = _head_sum_matrix()
    inv_n = 1.0 / RWKV_HEAD_DIM
    for pr in range(N_PAIRS):
        sl = slice(pr * PAIR, (pr + 1) * PAIR)
        ys = ysum[:, sl]
        mean = _head_sum(ys, hsum) * inv_n
        dlt = ys - mean
        var = _head_sum(dlt * dlt, hsum) * inv_n
        gn = dlt * lax.rsqrt(var + RWKV_GN_EPS) * lng_ref[:, sl] + lnb_ref[:, sl]
        bonus = _head_sum(rkk[:, sl], hsum) * v[:, sl]
        o_ref[0, :, sl] = ((gn + bonus) * gate[:, sl]).astype(o_ref.dtype)


def _rwkv_post(y_f, y_b, pmix, a2, a0, g2, k_a, r_k, ln_g, ln_b, tm=256):
    b, t, w = y_f.shape
    row = lambda bi, i: (bi, i, 0)
    vec = lambda bi, i: (0, 0)
    vec3 = lambda bi, i: (0, 0, 0)
    return pl.pallas_call(
        _rwkv_post_kernel,
        grid=(b, t // tm),
        in_specs=[pl.BlockSpec((1, tm, w), row),
                  pl.BlockSpec((1, tm, w), row),
                  pl.BlockSpec((1, tm, pmix.shape[2]), row),
                  pl.BlockSpec((2, ICL_LORA, w), vec3),
                  pl.BlockSpec((2, 1, w), vec3),
                  pl.BlockSpec((GATE_LORA, w), vec),
                  pl.BlockSpec((1, w), vec), pl.BlockSpec((1, w), vec),
                  pl.BlockSpec((1, w), vec), pl.BlockSpec((1, w), vec)],
        out_specs=pl.BlockSpec((1, tm, w), row),
        out_shape=jax.ShapeDtypeStruct((b, t, w), BF16),
        compiler_params=_cparams(("arbitrary", "arbitrary")),
        name="rwkv_post",
    )(y_f, y_b, pmix, a2, a0, g2, k_a, r_k, ln_g, ln_b)


def _gla_post_kernel(of_ref, ob_ref, g_ref, ng_ref, perm_ref, o_ref):
    n_r, n_c = o_ref.shape[1:3]
    dv = GLA_VAL_DIM
    for h in range(GLA_HEADS):
        sl = slice(h * dv, (h + 1) * dv)
        oh = of_ref[0, :, sl] + ob_ref[0, :, sl]
        oh = oh * lax.rsqrt(jnp.mean(oh * oh, axis=-1, keepdims=True) + GLA_NORM_EPS) * ng_ref[:, sl]
        gh = g_ref[0, :, sl]
        y = (oh * (gh * _sigmoid(gh))).astype(BF16)
        o_ref[0, :, :, sl] = _mm(perm_ref[...], y).reshape(n_r, n_c, dv).astype(o_ref.dtype)


def _gla_post(o_f, o_b, p_gla, norm_g, n_cols=16):
    b, t, w = o_f.shape
    n_r = t // GRID_W
    tm = n_r * n_cols
    perm_t = _colmajor_perm(n_r, n_cols).T
    row = lambda bi, i: (bi, i, 0)
    out = pl.pallas_call(
        _gla_post_kernel,
        grid=(b, GRID_W // n_cols),
        in_specs=[pl.BlockSpec((1, tm, w), row),
                  pl.BlockSpec((1, tm, w), row),
                  pl.BlockSpec((1, tm, w), lambda bi, i: (bi, i, 2)),
                  pl.BlockSpec((1, w), lambda bi, i: (0, 0)),
                  pl.BlockSpec((tm, tm), lambda bi, i: (0, 0))],
        out_specs=pl.BlockSpec((1, n_r, n_cols, w), lambda bi, i: (bi, 0, i, 0)),
        out_shape=jax.ShapeDtypeStruct((b, n_r, GRID_W, w), BF16),
        compiler_params=_cparams(("arbitrary", "arbitrary")),
        name="gla_post",
    )(o_f, o_b, p_gla, norm_g, perm_t)
    return out.reshape(b, t, w)


def _merge_kernel(ya_ref, yb_ref, wr_ref, wg_ref, ga_ref, gb_ref, o_ref, wr_scr, wg_scr):
    @pl.when(jnp.logical_and(pl.program_id(1) == 0, pl.program_id(2) == 0))
    def _():
        wr_scr[...] = wr_ref[...].astype(BF16)
        wg_scr[...] = wg_ref[...].astype(BF16)

    ma = _mm(ya_ref[0], wr_scr[...])
    mb = _mm(yb_ref[0], wg_scr[...])
    o_ref[0] = (ga_ref[0].astype(F32) * ma + gb_ref[0].astype(F32) * mb).astype(o_ref.dtype)


def _merge(ya, yb, w_r, w_g, p_gate, tm=1024, tn=1024):
    b, t, w = ya.shape
    d = w_r.shape[1]
    nj = d // tn
    return pl.pallas_call(
        _merge_kernel,
        grid=(nj, b, t // tm),
        in_specs=[pl.BlockSpec((1, tm, w), lambda j, bi, i: (bi, i, 0)),
                  pl.BlockSpec((1, tm, w), lambda j, bi, i: (bi, i, 0)),
                  pl.BlockSpec((w, tn), lambda j, bi, i: (0, j)),
                  pl.BlockSpec((w, tn), lambda j, bi, i: (0, j)),
                  pl.BlockSpec((1, tm, tn), lambda j, bi, i: (bi, i, j)),
                  pl.BlockSpec((1, tm, tn), lambda j, bi, i: (bi, i, j + nj))],
        out_specs=pl.BlockSpec((1, tm, tn), lambda j, bi, i: (bi, i, j)),
        out_shape=jax.ShapeDtypeStruct((b, t, d), BF16),
        scratch_shapes=[pltpu.VMEM((w, tn), BF16), pltpu.VMEM((w, tn), BF16)],
        compiler_params=_cparams(("arbitrary", "arbitrary", "arbitrary")),
        name="merge_branches",
    )(ya, yb, w_r, w_g, p_gate, p_gate)


def _mix_out_kernel(m_ref, w_ref, x_ref, gate_ref, npost_ref, npre_ref, sh_ref, sc_ref, x1_ref, h_ref):
    half = m_ref.shape[1] // 2
    for rs in (slice(0, half), slice(half, 2 * half)):
        z = _mm(m_ref[0, rs, :], w_ref[...])
        z = z * lax.rsqrt(jnp.mean(z * z, axis=-1, keepdims=True) + NORM_EPS) * npost_ref[...]
        x1 = x_ref[0, rs, :] + gate_ref[0] * z
        x1_ref[0, rs, :] = x1
        y = x1 * lax.rsqrt(jnp.mean(x1 * x1, axis=-1, keepdims=True) + NORM_EPS) * npre_ref[...]
        h_ref[0, rs, :] = (y * (1.0 + sc_ref[0]) + sh_ref[0]).astype(h_ref.dtype)


def _mix_out(m, w_out, x, gate, n_post, n_pre, shift, scale, tm=512):
    b, t, d = x.shape
    row = lambda bi, i: (bi, i, 0)
    per_b = lambda bi, i: (bi, 0, 0)
    vec = lambda bi, i: (0, 0)
    return pl.pallas_call(
        _mix_out_kernel,
        grid=(b, t // tm),
        in_specs=[pl.BlockSpec((1, tm, d), row),
                  pl.BlockSpec((d, d), vec),
                  pl.BlockSpec((1, tm, d), row),
                  pl.BlockSpec((1, 1, d), per_b),
                  pl.BlockSpec((1, d), vec), pl.BlockSpec((1, d), vec),
                  pl.BlockSpec((1, 1, d), per_b), pl.BlockSpec((1, 1, d), per_b)],
        out_specs=[pl.BlockSpec((1, tm, d), row), pl.BlockSpec((1, tm, d), row)],
        out_shape=[jax.ShapeDtypeStruct((b, t, d), F32), jax.ShapeDtypeStruct((b, t, d), BF16)],
        compiler_params=_cparams(("arbitrary", "arbitrary")),
        name="mix_out",
    )(m, w_out, x, gate, n_post, n_pre, shift, scale)


def _ffn_up_kernel(h_ref, wg_ref, wu_ref, o_ref, wg_scr, wu_scr):
    @pl.when(jnp.logical_and(pl.program_id(1) == 0, pl.program_id(2) == 0))
    def _():
        wg_scr[...] = wg_ref[...].astype(BF16)
        wu_scr[...] = wu_ref[...].astype(BF16)

    h = h_ref[0]
    a = _mm(h, wg_scr[...])
    u = _mm(h, wu_scr[...])
    o_ref[0] = (a * _sigmoid(a) * u).astype(o_ref.dtype)


def _ffn_up(h, w_gate, w_up, tm=1024, tn=512):
    b, t, d = h.shape
    f = w_gate.shape[1]
    return pl.pallas_call(
        _ffn_up_kernel,
        grid=(f // tn, b, t // tm),
        in_specs=[pl.BlockSpec((1, tm, d), lambda j, bi, i: (bi, i, 0)),
                  pl.BlockSpec((d, tn), lambda j, bi, i: (0, j)),
                  pl.BlockSpec((d, tn), lambda j, bi, i: (0, j))],
        out_specs=pl.BlockSpec((1, tm, tn), lambda j, bi, i: (bi, i, j)),
        out_shape=jax.ShapeDtypeStruct((b, t, f), BF16),
        scratch_shapes=[pltpu.VMEM((d, tn), BF16), pltpu.VMEM((d, tn), BF16)],
        compiler_params=_cparams(("arbitrary", "arbitrary", "arbitrary")),
        name="ffn_up",
    )(h, w_gate, w_up)


def _ffn_down_kernel(h_ref, w_ref, x_ref, gate_ref, npost_ref, o_ref, z_scr):
    j = pl.program_id(2)
    n_j = z_scr.shape[0]
    z_scr[j] = _mm(h_ref[0], w_ref[...])

    @pl.when(j == n_j - 1)
    def _():
        z = jnp.concatenate([z_scr[t] for t in range(n_j)], 1)
        z = z * lax.rsqrt(jnp.mean(z * z, axis=-1, keepdims=True) + NORM_EPS) * npost_ref[...]
        o_ref[0] = x_ref[0] + gate_ref[0] * z


def _ffn_down(h, w_down, x1, gate, n_post, tm=512, tn=512):
    b, t, f = h.shape
    d = w_down.shape[1]
    return pl.pallas_call(
        _ffn_down_kernel,
        grid=(b, t // tm, d // tn),
        in_specs=[pl.BlockSpec((1, tm, f), lambda bi, i, j: (bi, i, 0)),
                  pl.BlockSpec((f, tn), lambda bi, i, j: (0, j)),
                  pl.BlockSpec((1, tm, d), lambda bi, i, j: (bi, i, 0)),
                  pl.BlockSpec((1, 1, d), lambda bi, i, j: (bi, 0, 0)),
                  pl.BlockSpec((1, d), lambda bi, i, j: (0, 0))],
        out_specs=pl.BlockSpec((1, tm, d), lambda bi, i, j: (bi, i, 0)),
        out_shape=jax.ShapeDtypeStruct((b, t, d), F32),
        scratch_shapes=[pltpu.VMEM((d // tn, tm, tn), F32)],
        compiler_params=_cparams(("arbitrary", "arbitrary", "arbitrary")),
        name="ffn_down",
    )(h, w_down, x1, gate, n_post)


def _pad_cols(w, n):
    return jnp.pad(w, ((0, 0), (0, n - w.shape[1])))


def kernel(x, c, ctx, c_ctx, ada_w, ada_b, norm_pre_mix, norm_post_mix, norm_pre_ffn, norm_post_ffn, w_in, shift_mu, rwkv_w0, rwkv_w2, rwkv_a0, rwkv_a2, rwkv_g2, rwkv_k_k, rwkv_k_a, rwkv_r_k, rwkv_ln_g, rwkv_ln_b, w_rwkv_up, gla_alpha_up, gla_alpha_b, gla_norm_g, w_gla_up, w_out, ffn_w_gate, ffn_w_up, ffn_w_down):
    assert ada_w.shape[0] == 1, "single trunk layer"
    bsz, seq, d = x.shape
    n_ctx = ctx.shape[1]
    prec = BF16

    cvecs = jnp.concatenate([c, c_ctx[None, :], jnp.zeros((8 - bsz - 1, d), F32)], 0)
    mod = _modulation(cvecs, ada_w[0], ada_b[0])
    mod_x = mod[:bsz].reshape(bsz, 6, 1, d)
    shx1, scx1, gx1, shx2, scx2, gx2 = (mod_x[:, i] for i in range(6))
    mod_c = jnp.broadcast_to(mod[bsz].reshape(1, 6, 1, d), (bsz, 6, 1, d))
    shc1, scc1 = mod_c[:, 0], mod_c[:, 1]

    w_all = w_in[0]
    mix_in = RWKV_IN + GLA_IN
    w_rwkv = _pad_cols(w_all[:, :RWKV_IN], RWKV_PAD).astype(BF16)
    w_gla = _pad_cols(w_all[:, RWKV_IN:mix_in], GLA_PAD).astype(BF16)
    w_gate = w_all[:, mix_in:].astype(BF16)
    mu = _pad_cols(shift_mu, RWKV_PAD)
    n_pre = norm_pre_mix

    hx = _prenorm(x, shx1, scx1, n_pre)
    hc = _prenorm(ctx, shc1, scc1, n_pre)
    px_rwkv = _project_shift(hx, w_rwkv, mu, 1024, 512, "proj_rwkv")
    px_gla = _project_colmajor(hx, w_gla, 16, 512, "proj_gla")
    px_gate = _project(hx, w_gate, 1024, 1024, "proj_gate", gate_out=True)
    pc_rwkv = _project_shift(hc, w_rwkv, mu, n_ctx, 512, "proj_rwkv_ctx")
    pc_gla = _project(hc, w_gla, n_ctx, 512, "proj_gla_ctx")

    y_f, y_b = _rwkv_scan(px_rwkv, pc_rwkv, rwkv_w2[0], rwkv_w0[0][:, None, :], rwkv_a2[0], rwkv_a0[0][:, None, :],
                          rwkv_k_k, rwkv_k_a, prec)
    ya = _rwkv_post(y_f, y_b, px_rwkv, rwkv_a2[0], rwkv_a0[0][:, None, :], rwkv_g2[0], rwkv_k_a,
                    rwkv_r_k.reshape(1, RWKV_WIDTH), rwkv_ln_g, rwkv_ln_b)

    o_f, o_b = _gla_scan(px_gla, pc_gla, gla_alpha_up[0], gla_alpha_b[0][:, None, :], prec)
    yb = _gla_post(o_f, o_b, px_gla, gla_norm_g)

    m = _merge(ya, yb, w_rwkv_up[0], w_gla_up[0], px_gate)
    x1, h2 = _mix_out(m, w_out[0].astype(BF16), x, gx1, norm_post_mix, norm_pre_ffn, shx2, scx2)
    hf = _ffn_up(h2, ffn_w_gate[0], ffn_w_up[0])
    return _ffn_down(hf, ffn_w_down[0].astype(BF16), x1, gx2, norm_post_ffn)
```

```python
import functools

import jax
import jax.numpy as jnp
from jax import lax
from jax.experimental import pallas as pl
from jax.experimental.pallas import tpu as pltpu

F32 = jnp.float32
BF16 = jnp.bfloat16
HIGHEST = lax.Precision.HIGHEST

LANES = 128
SUBLANES = 8
VMEM_LIMIT_BYTES = 56 * 1024 * 1024

GRID_W = 64
CHUNK = 64
RWKV_HEADS, RWKV_HEAD_DIM = 16, 64
RWKV_WIDTH = RWKV_HEADS * RWKV_HEAD_DIM
HEAD_SHIFT = RWKV_HEAD_DIM.bit_length() - 1
DECAY_LORA = ICL_LORA = 96
GATE_LORA = 64
RWKV_GN_EPS = 64e-5
GLA_HEADS, GLA_KEY_DIM, GLA_VAL_DIM = 4, 128, 256
GLA_QK_WIDTH = GLA_HEADS * GLA_KEY_DIM
GLA_V_WIDTH = GLA_HEADS * GLA_VAL_DIM
GLA_GATE_LORA = 16
GLA_TAU = 16.0
LOG2_E = 1.4426950408889634
GLA_NORM_EPS = 1e-5
GLA_SUB = 8
NORM_EPS = 1e-6

RWKV_IN = 3 * RWKV_WIDTH + 2 * DECAY_LORA + 2 * ICL_LORA + GATE_LORA
RWKV_PAD = 3584
OFF_WD = 3 * RWKV_WIDTH
OFF_AD = OFF_WD + 2 * DECAY_LORA
OFF_GD = OFF_AD + 2 * ICL_LORA
GLA_IN = 2 * GLA_QK_WIDTH + 2 * GLA_V_WIDTH + 2 * GLA_GATE_LORA
GLA_BLK = 512
GLA_PAD = 7 * GLA_BLK
OFF_GLA_AD = 6 * GLA_BLK
OFF_GLA_G = 2 * GLA_QK_WIDTH + GLA_V_WIDTH

PAIR = 2 * RWKV_HEAD_DIM
N_PAIRS = RWKV_HEADS // 2


def _cparams(semantics):
    return pltpu.CompilerParams(dimension_semantics=semantics, vmem_limit_bytes=VMEM_LIMIT_BYTES)


def _sigmoid(z):
    return 1.0 / (1.0 + jnp.exp(-z))


def _softplus(z):
    return jnp.maximum(z, 0.0) + jnp.log(1.0 + jnp.exp(-jnp.abs(z)))


def _dot(a, b, dims, precision):
    if precision is BF16:
        a, b, precision = a.astype(BF16), b.astype(BF16), None
    return lax.dot_general(a, b, (dims, ((), ())), precision=precision, preferred_element_type=F32)


def _mm(a, b, precision=None):
    return _dot(a, b, ((1,), (0,)), precision)


def _mm_nt(a, b, precision=None):
    return _dot(a, b, ((1,), (1,)), precision)


def _mm_tn(a, b, precision=None):
    return _dot(a, b, ((0,), (0,)), precision)


def _split2(z):
    hi = z.astype(BF16)
    return hi, (z - hi.astype(F32)).astype(BF16)


def _mm_split(a, b):
    ah, al = _split2(a)
    bh, bl = _split2(b)
    return (_mm(al, bh) + _mm(ah, bl)) + _mm(ah, bh)


def _mm_exact_lhs(e, b):
    hi = b.astype(BF16)
    r = b - hi.astype(F32)
    mid = r.astype(BF16)
    lo = (r - mid.astype(F32)).astype(BF16)
    e = e.astype(BF16)
    return (_mm(e, lo) + _mm(e, mid)) + _mm(e, hi)


def _mod_kernel(c_ref, w_ref, b_ref, o_ref):
    s = c_ref[...]
    s = s * _sigmoid(s)
    o_ref[...] = _mm_split(s, w_ref[...]) + b_ref[...]


def _modulation(cvecs, ada_w, ada_b, tn=1024):
    m, d = cvecs.shape
    n = ada_w.shape[1]
    return pl.pallas_call(
        _mod_kernel,
        grid=(n // tn,),
        in_specs=[pl.BlockSpec((m, d), lambda j: (0, 0)),
                  pl.BlockSpec((d, tn), lambda j: (0, j)),
                  pl.BlockSpec((1, tn), lambda j: (0, j))],
        out_specs=pl.BlockSpec((m, tn), lambda j: (0, j)),
        out_shape=jax.ShapeDtypeStruct((m, n), F32),
        compiler_params=_cparams(("arbitrary",)),
        name="adaln_mod",
    )(cvecs, ada_w, ada_b.reshape(1, n))


def _prenorm_kernel(x_ref, sh_ref, sc_ref, g_ref, o_ref):
    x = x_ref[0]
    ms = jnp.mean(x * x, axis=-1, keepdims=True)
    y = x * lax.rsqrt(ms + NORM_EPS) * g_ref[...]
    o_ref[0] = (y * (1.0 + sc_ref[0]) + sh_ref[0]).astype(o_ref.dtype)


def _prenorm(x, shift, scale, gain, tm=256):
    b, t, d = x.shape
    row = lambda bi, i: (bi, i, 0)
    per_b = lambda bi, i: (bi, 0, 0)
    return pl.pallas_call(
        _prenorm_kernel,
        grid=(b, t // tm),
        in_specs=[pl.BlockSpec((1, tm, d), row), pl.BlockSpec((1, 1, d), per_b), pl.BlockSpec((1, 1, d), per_b),
                  pl.BlockSpec((1, d), lambda bi, i: (0, 0))],
        out_specs=pl.BlockSpec((1, tm, d), row),
        out_shape=jax.ShapeDtypeStruct((b, t, d), BF16),
        compiler_params=_cparams(("arbitrary", "arbitrary")),
        name="prenorm",
    )(x, shift, scale, gain)


def _proj_kernel(gate_out, h_ref, w_ref, o_ref):
    p = _mm(h_ref[0], w_ref[...])
    o_ref[0] = (_sigmoid(p) if gate_out else p).astype(o_ref.dtype)


def _project(h, w, tm, tn, name, gate_out=False):
    b, t, d = h.shape
    n = w.shape[1]
    return pl.pallas_call(
        functools.partial(_proj_kernel, gate_out),
        grid=(b, t // tm, n // tn),
        in_specs=[pl.BlockSpec((1, tm, d), lambda bi, i, j: (bi, i, 0)),
                  pl.BlockSpec((d, tn), lambda bi, i, j: (0, j))],
        out_specs=pl.BlockSpec((1, tm, tn), lambda bi, i, j: (bi, i, j)),
        out_shape=jax.ShapeDtypeStruct((b, t, n), BF16 if gate_out else F32),
        compiler_params=_cparams(("arbitrary", "arbitrary", "arbitrary")),
        name=name,
    )(h, w)


def _colmajor_perm(n_rows, n_cols):
    dst = jnp.arange(n_rows * n_cols)
    src = (dst % n_rows) * n_cols + dst // n_rows
    return (src[:, None] == dst[None, :]).astype(BF16)


def _proj_cm_kernel(h_ref, perm_ref, w_ref, o_ref, h_scr):
    @pl.when(pl.program_id(2) == 0)
    def _():
        n_r, n_c, d = h_ref.shape[1:]
        for cb in range(d // GLA_BLK):
            cs = slice(cb * GLA_BLK, (cb + 1) * GLA_BLK)
            hr = h_ref[0, :, :, cs].astype(F32).reshape(n_r * n_c, GLA_BLK).astype(BF16)
            h_scr[:, cs] = _mm(perm_ref[...], hr).astype(BF16)

    o_ref[0] = _mm(h_scr[...], w_ref[...])


def _project_colmajor(h, w, n_cols, tn, name):
    b, t, d = h.shape
    n = w.shape[1]
    n_r = t // GRID_W
    tm = n_r * n_cols
    h4 = h.reshape(b, n_r, GRID_W, d)
    perm = _colmajor_perm(n_r, n_cols)
    return pl.pallas_call(
        _proj_cm_kernel,
        grid=(b, GRID_W // n_cols, n // tn),
        in_specs=[pl.BlockSpec((1, n_r, n_cols, d), lambda bi, i, j: (bi, 0, i, 0)),
                  pl.BlockSpec((tm, tm), lambda bi, i, j: (0, 0)),
                  pl.BlockSpec((d, tn), lambda bi, i, j: (0, j))],
        out_specs=pl.BlockSpec((1, tm, tn), lambda bi, i, j: (bi, i, j)),
        out_shape=jax.ShapeDtypeStruct((b, t, n), F32),
        scratch_shapes=[pltpu.VMEM((tm, d), BF16)],
        compiler_params=_cparams(("arbitrary", "arbitrary", "arbitrary")),
        name=name,
    )(h4, perm, w)


HALO = 16


def _proj_shift_kernel(h_ref, hp_ref, hn_ref, w_ref, mu_ref, o_ref):
    i = pl.program_id(1)
    tm = h_ref.shape[1]
    p = _mm(h_ref[0], w_ref[...])
    ph = _mm(jnp.concatenate([hp_ref[0], hn_ref[0]], 0), w_ref[...])
    before = jnp.where(i == 0, 0.0, ph[HALO - 1:HALO])
    after = jnp.where(i == pl.num_programs(1) - 1, 0.0, ph[HALO:HALO + 1])
    row = lax.broadcasted_iota(jnp.int32, p.shape, 0)
    prev = jnp.where(row == 0, before, pltpu.roll(p, 1, 0))
    nxt = jnp.where(row == tm - 1, after, pltpu.roll(p, tm - 1, 0))
    o_ref[0] = p + mu_ref[...] * (0.5 * (prev + nxt) - p)


def _project_shift(h, w, mu, tm, tn, name):
    b, t, d = h.shape
    n = w.shape[1]
    rh, nh = tm // HALO, t // HALO
    return pl.pallas_call(
        _proj_shift_kernel,
        grid=(b, t // tm, n // tn),
        in_specs=[pl.BlockSpec((1, tm, d), lambda bi, i, j: (bi, i, 0)),
                  pl.BlockSpec((1, HALO, d), lambda bi, i, j: (bi, jnp.maximum(i * rh - 1, 0), 0)),
                  pl.BlockSpec((1, HALO, d), lambda bi, i, j: (bi, jnp.minimum((i + 1) * rh, nh - 1), 0)),
                  pl.BlockSpec((d, tn), lambda bi, i, j: (0, j)),
                  pl.BlockSpec((1, tn), lambda bi, i, j: (0, j))],
        out_specs=pl.BlockSpec((1, tm, tn), lambda bi, i, j: (bi, i, j)),
        out_shape=jax.ShapeDtypeStruct((b, t, n), F32),
        compiler_params=_cparams(("arbitrary", "arbitrary", "arbitrary")),
        name=name,
    )(h, h, h, w, mu)


def _head_sum_matrix():
    r = lax.broadcasted_iota(jnp.int32, (PAIR, PAIR), 0)
    c = lax.broadcasted_iota(jnp.int32, (PAIR, PAIR), 1)
    return jnp.where((r >> HEAD_SHIFT) == (c >> HEAD_SHIFT), 1.0, 0.0).astype(BF16)


def _head_sum(z, hsum):
    hi = z.astype(BF16)
    lo = (z - hi.astype(F32)).astype(BF16)
    return _mm(hi, hsum) + _mm(lo, hsum)


def _rwkv_kernel(prec, n_ctx_chunks, plf_ref, pcf_ref, plb_ref, pcb_ref, w2_ref, w0_ref, a2_ref, a0_ref, kk_ref, ka_ref,
                 yf_ref, yb_ref, st_scr):
    @pl.when(pl.program_id(1) == 0)
    def _():
        st_scr[...] = jnp.zeros_like(st_scr)

    is_ctx = pl.program_id(1) < n_ctx_chunks
    y_refs = (yf_ref, yb_ref)
    prep = [_rwkv_prep(dr == 1, jnp.where(is_ctx, pc_ref[0], pl_ref[0]), w2_ref.at[dr], w0_ref.at[dr], a2_ref.at[dr],
                       a0_ref.at[dr], kk_ref, ka_ref)
            for dr, (pl_ref, pc_ref) in enumerate(((plf_ref, pcf_ref), (plb_ref, pcb_ref)))]

    items = [(dr, pr) for pr in range(N_PAIRS) for dr in range(2)]
    n = range(len(items))
    per_pair = lambda name: [prep[dr][name][pr] for dr, pr in items]
    per_dir = lambda name: [prep[dr][name] for dr, _ in items]
    at, bt, kt, rt_, rs, bh, kh, v_swap, v_stack, p_end = (per_pair(k) for k in (
        "at", "bt", "kt", "rt", "rs", "bh", "kh", "v_swap", "v_stack", "p_end"))
    mask_n, mask_k, incl_c = per_dir("mask_n"), per_dir("mask_k"), per_dir("incl_c")

    c2 = 2 * CHUNK
    ri = lax.broadcasted_iota(jnp.int32, (c2, PAIR), 0)
    li = lax.broadcasted_iota(jnp.int32, (c2, PAIR), 1)
    ident = ri == li
    eye = jnp.where(ident, 1.0, 0.0).astype(F32)
    lane_e = lax.broadcasted_iota(jnp.int32, (CHUNK, PAIR), 1) < RWKV_HEAD_DIM
    cat0 = lambda *z: jnp.concatenate(z, 0)

    g_e = [_mm_nt(cat0(at[i][0], rs[i][0]), cat0(bt[i], kt[i]), prec) for i in n]
    g_o = [_mm_nt(cat0(at[i][1], rs[i][1]), cat0(kt[i], bt[i]), prec) for i in n]
    g_top = [cat0(g_e[i][0:CHUNK], g_o[i][0:CHUNK]) for i in n]
    nbd = [jnp.where(mask_n[i], g_top[i], 0.0) for i in n]
    aak = [jnp.where(mask_k[i], g_top[i], 0.0) for i in n]
    rab = [jnp.where(incl_c[i], jnp.where(lane_e, g_e[i][CHUNK:c2], g_o[i][CHUNK:c2]), 0.0) for i in n]
    rak = [jnp.where(incl_c[i], jnp.where(lane_e, g_o[i][CHUNK:c2], g_e[i][CHUNK:c2]), 0.0) for i in n]

    x = [_mm(aak[i], v_swap[i], prec) for i in n]
    n2 = [_mm(z, z, prec) for z in nbd]
    y0b = [_mm(rak[i], v_swap[i], prec) for i in n]
    n4 = [_mm(z, z, prec) for z in n2]
    imn = [eye - z for z in nbd]
    p1 = [imn[i] + _mm(imn[i], n2[i], prec) for i in n]
    n8 = [_mm(z, z, prec) for z in n4]
    nb = [_mm_tn(kh[i], v_stack[i], prec) for i in n]
    n16 = [_mm(z, z, prec) for z in n8]
    p2 = [eye + n4[i] + n8[i] + _mm(n4[i], n8[i], prec) for i in n]
    n32 = [_mm(z, z, prec) for z in n16]
    p12 = [_mm(p1[i], p2[i], prec) for i in n]
    p3 = [eye + n16[i] + n32[i] + _mm(n16[i], n32[i], prec) for i in n]
    tinv = [_mm(p12[i], p3[i], prec) for i in n]
    wu = [-_mm(tinv[i], jnp.concatenate([cat0(*at[i]), x[i]], 1), prec) for i in n]
    qy = [_mm(rab[i], wu[i], prec) for i in n]
    mn = [_mm_tn(bh[i], wu[i], prec) for i in n]
    q = [rt_[i] + qy[i][:, 0:PAIR] for i in n]
    m = [jnp.where(ident, p_end[i], 0.0) + mn[i][:, 0:PAIR] for i in n]
    qm = [_mm(cat0(q[i], m[i]), st_scr[dr, pr], prec) for i, (dr, pr) in enumerate(items)]
    for i, (dr, pr) in enumerate(items):
        y_refs[dr][0, :, pr * PAIR:(pr + 1) * PAIR] = qm[i][0:CHUNK] + qy[i][:, PAIR:2 * PAIR] + y0b[i]
        st_scr[dr, pr] = qm[i][CHUNK:CHUNK + PAIR] + mn[i][:, PAIR:2 * PAIR] + nb[i]


def _rwkv_prep(rev, p, w2_ref, w0_ref, a2_ref, a0_ref, kk_ref, ka_ref):
    hd = RWKV_HEAD_DIM
    c2 = 2 * CHUNK
    r = p[:, 0:RWKV_WIDTH]
    k = p[:, RWKV_WIDTH:2 * RWKV_WIDTH]
    v = p[:, 2 * RWKV_WIDTH:3 * RWKV_WIDTH]
    d_off = DECAY_LORA if rev else 0
    wd = p[:, OFF_WD + d_off:OFF_WD + d_off + DECAY_LORA]
    ad = p[:, OFF_AD + d_off:OFF_AD + d_off + ICL_LORA]

    w_log = -_softplus(-(w0_ref[...] + _mm_split(jnp.tanh(wd), w2_ref[...]))) - 0.5
    lw = -jnp.exp(w_log) * LOG2_E
    a = _sigmoid(a0_ref[...] + _mm(ad, a2_ref[...], BF16))
    kk_raw = k * kk_ref[...]
    kd = k * (1.0 + (a - 1.0) * ka_ref[...])

    ri = lax.broadcasted_iota(jnp.int32, (c2, PAIR), 0)
    li = lax.broadcasted_iota(jnp.int32, (c2, PAIR), 1)
    rt, lt = ri & (CHUNK - 1), li & (hd - 1)
    same_head = (ri >> HEAD_SHIFT) == (li >> HEAD_SHIFT)
    strict = (lt > rt) if rev else (lt < rt)
    rc = lax.broadcasted_iota(jnp.int32, (CHUNK, PAIR), 0)
    lc = lax.broadcasted_iota(jnp.int32, (CHUNK, PAIR), 1)
    lane_e = lc < hd
    hsum = _head_sum_matrix()

    ci = lax.broadcasted_iota(jnp.int32, (CHUNK, CHUNK), 0)
    cj = lax.broadcasted_iota(jnp.int32, (CHUNK, CHUNK), 1)
    tri = jnp.where((cj >= ci) if rev else (cj <= ci), 1.0, 0.0).astype(F32)
    cum = _mm_exact_lhs(tri, lw)
    total = cum[0:1] if rev else cum[CHUNK - 1:CHUNK]
    e_prev = jnp.exp2(cum - lw)
    e_neg = jnp.exp2(-cum)
    e_pos = jnp.exp2(cum)
    e_rest = jnp.exp2(total - cum)
    p_end = jnp.exp2(total)

    def split(z):
        ze = jnp.where(lane_e, z, 0.0)
        return ze, z - ze

    sls = [slice(pr * PAIR, (pr + 1) * PAIR) for pr in range(N_PAIRS)]
    cat0 = lambda *z: jnp.concatenate(z, 0)
    kkr = [kk_raw[:, sl] for sl in sls]
    nrm2 = [_head_sum(z * z, hsum) for z in kkr]
    kk = [z / jnp.maximum(jnp.sqrt(n), 1e-12) for z, n in zip(kkr, nrm2)]
    bb = [z * a[:, sl] for z, sl in zip(kk, sls)]
    rt_ = [r[:, sl] * e_pos[:, sl] for sl in sls]
    vs = [split(v[:, sl]) for sl in sls]
    return dict(
        mask_n=jnp.logical_and(same_head, strict),
        mask_k=jnp.logical_and(jnp.logical_not(same_head), strict),
        incl_c=((lc & (hd - 1)) >= rc) if rev else ((lc & (hd - 1)) <= rc),
        at=[split(z * e_prev[:, sl]) for z, sl in zip(kk, sls)],
        bt=[z * e_neg[:, sl] for z, sl in zip(bb, sls)],
        kt=[kd[:, sl] * e_neg[:, sl] for sl in sls],
        rt=rt_,
        rs=[split(z) for z in rt_],
        bh=[cat0(*split(z * e_rest[:, sl])) for z, sl in zip(bb, sls)],
        kh=[cat0(*split(kd[:, sl] * e_rest[:, sl])) for sl in sls],
        v_swap=[cat0(vo, ve) for ve, vo in vs],
        v_stack=[cat0(ve, vo) for ve, vo in vs],
        p_end=[p_end[:, sl] for sl in sls],
    )


def _scan_order(n_ctx_chunks, n_lat):
    n_steps = n_ctx_chunks + n_lat
    lat_f = lambda i: jnp.maximum(i - n_ctx_chunks, 0)
    lat_b = lambda i: jnp.where(i < n_ctx_chunks, n_lat - 1, n_steps - 1 - i)
    ctx_f = lambda i: jnp.minimum(i, n_ctx_chunks - 1)
    ctx_b = lambda i: jnp.maximum(n_ctx_chunks - 1 - i, 0)
    return n_steps, (lat_f, lat_b), (ctx_f, ctx_b)


def _rwkv_scan(p_lat, p_ctx, w2, w0, a2, a0, k_k, k_a, prec):
    b, t, w = p_lat.shape
    n_lat = t // CHUNK
    n_ctx_chunks = p_ctx.shape[1] // CHUNK
    n_steps, lat_of, ctx_of = _scan_order(n_ctx_chunks, n_lat)
    chunk = lambda f: pl.BlockSpec((1, CHUNK, w), lambda bi, i: (bi, f(i), 0))
    out = lambda f: pl.BlockSpec((1, CHUNK, RWKV_WIDTH), lambda bi, i: (bi, f(i), 0))
    vec = lambda bi, i: (0, 0)
    vec3 = lambda bi, i: (0, 0, 0)
    y_shape = jax.ShapeDtypeStruct((b, t, RWKV_WIDTH), F32)
    return pl.pallas_call(
        functools.partial(_rwkv_kernel, prec, n_ctx_chunks),
        grid=(b, n_steps),
        in_specs=[chunk(lat_of[0]), chunk(ctx_of[0]), chunk(lat_of[1]), chunk(ctx_of[1]),
                  pl.BlockSpec((2, DECAY_LORA, RWKV_WIDTH), vec3),
                  pl.BlockSpec((2, 1, RWKV_WIDTH), vec3),
                  pl.BlockSpec((2, ICL_LORA, RWKV_WIDTH), vec3),
                  pl.BlockSpec((2, 1, RWKV_WIDTH), vec3),
                  pl.BlockSpec((1, RWKV_WIDTH), vec),
                  pl.BlockSpec((1, RWKV_WIDTH), vec)],
        out_specs=[out(lat_of[0]), out(lat_of[1])],
        out_shape=[y_shape, y_shape],
        scratch_shapes=[pltpu.VMEM((2, N_PAIRS, PAIR, PAIR), F32)],
        compiler_params=_cparams(("arbitrary", "arbitrary")),
        name="rwkv7_scan",
    )(p_lat, p_ctx, p_lat, p_ctx, w2, w0, a2, a0, k_k, k_a)


def _gla_kernel(prec, n_ctx_chunks, *refs):
    lat = (refs[0:6], refs[6:12])
    aup_ref, ab_ref, of_ref, ob_ref, st_scr = refs[12:17]
    o_refs = (of_ref, ob_ref)

    @pl.when(pl.program_id(1) == 0)
    def _():
        st_scr[...] = jnp.zeros_like(st_scr)

    is_ctx = pl.program_id(1) < n_ctx_chunks
    dk, dv, sb = GLA_KEY_DIM, GLA_VAL_DIM, GLA_SUB
    n_sb = CHUNK // sb
    prep = [_gla_prep(dr == 1, is_ctx, *lat[dr], aup_ref.at[dr], ab_ref.at[dr]) for dr in range(2)]

    items = [(dr, h) for h in range(GLA_HEADS) for dr in range(2)]
    n = range(len(items))
    revs = [dr == 1 for dr, _ in items]
    qh, kh, bh, lah, toth, vh = ([prep[dr][name][h] for dr, h in items] for name in ("q", "k", "b", "la", "tot", "v"))
    st = [st_scr[dr, h] for dr, h in items]

    o_inter = [_mm_nt(qh[i] * jnp.exp2(bh[i]), st[i], prec) for i in n]
    st_new = [st[i] * jnp.exp2(toth[i]) + _mm_tn(vh[i], kh[i] * jnp.exp2(toth[i] - bh[i]), prec) for i in n]

    arow = lax.broadcasted_iota(jnp.int32, (CHUNK, CHUNK), 0)
    acol = lax.broadcasted_iota(jnp.int32, (CHUNK, CHUNK), 1)
    bcol = lax.broadcasted_iota(jnp.int32, (sb, CHUNK), 1)
    off_rows = [[] for _ in n]
    for blk in range(n_sb):
        rs = slice(blk * sb, (blk + 1) * sb)
        for i in n:
            rev = revs[i]
            if (blk == n_sb - 1) if rev else (blk == 0):
                off_rows[i].append(jnp.zeros((sb, CHUNK), F32))
                continue
            first = blk * sb + (sb - 1 if rev else 0)
            before = (bcol >= (blk + 1) * sb) if rev else (bcol < blk * sb)
            beta = bh[i][first:first + 1] - lah[i][first:first + 1]
            qs = qh[i][rs] * jnp.exp2(bh[i][rs] - beta)
            ksc = kh[i] * jnp.exp2(jnp.minimum(beta - bh[i], 0.0))
            off_rows[i].append(jnp.where(before, _mm_nt(qs, ksc, prec), 0.0))
    att = [jnp.concatenate(off_rows[i], 0) for i in n]

    in_blk = arow & (sb - 1)
    for s in range(sb):
        pick = lambda z: jnp.concatenate(
            [jnp.broadcast_to(z[blk * sb + s:blk * sb + s + 1], (sb, dk)) for blk in range(n_sb)], 0)
        on_col = acol == (arow & ~(sb - 1)) + s
        tgt = (jnp.logical_and(on_col, in_blk >= s), jnp.logical_and(on_col, in_blk <= s))
        for i in n:
            e = jnp.exp2(jnp.minimum(bh[i] - pick(bh[i]), 0.0))
            col = jnp.sum(qh[i] * pick(kh[i]) * e, axis=-1, keepdims=True)
            att[i] = jnp.where(tgt[revs[i]], col, att[i])

    for i, (dr, h) in enumerate(items):
        o_refs[dr][0, :, h * dv:(h + 1) * dv] = o_inter[i] + _mm(att[i], vh[i], prec)
        st_scr[dr, h] = st_new[i]


def _gla_prep(rev, is_ctx, q_ref, k_ref, v0_ref, v1_ref, ad_ref, ctx_ref, aup_ref, ab_ref):
    pc = ctx_ref[0]
    q = jnp.where(is_ctx, pc[:, 0:GLA_BLK], q_ref[0])
    k = jnp.where(is_ctx, pc[:, GLA_BLK:2 * GLA_BLK], k_ref[0])
    v = jnp.concatenate([jnp.where(is_ctx, pc[:, 2 * GLA_BLK:3 * GLA_BLK], v0_ref[0]),
                         jnp.where(is_ctx, pc[:, 3 * GLA_BLK:4 * GLA_BLK], v1_ref[0])], 1)
    d_off = GLA_GATE_LORA if rev else 0
    ad = jnp.where(is_ctx, pc[:, OFF_GLA_AD:OFF_GLA_AD + LANES], ad_ref[0][:, 0:LANES])
    ad = ad[:, d_off:d_off + GLA_GATE_LORA]

    la = -_softplus(-(_mm_split(ad, aup_ref[...]) + ab_ref[...])) * (LOG2_E / GLA_TAU)
    ci = lax.broadcasted_iota(jnp.int32, (CHUNK, CHUNK), 0)
    cj = lax.broadcasted_iota(jnp.int32, (CHUNK, CHUNK), 1)
    tri = jnp.where((cj >= ci) if rev else (cj <= ci), 1.0, 0.0).astype(F32)
    cum = _mm_exact_lhs(tri, la)
    total = cum[0:1] if rev else cum[CHUNK - 1:CHUNK]
    dk, dv = GLA_KEY_DIM, GLA_VAL_DIM
    ksl = [slice(h * dk, (h + 1) * dk) for h in range(GLA_HEADS)]
    scale = GLA_KEY_DIM ** -0.5
    return dict(q=[q[:, s_] * scale for s_ in ksl], k=[k[:, s_] for s_ in ksl], b=[cum[:, s_] for s_ in ksl],
                la=[la[:, s_] for s_ in ksl], tot=[total[:, s_] for s_ in ksl],
                v=[v[:, h * dv:(h + 1) * dv] for h in range(GLA_HEADS)])


def _gla_scan(p_lat, p_ctx, alpha_up, alpha_b, prec):
    b, t, w = p_lat.shape
    assert w == GLA_PAD and t == GRID_W * CHUNK
    n_ctx_chunks = p_ctx.shape[1] // CHUNK
    n_steps, col_of, ctx_of = _scan_order(n_ctx_chunks, GRID_W)

    def blocks(dr):
        lat = lambda m: pl.BlockSpec((1, CHUNK, GLA_BLK), lambda bi, i: (bi, col_of[dr](i), m))
        return [lat(0), lat(1), lat(2), lat(3), lat(OFF_GLA_AD // GLA_BLK), pl.BlockSpec((1, CHUNK, w), lambda bi, i: (bi, ctx_of[dr](i), 0))]

    out = lambda dr: pl.BlockSpec((1, CHUNK, GLA_V_WIDTH), lambda bi, i: (bi, col_of[dr](i), 0))
    vec3 = lambda bi, i: (0, 0, 0)
    o_shape = jax.ShapeDtypeStruct((b, t, GLA_V_WIDTH), F32)
    return pl.pallas_call(
        functools.partial(_gla_kernel, prec, n_ctx_chunks),
        grid=(b, n_steps),
        in_specs=blocks(0) + blocks(1) + [pl.BlockSpec((2, GLA_GATE_LORA, GLA_QK_WIDTH), vec3),
                                          pl.BlockSpec((2, 1, GLA_QK_WIDTH), vec3)],
        out_specs=[out(0), out(1)],
        out_shape=[o_shape, o_shape],
        scratch_shapes=[pltpu.VMEM((2, GLA_HEADS, GLA_VAL_DIM, GLA_KEY_DIM), F32)],
        compiler_params=_cparams(("arbitrary", "arbitrary")),
        name="gla_scan",
    )(*([p_lat] * 5 + [p_ctx]) * 2, alpha_up, alpha_b)


def _rwkv_post_kernel(yf_ref, yb_ref, pm_ref, a2_ref, a0_ref, g2_ref, ka_ref, rk_ref, lng_ref, lnb_ref, o_ref):
    p = pm_ref[0]
    r = p[:, 0:RWKV_WIDTH]
    k = p[:, RWKV_WIDTH:2 * RWKV_WIDTH]
    v = p[:, 2 * RWKV_WIDTH:3 * RWKV_WIDTH]
    ad_f = p[:, OFF_AD:OFF_AD + ICL_LORA]
    ad_b = p[:, OFF_AD + ICL_LORA:OFF_AD + 2 * ICL_LORA]
    gd = p[:, OFF_GD:OFF_GD + GATE_LORA]
    ka = ka_ref[...]
    a_f = _sigmoid(a0_ref[0] + _mm(ad_f, a2_ref[0], BF16))
    a_b = _sigmoid(a0_ref[1] + _mm(ad_b, a2_ref[1], BF16))
    kd_sum = k * (2.0 + (a_f + a_b - 2.0) * ka)
    gate = _mm(_sigmoid(gd), g2_ref[...], BF16)
    rkk = r * kd_sum * rk_ref[...]
    ysum = yf_ref[0] + yb_ref[0]
    hsum = _head_sum_matrix()
    inv_n = 1.0 / RWKV_HEAD_DIM
    for pr in range(N_PAIRS):
        sl = slice(pr * PAIR, (pr + 1) * PAIR)
        ys = ysum[:, sl]
        mean = _head_sum(ys, hsum) * inv_n
        dlt = ys - mean
        var = _head_sum(dlt * dlt, hsum) * inv_n
        gn = dlt * lax.rsqrt(var + RWKV_GN_EPS) * lng_ref[:, sl] + lnb_ref[:, sl]
        bonus = _head_sum(rkk[:, sl], hsum) * v[:, sl]
        o_ref[0, :, sl] = ((gn + bonus) * gate[:, sl]).astype(o_ref.dtype)


def _rwkv_post(y_f, y_b, pmix, a2, a0, g2, k_a, r_k, ln_g, ln_b, tm=256):
    b, t, w = y_f.shape
    row = lambda bi, i: (bi, i, 0)
    vec = lambda bi, i: (0, 0)
    vec3 = lambda bi, i: (0, 0, 0)
    return pl.pallas_call(
        _rwkv_post_kernel,
        grid=(b, t // tm),
        in_specs=[pl.BlockSpec((1, tm, w), row),
                  pl.BlockSpec((1, tm, w), row),
                  pl.BlockSpec((1, tm, pmix.shape[2]), row),
                  pl.BlockSpec((2, ICL_LORA, w), vec3),
                  pl.BlockSpec((2, 1, w), vec3),
                  pl.BlockSpec((GATE_LORA, w), vec),
                  pl.BlockSpec((1, w), vec), pl.BlockSpec((1, w), vec),
                  pl.BlockSpec((1, w), vec), pl.BlockSpec((1, w), vec)],
        out_specs=pl.BlockSpec((1, tm, w), row),
        out_shape=jax.ShapeDtypeStruct((b, t, w), BF16),
        compiler_params=_cparams(("arbitrary", "arbitrary")),
        name="rwkv_post",
    )(y_f, y_b, pmix, a2, a0, g2, k_a, r_k, ln_g, ln_b)


def _gla_post_kernel(of_ref, ob_ref, g_ref, ng_ref, perm_ref, o_ref):
    n_r, n_c = o_ref.shape[1:3]
    dv = GLA_VAL_DIM
    for h in range(GLA_HEADS):
        sl = slice(h * dv, (h + 1) * dv)
        oh = of_ref[0, :, sl] + ob_ref[0, :, sl]
        oh = oh * lax.rsqrt(jnp.mean(oh * oh, axis=-1, keepdims=True) + GLA_NORM_EPS) * ng_ref[:, sl]
        gh = g_ref[0, :, sl]
        y = (oh * (gh * _sigmoid(gh))).astype(BF16)
        o_ref[0, :, :, sl] = _mm(perm_ref[...], y).reshape(n_r, n_c, dv).astype(o_ref.dtype)


def _gla_post(o_f, o_b, p_gla, norm_g, n_cols=16):
    b, t, w = o_f.shape
    n_r = t // GRID_W
    tm = n_r * n_cols
    perm_t = _colmajor_perm(n_r, n_cols).T
    row = lambda bi, i: (bi, i, 0)
    out = pl.pallas_call(
        _gla_post_kernel,
        grid=(b, GRID_W // n_cols),
        in_specs=[pl.BlockSpec((1, tm, w), row),
                  pl.BlockSpec((1, tm, w), row),
                  pl.BlockSpec((1, tm, w), lambda bi, i: (bi, i, OFF_GLA_G // w)),
                  pl.BlockSpec((1, w), lambda bi, i: (0, 0)),
                  pl.BlockSpec((tm, tm), lambda bi, i: (0, 0))],
        out_specs=pl.BlockSpec((1, n_r, n_cols, w), lambda bi, i: (bi, 0, i, 0)),
        out_shape=jax.ShapeDtypeStruct((b, n_r, GRID_W, w), BF16),
        compiler_params=_cparams(("arbitrary", "arbitrary")),
        name="gla_post",
    )(o_f, o_b, p_gla, norm_g, perm_t)
    return out.reshape(b, t, w)


def _merge_kernel(ya_ref, yb_ref, wr_ref, wg_ref, ga_ref, gb_ref, o_ref, wr_scr, wg_scr):
    @pl.when(jnp.logical_and(pl.program_id(1) == 0, pl.program_id(2) == 0))
    def _():
        wr_scr[...] = wr_ref[...].astype(BF16)
        wg_scr[...] = wg_ref[...].astype(BF16)

    ma = _mm(ya_ref[0], wr_scr[...])
    mb = _mm(yb_ref[0], wg_scr[...])
    o_ref[0] = (ga_ref[0].astype(F32) * ma + gb_ref[0].astype(F32) * mb).astype(o_ref.dtype)


def _merge(ya, yb, w_r, w_g, p_gate, tm=1024, tn=1024):
    b, t, w = ya.shape
    d = w_r.shape[1]
    nj = d // tn
    return pl.pallas_call(
        _merge_kernel,
        grid=(nj, b, t // tm),
        in_specs=[pl.BlockSpec((1, tm, w), lambda j, bi, i: (bi, i, 0)),
                  pl.BlockSpec((1, tm, w), lambda j, bi, i: (bi, i, 0)),
                  pl.BlockSpec((w, tn), lambda j, bi, i: (0, j)),
                  pl.BlockSpec((w, tn), lambda j, bi, i: (0, j)),
                  pl.BlockSpec((1, tm, tn), lambda j, bi, i: (bi, i, j)),
                  pl.BlockSpec((1, tm, tn), lambda j, bi, i: (bi, i, j + nj))],
        out_specs=pl.BlockSpec((1, tm, tn), lambda j, bi, i: (bi, i, j)),
        out_shape=jax.ShapeDtypeStruct((b, t, d), BF16),
        scratch_shapes=[pltpu.VMEM((w, tn), BF16), pltpu.VMEM((w, tn), BF16)],
        compiler_params=_cparams(("arbitrary", "arbitrary", "arbitrary")),
        name="merge_branches",
    )(ya, yb, w_r, w_g, p_gate, p_gate)


def _mix_out_kernel(m_ref, w_ref, x_ref, gate_ref, npost_ref, npre_ref, sh_ref, sc_ref, x1_ref, h_ref):
    half = m_ref.shape[1] // 2
    for rs in (slice(0, half), slice(half, 2 * half)):
        z = _mm(m_ref[0, rs, :], w_ref[...])
        z = z * lax.rsqrt(jnp.mean(z * z, axis=-1, keepdims=True) + NORM_EPS) * npost_ref[...]
        x1 = x_ref[0, rs, :] + gate_ref[0] * z
        x1_ref[0, rs, :] = x1
        y = x1 * lax.rsqrt(jnp.mean(x1 * x1, axis=-1, keepdims=True) + NORM_EPS) * npre_ref[...]
        h_ref[0, rs, :] = (y * (1.0 + sc_ref[0]) + sh_ref[0]).astype(h_ref.dtype)


def _mix_out(m, w_out, x, gate, n_post, n_pre, shift, scale, tm=512):
    b, t, d = x.shape
    row = lambda bi, i: (bi, i, 0)
    per_b = lambda bi, i: (bi, 0, 0)
    vec = lambda bi, i: (0, 0)
    return pl.pallas_call(
        _mix_out_kernel,
        grid=(b, t // tm),
        in_specs=[pl.BlockSpec((1, tm, d), row),
                  pl.BlockSpec((d, d), vec),
                  pl.BlockSpec((1, tm, d), row),
                  pl.BlockSpec((1, 1, d), per_b),
                  pl.BlockSpec((1, d), vec), pl.BlockSpec((1, d), vec),
                  pl.BlockSpec((1, 1, d), per_b), pl.BlockSpec((1, 1, d), per_b)],
        out_specs=[pl.BlockSpec((1, tm, d), row), pl.BlockSpec((1, tm, d), row)],
        out_shape=[jax.ShapeDtypeStruct((b, t, d), F32), jax.ShapeDtypeStruct((b, t, d), BF16)],
        compiler_params=_cparams(("arbitrary", "arbitrary")),
        name="mix_out",
    )(m, w_out, x, gate, n_post, n_pre, shift, scale)


def _ffn_up_kernel(h_ref, wg_ref, wu_ref, o_ref, wg_scr, wu_scr):
    @pl.when(jnp.logical_and(pl.program_id(1) == 0, pl.program_id(2) == 0))
    def _():
        wg_scr[...] = wg_ref[...].astype(BF16)
        wu_scr[...] = wu_ref[...].astype(BF16)

    h = h_ref[0]
    a = _mm(h, wg_scr[...])
    u = _mm(h, wu_scr[...])
    o_ref[0] = (a * _sigmoid(a) * u).astype(o_ref.dtype)


def _ffn_up(h, w_gate, w_up, tm=1024, tn=512):
    b, t, d = h.shape
    f = w_gate.shape[1]
    return pl.pallas_call(
        _ffn_up_kernel,
        grid=(f // tn, b, t // tm),
        in_specs=[pl.BlockSpec((1, tm, d), lambda j, bi, i: (bi, i, 0)),
                  pl.BlockSpec((d, tn), lambda j, bi, i: (0, j)),
                  pl.BlockSpec((d, tn), lambda j, bi, i: (0, j))],
        out_specs=pl.BlockSpec((1, tm, tn), lambda j, bi, i: (bi, i, j)),
        out_shape=jax.ShapeDtypeStruct((b, t, f), BF16),
        scratch_shapes=[pltpu.VMEM((d, tn), BF16), pltpu.VMEM((d, tn), BF16)],
        compiler_params=_cparams(("arbitrary", "arbitrary", "arbitrary")),
        name="ffn_up",
    )(h, w_gate, w_up)


def _ffn_down_kernel(h_ref, w_ref, x_ref, gate_ref, npost_ref, o_ref, z_scr):
    j = pl.program_id(2)
    n_j = z_scr.shape[0]
    z_scr[j] = _mm(h_ref[0], w_ref[...])

    @pl.when(j == n_j - 1)
    def _():
        z = jnp.concatenate([z_scr[t] for t in range(n_j)], 1)
        z = z * lax.rsqrt(jnp.mean(z * z, axis=-1, keepdims=True) + NORM_EPS) * npost_ref[...]
        o_ref[0] = x_ref[0] + gate_ref[0] * z


def _ffn_down(h, w_down, x1, gate, n_post, tm=512, tn=512):
    b, t, f = h.shape
    d = w_down.shape[1]
    return pl.pallas_call(
        _ffn_down_kernel,
        grid=(b, t // tm, d // tn),
        in_specs=[pl.BlockSpec((1, tm, f), lambda bi, i, j: (bi, i, 0)),
                  pl.BlockSpec((f, tn), lambda bi, i, j: (0, j)),
                  pl.BlockSpec((1, tm, d), lambda bi, i, j: (bi, i, 0)),
                  pl.BlockSpec((1, 1, d), lambda bi, i, j: (bi, 0, 0)),
                  pl.BlockSpec((1, d), lambda bi, i, j: (0, 0))],
        out_specs=pl.BlockSpec((1, tm, d), lambda bi, i, j: (bi, i, 0)),
        out_shape=jax.ShapeDtypeStruct((b, t, d), F32),
        scratch_shapes=[pltpu.VMEM((d // tn, tm, tn), F32)],
        compiler_params=_cparams(("arbitrary", "arbitrary", "arbitrary")),
        name="ffn_down",
    )(h, w_down, x1, gate, n_post)


def _pad_cols(w, n):
    return jnp.pad(w, ((0, 0), (0, n - w.shape[1])))


def kernel(x, c, ctx, c_ctx, ada_w, ada_b, norm_pre_mix, norm_post_mix, norm_pre_ffn, norm_post_ffn, w_in, shift_mu, rwkv_w0, rwkv_w2, rwkv_a0, rwkv_a2, rwkv_g2, rwkv_k_k, rwkv_k_a, rwkv_r_k, rwkv_ln_g, rwkv_ln_b, w_rwkv_up, gla_alpha_up, gla_alpha_b, gla_norm_g, w_gla_up, w_out, ffn_w_gate, ffn_w_up, ffn_w_down):
    assert ada_w.shape[0] == 1, "single trunk layer"
    bsz, seq, d = x.shape
    n_ctx = ctx.shape[1]
    prec = BF16

    cvecs = jnp.concatenate([c, c_ctx[None, :], jnp.zeros((SUBLANES - bsz - 1, d), F32)], 0)
    mod = _modulation(cvecs, ada_w[0], ada_b[0])
    mod_x = mod[:bsz].reshape(bsz, 6, 1, d)
    shx1, scx1, gx1, shx2, scx2, gx2 = (mod_x[:, i] for i in range(6))
    mod_c = jnp.broadcast_to(mod[bsz].reshape(1, 6, 1, d), (bsz, 6, 1, d))
    shc1, scc1 = mod_c[:, 0], mod_c[:, 1]

    w_all = w_in[0]
    mix_in = RWKV_IN + GLA_IN
    w_rwkv = _pad_cols(w_all[:, :RWKV_IN], RWKV_PAD).astype(BF16)
    w_gla = _pad_cols(w_all[:, RWKV_IN:mix_in], GLA_PAD).astype(BF16)
    w_gate = w_all[:, mix_in:].astype(BF16)
    mu = _pad_cols(shift_mu, RWKV_PAD)
    n_pre = norm_pre_mix

    hx = _prenorm(x, shx1, scx1, n_pre)
    hc = _prenorm(ctx, shc1, scc1, n_pre)
    px_rwkv = _project_shift(hx, w_rwkv, mu, 1024, 512, "proj_rwkv")
    px_gla = _project_colmajor(hx, w_gla, 16, 512, "proj_gla")
    px_gate = _project(hx, w_gate, 1024, 1024, "proj_gate", gate_out=True)
    pc_rwkv = _project_shift(hc, w_rwkv, mu, n_ctx, 512, "proj_rwkv_ctx")
    pc_gla = _project(hc, w_gla, n_ctx, 512, "proj_gla_ctx")

    y_f, y_b = _rwkv_scan(px_rwkv, pc_rwkv, rwkv_w2[0], rwkv_w0[0][:, None, :], rwkv_a2[0], rwkv_a0[0][:, None, :],
                          rwkv_k_k, rwkv_k_a, prec)
    ya = _rwkv_post(y_f, y_b, px_rwkv, rwkv_a2[0], rwkv_a0[0][:, None, :], rwkv_g2[0], rwkv_k_a,
                    rwkv_r_k.reshape(1, RWKV_WIDTH), rwkv_ln_g, rwkv_ln_b)

    o_f, o_b = _gla_scan(px_gla, pc_gla, gla_alpha_up[0], gla_alpha_b[0][:, None, :], prec)
    yb = _gla_post(o_f, o_b, px_gla, gla_norm_g)

    m = _merge(ya, yb, w_rwkv_up[0], w_gla_up[0], px_gate)
    x1, h2 = _mix_out(m, w_out[0].astype(BF16), x, gx1, norm_post_mix, norm_pre_ffn, shx2, scx2)
    hf = _ffn_up(h2, ffn_w_gate[0], ffn_w_up[0])
    return _ffn_down(hf, ffn_w_down[0].astype(BF16), x1, gx2, norm_post_ffn)
```

```python
import functools

import jax
import jax.numpy as jnp
from jax import lax
from jax.experimental import pallas as pl
from jax.experimental.pallas import tpu as pltpu

F32 = jnp.float32
BF16 = jnp.bfloat16
HIGHEST = lax.Precision.HIGHEST

LANES = 128
SUBLANES = 8
VMEM_LIMIT_BYTES = 56 * 1024 * 1024

GRID_W = 64
CHUNK = 64
RWKV_HEADS, RWKV_HEAD_DIM = 16, 64
RWKV_WIDTH = RWKV_HEADS * RWKV_HEAD_DIM
HEAD_SHIFT = RWKV_HEAD_DIM.bit_length() - 1
DECAY_LORA = ICL_LORA = 96
GATE_LORA = 64
RWKV_GN_EPS = 64e-5
GLA_HEADS, GLA_KEY_DIM, GLA_VAL_DIM = 4, 128, 256
GLA_QK_WIDTH = GLA_HEADS * GLA_KEY_DIM
GLA_V_WIDTH = GLA_HEADS * GLA_VAL_DIM
GLA_GATE_LORA = 16
GLA_TAU = 16.0
LOG2_E = 1.4426950408889634
GLA_NORM_EPS = 1e-5
GLA_SUB = 8
NORM_EPS = 1e-6

RWKV_IN = 3 * RWKV_WIDTH + 2 * DECAY_LORA + 2 * ICL_LORA + GATE_LORA
RWKV_PAD = 3584
OFF_WD = 3 * RWKV_WIDTH
OFF_AD = OFF_WD + 2 * DECAY_LORA
OFF_GD = OFF_AD + 2 * ICL_LORA
GLA_IN = 2 * GLA_QK_WIDTH + 2 * GLA_V_WIDTH + 2 * GLA_GATE_LORA
GLA_BLK = 512
GLA_PAD = 7 * GLA_BLK
OFF_GLA_AD = 6 * GLA_BLK
OFF_GLA_G = 2 * GLA_QK_WIDTH + GLA_V_WIDTH

PAIR = 2 * RWKV_HEAD_DIM
N_PAIRS = RWKV_HEADS // 2


def _cparams(semantics):
    return pltpu.CompilerParams(dimension_semantics=semantics, vmem_limit_bytes=VMEM_LIMIT_BYTES)


def _sigmoid(z):
    return 1.0 / (1.0 + jnp.exp(-z))


def _softplus(z):
    return jnp.maximum(z, 0.0) + jnp.log(1.0 + jnp.exp(-jnp.abs(z)))


def _dot(a, b, dims, precision):
    if precision is BF16:
        a, b, precision = a.astype(BF16), b.astype(BF16), None
    return lax.dot_general(a, b, (dims, ((), ())), precision=precision, preferred_element_type=F32)


def _mm(a, b, precision=None):
    return _dot(a, b, ((1,), (0,)), precision)


def _mm_nt(a, b, precision=None):
    return _dot(a, b, ((1,), (1,)), precision)


def _mm_tn(a, b, precision=None):
    return _dot(a, b, ((0,), (0,)), precision)


def _split2(z):
    hi = z.astype(BF16)
    return hi, (z - hi.astype(F32)).astype(BF16)


def _mm_split(a, b):
    ah, al = _split2(a)
    bh, bl = _split2(b)
    return (_mm(al, bh) + _mm(ah, bl)) + _mm(ah, bh)


def _mm_exact_lhs(e, b):
    hi = b.astype(BF16)
    r = b - hi.astype(F32)
    mid = r.astype(BF16)
    lo = (r - mid.astype(F32)).astype(BF16)
    e = e.astype(BF16)
    return (_mm(e, lo) + _mm(e, mid)) + _mm(e, hi)


def _mod_kernel(c_ref, w_ref, b_ref, o_ref):
    s = c_ref[...]
    s = s * _sigmoid(s)
    o_ref[...] = _mm_split(s, w_ref[...]) + b_ref[...]


def _modulation(cvecs, ada_w, ada_b, tn=1024):
    m, d = cvecs.shape
    n = ada_w.shape[1]
    return pl.pallas_call(
        _mod_kernel,
        grid=(n // tn,),
        in_specs=[pl.BlockSpec((m, d), lambda j: (0, 0)),
                  pl.BlockSpec((d, tn), lambda j: (0, j)),
                  pl.BlockSpec((1, tn), lambda j: (0, j))],
        out_specs=pl.BlockSpec((m, tn), lambda j: (0, j)),
        out_shape=jax.ShapeDtypeStruct((m, n), F32),
        compiler_params=_cparams(("arbitrary",)),
        name="adaln_mod",
    )(cvecs, ada_w, ada_b.reshape(1, n))


def _prenorm_kernel(x_ref, sh_ref, sc_ref, g_ref, o_ref):
    x = x_ref[0]
    ms = jnp.mean(x * x, axis=-1, keepdims=True)
    y = x * lax.rsqrt(ms + NORM_EPS) * g_ref[...]
    o_ref[0] = (y * (1.0 + sc_ref[0]) + sh_ref[0]).astype(o_ref.dtype)


def _prenorm(x, shift, scale, gain, tm=256):
    b, t, d = x.shape
    row = lambda bi, i: (bi, i, 0)
    per_b = lambda bi, i: (bi, 0, 0)
    return pl.pallas_call(
        _prenorm_kernel,
        grid=(b, t // tm),
        in_specs=[pl.BlockSpec((1, tm, d), row), pl.BlockSpec((1, 1, d), per_b), pl.BlockSpec((1, 1, d), per_b),
                  pl.BlockSpec((1, d), lambda bi, i: (0, 0))],
        out_specs=pl.BlockSpec((1, tm, d), row),
        out_shape=jax.ShapeDtypeStruct((b, t, d), BF16),
        compiler_params=_cparams(("arbitrary", "arbitrary")),
        name="prenorm",
    )(x, shift, scale, gain)


def _proj_kernel(gate_out, h_ref, w_ref, o_ref):
    p = _mm(h_ref[0], w_ref[...])
    o_ref[0] = (_sigmoid(p) if gate_out else p).astype(o_ref.dtype)


def _project(h, w, tm, tn, name, gate_out=False):
    b, t, d = h.shape
    n = w.shape[1]
    return pl.pallas_call(
        functools.partial(_proj_kernel, gate_out),
        grid=(b, t // tm, n // tn),
        in_specs=[pl.BlockSpec((1, tm, d), lambda bi, i, j: (bi, i, 0)),
                  pl.BlockSpec((d, tn), lambda bi, i, j: (0, j))],
        out_specs=pl.BlockSpec((1, tm, tn), lambda bi, i, j: (bi, i, j)),
        out_shape=jax.ShapeDtypeStruct((b, t, n), BF16 if gate_out else F32),
        compiler_params=_cparams(("arbitrary", "arbitrary", "arbitrary")),
        name=name,
    )(h, w)


def _colmajor_perm(n_rows, n_cols):
    dst = jnp.arange(n_rows * n_cols)
    src = (dst % n_rows) * n_cols + dst // n_rows
    return (src[:, None] == dst[None, :]).astype(BF16)


def _proj_cm_kernel(h_ref, perm_ref, w_ref, o_ref, h_scr):
    @pl.when(pl.program_id(2) == 0)
    def _():
        n_r, n_c, d = h_ref.shape[1:]
        for cb in range(d // GLA_BLK):
            cs = slice(cb * GLA_BLK, (cb + 1) * GLA_BLK)
            hr = h_ref[0, :, :, cs].astype(F32).reshape(n_r * n_c, GLA_BLK).astype(BF16)
            h_scr[:, cs] = _mm(perm_ref[...], hr).astype(BF16)

    o_ref[0] = _mm(h_scr[...], w_ref[...])


def _project_colmajor(h, w, n_cols, tn, name):
    b, t, d = h.shape
    n = w.shape[1]
    n_r = t // GRID_W
    tm = n_r * n_cols
    h4 = h.reshape(b, n_r, GRID_W, d)
    perm = _colmajor_perm(n_r, n_cols)
    return pl.pallas_call(
        _proj_cm_kernel,
        grid=(b, GRID_W // n_cols, n // tn),
        in_specs=[pl.BlockSpec((1, n_r, n_cols, d), lambda bi, i, j: (bi, 0, i, 0)),
                  pl.BlockSpec((tm, tm), lambda bi, i, j: (0, 0)),
                  pl.BlockSpec((d, tn), lambda bi, i, j: (0, j))],
        out_specs=pl.BlockSpec((1, tm, tn), lambda bi, i, j: (bi, i, j)),
        out_shape=jax.ShapeDtypeStruct((b, t, n), F32),
        scratch_shapes=[pltpu.VMEM((tm, d), BF16)],
        compiler_params=_cparams(("arbitrary", "arbitrary", "arbitrary")),
        name=name,
    )(h4, perm, w)


HALO = 16


def _proj_shift_kernel(h_ref, hp_ref, hn_ref, w_ref, mu_ref, o_ref):
    i = pl.program_id(1)
    tm = h_ref.shape[1]
    p = _mm(h_ref[0], w_ref[...])
    ph = _mm(jnp.concatenate([hp_ref[0], hn_ref[0]], 0), w_ref[...])
    before = jnp.where(i == 0, 0.0, ph[HALO - 1:HALO])
    after = jnp.where(i == pl.num_programs(1) - 1, 0.0, ph[HALO:HALO + 1])
    row = lax.broadcasted_iota(jnp.int32, p.shape, 0)
    prev = jnp.where(row == 0, before, pltpu.roll(p, 1, 0))
    nxt = jnp.where(row == tm - 1, after, pltpu.roll(p, tm - 1, 0))
    o_ref[0] = p + mu_ref[...] * (0.5 * (prev + nxt) - p)


def _project_shift(h, w, mu, tm, tn, name):
    b, t, d = h.shape
    n = w.shape[1]
    rh, nh = tm // HALO, t // HALO
    return pl.pallas_call(
        _proj_shift_kernel,
        grid=(b, t // tm, n // tn),
        in_specs=[pl.BlockSpec((1, tm, d), lambda bi, i, j: (bi, i, 0)),
                  pl.BlockSpec((1, HALO, d), lambda bi, i, j: (bi, jnp.maximum(i * rh - 1, 0), 0)),
                  pl.BlockSpec((1, HALO, d), lambda bi, i, j: (bi, jnp.minimum((i + 1) * rh, nh - 1), 0)),
                  pl.BlockSpec((d, tn), lambda bi, i, j: (0, j)),
                  pl.BlockSpec((1, tn), lambda bi, i, j: (0, j))],
        out_specs=pl.BlockSpec((1, tm, tn), lambda bi, i, j: (bi, i, j)),
        out_shape=jax.ShapeDtypeStruct((b, t, n), F32),
        compiler_params=_cparams(("arbitrary", "arbitrary", "arbitrary")),
        name=name,
    )(h, h, h, w, mu)


def _head_sum_matrix():
    r = lax.broadcasted_iota(jnp.int32, (PAIR, PAIR), 0)
    c = lax.broadcasted_iota(jnp.int32, (PAIR, PAIR), 1)
    return jnp.where((r >> HEAD_SHIFT) == (c >> HEAD_SHIFT), 1.0, 0.0).astype(BF16)


def _head_sum(z, hsum):
    hi = z.astype(BF16)
    lo = (z - hi.astype(F32)).astype(BF16)
    return _mm(hi, hsum) + _mm(lo, hsum)


def _rwkv_kernel(prec, n_ctx_chunks, plf_ref, pcf_ref, plb_ref, pcb_ref, w2_ref, w0_ref, a2_ref, a0_ref, kk_ref, ka_ref,
                 yf_ref, yb_ref, st_scr):
    @pl.when(pl.program_id(1) == 0)
    def _():
        st_scr[...] = jnp.zeros_like(st_scr)

    is_ctx = pl.program_id(1) < n_ctx_chunks
    y_refs = (yf_ref, yb_ref)
    prep = [_rwkv_prep(dr == 1, jnp.where(is_ctx, pc_ref[0], pl_ref[0]), w2_ref.at[dr], w0_ref.at[dr], a2_ref.at[dr],
                       a0_ref.at[dr], kk_ref, ka_ref)
            for dr, (pl_ref, pc_ref) in enumerate(((plf_ref, pcf_ref), (plb_ref, pcb_ref)))]

    items = [(dr, pr) for pr in range(N_PAIRS) for dr in range(2)]
    n = range(len(items))
    per_pair = lambda name: [prep[dr][name][pr] for dr, pr in items]
    per_dir = lambda name: [prep[dr][name] for dr, _ in items]
    at, bt, kt, rt_, rs, bh, kh, v_swap, v_stack, p_end = (per_pair(k) for k in (
        "at", "bt", "kt", "rt", "rs", "bh", "kh", "v_swap", "v_stack", "p_end"))
    mask_n, mask_k, incl_c = per_dir("mask_n"), per_dir("mask_k"), per_dir("incl_c")

    c2 = 2 * CHUNK
    ri = lax.broadcasted_iota(jnp.int32, (c2, PAIR), 0)
    li = lax.broadcasted_iota(jnp.int32, (c2, PAIR), 1)
    ident = ri == li
    eye = jnp.where(ident, 1.0, 0.0).astype(F32)
    lane_e = lax.broadcasted_iota(jnp.int32, (CHUNK, PAIR), 1) < RWKV_HEAD_DIM
    cat0 = lambda *z: jnp.concatenate(z, 0)

    g_e = [_mm_nt(cat0(at[i][0], rs[i][0]), cat0(bt[i], kt[i]), prec) for i in n]
    g_o = [_mm_nt(cat0(at[i][1], rs[i][1]), cat0(kt[i], bt[i]), prec) for i in n]
    g_top = [cat0(g_e[i][0:CHUNK], g_o[i][0:CHUNK]) for i in n]
    nbd = [jnp.where(mask_n[i], g_top[i], 0.0) for i in n]
    aak = [jnp.where(mask_k[i], g_top[i], 0.0) for i in n]
    rab = [jnp.where(incl_c[i], jnp.where(lane_e, g_e[i][CHUNK:c2], g_o[i][CHUNK:c2]), 0.0) for i in n]
    rak = [jnp.where(incl_c[i], jnp.where(lane_e, g_o[i][CHUNK:c2], g_e[i][CHUNK:c2]), 0.0) for i in n]

    x = [_mm(aak[i], v_swap[i], prec) for i in n]
    n2 = [_mm(z, z, prec) for z in nbd]
    y0b = [_mm(rak[i], v_swap[i], prec) for i in n]
    n4 = [_mm(z, z, prec) for z in n2]
    imn = [eye - z for z in nbd]
    p1 = [imn[i] + _mm(imn[i], n2[i], prec) for i in n]
    n8 = [_mm(z, z, prec) for z in n4]
    nb = [_mm_tn(kh[i], v_stack[i], prec) for i in n]
    n16 = [_mm(z, z, prec) for z in n8]
    p2 = [eye + n4[i] + n8[i] + _mm(n4[i], n8[i], prec) for i in n]
    n32 = [_mm(z, z, prec) for z in n16]
    p12 = [_mm(p1[i], p2[i], prec) for i in n]
    p3 = [eye + n16[i] + n32[i] + _mm(n16[i], n32[i], prec) for i in n]
    tinv = [_mm(p12[i], p3[i], prec) for i in n]
    wu = [-_mm(tinv[i], jnp.concatenate([cat0(*at[i]), x[i]], 1), prec) for i in n]
    qy = [_mm(rab[i], wu[i], prec) for i in n]
    mn = [_mm_tn(bh[i], wu[i], prec) for i in n]
    q = [rt_[i] + qy[i][:, 0:PAIR] for i in n]
    m = [jnp.where(ident, p_end[i], 0.0) + mn[i][:, 0:PAIR] for i in n]
    qm = [_mm(cat0(q[i], m[i]), st_scr[dr, pr], prec) for i, (dr, pr) in enumerate(items)]
    for i, (dr, pr) in enumerate(items):
        y_refs[dr][0, :, pr * PAIR:(pr + 1) * PAIR] = qm[i][0:CHUNK] + qy[i][:, PAIR:2 * PAIR] + y0b[i]
        st_scr[dr, pr] = qm[i][CHUNK:CHUNK + PAIR] + mn[i][:, PAIR:2 * PAIR] + nb[i]


def _rwkv_prep(rev, p, w2_ref, w0_ref, a2_ref, a0_ref, kk_ref, ka_ref):
    hd = RWKV_HEAD_DIM
    c2 = 2 * CHUNK
    r = p[:, 0:RWKV_WIDTH]
    k = p[:, RWKV_WIDTH:2 * RWKV_WIDTH]
    v = p[:, 2 * RWKV_WIDTH:3 * RWKV_WIDTH]
    d_off = DECAY_LORA if rev else 0
    wd = p[:, OFF_WD + d_off:OFF_WD + d_off + DECAY_LORA]
    ad = p[:, OFF_AD + d_off:OFF_AD + d_off + ICL_LORA]

    w_log = -_softplus(-(w0_ref[...] + _mm_split(jnp.tanh(wd), w2_ref[...]))) - 0.5
    lw = -jnp.exp(w_log) * LOG2_E
    a = _sigmoid(a0_ref[...] + _mm(ad, a2_ref[...], BF16))
    kk_raw = k * kk_ref[...]
    kd = k * (1.0 + (a - 1.0) * ka_ref[...])

    ri = lax.broadcasted_iota(jnp.int32, (c2, PAIR), 0)
    li = lax.broadcasted_iota(jnp.int32, (c2, PAIR), 1)
    rt, lt = ri & (CHUNK - 1), li & (hd - 1)
    same_head = (ri >> HEAD_SHIFT) == (li >> HEAD_SHIFT)
    strict = (lt > rt) if rev else (lt < rt)
    rc = lax.broadcasted_iota(jnp.int32, (CHUNK, PAIR), 0)
    lc = lax.broadcasted_iota(jnp.int32, (CHUNK, PAIR), 1)
    lane_e = lc < hd
    hsum = _head_sum_matrix()

    ci = lax.broadcasted_iota(jnp.int32, (CHUNK, CHUNK), 0)
    cj = lax.broadcasted_iota(jnp.int32, (CHUNK, CHUNK), 1)
    tri = jnp.where((cj >= ci) if rev else (cj <= ci), 1.0, 0.0).astype(F32)
    cum = _mm_exact_lhs(tri, lw)
    total = cum[0:1] if rev else cum[CHUNK - 1:CHUNK]
    e_prev = jnp.exp2(cum - lw)
    e_neg = jnp.exp2(-cum)
    e_pos = jnp.exp2(cum)
    e_rest = jnp.exp2(total - cum)
    p_end = jnp.exp2(total)

    def split(z):
        ze = jnp.where(lane_e, z, 0.0)
        return ze, z - ze

    sls = [slice(pr * PAIR, (pr + 1) * PAIR) for pr in range(N_PAIRS)]
    cat0 = lambda *z: jnp.concatenate(z, 0)
    kkr = [kk_raw[:, sl] for sl in sls]
    nrm2 = [_head_sum(z * z, hsum) for z in kkr]
    kk = [z / jnp.maximum(jnp.sqrt(n), 1e-12) for z, n in zip(kkr, nrm2)]
    bb = [z * a[:, sl] for z, sl in zip(kk, sls)]
    rt_ = [r[:, sl] * e_pos[:, sl] for sl in sls]
    vs = [split(v[:, sl]) for sl in sls]
    return dict(
        mask_n=jnp.logical_and(same_head, strict),
        mask_k=jnp.logical_and(jnp.logical_not(same_head), strict),
        incl_c=((lc & (hd - 1)) >= rc) if rev else ((lc & (hd - 1)) <= rc),
        at=[split(z * e_prev[:, sl]) for z, sl in zip(kk, sls)],
        bt=[z * e_neg[:, sl] for z, sl in zip(bb, sls)],
        kt=[kd[:, sl] * e_neg[:, sl] for sl in sls],
        rt=rt_,
        rs=[split(z) for z in rt_],
        bh=[cat0(*split(z * e_rest[:, sl])) for z, sl in zip(bb, sls)],
        kh=[cat0(*split(kd[:, sl] * e_rest[:, sl])) for sl in sls],
        v_swap=[cat0(vo, ve) for ve, vo in vs],
        v_stack=[cat0(ve, vo) for ve, vo in vs],
        p_end=[p_end[:, sl] for sl in sls],
    )


def _scan_order(n_ctx_chunks, n_lat):
    n_steps = n_ctx_chunks + n_lat
    lat_f = lambda i: jnp.maximum(i - n_ctx_chunks, 0)
    lat_b = lambda i: jnp.where(i < n_ctx_chunks, n_lat - 1, n_steps - 1 - i)
    ctx_f = lambda i: jnp.minimum(i, n_ctx_chunks - 1)
    ctx_b = lambda i: jnp.maximum(n_ctx_chunks - 1 - i, 0)
    return n_steps, (lat_f, lat_b), (ctx_f, ctx_b)


def _rwkv_scan(p_lat, p_ctx, w2, w0, a2, a0, k_k, k_a, prec):
    b, t, w = p_lat.shape
    n_lat = t // CHUNK
    n_ctx_chunks = p_ctx.shape[1] // CHUNK
    n_steps, lat_of, ctx_of = _scan_order(n_ctx_chunks, n_lat)
    chunk = lambda f: pl.BlockSpec((1, CHUNK, w), lambda bi, i: (bi, f(i), 0))
    out = lambda f: pl.BlockSpec((1, CHUNK, RWKV_WIDTH), lambda bi, i: (bi, f(i), 0))
    vec = lambda bi, i: (0, 0)
    vec3 = lambda bi, i: (0, 0, 0)
    y_shape = jax.ShapeDtypeStruct((b, t, RWKV_WIDTH), F32)
    return pl.pallas_call(
        functools.partial(_rwkv_kernel, prec, n_ctx_chunks),
        grid=(b, n_steps),
        in_specs=[chunk(lat_of[0]), chunk(ctx_of[0]), chunk(lat_of[1]), chunk(ctx_of[1]),
                  pl.BlockSpec((2, DECAY_LORA, RWKV_WIDTH), vec3),
                  pl.BlockSpec((2, 1, RWKV_WIDTH), vec3),
                  pl.BlockSpec((2, ICL_LORA, RWKV_WIDTH), vec3),
                  pl.BlockSpec((2, 1, RWKV_WIDTH), vec3),
                  pl.BlockSpec((1, RWKV_WIDTH), vec),
                  pl.BlockSpec((1, RWKV_WIDTH), vec)],
        out_specs=[out(lat_of[0]), out(lat_of[1])],
        out_shape=[y_shape, y_shape],
        scratch_shapes=[pltpu.VMEM((2, N_PAIRS, PAIR, PAIR), F32)],
        compiler_params=_cparams(("arbitrary", "arbitrary")),
        name="rwkv7_scan",
    )(p_lat, p_ctx, p_lat, p_ctx, w2, w0, a2, a0, k_k, k_a)


def _gla_kernel(prec, n_ctx_chunks, *refs):
    lat = (refs[0:6], refs[6:12])
    aup_ref, ab_ref, of_ref, ob_ref, st_scr = refs[12:17]
    o_refs = (of_ref, ob_ref)

    @pl.when(pl.program_id(1) == 0)
    def _():
        st_scr[...] = jnp.zeros_like(st_scr)

    is_ctx = pl.program_id(1) < n_ctx_chunks
    dk, dv, sb = GLA_KEY_DIM, GLA_VAL_DIM, GLA_SUB
    n_sb = CHUNK // sb
    prep = [_gla_prep(dr == 1, is_ctx, *lat[dr], aup_ref.at[dr], ab_ref.at[dr]) for dr in range(2)]

    items = [(dr, h) for h in range(GLA_HEADS) for dr in range(2)]
    n = range(len(items))
    revs = [dr == 1 for dr, _ in items]
    qh, kh, bh, lah, toth, vh = ([prep[dr][name][h] for dr, h in items] for name in ("q", "k", "b", "la", "tot", "v"))
    st = [st_scr[dr, h] for dr, h in items]

    o_inter = [_mm_nt(qh[i] * jnp.exp2(bh[i]), st[i], prec) for i in n]
    st_new = [st[i] * jnp.exp2(toth[i]) + _mm_tn(vh[i], kh[i] * jnp.exp2(toth[i] - bh[i]), prec) for i in n]

    arow = lax.broadcasted_iota(jnp.int32, (CHUNK, CHUNK), 0)
    acol = lax.broadcasted_iota(jnp.int32, (CHUNK, CHUNK), 1)
    bcol = lax.broadcasted_iota(jnp.int32, (sb, CHUNK), 1)
    off_rows = [[] for _ in n]
    for blk in range(n_sb):
        rs = slice(blk * sb, (blk + 1) * sb)
        for i in n:
            rev = revs[i]
            if (blk == n_sb - 1) if rev else (blk == 0):
                off_rows[i].append(jnp.zeros((sb, CHUNK), F32))
                continue
            first = blk * sb + (sb - 1 if rev else 0)
            before = (bcol >= (blk + 1) * sb) if rev else (bcol < blk * sb)
            beta = bh[i][first:first + 1] - lah[i][first:first + 1]
            qs = qh[i][rs] * jnp.exp2(bh[i][rs] - beta)
            ksc = kh[i] * jnp.exp2(jnp.minimum(beta - bh[i], 0.0))
            off_rows[i].append(jnp.where(before, _mm_nt(qs, ksc, prec), 0.0))
    att = [jnp.concatenate(off_rows[i], 0) for i in n]

    in_blk = arow & (sb - 1)
    for s in range(sb):
        pick = lambda z: jnp.concatenate(
            [jnp.broadcast_to(z[blk * sb + s:blk * sb + s + 1], (sb, dk)) for blk in range(n_sb)], 0)
        on_col = acol == (arow & ~(sb - 1)) + s
        tgt = (jnp.logical_and(on_col, in_blk >= s), jnp.logical_and(on_col, in_blk <= s))
        for i in n:
            e = jnp.exp2(jnp.minimum(bh[i] - pick(bh[i]), 0.0))
            col = jnp.sum(qh[i] * pick(kh[i]) * e, axis=-1, keepdims=True)
            att[i] = jnp.where(tgt[revs[i]], col, att[i])

    for i, (dr, h) in enumerate(items):
        o_refs[dr][0, :, h * dv:(h + 1) * dv] = o_inter[i] + _mm(att[i], vh[i], prec)
        st_scr[dr, h] = st_new[i]


def _gla_prep(rev, is_ctx, q_ref, k_ref, v0_ref, v1_ref, ad_ref, ctx_ref, aup_ref, ab_ref):
    pc = ctx_ref[0]
    q = jnp.where(is_ctx, pc[:, 0:GLA_BLK], q_ref[0])
    k = jnp.where(is_ctx, pc[:, GLA_BLK:2 * GLA_BLK], k_ref[0])
    v = jnp.concatenate([jnp.where(is_ctx, pc[:, 2 * GLA_BLK:3 * GLA_BLK], v0_ref[0]),
                         jnp.where(is_ctx, pc[:, 3 * GLA_BLK:4 * GLA_BLK], v1_ref[0])], 1)
    d_off = GLA_GATE_LORA if rev else 0
    ad = jnp.where(is_ctx, pc[:, OFF_GLA_AD:OFF_GLA_AD + LANES], ad_ref[0][:, 0:LANES])
    ad = ad[:, d_off:d_off + GLA_GATE_LORA]

    la = -_softplus(-(_mm_split(ad, aup_ref[...]) + ab_ref[...])) * (LOG2_E / GLA_TAU)
    ci = lax.broadcasted_iota(jnp.int32, (CHUNK, CHUNK), 0)
    cj = lax.broadcasted_iota(jnp.int32, (CHUNK, CHUNK), 1)
    tri = jnp.where((cj >= ci) if rev else (cj <= ci), 1.0, 0.0).astype(F32)
    cum = _mm_exact_lhs(tri, la)
    total = cum[0:1] if rev else cum[CHUNK - 1:CHUNK]
    dk, dv = GLA_KEY_DIM, GLA_VAL_DIM
    ksl = [slice(h * dk, (h + 1) * dk) for h in range(GLA_HEADS)]
    scale = GLA_KEY_DIM ** -0.5
    return dict(q=[q[:, s_] * scale for s_ in ksl], k=[k[:, s_] for s_ in ksl], b=[cum[:, s_] for s_ in ksl],
                la=[la[:, s_] for s_ in ksl], tot=[total[:, s_] for s_ in ksl],
                v=[v[:, h * dv:(h + 1) * dv] for h in range(GLA_HEADS)])


def _gla_scan(p_lat, p_ctx, alpha_up, alpha_b, prec):
    b, t, w = p_lat.shape
    assert w == GLA_PAD and t == GRID_W * CHUNK
    n_ctx_chunks = p_ctx.shape[1] // CHUNK
    n_steps, col_of, ctx_of = _scan_order(n_ctx_chunks, GRID_W)

    def blocks(dr):
        lat = lambda m: pl.BlockSpec((1, CHUNK, GLA_BLK), lambda bi, i: (bi, col_of[dr](i), m))
        return [lat(0), lat(1), lat(2), lat(3), lat(OFF_GLA_AD // GLA_BLK), pl.BlockSpec((1, CHUNK, w), lambda bi, i: (bi, ctx_of[dr](i), 0))]

    out = lambda dr: pl.BlockSpec((1, CHUNK, GLA_V_WIDTH), lambda bi, i: (bi, col_of[dr](i), 0))
    vec3 = lambda bi, i: (0, 0, 0)
    o_shape = jax.ShapeDtypeStruct((b, t, GLA_V_WIDTH), F32)
    return pl.pallas_call(
        functools.partial(_gla_kernel, prec, n_ctx_chunks),
        grid=(b, n_steps),
        in_specs=blocks(0) + blocks(1) + [pl.BlockSpec((2, GLA_GATE_LORA, GLA_QK_WIDTH), vec3),
                                          pl.BlockSpec((2, 1, GLA_QK_WIDTH), vec3)],
        out_specs=[out(0), out(1)],
        out_shape=[o_shape, o_shape],
        scratch_shapes=[pltpu.VMEM((2, GLA_HEADS, GLA_VAL_DIM, GLA_KEY_DIM), F32)],
        compiler_params=_cparams(("arbitrary", "arbitrary")),
        name="gla_scan",
    )(*([p_lat] * 5 + [p_ctx]) * 2, alpha_up, alpha_b)


def _rwkv_post_kernel(yf_ref, yb_ref, pm_ref, a2_ref, a0_ref, g2_ref, ka_ref, rk_ref, lng_ref, lnb_ref, o_ref):
    p = pm_ref[0]
    r = p[:, 0:RWKV_WIDTH]
    k = p[:, RWKV_WIDTH:2 * RWKV_WIDTH]
    v = p[:, 2 * RWKV_WIDTH:3 * RWKV_WIDTH]
    ad_f = p[:, OFF_AD:OFF_AD + ICL_LORA]
    ad_b = p[:, OFF_AD + ICL_LORA:OFF_AD + 2 * ICL_LORA]
    gd = p[:, OFF_GD:OFF_GD + GATE_LORA]
    ka = ka_ref[...]
    a_f = _sigmoid(a0_ref[0] + _mm(ad_f, a2_ref[0], BF16))
    a_b = _sigmoid(a0_ref[1] + _mm(ad_b, a2_ref[1], BF16))
    kd_sum = k * (2.0 + (a_f + a_b - 2.0) * ka)
    gate = _mm(_sigmoid(gd), g2_ref[...], BF16)
    rkk = r * kd_sum * rk_ref[...]
    ysum = yf_ref[0] + yb_ref[0]
    hsum = _head_sum_matrix()
    inv_n = 1.0 / RWKV_HEAD_DIM
    for pr in range(N_PAIRS):
        sl = slice(pr * PAIR, (pr + 1) * PAIR)
        ys = ysum[:, sl]
        mean = _head_sum(ys, hsum) * inv_n
        dlt = ys - mean
        var = _head_sum(dlt * dlt, hsum) * inv_n
        gn = dlt * lax.rsqrt(var + RWKV_GN_EPS) * lng_ref[:, sl] + lnb_ref[:, sl]
        bonus = _head_sum(rkk[:, sl], hsum) * v[:, sl]
        o_ref[0, :, sl] = ((gn + bonus) * gate[:, sl]).astype(o_ref.dtype)


def _rwkv_post(y_f, y_b, pmix, a2, a0, g2, k_a, r_k, ln_g, ln_b, tm=256):
    b, t, w = y_f.shape
    row = lambda bi, i: (bi, i, 0)
    vec = lambda bi, i: (0, 0)
    vec3 = lambda bi, i: (0, 0, 0)
    return pl.pallas_call(
        _rwkv_post_kernel,
        grid=(b, t // tm),
        in_specs=[pl.BlockSpec((1, tm, w), row),
                  pl.BlockSpec((1, tm, w), row),
                  pl.BlockSpec((1, tm, pmix.shape[2]), row),
                  pl.BlockSpec((2, ICL_LORA, w), vec3),
                  pl.BlockSpec((2, 1, w), vec3),
                  pl.BlockSpec((GATE_LORA, w), vec),
                  pl.BlockSpec((1, w), vec), pl.BlockSpec((1, w), vec),
                  pl.BlockSpec((1, w), vec), pl.BlockSpec((1, w), vec)],
        out_specs=pl.BlockSpec((1, tm, w), row),
        out_shape=jax.ShapeDtypeStruct((b, t, w), BF16),
        compiler_params=_cparams(("arbitrary", "arbitrary")),
        name="rwkv_post",
    )(y_f, y_b, pmix, a2, a0, g2, k_a, r_k, ln_g, ln_b)


def _gla_post_kernel(of_ref, ob_ref, g_ref, ng_ref, perm_ref, o_ref):
    n_r, n_c = o_ref.shape[1:3]
    dv = GLA_VAL_DIM
    for h in range(GLA_HEADS):
        sl = slice(h * dv, (h + 1) * dv)
        oh = of_ref[0, :, sl] + ob_ref[0, :, sl]
        oh = oh * lax.rsqrt(jnp.mean(oh * oh, axis=-1, keepdims=True) + GLA_NORM_EPS) * ng_ref[:, sl]
        gh = g_ref[0, :, sl]
        y = (oh * (gh * _sigmoid(gh))).astype(BF16)
        o_ref[0, :, :, sl] = _mm(perm_ref[...], y).reshape(n_r, n_c, dv).astype(o_ref.dtype)


def _gla_post(o_f, o_b, p_gla, norm_g, n_cols=16):
    b, t, w = o_f.shape
    n_r = t // GRID_W
    tm = n_r * n_cols
    perm_t = _colmajor_perm(n_r, n_cols).T
    row = lambda bi, i: (bi, i, 0)
    out = pl.pallas_call(
        _gla_post_kernel,
        grid=(b, GRID_W // n_cols),
        in_specs=[pl.BlockSpec((1, tm, w), row),
                  pl.BlockSpec((1, tm, w), row),
                  pl.BlockSpec((1, tm, w), lambda bi, i: (bi, i, OFF_GLA_G // w)),
                  pl.BlockSpec((1, w), lambda bi, i: (0, 0)),
                  pl.BlockSpec((tm, tm), lambda bi, i: (0, 0))],
        out_specs=pl.BlockSpec((1, n_r, n_cols, w), lambda bi, i: (bi, 0, i, 0)),
        out_shape=jax.ShapeDtypeStruct((b, n_r, GRID_W, w), BF16),
        compiler_params=_cparams(("arbitrary", "arbitrary")),
        name="gla_post",
    )(o_f, o_b, p_gla, norm_g, perm_t)
    return out.reshape(b, t, w)


def _merge_kernel(ya_ref, yb_ref, wr_ref, wg_ref, ga_ref, gb_ref, o_ref, wr_scr, wg_scr):
    @pl.when(jnp.logical_and(pl.program_id(1) == 0, pl.program_id(2) == 0))
    def _():
        wr_scr[...] = wr_ref[...].astype(BF16)
        wg_scr[...] = wg_ref[...].astype(BF16)

    ma = _mm(ya_ref[0], wr_scr[...])
    mb = _mm(yb_ref[0], wg_scr[...])
    o_ref[0] = (ga_ref[0].astype(F32) * ma + gb_ref[0].astype(F32) * mb).astype(o_ref.dtype)


def _merge(ya, yb, w_r, w_g, p_gate, tm=1024, tn=1024):
    b, t, w = ya.shape
    d = w_r.shape[1]
    nj = d // tn
    return pl.pallas_call(
        _merge_kernel,
        grid=(nj, b, t // tm),
        in_specs=[pl.BlockSpec((1, tm, w), lambda j, bi, i: (bi, i, 0)),
                  pl.BlockSpec((1, tm, w), lambda j, bi, i: (bi, i, 0)),
                  pl.BlockSpec((w, tn), lambda j, bi, i: (0, j)),
                  pl.BlockSpec((w, tn), lambda j, bi, i: (0, j)),
                  pl.BlockSpec((1, tm, tn), lambda j, bi, i: (bi, i, j)),
                  pl.BlockSpec((1, tm, tn), lambda j, bi, i: (bi, i, j + nj))],
        out_specs=pl.BlockSpec((1, tm, tn), lambda j, bi, i: (bi, i, j)),
        out_shape=jax.ShapeDtypeStruct((b, t, d), BF16),
        scratch_shapes=[pltpu.VMEM((w, tn), BF16), pltpu.VMEM((w, tn), BF16)],
        compiler_params=_cparams(("arbitrary", "arbitrary", "arbitrary")),
        name="merge_branches",
    )(ya, yb, w_r, w_g, p_gate, p_gate)


def _mix_out_kernel(m_ref, w_ref, x_ref, gate_ref, npost_ref, npre_ref, sh_ref, sc_ref, x1_ref, h_ref):
    half = m_ref.shape[1] // 2
    for rs in (slice(0, half), slice(half, 2 * half)):
        z = _mm(m_ref[0, rs, :], w_ref[...])
        z = z * lax.rsqrt(jnp.mean(z * z, axis=-1, keepdims=True) + NORM_EPS) * npost_ref[...]
        x1 = x_ref[0, rs, :] + gate_ref[0] * z
        x1_ref[0, rs, :] = x1
        y = x1 * lax.rsqrt(jnp.mean(x1 * x1, axis=-1, keepdims=True) + NORM_EPS) * npre_ref[...]
        h_ref[0, rs, :] = (y * (1.0 + sc_ref[0]) + sh_ref[0]).astype(h_ref.dtype)


def _mix_out(m, w_out, x, gate, n_post, n_pre, shift, scale, tm=512):
    b, t, d = x.shape
    row = lambda bi, i: (bi, i, 0)
    per_b = lambda bi, i: (bi, 0, 0)
    vec = lambda bi, i: (0, 0)
    return pl.pallas_call(
        _mix_out_kernel,
        grid=(b, t // tm),
        in_specs=[pl.BlockSpec((1, tm, d), row),
                  pl.BlockSpec((d, d), vec),
                  pl.BlockSpec((1, tm, d), row),
                  pl.BlockSpec((1, 1, d), per_b),
                  pl.BlockSpec((1, d), vec), pl.BlockSpec((1, d), vec),
                  pl.BlockSpec((1, 1, d), per_b), pl.BlockSpec((1, 1, d), per_b)],
        out_specs=[pl.BlockSpec((1, tm, d), row), pl.BlockSpec((1, tm, d), row)],
        out_shape=[jax.ShapeDtypeStruct((b, t, d), F32), jax.ShapeDtypeStruct((b, t, d), BF16)],
        compiler_params=_cparams(("arbitrary", "arbitrary")),
        name="mix_out",
    )(m, w_out, x, gate, n_post, n_pre, shift, scale)


def _ffn_up_kernel(h_ref, wg_ref, wu_ref, o_ref, wg_scr, wu_scr):
    @pl.when(jnp.logical_and(pl.program_id(1) == 0, pl.program_id(2) == 0))
    def _():
        wg_scr[...] = wg_ref[...].astype(BF16)
        wu_scr[...] = wu_ref[...].astype(BF16)

    h = h_ref[0]
    a = _mm(h, wg_scr[...])
    u = _mm(h, wu_scr[...])
    o_ref[0] = (a * _sigmoid(a) * u).astype(o_ref.dtype)


def _ffn_up(h, w_gate, w_up, tm=2048, tn=512):
    b, t, d = h.shape
    f = w_gate.shape[1]
    return pl.pallas_call(
        _ffn_up_kernel,
        grid=(f // tn, b, t // tm),
        in_specs=[pl.BlockSpec((1, tm, d), lambda j, bi, i: (bi, i, 0)),
                  pl.BlockSpec((d, tn), lambda j, bi, i: (0, j)),
                  pl.BlockSpec((d, tn), lambda j, bi, i: (0, j))],
        out_specs=pl.BlockSpec((1, tm, tn), lambda j, bi, i: (bi, i, j)),
        out_shape=jax.ShapeDtypeStruct((b, t, f), BF16),
        scratch_shapes=[pltpu.VMEM((d, tn), BF16), pltpu.VMEM((d, tn), BF16)],
        compiler_params=_cparams(("arbitrary", "arbitrary", "arbitrary")),
        name="ffn_up",
    )(h, w_gate, w_up)


def _ffn_down_kernel(h_ref, w_ref, x_ref, gate_ref, npost_ref, o_ref, z_scr):
    j = pl.program_id(2)
    n_j = z_scr.shape[0]
    z_scr[j] = _mm(h_ref[0], w_ref[...])

    @pl.when(j == n_j - 1)
    def _():
        z = jnp.concatenate([z_scr[t] for t in range(n_j)], 1)
        z = z * lax.rsqrt(jnp.mean(z * z, axis=-1, keepdims=True) + NORM_EPS) * npost_ref[...]
        o_ref[0] = x_ref[0] + gate_ref[0] * z


def _ffn_down(h, w_down, x1, gate, n_post, tm=512, tn=512):
    b, t, f = h.shape
    d = w_down.shape[1]
    return pl.pallas_call(
        _ffn_down_kernel,
        grid=(b, t // tm, d // tn),
        in_specs=[pl.BlockSpec((1, tm, f), lambda bi, i, j: (bi, i, 0)),
                  pl.BlockSpec((f, tn), lambda bi, i, j: (0, j)),
                  pl.BlockSpec((1, tm, d), lambda bi, i, j: (bi, i, 0)),
                  pl.BlockSpec((1, 1, d), lambda bi, i, j: (bi, 0, 0)),
                  pl.BlockSpec((1, d), lambda bi, i, j: (0, 0))],
        out_specs=pl.BlockSpec((1, tm, d), lambda bi, i, j: (bi, i, 0)),
        out_shape=jax.ShapeDtypeStruct((b, t, d), F32),
        scratch_shapes=[pltpu.VMEM((d // tn, tm, tn), F32)],
        compiler_params=_cparams(("arbitrary", "arbitrary", "arbitrary")),
        name="ffn_down",
    )(h, w_down, x1, gate, n_post)


def _pad_cols(w, n):
    return jnp.pad(w, ((0, 0), (0, n - w.shape[1])))


def kernel(x, c, ctx, c_ctx, ada_w, ada_b, norm_pre_mix, norm_post_mix, norm_pre_ffn, norm_post_ffn, w_in, shift_mu, rwkv_w0, rwkv_w2, rwkv_a0, rwkv_a2, rwkv_g2, rwkv_k_k, rwkv_k_a, rwkv_r_k, rwkv_ln_g, rwkv_ln_b, w_rwkv_up, gla_alpha_up, gla_alpha_b, gla_norm_g, w_gla_up, w_out, ffn_w_gate, ffn_w_up, ffn_w_down):
    assert ada_w.shape[0] == 1, "single trunk layer"
    bsz, seq, d = x.shape
    n_ctx = ctx.shape[1]
    prec = BF16

    cvecs = jnp.concatenate([c, c_ctx[None, :], jnp.zeros((SUBLANES - bsz - 1, d), F32)], 0)
    mod = _modulation(cvecs, ada_w[0], ada_b[0])
    mod_x = mod[:bsz].reshape(bsz, 6, 1, d)
    shx1, scx1, gx1, shx2, scx2, gx2 = (mod_x[:, i] for i in range(6))
    mod_c = jnp.broadcast_to(mod[bsz].reshape(1, 6, 1, d), (bsz, 6, 1, d))
    shc1, scc1 = mod_c[:, 0], mod_c[:, 1]

    w_all = w_in[0]
    mix_in = RWKV_IN + GLA_IN
    w_rwkv = _pad_cols(w_all[:, :RWKV_IN], RWKV_PAD).astype(BF16)
    w_gla = _pad_cols(w_all[:, RWKV_IN:mix_in], GLA_PAD).astype(BF16)
    w_gate = w_all[:, mix_in:].astype(BF16)
    mu = _pad_cols(shift_mu, RWKV_PAD)
    n_pre = norm_pre_mix

    hx = _prenorm(x, shx1, scx1, n_pre)
    hc = _prenorm(ctx, shc1, scc1, n_pre)
    px_rwkv = _project_shift(hx, w_rwkv, mu, 2048, 512, "proj_rwkv")
    px_gla = _project_colmajor(hx, w_gla, 16, 512, "proj_gla")
    px_gate = _project(hx, w_gate, 2048, 1024, "proj_gate", gate_out=True)
    pc_rwkv = _project_shift(hc, w_rwkv, mu, n_ctx, 512, "proj_rwkv_ctx")
    pc_gla = _project(hc, w_gla, n_ctx, 512, "proj_gla_ctx")

    y_f, y_b = _rwkv_scan(px_rwkv, pc_rwkv, rwkv_w2[0], rwkv_w0[0][:, None, :], rwkv_a2[0], rwkv_a0[0][:, None, :],
                          rwkv_k_k, rwkv_k_a, prec)
    ya = _rwkv_post(y_f, y_b, px_rwkv, rwkv_a2[0], rwkv_a0[0][:, None, :], rwkv_g2[0], rwkv_k_a,
                    rwkv_r_k.reshape(1, RWKV_WIDTH), rwkv_ln_g, rwkv_ln_b)

    o_f, o_b = _gla_scan(px_gla, pc_gla, gla_alpha_up[0], gla_alpha_b[0][:, None, :], prec)
    yb = _gla_post(o_f, o_b, px_gla, gla_norm_g)

    m = _merge(ya, yb, w_rwkv_up[0], w_gla_up[0], px_gate)
    x1, h2 = _mix_out(m, w_out[0].astype(BF16), x, gx1, norm_post_mix, norm_pre_ffn, shx2, scx2)
    hf = _ffn_up(h2, ffn_w_gate[0], ffn_w_up[0])
    return _ffn_down(hf, ffn_w_down[0].astype(BF16), x1, gx2, norm_post_ffn)
```

```python
import functools

import jax
import jax.numpy as jnp
from jax import lax
from jax.experimental import pallas as pl
from jax.experimental.pallas import tpu as pltpu

F32 = jnp.float32
BF16 = jnp.bfloat16
HIGHEST = lax.Precision.HIGHEST

LANES = 128
SUBLANES = 8
VMEM_LIMIT_BYTES = 56 * 1024 * 1024

GRID_W = 64
CHUNK = 64
RWKV_HEADS, RWKV_HEAD_DIM = 16, 64
RWKV_WIDTH = RWKV_HEADS * RWKV_HEAD_DIM
HEAD_SHIFT = RWKV_HEAD_DIM.bit_length() - 1
DECAY_LORA = ICL_LORA = 96
GATE_LORA = 64
RWKV_GN_EPS = 64e-5
GLA_HEADS, GLA_KEY_DIM, GLA_VAL_DIM = 4, 128, 256
GLA_QK_WIDTH = GLA_HEADS * GLA_KEY_DIM
GLA_V_WIDTH = GLA_HEADS * GLA_VAL_DIM
GLA_GATE_LORA = 16
GLA_TAU = 16.0
LOG2_E = 1.4426950408889634
GLA_NORM_EPS = 1e-5
GLA_SUB = 8
NORM_EPS = 1e-6
FFN_EPILOGUE_ROWS = 256

RWKV_IN = 3 * RWKV_WIDTH + 2 * DECAY_LORA + 2 * ICL_LORA + GATE_LORA
RWKV_PAD = 3584
OFF_WD = 3 * RWKV_WIDTH
OFF_AD = OFF_WD + 2 * DECAY_LORA
OFF_GD = OFF_AD + 2 * ICL_LORA
GLA_IN = 2 * GLA_QK_WIDTH + 2 * GLA_V_WIDTH + 2 * GLA_GATE_LORA
GLA_BLK = 512
GLA_PAD = 7 * GLA_BLK
OFF_GLA_AD = 6 * GLA_BLK
OFF_GLA_G = 2 * GLA_QK_WIDTH + GLA_V_WIDTH

PAIR = 2 * RWKV_HEAD_DIM
N_PAIRS = RWKV_HEADS // 2


def _cparams(semantics):
    return pltpu.CompilerParams(dimension_semantics=semantics, vmem_limit_bytes=VMEM_LIMIT_BYTES)


def _sigmoid(z):
    return 1.0 / (1.0 + jnp.exp(-z))


def _softplus(z):
    return jnp.maximum(z, 0.0) + jnp.log(1.0 + jnp.exp(-jnp.abs(z)))


def _dot(a, b, dims, precision):
    if precision is BF16:
        a, b, precision = a.astype(BF16), b.astype(BF16), None
    return lax.dot_general(a, b, (dims, ((), ())), precision=precision, preferred_element_type=F32)


def _mm(a, b, precision=None):
    return _dot(a, b, ((1,), (0,)), precision)


def _mm_nt(a, b, precision=None):
    return _dot(a, b, ((1,), (1,)), precision)


def _mm_tn(a, b, precision=None):
    return _dot(a, b, ((0,), (0,)), precision)


def _split2(z):
    hi = z.astype(BF16)
    return hi, (z - hi.astype(F32)).astype(BF16)


def _mm_split(a, b):
    ah, al = _split2(a)
    bh, bl = _split2(b)
    return (_mm(al, bh) + _mm(ah, bl)) + _mm(ah, bh)


def _mm_exact_lhs(e, b):
    hi = b.astype(BF16)
    r = b - hi.astype(F32)
    mid = r.astype(BF16)
    lo = (r - mid.astype(F32)).astype(BF16)
    e = e.astype(BF16)
    return (_mm(e, lo) + _mm(e, mid)) + _mm(e, hi)


def _mod_kernel(c_ref, w_ref, b_ref, o_ref):
    s = c_ref[...]
    s = s * _sigmoid(s)
    o_ref[...] = _mm_split(s, w_ref[...]) + b_ref[...]


def _modulation(cvecs, ada_w, ada_b, tn=1024):
    m, d = cvecs.shape
    n = ada_w.shape[1]
    return pl.pallas_call(
        _mod_kernel,
        grid=(n // tn,),
        in_specs=[pl.BlockSpec((m, d), lambda j: (0, 0)),
                  pl.BlockSpec((d, tn), lambda j: (0, j)),
                  pl.BlockSpec((1, tn), lambda j: (0, j))],
        out_specs=pl.BlockSpec((m, tn), lambda j: (0, j)),
        out_shape=jax.ShapeDtypeStruct((m, n), F32),
        compiler_params=_cparams(("arbitrary",)),
        name="adaln_mod",
    )(cvecs, ada_w, ada_b.reshape(1, n))


def _prenorm_kernel(x_ref, sh_ref, sc_ref, g_ref, o_ref):
    x = x_ref[0]
    ms = jnp.mean(x * x, axis=-1, keepdims=True)
    y = x * lax.rsqrt(ms + NORM_EPS) * g_ref[...]
    o_ref[0] = (y * (1.0 + sc_ref[0]) + sh_ref[0]).astype(o_ref.dtype)


def _prenorm(x, shift, scale, gain, tm=256):
    b, t, d = x.shape
    row = lambda bi, i: (bi, i, 0)
    per_b = lambda bi, i: (bi, 0, 0)
    return pl.pallas_call(
        _prenorm_kernel,
        grid=(b, t // tm),
        in_specs=[pl.BlockSpec((1, tm, d), row), pl.BlockSpec((1, 1, d), per_b), pl.BlockSpec((1, 1, d), per_b),
                  pl.BlockSpec((1, d), lambda bi, i: (0, 0))],
        out_specs=pl.BlockSpec((1, tm, d), row),
        out_shape=jax.ShapeDtypeStruct((b, t, d), BF16),
        compiler_params=_cparams(("arbitrary", "arbitrary")),
        name="prenorm",
    )(x, shift, scale, gain)


def _proj_kernel(gate_out, h_ref, w_ref, o_ref):
    p = _mm(h_ref[0], w_ref[...])
    o_ref[0] = (_sigmoid(p) if gate_out else p).astype(o_ref.dtype)


def _project(h, w, tm, tn, name, gate_out=False):
    b, t, d = h.shape
    n = w.shape[1]
    return pl.pallas_call(
        functools.partial(_proj_kernel, gate_out),
        grid=(b, t // tm, n // tn),
        in_specs=[pl.BlockSpec((1, tm, d), lambda bi, i, j: (bi, i, 0)),
                  pl.BlockSpec((d, tn), lambda bi, i, j: (0, j))],
        out_specs=pl.BlockSpec((1, tm, tn), lambda bi, i, j: (bi, i, j)),
        out_shape=jax.ShapeDtypeStruct((b, t, n), BF16 if gate_out else F32),
        compiler_params=_cparams(("arbitrary", "arbitrary", "arbitrary")),
        name=name,
    )(h, w)


def _colmajor_perm(n_rows, n_cols):
    dst = jnp.arange(n_rows * n_cols)
    src = (dst % n_rows) * n_cols + dst // n_rows
    return (src[:, None] == dst[None, :]).astype(BF16)


def _proj_cm_kernel(h_ref, perm_ref, w_ref, o_ref, h_scr):
    @pl.when(pl.program_id(2) == 0)
    def _():
        n_r, n_c, d = h_ref.shape[1:]
        for cb in range(d // GLA_BLK):
            cs = slice(cb * GLA_BLK, (cb + 1) * GLA_BLK)
            hr = h_ref[0, :, :, cs].astype(F32).reshape(n_r * n_c, GLA_BLK).astype(BF16)
            h_scr[:, cs] = _mm(perm_ref[...], hr).astype(BF16)

    o_ref[0] = _mm(h_scr[...], w_ref[...])


def _project_colmajor(h, w, n_cols, tn, name):
    b, t, d = h.shape
    n = w.shape[1]
    n_r = t // GRID_W
    tm = n_r * n_cols
    h4 = h.reshape(b, n_r, GRID_W, d)
    perm = _colmajor_perm(n_r, n_cols)
    return pl.pallas_call(
        _proj_cm_kernel,
        grid=(b, GRID_W // n_cols, n // tn),
        in_specs=[pl.BlockSpec((1, n_r, n_cols, d), lambda bi, i, j: (bi, 0, i, 0)),
                  pl.BlockSpec((tm, tm), lambda bi, i, j: (0, 0)),
                  pl.BlockSpec((d, tn), lambda bi, i, j: (0, j))],
        out_specs=pl.BlockSpec((1, tm, tn), lambda bi, i, j: (bi, i, j)),
        out_shape=jax.ShapeDtypeStruct((b, t, n), F32),
        scratch_shapes=[pltpu.VMEM((tm, d), BF16)],
        compiler_params=_cparams(("arbitrary", "arbitrary", "arbitrary")),
        name=name,
    )(h4, perm, w)


HALO = 16


def _proj_shift_kernel(h_ref, hp_ref, hn_ref, w_ref, mu_ref, o_ref):
    i = pl.program_id(1)
    tm = h_ref.shape[1]
    p = _mm(h_ref[0], w_ref[...])
    ph = _mm(jnp.concatenate([hp_ref[0], hn_ref[0]], 0), w_ref[...])
    before = jnp.where(i == 0, 0.0, ph[HALO - 1:HALO])
    after = jnp.where(i == pl.num_programs(1) - 1, 0.0, ph[HALO:HALO + 1])
    row = lax.broadcasted_iota(jnp.int32, p.shape, 0)
    prev = jnp.where(row == 0, before, pltpu.roll(p, 1, 0))
    nxt = jnp.where(row == tm - 1, after, pltpu.roll(p, tm - 1, 0))
    o_ref[0] = p + mu_ref[...] * (0.5 * (prev + nxt) - p)


def _project_shift(h, w, mu, tm, tn, name):
    b, t, d = h.shape
    n = w.shape[1]
    rh, nh = tm // HALO, t // HALO
    return pl.pallas_call(
        _proj_shift_kernel,
        grid=(b, t // tm, n // tn),
        in_specs=[pl.BlockSpec((1, tm, d), lambda bi, i, j: (bi, i, 0)),
                  pl.BlockSpec((1, HALO, d), lambda bi, i, j: (bi, jnp.maximum(i * rh - 1, 0), 0)),
                  pl.BlockSpec((1, HALO, d), lambda bi, i, j: (bi, jnp.minimum((i + 1) * rh, nh - 1), 0)),
                  pl.BlockSpec((d, tn), lambda bi, i, j: (0, j)),
                  pl.BlockSpec((1, tn), lambda bi, i, j: (0, j))],
        out_specs=pl.BlockSpec((1, tm, tn), lambda bi, i, j: (bi, i, j)),
        out_shape=jax.ShapeDtypeStruct((b, t, n), F32),
        compiler_params=_cparams(("arbitrary", "arbitrary", "arbitrary")),
        name=name,
    )(h, h, h, w, mu)


def _head_sum_matrix():
    r = lax.broadcasted_iota(jnp.int32, (PAIR, PAIR), 0)
    c = lax.broadcasted_iota(jnp.int32, (PAIR, PAIR), 1)
    return jnp.where((r >> HEAD_SHIFT) == (c >> HEAD_SHIFT), 1.0, 0.0).astype(BF16)


def _head_sum(z, hsum):
    hi = z.astype(BF16)
    lo = (z - hi.astype(F32)).astype(BF16)
    return _mm(hi, hsum) + _mm(lo, hsum)


def _rwkv_kernel(prec, n_ctx_chunks, plf_ref, pcf_ref, plb_ref, pcb_ref, w2_ref, w0_ref, a2_ref, a0_ref, kk_ref, ka_ref,
                 yf_ref, yb_ref, st_scr):
    @pl.when(pl.program_id(1) == 0)
    def _():
        st_scr[...] = jnp.zeros_like(st_scr)

    is_ctx = pl.program_id(1) < n_ctx_chunks
    y_refs = (yf_ref, yb_ref)
    prep = [_rwkv_prep(dr == 1, jnp.where(is_ctx, pc_ref[0], pl_ref[0]), w2_ref.at[dr], w0_ref.at[dr], a2_ref.at[dr],
                       a0_ref.at[dr], kk_ref, ka_ref)
            for dr, (pl_ref, pc_ref) in enumerate(((plf_ref, pcf_ref), (plb_ref, pcb_ref)))]

    items = [(dr, pr) for pr in range(N_PAIRS) for dr in range(2)]
    n = range(len(items))
    per_pair = lambda name: [prep[dr][name][pr] for dr, pr in items]
    per_dir = lambda name: [prep[dr][name] for dr, _ in items]
    at, bt, kt, rt_, rs, bh, kh, v_swap, v_stack, p_end = (per_pair(k) for k in (
        "at", "bt", "kt", "rt", "rs", "bh", "kh", "v_swap", "v_stack", "p_end"))
    mask_n, mask_k, incl_c = per_dir("mask_n"), per_dir("mask_k"), per_dir("incl_c")

    c2 = 2 * CHUNK
    ri = lax.broadcasted_iota(jnp.int32, (c2, PAIR), 0)
    li = lax.broadcasted_iota(jnp.int32, (c2, PAIR), 1)
    ident = ri == li
    eye = jnp.where(ident, 1.0, 0.0).astype(F32)
    lane_e = lax.broadcasted_iota(jnp.int32, (CHUNK, PAIR), 1) < RWKV_HEAD_DIM
    cat0 = lambda *z: jnp.concatenate(z, 0)

    g_e = [_mm_nt(cat0(at[i][0], rs[i][0]), cat0(bt[i], kt[i]), prec) for i in n]
    g_o = [_mm_nt(cat0(at[i][1], rs[i][1]), cat0(kt[i], bt[i]), prec) for i in n]
    g_top = [cat0(g_e[i][0:CHUNK], g_o[i][0:CHUNK]) for i in n]
    nbd = [jnp.where(mask_n[i], g_top[i], 0.0) for i in n]
    aak = [jnp.where(mask_k[i], g_top[i], 0.0) for i in n]
    rab = [jnp.where(incl_c[i], jnp.where(lane_e, g_e[i][CHUNK:c2], g_o[i][CHUNK:c2]), 0.0) for i in n]
    rak = [jnp.where(incl_c[i], jnp.where(lane_e, g_o[i][CHUNK:c2], g_e[i][CHUNK:c2]), 0.0) for i in n]

    x = [_mm(aak[i], v_swap[i], prec) for i in n]
    n2 = [_mm(z, z, prec) for z in nbd]
    y0b = [_mm(rak[i], v_swap[i], prec) for i in n]
    n4 = [_mm(z, z, prec) for z in n2]
    imn = [eye - z for z in nbd]
    p1 = [imn[i] + _mm(imn[i], n2[i], prec) for i in n]
    n8 = [_mm(z, z, prec) for z in n4]
    nb = [_mm_tn(kh[i], v_stack[i], prec) for i in n]
    n16 = [_mm(z, z, prec) for z in n8]
    p2 = [eye + n4[i] + n8[i] + _mm(n4[i], n8[i], prec) for i in n]
    n32 = [_mm(z, z, prec) for z in n16]
    p12 = [_mm(p1[i], p2[i], prec) for i in n]
    p3 = [eye + n16[i] + n32[i] + _mm(n16[i], n32[i], prec) for i in n]
    tinv = [_mm(p12[i], p3[i], prec) for i in n]
    wu = [-_mm(tinv[i], jnp.concatenate([cat0(*at[i]), x[i]], 1), prec) for i in n]
    qy = [_mm(rab[i], wu[i], prec) for i in n]
    mn = [_mm_tn(bh[i], wu[i], prec) for i in n]
    q = [rt_[i] + qy[i][:, 0:PAIR] for i in n]
    m = [jnp.where(ident, p_end[i], 0.0) + mn[i][:, 0:PAIR] for i in n]
    qm = [_mm(cat0(q[i], m[i]), st_scr[dr, pr], prec) for i, (dr, pr) in enumerate(items)]
    for i, (dr, pr) in enumerate(items):
        y_refs[dr][0, :, pr * PAIR:(pr + 1) * PAIR] = qm[i][0:CHUNK] + qy[i][:, PAIR:2 * PAIR] + y0b[i]
        st_scr[dr, pr] = qm[i][CHUNK:CHUNK + PAIR] + mn[i][:, PAIR:2 * PAIR] + nb[i]


def _rwkv_prep(rev, p, w2_ref, w0_ref, a2_ref, a0_ref, kk_ref, ka_ref):
    hd = RWKV_HEAD_DIM
    c2 = 2 * CHUNK
    r = p[:, 0:RWKV_WIDTH]
    k = p[:, RWKV_WIDTH:2 * RWKV_WIDTH]
    v = p[:, 2 * RWKV_WIDTH:3 * RWKV_WIDTH]
    d_off = DECAY_LORA if rev else 0
    wd = p[:, OFF_WD + d_off:OFF_WD + d_off + DECAY_LORA]
    ad = p[:, OFF_AD + d_off:OFF_AD + d_off + ICL_LORA]

    w_log = -_softplus(-(w0_ref[...] + _mm_split(jnp.tanh(wd), w2_ref[...]))) - 0.5
    lw = -jnp.exp(w_log) * LOG2_E
    a = _sigmoid(a0_ref[...] + _mm(ad, a2_ref[...], BF16))
    kk_raw = k * kk_ref[...]
    kd = k * (1.0 + (a - 1.0) * ka_ref[...])

    ri = lax.broadcasted_iota(jnp.int32, (c2, PAIR), 0)
    li = lax.broadcasted_iota(jnp.int32, (c2, PAIR), 1)
    rt, lt = ri & (CHUNK - 1), li & (hd - 1)
    same_head = (ri >> HEAD_SHIFT) == (li >> HEAD_SHIFT)
    strict = (lt > rt) if rev else (lt < rt)
    rc = lax.broadcasted_iota(jnp.int32, (CHUNK, PAIR), 0)
    lc = lax.broadcasted_iota(jnp.int32, (CHUNK, PAIR), 1)
    lane_e = lc < hd
    hsum = _head_sum_matrix()

    ci = lax.broadcasted_iota(jnp.int32, (CHUNK, CHUNK), 0)
    cj = lax.broadcasted_iota(jnp.int32, (CHUNK, CHUNK), 1)
    tri = jnp.where((cj >= ci) if rev else (cj <= ci), 1.0, 0.0).astype(F32)
    cum = _mm_exact_lhs(tri, lw)
    total = cum[0:1] if rev else cum[CHUNK - 1:CHUNK]
    e_prev = jnp.exp2(cum - lw)
    e_neg = jnp.exp2(-cum)
    e_pos = jnp.exp2(cum)
    e_rest = jnp.exp2(total - cum)
    p_end = jnp.exp2(total)

    def split(z):
        ze = jnp.where(lane_e, z, 0.0)
        return ze, z - ze

    sls = [slice(pr * PAIR, (pr + 1) * PAIR) for pr in range(N_PAIRS)]
    cat0 = lambda *z: jnp.concatenate(z, 0)
    kkr = [kk_raw[:, sl] for sl in sls]
    nrm2 = [_head_sum(z * z, hsum) for z in kkr]
    kk = [z / jnp.maximum(jnp.sqrt(n), 1e-12) for z, n in zip(kkr, nrm2)]
    bb = [z * a[:, sl] for z, sl in zip(kk, sls)]
    rt_ = [r[:, sl] * e_pos[:, sl] for sl in sls]
    vs = [split(v[:, sl]) for sl in sls]
    return dict(
        mask_n=jnp.logical_and(same_head, strict),
        mask_k=jnp.logical_and(jnp.logical_not(same_head), strict),
        incl_c=((lc & (hd - 1)) >= rc) if rev else ((lc & (hd - 1)) <= rc),
        at=[split(z * e_prev[:, sl]) for z, sl in zip(kk, sls)],
        bt=[z * e_neg[:, sl] for z, sl in zip(bb, sls)],
        kt=[kd[:, sl] * e_neg[:, sl] for sl in sls],
        rt=rt_,
        rs=[split(z) for z in rt_],
        bh=[cat0(*split(z * e_rest[:, sl])) for z, sl in zip(bb, sls)],
        kh=[cat0(*split(kd[:, sl] * e_rest[:, sl])) for sl in sls],
        v_swap=[cat0(vo, ve) for ve, vo in vs],
        v_stack=[cat0(ve, vo) for ve, vo in vs],
        p_end=[p_end[:, sl] for sl in sls],
    )


def _scan_order(n_ctx_chunks, n_lat):
    n_steps = n_ctx_chunks + n_lat
    lat_f = lambda i: jnp.maximum(i - n_ctx_chunks, 0)
    lat_b = lambda i: jnp.where(i < n_ctx_chunks, n_lat - 1, n_steps - 1 - i)
    ctx_f = lambda i: jnp.minimum(i, n_ctx_chunks - 1)
    ctx_b = lambda i: jnp.maximum(n_ctx_chunks - 1 - i, 0)
    return n_steps, (lat_f, lat_b), (ctx_f, ctx_b)


def _rwkv_scan(p_lat, p_ctx, w2, w0, a2, a0, k_k, k_a, prec):
    b, t, w = p_lat.shape
    n_lat = t // CHUNK
    n_ctx_chunks = p_ctx.shape[1] // CHUNK
    n_steps, lat_of, ctx_of = _scan_order(n_ctx_chunks, n_lat)
    chunk = lambda f: pl.BlockSpec((1, CHUNK, w), lambda bi, i: (bi, f(i), 0))
    out = lambda f: pl.BlockSpec((1, CHUNK, RWKV_WIDTH), lambda bi, i: (bi, f(i), 0))
    vec = lambda bi, i: (0, 0)
    vec3 = lambda bi, i: (0, 0, 0)
    y_shape = jax.ShapeDtypeStruct((b, t, RWKV_WIDTH), F32)
    return pl.pallas_call(
        functools.partial(_rwkv_kernel, prec, n_ctx_chunks),
        grid=(b, n_steps),
        in_specs=[chunk(lat_of[0]), chunk(ctx_of[0]), chunk(lat_of[1]), chunk(ctx_of[1]),
                  pl.BlockSpec((2, DECAY_LORA, RWKV_WIDTH), vec3),
                  pl.BlockSpec((2, 1, RWKV_WIDTH), vec3),
                  pl.BlockSpec((2, ICL_LORA, RWKV_WIDTH), vec3),
                  pl.BlockSpec((2, 1, RWKV_WIDTH), vec3),
                  pl.BlockSpec((1, RWKV_WIDTH), vec),
                  pl.BlockSpec((1, RWKV_WIDTH), vec)],
        out_specs=[out(lat_of[0]), out(lat_of[1])],
        out_shape=[y_shape, y_shape],
        scratch_shapes=[pltpu.VMEM((2, N_PAIRS, PAIR, PAIR), F32)],
        compiler_params=_cparams(("arbitrary", "arbitrary")),
        name="rwkv7_scan",
    )(p_lat, p_ctx, p_lat, p_ctx, w2, w0, a2, a0, k_k, k_a)


def _gla_kernel(prec, n_ctx_chunks, *refs):
    lat = (refs[0:6], refs[6:12])
    aup_ref, ab_ref, of_ref, ob_ref, st_scr = refs[12:17]
    o_refs = (of_ref, ob_ref)

    @pl.when(pl.program_id(1) == 0)
    def _():
        st_scr[...] = jnp.zeros_like(st_scr)

    is_ctx = pl.program_id(1) < n_ctx_chunks
    dk, dv, sb = GLA_KEY_DIM, GLA_VAL_DIM, GLA_SUB
    n_sb = CHUNK // sb
    prep = [_gla_prep(dr == 1, is_ctx, *lat[dr], aup_ref.at[dr], ab_ref.at[dr]) for dr in range(2)]

    items = [(dr, h) for h in range(GLA_HEADS) for dr in range(2)]
    n = range(len(items))
    revs = [dr == 1 for dr, _ in items]
    qh, kh, bh, lah, toth, vh = ([prep[dr][name][h] for dr, h in items] for name in ("q", "k", "b", "la", "tot", "v"))
    st = [st_scr[dr, h] for dr, h in items]

    o_inter = [_mm_nt(qh[i] * jnp.exp2(bh[i]), st[i], prec) for i in n]
    st_new = [st[i] * jnp.exp2(toth[i]) + _mm_tn(vh[i], kh[i] * jnp.exp2(toth[i] - bh[i]), prec) for i in n]

    arow = lax.broadcasted_iota(jnp.int32, (CHUNK, CHUNK), 0)
    acol = lax.broadcasted_iota(jnp.int32, (CHUNK, CHUNK), 1)
    bcol = lax.broadcasted_iota(jnp.int32, (sb, CHUNK), 1)
    off_rows = [[] for _ in n]
    for blk in range(n_sb):
        rs = slice(blk * sb, (blk + 1) * sb)
        for i in n:
            rev = revs[i]
            if (blk == n_sb - 1) if rev else (blk == 0):
                off_rows[i].append(jnp.zeros((sb, CHUNK), F32))
                continue
            first = blk * sb + (sb - 1 if rev else 0)
            before = (bcol >= (blk + 1) * sb) if rev else (bcol < blk * sb)
            beta = bh[i][first:first + 1] - lah[i][first:first + 1]
            qs = qh[i][rs] * jnp.exp2(bh[i][rs] - beta)
            ksc = kh[i] * jnp.exp2(jnp.minimum(beta - bh[i], 0.0))
            off_rows[i].append(jnp.where(before, _mm_nt(qs, ksc, prec), 0.0))
    att = [jnp.concatenate(off_rows[i], 0) for i in n]

    in_blk = arow & (sb - 1)
    for s in range(sb):
        pick = lambda z: jnp.concatenate(
            [jnp.broadcast_to(z[blk * sb + s:blk * sb + s + 1], (sb, dk)) for blk in range(n_sb)], 0)
        on_col = acol == (arow & ~(sb - 1)) + s
        tgt = (jnp.logical_and(on_col, in_blk >= s), jnp.logical_and(on_col, in_blk <= s))
        for i in n:
            e = jnp.exp2(jnp.minimum(bh[i] - pick(bh[i]), 0.0))
            col = jnp.sum(qh[i] * pick(kh[i]) * e, axis=-1, keepdims=True)
            att[i] = jnp.where(tgt[revs[i]], col, att[i])

    for i, (dr, h) in enumerate(items):
        o_refs[dr][0, :, h * dv:(h + 1) * dv] = o_inter[i] + _mm(att[i], vh[i], prec)
        st_scr[dr, h] = st_new[i]


def _gla_prep(rev, is_ctx, q_ref, k_ref, v0_ref, v1_ref, ad_ref, ctx_ref, aup_ref, ab_ref):
    pc = ctx_ref[0]
    q = jnp.where(is_ctx, pc[:, 0:GLA_BLK], q_ref[0])
    k = jnp.where(is_ctx, pc[:, GLA_BLK:2 * GLA_BLK], k_ref[0])
    v = jnp.concatenate([jnp.where(is_ctx, pc[:, 2 * GLA_BLK:3 * GLA_BLK], v0_ref[0]),
                         jnp.where(is_ctx, pc[:, 3 * GLA_BLK:4 * GLA_BLK], v1_ref[0])], 1)
    d_off = GLA_GATE_LORA if rev else 0
    ad = jnp.where(is_ctx, pc[:, OFF_GLA_AD:OFF_GLA_AD + LANES], ad_ref[0][:, 0:LANES])
    ad = ad[:, d_off:d_off + GLA_GATE_LORA]

    la = -_softplus(-(_mm_split(ad, aup_ref[...]) + ab_ref[...])) * (LOG2_E / GLA_TAU)
    ci = lax.broadcasted_iota(jnp.int32, (CHUNK, CHUNK), 0)
    cj = lax.broadcasted_iota(jnp.int32, (CHUNK, CHUNK), 1)
    tri = jnp.where((cj >= ci) if rev else (cj <= ci), 1.0, 0.0).astype(F32)
    cum = _mm_exact_lhs(tri, la)
    total = cum[0:1] if rev else cum[CHUNK - 1:CHUNK]
    dk, dv = GLA_KEY_DIM, GLA_VAL_DIM
    ksl = [slice(h * dk, (h + 1) * dk) for h in range(GLA_HEADS)]
    scale = GLA_KEY_DIM ** -0.5
    return dict(q=[q[:, s_] * scale for s_ in ksl], k=[k[:, s_] for s_ in ksl], b=[cum[:, s_] for s_ in ksl],
                la=[la[:, s_] for s_ in ksl], tot=[total[:, s_] for s_ in ksl],
                v=[v[:, h * dv:(h + 1) * dv] for h in range(GLA_HEADS)])


def _gla_scan(p_lat, p_ctx, alpha_up, alpha_b, prec):
    b, t, w = p_lat.shape
    assert w == GLA_PAD and t == GRID_W * CHUNK
    n_ctx_chunks = p_ctx.shape[1] // CHUNK
    n_steps, col_of, ctx_of = _scan_order(n_ctx_chunks, GRID_W)

    def blocks(dr):
        lat = lambda m: pl.BlockSpec((1, CHUNK, GLA_BLK), lambda bi, i: (bi, col_of[dr](i), m))
        return [lat(0), lat(1), lat(2), lat(3), lat(OFF_GLA_AD // GLA_BLK), pl.BlockSpec((1, CHUNK, w), lambda bi, i: (bi, ctx_of[dr](i), 0))]

    out = lambda dr: pl.BlockSpec((1, CHUNK, GLA_V_WIDTH), lambda bi, i: (bi, col_of[dr](i), 0))
    vec3 = lambda bi, i: (0, 0, 0)
    o_shape = jax.ShapeDtypeStruct((b, t, GLA_V_WIDTH), F32)
    return pl.pallas_call(
        functools.partial(_gla_kernel, prec, n_ctx_chunks),
        grid=(b, n_steps),
        in_specs=blocks(0) + blocks(1) + [pl.BlockSpec((2, GLA_GATE_LORA, GLA_QK_WIDTH), vec3),
                                          pl.BlockSpec((2, 1, GLA_QK_WIDTH), vec3)],
        out_specs=[out(0), out(1)],
        out_shape=[o_shape, o_shape],
        scratch_shapes=[pltpu.VMEM((2, GLA_HEADS, GLA_VAL_DIM, GLA_KEY_DIM), F32)],
        compiler_params=_cparams(("arbitrary", "arbitrary")),
        name="gla_scan",
    )(*([p_lat] * 5 + [p_ctx]) * 2, alpha_up, alpha_b)


def _rwkv_post_kernel(yf_ref, yb_ref, pm_ref, a2_ref, a0_ref, g2_ref, ka_ref, rk_ref, lng_ref, lnb_ref, o_ref):
    p = pm_ref[0]
    r = p[:, 0:RWKV_WIDTH]
    k = p[:, RWKV_WIDTH:2 * RWKV_WIDTH]
    v = p[:, 2 * RWKV_WIDTH:3 * RWKV_WIDTH]
    ad_f = p[:, OFF_AD:OFF_AD + ICL_LORA]
    ad_b = p[:, OFF_AD + ICL_LORA:OFF_AD + 2 * ICL_LORA]
    gd = p[:, OFF_GD:OFF_GD + GATE_LORA]
    ka = ka_ref[...]
    a_f = _sigmoid(a0_ref[0] + _mm(ad_f, a2_ref[0], BF16))
    a_b = _sigmoid(a0_ref[1] + _mm(ad_b, a2_ref[1], BF16))
    kd_sum = k * (2.0 + (a_f + a_b - 2.0) * ka)
    gate = _mm(_sigmoid(gd), g2_ref[...], BF16)
    rkk = r * kd_sum * rk_ref[...]
    ysum = yf_ref[0] + yb_ref[0]
    hsum = _head_sum_matrix()
    inv_n = 1.0 / RWKV_HEAD_DIM
    for pr in range(N_PAIRS):
        sl = slice(pr * PAIR, (pr + 1) * PAIR)
        ys = ysum[:, sl]
        mean = _head_sum(ys, hsum) * inv_n
        dlt = ys - mean
        var = _head_sum(dlt * dlt, hsum) * inv_n
        gn = dlt * lax.rsqrt(var + RWKV_GN_EPS) * lng_ref[:, sl] + lnb_ref[:, sl]
        bonus = _head_sum(rkk[:, sl], hsum) * v[:, sl]
        o_ref[0, :, sl] = ((gn + bonus) * gate[:, sl]).astype(o_ref.dtype)


def _rwkv_post(y_f, y_b, pmix, a2, a0, g2, k_a, r_k, ln_g, ln_b, tm=256):
    b, t, w = y_f.shape
    row = lambda bi, i: (bi, i, 0)
    vec = lambda bi, i: (0, 0)
    vec3 = lambda bi, i: (0, 0, 0)
    return pl.pallas_call(
        _rwkv_post_kernel,
        grid=(b, t // tm),
        in_specs=[pl.BlockSpec((1, tm, w), row),
                  pl.BlockSpec((1, tm, w), row),
                  pl.BlockSpec((1, tm, pmix.shape[2]), row),
                  pl.BlockSpec((2, ICL_LORA, w), vec3),
                  pl.BlockSpec((2, 1, w), vec3),
                  pl.BlockSpec((GATE_LORA, w), vec),
                  pl.BlockSpec((1, w), vec), pl.BlockSpec((1, w), vec),
                  pl.BlockSpec((1, w), vec), pl.BlockSpec((1, w), vec)],
        out_specs=pl.BlockSpec((1, tm, w), row),
        out_shape=jax.ShapeDtypeStruct((b, t, w), BF16),
        compiler_params=_cparams(("arbitrary", "arbitrary")),
        name="rwkv_post",
    )(y_f, y_b, pmix, a2, a0, g2, k_a, r_k, ln_g, ln_b)


def _gla_post_kernel(of_ref, ob_ref, g_ref, ng_ref, perm_ref, o_ref):
    n_r, n_c = o_ref.shape[1:3]
    dv = GLA_VAL_DIM
    for h in range(GLA_HEADS):
        sl = slice(h * dv, (h + 1) * dv)
        oh = of_ref[0, :, sl] + ob_ref[0, :, sl]
        oh = oh * lax.rsqrt(jnp.mean(oh * oh, axis=-1, keepdims=True) + GLA_NORM_EPS) * ng_ref[:, sl]
        gh = g_ref[0, :, sl]
        y = (oh * (gh * _sigmoid(gh))).astype(BF16)
        o_ref[0, :, :, sl] = _mm(perm_ref[...], y).reshape(n_r, n_c, dv).astype(o_ref.dtype)


def _gla_post(o_f, o_b, p_gla, norm_g, n_cols=16):
    b, t, w = o_f.shape
    n_r = t // GRID_W
    tm = n_r * n_cols
    perm_t = _colmajor_perm(n_r, n_cols).T
    row = lambda bi, i: (bi, i, 0)
    out = pl.pallas_call(
        _gla_post_kernel,
        grid=(b, GRID_W // n_cols),
        in_specs=[pl.BlockSpec((1, tm, w), row),
                  pl.BlockSpec((1, tm, w), row),
                  pl.BlockSpec((1, tm, w), lambda bi, i: (bi, i, OFF_GLA_G // w)),
                  pl.BlockSpec((1, w), lambda bi, i: (0, 0)),
                  pl.BlockSpec((tm, tm), lambda bi, i: (0, 0))],
        out_specs=pl.BlockSpec((1, n_r, n_cols, w), lambda bi, i: (bi, 0, i, 0)),
        out_shape=jax.ShapeDtypeStruct((b, n_r, GRID_W, w), BF16),
        compiler_params=_cparams(("arbitrary", "arbitrary")),
        name="gla_post",
    )(o_f, o_b, p_gla, norm_g, perm_t)
    return out.reshape(b, t, w)


def _merge_kernel(ya_ref, yb_ref, wr_ref, wg_ref, ga_ref, gb_ref, o_ref, wr_scr, wg_scr):
    @pl.when(jnp.logical_and(pl.program_id(1) == 0, pl.program_id(2) == 0))
    def _():
        wr_scr[...] = wr_ref[...].astype(BF16)
        wg_scr[...] = wg_ref[...].astype(BF16)

    ma = _mm(ya_ref[0], wr_scr[...])
    mb = _mm(yb_ref[0], wg_scr[...])
    o_ref[0] = (ga_ref[0].astype(F32) * ma + gb_ref[0].astype(F32) * mb).astype(o_ref.dtype)


def _merge(ya, yb, w_r, w_g, p_gate, tm=1024, tn=1024):
    b, t, w = ya.shape
    d = w_r.shape[1]
    nj = d // tn
    return pl.pallas_call(
        _merge_kernel,
        grid=(nj, b, t // tm),
        in_specs=[pl.BlockSpec((1, tm, w), lambda j, bi, i: (bi, i, 0)),
                  pl.BlockSpec((1, tm, w), lambda j, bi, i: (bi, i, 0)),
                  pl.BlockSpec((w, tn), lambda j, bi, i: (0, j)),
                  pl.BlockSpec((w, tn), lambda j, bi, i: (0, j)),
                  pl.BlockSpec((1, tm, tn), lambda j, bi, i: (bi, i, j)),
                  pl.BlockSpec((1, tm, tn), lambda j, bi, i: (bi, i, j + nj))],
        out_specs=pl.BlockSpec((1, tm, tn), lambda j, bi, i: (bi, i, j)),
        out_shape=jax.ShapeDtypeStruct((b, t, d), BF16),
        scratch_shapes=[pltpu.VMEM((w, tn), BF16), pltpu.VMEM((w, tn), BF16)],
        compiler_params=_cparams(("arbitrary", "arbitrary", "arbitrary")),
        name="merge_branches",
    )(ya, yb, w_r, w_g, p_gate, p_gate)


def _mix_out_kernel(m_ref, w_ref, x_ref, gate_ref, npost_ref, npre_ref, sh_ref, sc_ref, x1_ref, h_ref):
    half = m_ref.shape[1] // 2
    for rs in (slice(0, half), slice(half, 2 * half)):
        z = _mm(m_ref[0, rs, :], w_ref[...])
        z = z * lax.rsqrt(jnp.mean(z * z, axis=-1, keepdims=True) + NORM_EPS) * npost_ref[...]
        x1 = x_ref[0, rs, :] + gate_ref[0] * z
        x1_ref[0, rs, :] = x1
        y = x1 * lax.rsqrt(jnp.mean(x1 * x1, axis=-1, keepdims=True) + NORM_EPS) * npre_ref[...]
        h_ref[0, rs, :] = (y * (1.0 + sc_ref[0]) + sh_ref[0]).astype(h_ref.dtype)


def _mix_out(m, w_out, x, gate, n_post, n_pre, shift, scale, tm=512):
    b, t, d = x.shape
    row = lambda bi, i: (bi, i, 0)
    per_b = lambda bi, i: (bi, 0, 0)
    vec = lambda bi, i: (0, 0)
    return pl.pallas_call(
        _mix_out_kernel,
        grid=(b, t // tm),
        in_specs=[pl.BlockSpec((1, tm, d), row),
                  pl.BlockSpec((d, d), vec),
                  pl.BlockSpec((1, tm, d), row),
                  pl.BlockSpec((1, 1, d), per_b),
                  pl.BlockSpec((1, d), vec), pl.BlockSpec((1, d), vec),
                  pl.BlockSpec((1, 1, d), per_b), pl.BlockSpec((1, 1, d), per_b)],
        out_specs=[pl.BlockSpec((1, tm, d), row), pl.BlockSpec((1, tm, d), row)],
        out_shape=[jax.ShapeDtypeStruct((b, t, d), F32), jax.ShapeDtypeStruct((b, t, d), BF16)],
        compiler_params=_cparams(("arbitrary", "arbitrary")),
        name="mix_out",
    )(m, w_out, x, gate, n_post, n_pre, shift, scale)


def _ffn_up_kernel(h_ref, wg_ref, wu_ref, o_ref, wg_scr, wu_scr):
    @pl.when(jnp.logical_and(pl.program_id(1) == 0, pl.program_id(2) == 0))
    def _():
        wg_scr[...] = wg_ref[...].astype(BF16)
        wu_scr[...] = wu_ref[...].astype(BF16)

    h = h_ref[0]
    a = _mm(h, wg_scr[...])
    u = _mm(h, wu_scr[...])
    o_ref[0] = (a * _sigmoid(a) * u).astype(o_ref.dtype)


def _ffn_up(h, w_gate, w_up, tm=2048, tn=512):
    b, t, d = h.shape
    f = w_gate.shape[1]
    return pl.pallas_call(
        _ffn_up_kernel,
        grid=(f // tn, b, t // tm),
        in_specs=[pl.BlockSpec((1, tm, d), lambda j, bi, i: (bi, i, 0)),
                  pl.BlockSpec((d, tn), lambda j, bi, i: (0, j)),
                  pl.BlockSpec((d, tn), lambda j, bi, i: (0, j))],
        out_specs=pl.BlockSpec((1, tm, tn), lambda j, bi, i: (bi, i, j)),
        out_shape=jax.ShapeDtypeStruct((b, t, f), BF16),
        scratch_shapes=[pltpu.VMEM((d, tn), BF16), pltpu.VMEM((d, tn), BF16)],
        compiler_params=_cparams(("arbitrary", "arbitrary", "arbitrary")),
        name="ffn_up",
    )(h, w_gate, w_up)


def _ffn_down_kernel(h_ref, w_ref, x_ref, gate_ref, npost_ref, o_ref, z_scr):
    j = pl.program_id(2)
    n_j, tm = z_scr.shape[0], z_scr.shape[1]
    z_scr[j] = _mm(h_ref[0], w_ref[...])

    @pl.when(j == n_j - 1)
    def _():
        for r0 in range(0, tm, FFN_EPILOGUE_ROWS):
            rs = slice(r0, r0 + FFN_EPILOGUE_ROWS)
            z = jnp.concatenate([z_scr[t, rs, :] for t in range(n_j)], 1)
            z = z * lax.rsqrt(jnp.mean(z * z, axis=-1, keepdims=True) + NORM_EPS) * npost_ref[...]
            o_ref[0, rs, :] = x_ref[0, rs, :] + gate_ref[0] * z


def _ffn_down(h, w_down, x1, gate, n_post, tm=1024, tn=256):
    b, t, f = h.shape
    d = w_down.shape[1]
    once = pl.Buffered(1)
    return pl.pallas_call(
        _ffn_down_kernel,
        grid=(b, t // tm, d // tn),
        in_specs=[pl.BlockSpec((1, tm, f), lambda bi, i, j: (bi, i, 0), pipeline_mode=once),
                  pl.BlockSpec((f, tn), lambda bi, i, j: (0, j)),
                  pl.BlockSpec((1, tm, d), lambda bi, i, j: (bi, i, 0), pipeline_mode=once),
                  pl.BlockSpec((1, 1, d), lambda bi, i, j: (bi, 0, 0)),
                  pl.BlockSpec((1, d), lambda bi, i, j: (0, 0))],
        out_specs=pl.BlockSpec((1, tm, d), lambda bi, i, j: (bi, i, 0)),
        out_shape=jax.ShapeDtypeStruct((b, t, d), F32),
        scratch_shapes=[pltpu.VMEM((d // tn, tm, tn), F32)],
        compiler_params=_cparams(("arbitrary", "arbitrary", "arbitrary")),
        name="ffn_down",
    )(h, w_down, x1, gate, n_post)


def _pad_cols(w, n):
    return jnp.pad(w, ((0, 0), (0, n - w.shape[1])))


def kernel(x, c, ctx, c_ctx, ada_w, ada_b, norm_pre_mix, norm_post_mix, norm_pre_ffn, norm_post_ffn, w_in, shift_mu, rwkv_w0, rwkv_w2, rwkv_a0, rwkv_a2, rwkv_g2, rwkv_k_k, rwkv_k_a, rwkv_r_k, rwkv_ln_g, rwkv_ln_b, w_rwkv_up, gla_alpha_up, gla_alpha_b, gla_norm_g, w_gla_up, w_out, ffn_w_gate, ffn_w_up, ffn_w_down):
    assert ada_w.shape[0] == 1, "single trunk layer"
    bsz, seq, d = x.shape
    n_ctx = ctx.shape[1]
    prec = BF16

    cvecs = jnp.concatenate([c, c_ctx[None, :], jnp.zeros((SUBLANES - bsz - 1, d), F32)], 0)
    mod = _modulation(cvecs, ada_w[0], ada_b[0])
    mod_x = mod[:bsz].reshape(bsz, 6, 1, d)
    shx1, scx1, gx1, shx2, scx2, gx2 = (mod_x[:, i] for i in range(6))
    mod_c = jnp.broadcast_to(mod[bsz].reshape(1, 6, 1, d), (bsz, 6, 1, d))
    shc1, scc1 = mod_c[:, 0], mod_c[:, 1]

    w_all = w_in[0]
    mix_in = RWKV_IN + GLA_IN
    w_rwkv = _pad_cols(w_all[:, :RWKV_IN], RWKV_PAD).astype(BF16)
    w_gla = _pad_cols(w_all[:, RWKV_IN:mix_in], GLA_PAD).astype(BF16)
    w_gate = w_all[:, mix_in:].astype(BF16)
    mu = _pad_cols(shift_mu, RWKV_PAD)
    n_pre = norm_pre_mix

    hx = _prenorm(x, shx1, scx1, n_pre)
    hc = _prenorm(ctx, shc1, scc1, n_pre)
    px_rwkv = _project_shift(hx, w_rwkv, mu, 2048, 512, "proj_rwkv")
    px_gla = _project_colmajor(hx, w_gla, 16, 512, "proj_gla")
    px_gate = _project(hx, w_gate, 2048, 1024, "proj_gate", gate_out=True)
    pc_rwkv = _project_shift(hc, w_rwkv, mu, n_ctx, 512, "proj_rwkv_ctx")
    pc_gla = _project(hc, w_gla, n_ctx, 512, "proj_gla_ctx")

    y_f, y_b = _rwkv_scan(px_rwkv, pc_rwkv, rwkv_w2[0], rwkv_w0[0][:, None, :], rwkv_a2[0], rwkv_a0[0][:, None, :],
                          rwkv_k_k, rwkv_k_a, prec)
    ya = _rwkv_post(y_f, y_b, px_rwkv, rwkv_a2[0], rwkv_a0[0][:, None, :], rwkv_g2[0], rwkv_k_a,
                    rwkv_r_k.reshape(1, RWKV_WIDTH), rwkv_ln_g, rwkv_ln_b)

    o_f, o_b = _gla_scan(px_gla, pc_gla, gla_alpha_up[0], gla_alpha_b[0][:, None, :], prec)
    yb = _gla_post(o_f, o_b, px_gla, gla_norm_g)

    m = _merge(ya, yb, w_rwkv_up[0], w_gla_up[0], px_gate)
    x1, h2 = _mix_out(m, w_out[0].astype(BF16), x, gx1, norm_post_mix, norm_pre_ffn, shx2, scx2)
    hf = _ffn_up(h2, ffn_w_gate[0], ffn_w_up[0])
    return _ffn_down(hf, ffn_w_down[0].astype(BF16), x1, gx2, norm_post_ffn)
```
